```python
import math
import jax, jax.numpy as jnp
from jax import lax
import numpy as np

D_MODEL = 1024
BATCH = 8
SEQ = 2048
DEPTH = 2

N_MEM = 256
GRID_W = 64
HY_W = 512
NA_HEADS = 8
NA_HEAD_DIM = 64
NA_W = NA_HEADS * NA_HEAD_DIM
NA_WIN_ROWS = 8
NA_WIN_COLS = 16
SC_W = 512
XA_HEADS = 4
XA_HEAD_DIM = D_MODEL // XA_HEADS
D_FF = 2816
HY_ORDER = 2
HY_EMB = 33
HY_HIDDEN = 64
HY_FAST_DECAY = 0.3
HY_SLOW_DECAY = 1.5
HY_TARGET = 1e-2
N_BRANCH = 3
EPS = 1e-6
PROJ_W = 3 * HY_W + 3 * NA_W + 3 * SC_W + N_BRANCH * D_MODEL

kernel_name = "hybrid_hyena_natten_shortconv_encoder"


def rms_norm(x, g):
    xf = x.astype(jnp.float32)
    y = xf * lax.rsqrt(jnp.mean(xf * xf, axis=-1, keepdims=True) + EPS)
    return (y * g.astype(jnp.float32)).astype(x.dtype)


def dwconv3(x, w):
    xp = jnp.pad(x, ((0, 0), (1, 1), (0, 0)))
    return xp[:, :-2] * w[0] + xp[:, 1:-1] * w[1] + xp[:, 2:] * w[2]


def hyena_filters(L, w1, b1, w2, b2, w3, freq):
    f32 = jnp.float32
    t = jnp.linspace(0.0, 1.0, L, dtype=f32)[:, None]
    bands = (HY_EMB - 1) // 2
    w = 2.0 * math.pi * jnp.arange(L, dtype=f32)[:, None] / L
    f = jnp.linspace(1e-4, bands - 1, bands, dtype=f32)[None, :]
    z = jnp.concatenate([t, jnp.cos(f * w), -jnp.sin(f * w)], axis=-1)
    h = jnp.sin(freq[0].astype(f32) * (z @ w1.astype(f32) + b1.astype(f32)))
    h = jnp.sin(freq[1].astype(f32) * (h @ w2.astype(f32) + b2.astype(f32)))
    h = (h @ w3.astype(f32)).reshape(L, 2, HY_ORDER, HY_W)
    deltas = jnp.abs(jnp.linspace(math.log(HY_TARGET) / HY_SLOW_DECAY,
                                  math.log(HY_TARGET) / HY_FAST_DECAY, HY_W, dtype=f32))
    h = h * jnp.exp(-t * deltas)[:, None, None, :]
    h_fwd, h_bwd = h[:, 0], h[:, 1]
    k2 = jnp.concatenate([h_fwd, jnp.zeros((1, HY_ORDER, HY_W), f32), h_bwd[1:][::-1]], axis=0)
    return jnp.fft.rfft(k2, axis=0)


def long_conv(z, kf, bias):
    L = z.shape[1]
    zf32 = z.astype(jnp.float32)
    zf = jnp.fft.rfft(zf32, n=2 * L, axis=1)
    y = jnp.fft.irfft(zf * kf[None], n=2 * L, axis=1)[:, :L]
    return (y + zf32 * bias.astype(jnp.float32)).astype(z.dtype)


def hyena_mixer(u, short_w, kf, bias):
    u = dwconv3(u, short_w)
    v, x1, x2 = jnp.split(u, 3, axis=-1)
    z = x1 * long_conv(v, kf[:, 0], bias[0])
    return x2 * long_conv(z, kf[:, 1], bias[1])


def neighbourhood_attention(q, k, v, rpb):
    B, L, _ = q.shape
    rows = L // GRID_W
    kr = min(NA_WIN_ROWS, rows)

    def grid(t):
        return t.reshape(B, rows, GRID_W, NA_HEADS, NA_HEAD_DIM).transpose(0, 3, 1, 2, 4)

    qg = grid(q) * (NA_HEAD_DIM ** -0.5)
    kg, vg = grid(k), grid(v)
    r = jnp.arange(rows)
    row_idx = jnp.clip(r - kr // 2, 0, rows - kr)[:, None] + jnp.arange(kr)[None, :]
    k_rows = kg[:, :, row_idx]
    v_rows = vg[:, :, row_idx]
    c = jnp.arange(GRID_W)
    col_start = jnp.clip(c - NA_WIN_COLS // 2, 0, GRID_W - NA_WIN_COLS)
    col_mask = (c[None, :] >= col_start[:, None]) & (c[None, :] < col_start[:, None] + NA_WIN_COLS)
    dr = row_idx - r[:, None] + (NA_WIN_ROWS - 1)
    dc = jnp.clip(c[None, :] - c[:, None] + NA_WIN_COLS - 1, 0, 2 * NA_WIN_COLS - 2)
    bias = rpb[:, dr][..., dc].transpose(0, 1, 3, 2, 4)
    s = jnp.einsum('bhrqd,bhrkcd->bhrqkc', qg, k_rows).astype(jnp.float32)
    s = s + bias[None].astype(jnp.float32)
    s = jnp.where(col_mask[:, None, :], s, -1e30)
    p = jax.nn.softmax(s, axis=(-2, -1)).astype(v.dtype)
    o = jnp.einsum('bhrqkc,bhrkcd->bhrqd', p, v_rows)
    return o.transpose(0, 2, 3, 1, 4).reshape(B, L, NA_W)


def short_conv_mixer(b_gate, c_gate, x_in, w):
    return b_gate * dwconv3(c_gate * x_in, w)


def memory_cross_attention(h, mem_n, wq, wkv, wo):
    B, L, _ = h.shape
    M = mem_n.shape[1]
    q = (h @ wq).reshape(B, L, XA_HEADS, XA_HEAD_DIM)
    km, vm = jnp.split(mem_n @ wkv, 2, axis=-1)
    km = km.reshape(B, M, XA_HEADS, XA_HEAD_DIM)
    vm = vm.reshape(B, M, XA_HEADS, XA_HEAD_DIM)
    s = jnp.einsum('bshd,bmhd->bhsm', q, km).astype(jnp.float32) * (XA_HEAD_DIM ** -0.5)
    p = jax.nn.softmax(s, axis=-1).astype(h.dtype)
    o = jnp.einsum('bhsm,bmhd->bshd', p, vm).reshape(B, L, D_MODEL)
    return o @ wo


def conv_glu_ffn(h, w_up, w_conv, w_down):
    u = dwconv3(h @ w_up, w_conv)
    g, val = jnp.split(u, 2, axis=-1)
    return (jax.nn.gelu(g, approximate=True) * val) @ w_down


def setup_inputs(seed: int = 0) -> dict:
    key = jax.random.key(seed)
    ks = jax.random.split(key, 24)

    def nrm(k, shape, scale):
        return jax.random.normal(k, shape, jnp.float32) * scale

    return {
        "x": nrm(ks[0], (BATCH, SEQ, D_MODEL), 1.0),
        "mem": nrm(ks[1], (BATCH, N_MEM, D_MODEL), 1.0),
        "norm_gains": 1.0 + nrm(ks[2], (DEPTH, 6, D_MODEL), 0.05),
        "mem_norm": 1.0 + nrm(ks[3], (DEPTH, D_MODEL), 0.05),
        "w_in": nrm(ks[4], (DEPTH, D_MODEL, PROJ_W), D_MODEL ** -0.5),
        "gate_bias": nrm(ks[5], (DEPTH, N_BRANCH, D_MODEL), 0.02),
        "hy_short_w": nrm(ks[6], (DEPTH, 3, 3 * HY_W), 3 ** -0.5),
        "hy_w1": nrm(ks[7], (DEPTH, HY_EMB, HY_HIDDEN), HY_EMB ** -0.5),
        "hy_b1": nrm(ks[8], (DEPTH, HY_HIDDEN), 0.02),
        "hy_w2": nrm(ks[9], (DEPTH, HY_HIDDEN, HY_HIDDEN), HY_HIDDEN ** -0.5),
        "hy_b2": nrm(ks[10], (DEPTH, HY_HIDDEN), 0.02),
        "hy_w3": nrm(ks[11], (DEPTH, HY_HIDDEN, 2 * HY_ORDER * HY_W), 0.05 * HY_HIDDEN ** -0.5),
        "hy_freq": 1.0 + nrm(ks[12], (DEPTH, 2, HY_HIDDEN), 0.05),
        "hy_bias": nrm(ks[13], (DEPTH, HY_ORDER, HY_W), 0.1),
        "na_rpb": nrm(ks[14], (DEPTH, NA_HEADS, 2 * NA_WIN_ROWS - 1, 2 * NA_WIN_COLS - 1), 0.02),
        "sc_conv_w": nrm(ks[15], (DEPTH, 3, SC_W), 3 ** -0.5),
        "w_branch": nrm(ks[16], (DEPTH, N_BRANCH, HY_W, D_MODEL), HY_W ** -0.5),
        "w_out": nrm(ks[17], (DEPTH, D_MODEL, D_MODEL), D_MODEL ** -0.5),
        "xa_wq": nrm(ks[18], (DEPTH, D_MODEL, D_MODEL), D_MODEL ** -0.5),
        "xa_wkv": nrm(ks[19], (DEPTH, D_MODEL, 2 * D_MODEL), D_MODEL ** -0.5),
        "xa_wo": nrm(ks[20], (DEPTH, D_MODEL, D_MODEL), D_MODEL ** -0.5),
        "ffn_up": nrm(ks[21], (DEPTH, D_MODEL, 2 * D_FF), D_MODEL ** -0.5),
        "ffn_conv": nrm(ks[22], (DEPTH, 3, 2 * D_FF), 3 ** -0.5),
        "ffn_down": nrm(ks[23], (DEPTH, D_FF, D_MODEL), D_FF ** -0.5),
    }


def reference(x, mem, norm_gains, mem_norm, w_in, gate_bias, hy_short_w, hy_w1, hy_b1, hy_w2, hy_b2,
              hy_w3, hy_freq, hy_bias, na_rpb, sc_conv_w, w_branch, w_out, xa_wq, xa_wkv, xa_wo,
              ffn_up, ffn_conv, ffn_down):
    B, L, _ = x.shape
    splits = [3 * HY_W, 3 * HY_W + 3 * NA_W, 3 * HY_W + 3 * NA_W + 3 * SC_W]
    for l in range(DEPTH):
        g = norm_gains[l]
        h = rms_norm(x, g[0])
        proj = h @ w_in[l]
        hy_u, na_qkv, sc_u, gate_pre = jnp.split(proj, splits, axis=-1)
        kf = hyena_filters(L, hy_w1[l], hy_b1[l], hy_w2[l], hy_b2[l], hy_w3[l], hy_freq[l])
        y_a = hyena_mixer(hy_u, hy_short_w[l], kf, hy_bias[l])
        q, k, v = jnp.split(na_qkv, 3, axis=-1)
        y_b = neighbourhood_attention(q, k, v, na_rpb[l])
        b_gate, c_gate, x_in = jnp.split(sc_u, 3, axis=-1)
        y_c = short_conv_mixer(b_gate, c_gate, x_in, sc_conv_w[l])
        gates = jax.nn.sigmoid(gate_pre.reshape(B, L, N_BRANCH, D_MODEL) + gate_bias[l])
        merged = (gates[:, :, 0] * (y_a @ w_branch[l, 0])
                  + gates[:, :, 1] * (y_b @ w_branch[l, 1])
                  + gates[:, :, 2] * (y_c @ w_branch[l, 2]))
        x = x + rms_norm(merged @ w_out[l], g[1])
        h = rms_norm(x, g[2])
        mem_n = rms_norm(mem, mem_norm[l])
        x = x + rms_norm(memory_cross_attention(h, mem_n, xa_wq[l], xa_wkv[l], xa_wo[l]), g[3])
        h = rms_norm(x, g[4])
        x = x + rms_norm(conv_glu_ffn(h, ffn_up[l], ffn_conv[l], ffn_down[l]), g[5])
    return x
```

```python
import functools
import math

import jax
import jax.numpy as jnp
from jax import lax
from jax.experimental import pallas as pl
from jax.experimental.pallas import tpu as pltpu

D_MODEL = 1024
SEQ = 2048
N_MEM = 256
GRID_W = 64
HY_W = 512
NA_HEADS = 8
NA_HEAD_DIM = 64
NA_W = NA_HEADS * NA_HEAD_DIM
NA_WIN_ROWS = 8
NA_WIN_COLS = 16
SC_W = 512
XA_HEADS = 4
XA_HEAD_DIM = D_MODEL // XA_HEADS
D_FF = 2816
HY_ORDER = 2
HY_EMB = 33
HY_HIDDEN = 64
HY_FAST_DECAY = 0.3
HY_SLOW_DECAY = 1.5
HY_TARGET = 1e-2
N_BRANCH = 3
EPS = 1e-6

FFT_N = 2 * SEQ
NA_ROWS = SEQ // GRID_W
NA_KEYS = NA_WIN_ROWS * GRID_W
NA_GROUP = 4
COL_TILE = 256
ROW_TILE = 512
VMEM_LIMIT = 60 * 1024 * 1024

BF16 = jnp.bfloat16
F32 = jnp.float32


def _params(*sem):
    return pltpu.CompilerParams(dimension_semantics=sem, vmem_limit_bytes=VMEM_LIMIT)


def _rms(xf, g):
    ms = jnp.mean(xf * xf, axis=-1, keepdims=True)
    return xf * lax.rsqrt(ms + EPS) * g


def _dwconv3(u, w_ref):
    n = u.shape[0]
    row = lax.broadcasted_iota(jnp.int32, u.shape, 0)
    prev = jnp.where(row == 0, 0.0, pltpu.roll(u, 1, 0))
    nxt = jnp.where(row == n - 1, 0.0, pltpu.roll(u, n - 1, 0))
    return prev * w_ref[0:1, :] + u * w_ref[1:2, :] + nxt * w_ref[2:3, :]


def _prenorm_kernel(x_ref, g_ref, o_ref):
    o_ref[...] = _rms(x_ref[...], g_ref[...]).astype(BF16)


def _prenorm(x2d, g):
    n = x2d.shape[0]
    tm = 1024
    return pl.pallas_call(
        _prenorm_kernel,
        grid=(n // tm,),
        in_specs=[pl.BlockSpec((tm, D_MODEL), lambda i: (i, 0)),
                  pl.BlockSpec((1, D_MODEL), lambda i: (0, 0))],
        out_specs=pl.BlockSpec((tm, D_MODEL), lambda i: (i, 0)),
        out_shape=jax.ShapeDtypeStruct((n, D_MODEL), BF16),
        compiler_params=_params("arbitrary"),
        name="prenorm",
    )(x2d, g)


def _filter_mlp_kernel(z_ref, w1_ref, b1_ref, w2_ref, b2_ref, w3_ref, f_ref, t_ref, dl_ref,
                       hs_ref, hd_ref):
    hp = lax.Precision.HIGHEST
    h = jnp.sin(f_ref[0:1, :] * (jnp.dot(z_ref[...], w1_ref[...], precision=hp) + b1_ref[...]))
    h = jnp.sin(f_ref[1:2, :] * (jnp.dot(h, w2_ref[...], precision=hp) + b2_ref[...]))
    decay = jnp.exp(-t_ref[...] * dl_ref[...])
    row = lax.broadcasted_iota(jnp.int32, (SEQ, HY_W), 0)
    for o in range(HY_ORDER):
        c_f = o * HY_W
        c_b = HY_ORDER * HY_W + o * HY_W
        hf = jnp.dot(h, w3_ref[:, c_f:c_f + HY_W], precision=hp) * decay
        hb = jnp.dot(h, w3_ref[:, c_b:c_b + HY_W], precision=hp) * decay
        hb = jnp.where(row == 0, 0.0, hb)
        hs_ref[:, c_f:c_f + HY_W] = (hf + hb).astype(BF16)
        hd_ref[:, c_f:c_f + HY_W] = (hb - hf).astype(BF16)


def _filter_dft_kernel(c_ref, s_ref, hs_ref, hd_ref, kr_ref, ki_ref, kn_ref):
    row = lax.broadcasted_iota(jnp.int32, (SEQ, COL_TILE), 0)
    wk = jnp.where(row == 0, 1.0 / FFT_N, 2.0 / FFT_N)
    hs = hs_ref[...]
    kr_ref[...] = jnp.dot(c_ref[...], hs, preferred_element_type=F32) * wk
    ki_ref[...] = jnp.dot(s_ref[...], hd_ref[...], preferred_element_type=F32) * wk
    sign = jnp.where((row & 1) == 0, 1.0, -1.0)
    kn_ref[...] = jnp.sum(hs.astype(F32) * sign, axis=0, keepdims=True) * (1.0 / FFT_N)


def _hyena_filters(dft_c, dft_s, w1, b1, w2, b2, w3, freq):
    t = jnp.linspace(0.0, 1.0, SEQ, dtype=F32)[:, None]
    bands = (HY_EMB - 1) // 2
    w = 2.0 * math.pi * jnp.arange(SEQ, dtype=F32)[:, None] / SEQ
    f = jnp.linspace(1e-4, bands - 1, bands, dtype=F32)[None, :]
    z = jnp.concatenate([t, jnp.cos(f * w), -jnp.sin(f * w)], axis=-1)
    z = jnp.pad(z, ((0, 0), (0, HY_HIDDEN - HY_EMB)))
    w1p = jnp.pad(w1.astype(F32), ((0, HY_HIDDEN - HY_EMB), (0, 0)))
    deltas = jnp.abs(jnp.linspace(math.log(HY_TARGET) / HY_SLOW_DECAY,
                                  math.log(HY_TARGET) / HY_FAST_DECAY, HY_W, dtype=F32))[None, :]
    width = HY_ORDER * HY_W
    hs, hd = pl.pallas_call(
        _filter_mlp_kernel,
        out_shape=(jax.ShapeDtypeStruct((SEQ, width), BF16),
                   jax.ShapeDtypeStruct((SEQ, width), BF16)),
        compiler_params=pltpu.CompilerParams(vmem_limit_bytes=VMEM_LIMIT),
        name="hyena_filter_mlp",
    )(z, w1p, b1[None].astype(F32), w2.astype(F32), b2[None].astype(F32), w3.astype(F32),
      freq.astype(F32), t, deltas)
    nt = width // COL_TILE
    full = pl.BlockSpec((SEQ, SEQ), lambda j: (0, 0))
    col = pl.BlockSpec((SEQ, COL_TILE), lambda j: (0, j))
    return pl.pallas_call(
        _filter_dft_kernel,
        grid=(nt,),
        in_specs=[full, full, col, col],
        out_specs=(col, col, pl.BlockSpec((1, COL_TILE), lambda j: (0, j))),
        out_shape=(jax.ShapeDtypeStruct((SEQ, width), F32),
                   jax.ShapeDtypeStruct((SEQ, width), F32),
                   jax.ShapeDtypeStruct((1, width), F32)),
        compiler_params=_params("arbitrary"),
        name="hyena_filter_dft",
    )(dft_c, dft_s, hs, hd)


def _hyena_kernel(hn_ref, wv_ref, w1_ref, w2_ref, sv_ref, s1_ref, s2_ref, c_ref, s_ref,
                  kr0_ref, ki0_ref, kn0_ref, kr1_ref, ki1_ref, kn1_ref, bias_ref,
                  o_ref, z32_ref, zb_ref, yr_ref, yi_ref, x1_ref, x2_ref):
    hn = hn_ref[...]
    z32_ref[...] = _dwconv3(jnp.dot(hn, wv_ref[...], preferred_element_type=F32), sv_ref)
    x1_ref[...] = _dwconv3(jnp.dot(hn, w1_ref[...], preferred_element_type=F32), s1_ref)
    x2_ref[...] = _dwconv3(jnp.dot(hn, w2_ref[...], preferred_element_type=F32), s2_ref)
    row = lax.broadcasted_iota(jnp.int32, (SEQ, COL_TILE), 0)
    sign = jnp.where((row & 1) == 0, 1.0, -1.0)

    def long_conv(kr_ref, ki_ref, kn_ref, o):
        z = z32_ref[...]
        zb_ref[...] = z.astype(BF16)
        zb = zb_ref[...]
        vr = jnp.dot(c_ref[...], zb, preferred_element_type=F32)
        va = jnp.dot(s_ref[...], zb, preferred_element_type=F32)
        kr = kr_ref[...]
        ki = ki_ref[...]
        yr_ref[...] = (vr * kr + va * ki).astype(BF16)
        yi_ref[...] = (vr * ki - va * kr).astype(BF16)
        vn = jnp.sum(zb.astype(F32) * sign, axis=0, keepdims=True)
        y = (jnp.dot(c_ref[...], yr_ref[...], preferred_element_type=F32)
             - jnp.dot(s_ref[...], yi_ref[...], preferred_element_type=F32)
             + sign * (vn * kn_ref[...]))
        return y + z * bias_ref[o:o + 1, :]

    z32_ref[...] = x1_ref[...] * long_conv(kr0_ref, ki0_ref, kn0_ref, 0)
    o_ref[...] = (x2_ref[...] * long_conv(kr1_ref, ki1_ref, kn1_ref, 1)).astype(BF16)


def _hyena_mixer(hn, w_in, short_w, dft_c, dft_s, kr, ki, kn, bias):
    b = hn.shape[0]
    nt = HY_W // COL_TILE
    once = pl.Buffered(1)

    def wcol(k):
        return pl.BlockSpec((D_MODEL, COL_TILE), lambda j, i, k=k: (0, k * nt + j))

    def scol(k):
        return pl.BlockSpec((3, COL_TILE), lambda j, i, k=k: (0, k * nt + j))

    def kcol(o, rows):
        return pl.BlockSpec((rows, COL_TILE), lambda j, i, o=o: (0, o * nt + j), pipeline_mode=once)

    full = pl.BlockSpec((SEQ, SEQ), lambda j, i: (0, 0), pipeline_mode=once)
    return pl.pallas_call(
        _hyena_kernel,
        grid=(nt, b),
        in_specs=[pl.BlockSpec((None, SEQ, D_MODEL), lambda j, i: (i, 0, 0)),
                  wcol(0), wcol(1), wcol(2), scol(0), scol(1), scol(2), full, full,
                  kcol(0, SEQ), kcol(0, SEQ), kcol(0, 1), kcol(1, SEQ), kcol(1, SEQ), kcol(1, 1),
                  pl.BlockSpec((HY_ORDER, COL_TILE), lambda j, i: (0, j))],
        out_specs=pl.BlockSpec((None, SEQ, COL_TILE), lambda j, i: (i, 0, j)),
        out_shape=jax.ShapeDtypeStruct((b, SEQ, HY_W), BF16),
        scratch_shapes=[pltpu.VMEM((SEQ, COL_TILE), F32), pltpu.VMEM((SEQ, COL_TILE), BF16),
                        pltpu.VMEM((SEQ, COL_TILE), BF16), pltpu.VMEM((SEQ, COL_TILE), BF16),
                        pltpu.VMEM((SEQ, COL_TILE), F32), pltpu.VMEM((SEQ, COL_TILE), F32)],
        compiler_params=_params("arbitrary", "arbitrary"),
        name="hyena_mixer",
    )(hn, w_in, w_in, w_in, short_w, short_w, short_w, dft_c, dft_s,
      kr, ki, kn, kr, ki, kn, bias)


def _shortconv_kernel(hn_ref, wb_ref, wc_ref, wx_ref, cw_ref, o_ref):
    hn = hn_ref[...]
    bg = jnp.dot(hn, wb_ref[...], preferred_element_type=F32)
    cg = jnp.dot(hn, wc_ref[...], preferred_element_type=F32)
    xi = jnp.dot(hn, wx_ref[...], preferred_element_type=F32)
    o_ref[...] = (bg * _dwconv3(cg * xi, cw_ref)).astype(BF16)


def _shortconv_mixer(hn, w_in, conv_w):
    b = hn.shape[0]
    nt = SC_W // COL_TILE
    base = (3 * HY_W + 3 * NA_W) // COL_TILE

    def wcol(k):
        return pl.BlockSpec((D_MODEL, COL_TILE), lambda j, i, k=k: (0, base + k * nt + j))

    return pl.pallas_call(
        _shortconv_kernel,
        grid=(nt, b),
        in_specs=[pl.BlockSpec((None, SEQ, D_MODEL), lambda j, i: (i, 0, 0)),
                  wcol(0), wcol(1), wcol(2),
                  pl.BlockSpec((3, COL_TILE), lambda j, i: (0, j))],
        out_specs=pl.BlockSpec((None, SEQ, COL_TILE), lambda j, i: (i, 0, j)),
        out_shape=jax.ShapeDtypeStruct((b, SEQ, SC_W), BF16),
        compiler_params=_params("arbitrary", "arbitrary"),
        name="shortconv_mixer",
    )(hn, w_in, w_in, w_in, conv_w)


def _na_kernel(hn_ref, wq_ref, wk_ref, wv_ref, bias_ref, o_ref, q_ref, k_ref, v_ref):
    hn = hn_ref[...]
    q = jnp.dot(hn, wq_ref[...], preferred_element_type=F32) * (NA_HEAD_DIM ** -0.5)
    k = jnp.dot(hn, wk_ref[...], preferred_element_type=F32)
    v = jnp.dot(hn, wv_ref[...], preferred_element_type=F32)
    for h in range(NA_GROUP):
        sl = slice(h * NA_HEAD_DIM, (h + 1) * NA_HEAD_DIM)
        q_ref[h] = q[:, sl].astype(BF16)
        k_ref[h] = k[:, sl].astype(BF16)
        v_ref[h] = v[:, sl].astype(BF16)

    def row_body(r, carry):
        w0 = jnp.clip(r - NA_WIN_ROWS // 2, 0, NA_ROWS - NA_WIN_ROWS)
        off = w0 - r + (NA_WIN_ROWS - 1)
        q0 = pl.multiple_of(r * GRID_W, GRID_W)
        k0 = pl.multiple_of(w0 * GRID_W, GRID_W)
        for h in range(NA_GROUP):
            qh = q_ref[h, pl.ds(q0, GRID_W), :]
            kh = k_ref[h, pl.ds(k0, NA_KEYS), :]
            vh = v_ref[h, pl.ds(k0, NA_KEYS), :]
            s = lax.dot_general(qh, kh, (((1,), (1,)), ((), ())), preferred_element_type=F32)
            s = s + bias_ref[h, off]
            p = jnp.exp(s - jnp.max(s, axis=-1, keepdims=True))
            den = jnp.sum(p, axis=-1, keepdims=True)
            pv = jnp.dot(p.astype(BF16), vh, preferred_element_type=F32)
            o_ref[pl.ds(q0, GRID_W), h * NA_HEAD_DIM:(h + 1) * NA_HEAD_DIM] = (pv / den).astype(BF16)
        return carry

    lax.fori_loop(0, NA_ROWS, row_body, 0)


def _na_bias(rpb):
    c = jnp.arange(GRID_W)
    col_start = jnp.clip(c - NA_WIN_COLS // 2, 0, GRID_W - NA_WIN_COLS)
    col_mask = (c[None, :] >= col_start[:, None]) & (c[None, :] < col_start[:, None] + NA_WIN_COLS)
    dc = jnp.clip(c[None, :] - c[:, None] + NA_WIN_COLS - 1, 0, 2 * NA_WIN_COLS - 2)
    dr = jnp.arange(NA_WIN_ROWS)[:, None] + jnp.arange(NA_WIN_ROWS)[None, :]
    bias = rpb.astype(F32)[:, dr][..., dc]
    bias = jnp.where(col_mask[None, None, None], bias, -1e30)
    return bias.transpose(0, 1, 3, 2, 4).reshape(NA_HEADS, NA_WIN_ROWS, GRID_W, NA_KEYS)


def _na_mixer(hn, w_in, bias):
    b = hn.shape[0]
    gw = NA_GROUP * NA_HEAD_DIM
    ng = NA_W // gw
    base = 3 * HY_W // gw

    def wcol(k):
        return pl.BlockSpec((D_MODEL, gw), lambda j, i, k=k: (0, base + k * ng + j))

    return pl.pallas_call(
        _na_kernel,
        grid=(ng, b),
        in_specs=[pl.BlockSpec((None, SEQ, D_MODEL), lambda j, i: (i, 0, 0)),
                  wcol(0), wcol(1), wcol(2),
                  pl.BlockSpec((NA_GROUP, NA_WIN_ROWS, GRID_W, NA_KEYS), lambda j, i: (j, 0, 0, 0))],
        out_specs=pl.BlockSpec((None, SEQ, gw), lambda j, i: (i, 0, j)),
        out_shape=jax.ShapeDtypeStruct((b, SEQ, NA_W), BF16),
        scratch_shapes=[pltpu.VMEM((NA_GROUP, SEQ, NA_HEAD_DIM), BF16) for _ in range(3)],
        compiler_params=_params("arbitrary", "arbitrary"),
        name="na_mixer",
    )(hn, w_in, w_in, w_in, bias)


def _merge_kernel(hn_ref, x_ref, ya_ref, yb_ref, yc_ref, wg_ref, gb_ref, wb_ref, wo_ref, g_ref, o_ref):
    hn = hn_ref[...]
    merged = None
    for i, y_ref in enumerate((ya_ref, yb_ref, yc_ref)):
        pre = jnp.dot(hn, wg_ref[:, i * D_MODEL:(i + 1) * D_MODEL], preferred_element_type=F32)
        gate = jax.nn.sigmoid(pre + gb_ref[i:i + 1, :])
        term = gate * jnp.dot(y_ref[...], wb_ref[i], preferred_element_type=F32)
        merged = term if merged is None else merged + term
    out = jnp.dot(merged.astype(BF16), wo_ref[...], preferred_element_type=F32)
    o_ref[...] = x_ref[...] + _rms(out, g_ref[...])


def _merge(hn2d, x2d, ya, yb, yc, w_gate, gate_bias, w_branch, w_out, g):
    n = x2d.shape[0]
    tm = ROW_TILE
    once = pl.Buffered(1)
    rows = lambda w: pl.BlockSpec((tm, w), lambda i: (i, 0))
    return pl.pallas_call(
        _merge_kernel,
        grid=(n // tm,),
        in_specs=[rows(D_MODEL), rows(D_MODEL), rows(HY_W), rows(NA_W), rows(SC_W),
                  pl.BlockSpec((D_MODEL, N_BRANCH * D_MODEL), lambda i: (0, 0), pipeline_mode=once),
                  pl.BlockSpec((N_BRANCH, D_MODEL), lambda i: (0, 0)),
                  pl.BlockSpec((N_BRANCH, HY_W, D_MODEL), lambda i: (0, 0, 0), pipeline_mode=once),
                  pl.BlockSpec((D_MODEL, D_MODEL), lambda i: (0, 0), pipeline_mode=once),
                  pl.BlockSpec((1, D_MODEL), lambda i: (0, 0))],
        out_specs=rows(D_MODEL),
        out_shape=jax.ShapeDtypeStruct((n, D_MODEL), F32),
        compiler_params=_params("arbitrary"),
        name="merge",
    )(hn2d, x2d, ya, yb, yc, w_gate, gate_bias, w_branch, w_out, g)


def _kv_kernel(m_ref, g_ref, w_ref, o_ref):
    mn = _rms(m_ref[...], g_ref[...]).astype(BF16)
    o_ref[...] = jnp.dot(mn, w_ref[...], preferred_element_type=F32).astype(BF16)


def _mem_kv(mem, g, wkv):
    b = mem.shape[0]
    return pl.pallas_call(
        _kv_kernel,
        grid=(b,),
        in_specs=[pl.BlockSpec((None, N_MEM, D_MODEL), lambda i: (i, 0, 0)),
                  pl.BlockSpec((1, D_MODEL), lambda i: (0, 0)),
                  pl.BlockSpec((D_MODEL, 2 * D_MODEL), lambda i: (0, 0))],
        out_specs=pl.BlockSpec((None, N_MEM, 2 * D_MODEL), lambda i: (i, 0, 0)),
        out_shape=jax.ShapeDtypeStruct((b, N_MEM, 2 * D_MODEL), BF16),
        compiler_params=_params("arbitrary"),
        name="mem_kv",
    )(mem, g, wkv)


def _xattn_kernel(x_ref, kv_ref, wq_ref, wo_ref, gq_ref, go_ref, gn_ref, o_ref, hn_ref):
    x = x_ref[...]
    h = _rms(x, gq_ref[...]).astype(BF16)
    q = (jnp.dot(h, wq_ref[...], preferred_element_type=F32) * (XA_HEAD_DIM ** -0.5)).astype(BF16)
    heads = []
    for i in range(XA_HEADS):
        sl = slice(i * XA_HEAD_DIM, (i + 1) * XA_HEAD_DIM)
        km = kv_ref[:, sl]
        vm = kv_ref[:, D_MODEL + i * XA_HEAD_DIM:D_MODEL + (i + 1) * XA_HEAD_DIM]
        s = lax.dot_general(q[:, sl], km, (((1,), (1,)), ((), ())), preferred_element_type=F32)
        p = jnp.exp(s - jnp.max(s, axis=-1, keepdims=True))
        den = jnp.sum(p, axis=-1, keepdims=True)
        heads.append((jnp.dot(p.astype(BF16), vm, preferred_element_type=F32) / den).astype(BF16))
    o = jnp.dot(jnp.concatenate(heads, axis=-1), wo_ref[...], preferred_element_type=F32)
    xn = x + _rms(o, go_ref[...])
    o_ref[...] = xn
    hn_ref[...] = _rms(xn, gn_ref[...]).astype(BF16)


def _xattn(x, kv, wq, wo, gq, go, gn):
    b = x.shape[0]
    tm = ROW_TILE
    once = pl.Buffered(1)
    rows = pl.BlockSpec((None, tm, D_MODEL), lambda i, j: (i, j, 0))
    gain = pl.BlockSpec((1, D_MODEL), lambda i, j: (0, 0))
    wfull = pl.BlockSpec((D_MODEL, D_MODEL), lambda i, j: (0, 0), pipeline_mode=once)
    return pl.pallas_call(
        _xattn_kernel,
        grid=(b, SEQ // tm),
        in_specs=[rows, pl.BlockSpec((None, N_MEM, 2 * D_MODEL), lambda i, j: (i, 0, 0)),
                  wfull, wfull, gain, gain, gain],
        out_specs=(rows, rows),
        out_shape=(jax.ShapeDtypeStruct((b, SEQ, D_MODEL), F32),
                   jax.ShapeDtypeStruct((b, SEQ, D_MODEL), BF16)),
        compiler_params=_params("arbitrary", "arbitrary"),
        name="xattn",
    )(x, kv, wq, wo, gq, go, gn)


def _gelu_tanh(x):
    return 0.5 * x * (1.0 + jnp.tanh(math.sqrt(2.0 / math.pi) * (x + 0.044715 * (x * x * x))))


def _ffn_kernel(hn_ref, wg_ref, wv_ref, cg_ref, cv_ref, wd_ref, o_ref):
    hn = hn_ref[...]
    ug = _dwconv3(jnp.dot(hn, wg_ref[...], preferred_element_type=F32), cg_ref)
    uv = _dwconv3(jnp.dot(hn, wv_ref[...], preferred_element_type=F32), cv_ref)
    act = (_gelu_tanh(ug) * uv).astype(BF16)
    part = jnp.dot(act, wd_ref[...], preferred_element_type=F32)

    @pl.when(pl.program_id(1) == 0)
    def _():
        o_ref[...] = part

    @pl.when(pl.program_id(1) != 0)
    def _():
        o_ref[...] += part


def _ffn(hn, w_up, w_conv, w_down):
    b = hn.shape[0]
    nk = D_FF // COL_TILE
    return pl.pallas_call(
        _ffn_kernel,
        grid=(b, nk),
        in_specs=[pl.BlockSpec((None, SEQ, D_MODEL), lambda i, k: (i, 0, 0)),
                  pl.BlockSpec((D_MODEL, COL_TILE), lambda i, k: (0, k)),
                  pl.BlockSpec((D_MODEL, COL_TILE), lambda i, k: (0, nk + k)),
                  pl.BlockSpec((3, COL_TILE), lambda i, k: (0, k)),
                  pl.BlockSpec((3, COL_TILE), lambda i, k: (0, nk + k)),
                  pl.BlockSpec((COL_TILE, D_MODEL), lambda i, k: (k, 0))],
        out_specs=pl.BlockSpec((None, SEQ, D_MODEL), lambda i, k: (i, 0, 0)),
        out_shape=jax.ShapeDtypeStruct((b, SEQ, D_MODEL), F32),
        compiler_params=_params("arbitrary", "arbitrary"),
        name="ffn",
    )(hn, w_up, w_up, w_conv, w_conv, w_down)


def _residual_kernel(x_ref, f_ref, g_ref, gn_ref, o_ref, hn_ref):
    xn = x_ref[...] + _rms(f_ref[...], g_ref[...])
    o_ref[...] = xn
    hn_ref[...] = _rms(xn, gn_ref[...]).astype(BF16)


def _residual(x2d, f2d, g, gn):
    n = x2d.shape[0]
    tm = 1024
    rows = pl.BlockSpec((tm, D_MODEL), lambda i: (i, 0))
    gain = pl.BlockSpec((1, D_MODEL), lambda i: (0, 0))
    return pl.pallas_call(
        _residual_kernel,
        grid=(n // tm,),
        in_specs=[rows, rows, gain, gain],
        out_specs=(rows, rows),
        out_shape=(jax.ShapeDtypeStruct((n, D_MODEL), F32),
                   jax.ShapeDtypeStruct((n, D_MODEL), BF16)),
        compiler_params=_params("arbitrary"),
        name="residual",
    )(x2d, f2d, g, gn)


def _dft_matrices():
    k = jnp.arange(SEQ, dtype=jnp.int32)
    ang = ((k[:, None] * k[None, :]) % FFT_N).astype(F32) * (2.0 * math.pi / FFT_N)
    return jnp.cos(ang).astype(BF16), jnp.sin(ang).astype(BF16)


def kernel(x, mem, norm_gains, mem_norm, w_in, gate_bias, hy_short_w, hy_w1, hy_b1, hy_w2, hy_b2,
           hy_w3, hy_freq, hy_bias, na_rpb, sc_conv_w, w_branch, w_out, xa_wq, xa_wkv, xa_wo,
           ffn_up, ffn_conv, ffn_down):
    b, l, d = x.shape
    depth = w_in.shape[0]
    assert (l, d) == (SEQ, D_MODEL) and mem.shape[1:] == (N_MEM, D_MODEL)
    n = b * l
    dft_c, dft_s = _dft_matrices()
    gains = norm_gains.astype(F32)
    x2d = x.reshape(n, d)
    hn = _prenorm(x2d, gains[0, 0][None])
    for i in range(depth):
        g = gains[i]
        w_in_b = w_in[i].astype(BF16)
        kr, ki, kn = _hyena_filters(dft_c, dft_s, hy_w1[i], hy_b1[i], hy_w2[i], hy_b2[i], hy_w3[i],
                                    hy_freq[i])
        hn3 = hn.reshape(b, l, d)
        ya = _hyena_mixer(hn3, w_in_b, hy_short_w[i].astype(F32), dft_c, dft_s, kr, ki, kn,
                          hy_bias[i].astype(F32))
        yb = _na_mixer(hn3, w_in_b, _na_bias(na_rpb[i]))
        yc = _shortconv_mixer(hn3, w_in_b, sc_conv_w[i].astype(F32))
        w_gate = w_in_b[:, 3 * HY_W + 3 * NA_W + 3 * SC_W:]
        x2d = _merge(hn, x2d, ya.reshape(n, HY_W), yb.reshape(n, NA_W), yc.reshape(n, SC_W), w_gate,
                     gate_bias[i].astype(F32), w_branch[i].astype(BF16), w_out[i].astype(BF16), g[1][None])
        kv = _mem_kv(mem, mem_norm[i].astype(F32)[None], xa_wkv[i].astype(BF16))
        x3, hn2 = _xattn(x2d.reshape(b, l, d), kv, xa_wq[i].astype(BF16), xa_wo[i].astype(BF16),
                         g[2][None], g[3][None], g[4][None])
        f = _ffn(hn2, ffn_up[i].astype(BF16), ffn_conv[i].astype(F32), ffn_down[i].astype(BF16))
        g_next = gains[i + 1, 0] if i + 1 < depth else g[0]
        x2d, hn = _residual(x3.reshape(n, d), f.reshape(n, d), g[5][None], g_next[None])
    return x2d.reshape(b, l, d)
```

```python
import functools
import math

import jax
import jax.numpy as jnp
from jax import lax
from jax.experimental import pallas as pl
from jax.experimental.pallas import tpu as pltpu

D_MODEL = 1024
SEQ = 2048
N_MEM = 256
GRID_W = 64
HY_W = 512
NA_HEADS = 8
NA_HEAD_DIM = 64
NA_W = NA_HEADS * NA_HEAD_DIM
NA_WIN_ROWS = 8
NA_WIN_COLS = 16
SC_W = 512
XA_HEADS = 4
XA_HEAD_DIM = D_MODEL // XA_HEADS
D_FF = 2816
HY_ORDER = 2
HY_EMB = 33
HY_HIDDEN = 64
HY_FAST_DECAY = 0.3
HY_SLOW_DECAY = 1.5
HY_TARGET = 1e-2
N_BRANCH = 3
EPS = 1e-6

FFT_N = 2 * SEQ
NA_ROWS = SEQ // GRID_W
NA_KEYS = NA_WIN_ROWS * GRID_W
NA_GROUP = 4
COL_TILE = 256
ROW_TILE = 512
CONV_PAD = 8
VMEM_LIMIT = 60 * 1024 * 1024

BF16 = jnp.bfloat16
F32 = jnp.float32


def _params(*sem):
    return pltpu.CompilerParams(dimension_semantics=sem, vmem_limit_bytes=VMEM_LIMIT)


def _rms(xf, g):
    ms = jnp.mean(xf * xf, axis=-1, keepdims=True)
    return xf * lax.rsqrt(ms + EPS) * g


def _dwconv3(u, w_ref):
    n = u.shape[0]
    row = lax.broadcasted_iota(jnp.int32, u.shape, 0)
    prev = jnp.where(row == 0, 0.0, pltpu.roll(u, 1, 0))
    nxt = jnp.where(row == n - 1, 0.0, pltpu.roll(u, n - 1, 0))
    return prev * w_ref[0:1, :] + u * w_ref[1:2, :] + nxt * w_ref[2:3, :]


def _dwconv3_padded(u, w_ref, pad_ref):
    n = u.shape[0]
    zeros = jnp.zeros((CONV_PAD, u.shape[1]), F32)
    pad_ref[0:CONV_PAD, :] = zeros
    pad_ref[CONV_PAD + n:, :] = zeros
    pad_ref[CONV_PAD:CONV_PAD + n, :] = u
    prev = pad_ref[CONV_PAD - 1:CONV_PAD - 1 + n, :]
    nxt = pad_ref[CONV_PAD + 1:CONV_PAD + 1 + n, :]
    return prev * w_ref[0:1, :] + u * w_ref[1:2, :] + nxt * w_ref[2:3, :]


def _conv_scratch(width):
    return pltpu.VMEM((SEQ + 2 * CONV_PAD, width), F32)


def _prenorm_kernel(x_ref, g_ref, o_ref):
    o_ref[...] = _rms(x_ref[...], g_ref[...]).astype(BF16)


def _prenorm(x2d, g):
    n = x2d.shape[0]
    tm = 1024
    return pl.pallas_call(
        _prenorm_kernel,
        grid=(n // tm,),
        in_specs=[pl.BlockSpec((tm, D_MODEL), lambda i: (i, 0)),
                  pl.BlockSpec((1, D_MODEL), lambda i: (0, 0))],
        out_specs=pl.BlockSpec((tm, D_MODEL), lambda i: (i, 0)),
        out_shape=jax.ShapeDtypeStruct((n, D_MODEL), BF16),
        compiler_params=_params("arbitrary"),
        name="prenorm",
    )(x2d, g)


def _filter_mlp_kernel(z_ref, w1_ref, b1_ref, w2_ref, b2_ref, w3_ref, f_ref, t_ref, dl_ref,
                       hs_ref, hd_ref):
    hp = lax.Precision.HIGHEST
    h = jnp.sin(f_ref[0:1, :] * (jnp.dot(z_ref[...], w1_ref[...], precision=hp) + b1_ref[...]))
    h = jnp.sin(f_ref[1:2, :] * (jnp.dot(h, w2_ref[...], precision=hp) + b2_ref[...]))
    decay = jnp.exp(-t_ref[...] * dl_ref[...])
    row = lax.broadcasted_iota(jnp.int32, (SEQ, HY_W), 0)
    for o in range(HY_ORDER):
        c_f = o * HY_W
        c_b = HY_ORDER * HY_W + o * HY_W
        hf = jnp.dot(h, w3_ref[:, c_f:c_f + HY_W], precision=hp) * decay
        hb = jnp.dot(h, w3_ref[:, c_b:c_b + HY_W], precision=hp) * decay
        hb = jnp.where(row == 0, 0.0, hb)
        hs_ref[:, c_f:c_f + HY_W] = (hf + hb).astype(BF16)
        hd_ref[:, c_f:c_f + HY_W] = (hb - hf).astype(BF16)


def _filter_dft_kernel(c_ref, s_ref, hs_ref, hd_ref, kr_ref, ki_ref, kn_ref):
    row = lax.broadcasted_iota(jnp.int32, (SEQ, COL_TILE), 0)
    wk = jnp.where(row == 0, 1.0 / FFT_N, 2.0 / FFT_N)
    hs = hs_ref[...]
    kr_ref[...] = jnp.dot(c_ref[...], hs, preferred_element_type=F32) * wk
    ki_ref[...] = jnp.dot(s_ref[...], hd_ref[...], preferred_element_type=F32) * wk
    sign = jnp.where((row & 1) == 0, 1.0, -1.0)
    kn_ref[...] = jnp.sum(hs.astype(F32) * sign, axis=0, keepdims=True) * (1.0 / FFT_N)


def _hyena_filters(dft_c, dft_s, w1, b1, w2, b2, w3, freq):
    t = jnp.linspace(0.0, 1.0, SEQ, dtype=F32)[:, None]
    bands = (HY_EMB - 1) // 2
    w = 2.0 * math.pi * jnp.arange(SEQ, dtype=F32)[:, None] / SEQ
    f = jnp.linspace(1e-4, bands - 1, bands, dtype=F32)[None, :]
    z = jnp.concatenate([t, jnp.cos(f * w), -jnp.sin(f * w)], axis=-1)
    z = jnp.pad(z, ((0, 0), (0, HY_HIDDEN - HY_EMB)))
    w1p = jnp.pad(w1.astype(F32), ((0, HY_HIDDEN - HY_EMB), (0, 0)))
    deltas = jnp.abs(jnp.linspace(math.log(HY_TARGET) / HY_SLOW_DECAY,
                                  math.log(HY_TARGET) / HY_FAST_DECAY, HY_W, dtype=F32))[None, :]
    width = HY_ORDER * HY_W
    hs, hd = pl.pallas_call(
        _filter_mlp_kernel,
        out_shape=(jax.ShapeDtypeStruct((SEQ, width), BF16),
                   jax.ShapeDtypeStruct((SEQ, width), BF16)),
        compiler_params=pltpu.CompilerParams(vmem_limit_bytes=VMEM_LIMIT),
        name="hyena_filter_mlp",
    )(z, w1p, b1[None].astype(F32), w2.astype(F32), b2[None].astype(F32), w3.astype(F32),
      freq.astype(F32), t, deltas)
    nt = width // COL_TILE
    full = pl.BlockSpec((SEQ, SEQ), lambda j: (0, 0))
    col = pl.BlockSpec((SEQ, COL_TILE), lambda j: (0, j))
    return pl.pallas_call(
        _filter_dft_kernel,
        grid=(nt,),
        in_specs=[full, full, col, col],
        out_specs=(col, col, pl.BlockSpec((1, COL_TILE), lambda j: (0, j))),
        out_shape=(jax.ShapeDtypeStruct((SEQ, width), F32),
                   jax.ShapeDtypeStruct((SEQ, width), F32),
                   jax.ShapeDtypeStruct((1, width), F32)),
        compiler_params=_params("arbitrary"),
        name="hyena_filter_dft",
    )(dft_c, dft_s, hs, hd)


def _hyena_kernel(hn_ref, wv_ref, w1_ref, w2_ref, sv_ref, s1_ref, s2_ref, c_ref, s_ref,
                  kr0_ref, ki0_ref, kn0_ref, kr1_ref, ki1_ref, kn1_ref, bias_ref,
                  o_ref, z32_ref, zb_ref, yr_ref, yi_ref, x1_ref, x2_ref):
    hn = hn_ref[...]
    z32_ref[...] = _dwconv3(jnp.dot(hn, wv_ref[...], preferred_element_type=F32), sv_ref)
    x1_ref[...] = _dwconv3(jnp.dot(hn, w1_ref[...], preferred_element_type=F32), s1_ref)
    x2_ref[...] = _dwconv3(jnp.dot(hn, w2_ref[...], preferred_element_type=F32), s2_ref)
    row = lax.broadcasted_iota(jnp.int32, (SEQ, COL_TILE), 0)
    sign = jnp.where((row & 1) == 0, 1.0, -1.0)

    def long_conv(kr_ref, ki_ref, kn_ref, o):
        z = z32_ref[...]
        zb_ref[...] = z.astype(BF16)
        zb = zb_ref[...]
        vr = jnp.dot(c_ref[...], zb, preferred_element_type=F32)
        va = jnp.dot(s_ref[...], zb, preferred_element_type=F32)
        kr = kr_ref[...]
        ki = ki_ref[...]
        yr_ref[...] = (vr * kr + va * ki).astype(BF16)
        yi_ref[...] = (vr * ki - va * kr).astype(BF16)
        vn = jnp.sum(zb.astype(F32) * sign, axis=0, keepdims=True)
        y = (jnp.dot(c_ref[...], yr_ref[...], preferred_element_type=F32)
             - jnp.dot(s_ref[...], yi_ref[...], preferred_element_type=F32)
             + sign * (vn * kn_ref[...]))
        return y + z * bias_ref[o:o + 1, :]

    z32_ref[...] = x1_ref[...] * long_conv(kr0_ref, ki0_ref, kn0_ref, 0)
    o_ref[...] = (x2_ref[...] * long_conv(kr1_ref, ki1_ref, kn1_ref, 1)).astype(BF16)


def _hyena_mixer(hn, w_in, short_w, dft_c, dft_s, kr, ki, kn, bias):
    b = hn.shape[0]
    nt = HY_W // COL_TILE
    once = pl.Buffered(1)

    def wcol(k):
        return pl.BlockSpec((D_MODEL, COL_TILE), lambda j, i, k=k: (0, k * nt + j))

    def scol(k):
        return pl.BlockSpec((3, COL_TILE), lambda j, i, k=k: (0, k * nt + j))

    def kcol(o, rows):
        return pl.BlockSpec((rows, COL_TILE), lambda j, i, o=o: (0, o * nt + j), pipeline_mode=once)

    full = pl.BlockSpec((SEQ, SEQ), lambda j, i: (0, 0), pipeline_mode=once)
    return pl.pallas_call(
        _hyena_kernel,
        grid=(nt, b),
        in_specs=[pl.BlockSpec((None, SEQ, D_MODEL), lambda j, i: (i, 0, 0)),
                  wcol(0), wcol(1), wcol(2), scol(0), scol(1), scol(2), full, full,
                  kcol(0, SEQ), kcol(0, SEQ), kcol(0, 1), kcol(1, SEQ), kcol(1, SEQ), kcol(1, 1),
                  pl.BlockSpec((HY_ORDER, COL_TILE), lambda j, i: (0, j))],
        out_specs=pl.BlockSpec((None, SEQ, COL_TILE), lambda j, i: (i, 0, j)),
        out_shape=jax.ShapeDtypeStruct((b, SEQ, HY_W), BF16),
        scratch_shapes=[pltpu.VMEM((SEQ, COL_TILE), F32), pltpu.VMEM((SEQ, COL_TILE), BF16),
                        pltpu.VMEM((SEQ, COL_TILE), BF16), pltpu.VMEM((SEQ, COL_TILE), BF16),
                        pltpu.VMEM((SEQ, COL_TILE), F32), pltpu.VMEM((SEQ, COL_TILE), F32)],
        compiler_params=_params("arbitrary", "arbitrary"),
        name="hyena_mixer",
    )(hn, w_in, w_in, w_in, short_w, short_w, short_w, dft_c, dft_s,
      kr, ki, kn, kr, ki, kn, bias)


def _shortconv_kernel(hn_ref, wb_ref, wc_ref, wx_ref, cw_ref, o_ref, pad_ref):
    hn = hn_ref[...]
    bg = jnp.dot(hn, wb_ref[...].astype(BF16), preferred_element_type=F32)
    cg = jnp.dot(hn, wc_ref[...].astype(BF16), preferred_element_type=F32)
    xi = jnp.dot(hn, wx_ref[...].astype(BF16), preferred_element_type=F32)
    o_ref[...] = (bg * _dwconv3_padded(cg * xi, cw_ref, pad_ref)).astype(BF16)


def _w_in_cols(layer, width, first):
    return lambda k, nt: pl.BlockSpec((None, D_MODEL, width),
                                      lambda j, i: (layer, 0, first + k * nt + j))


def _shortconv_mixer(hn, w_in, layer, conv_w):
    b = hn.shape[0]
    nt = SC_W // COL_TILE
    wcol = _w_in_cols(layer, COL_TILE, (3 * HY_W + 3 * NA_W) // COL_TILE)
    return pl.pallas_call(
        _shortconv_kernel,
        grid=(nt, b),
        in_specs=[pl.BlockSpec((None, SEQ, D_MODEL), lambda j, i: (i, 0, 0)),
                  wcol(0, nt), wcol(1, nt), wcol(2, nt),
                  pl.BlockSpec((3, COL_TILE), lambda j, i: (0, j))],
        out_specs=pl.BlockSpec((None, SEQ, COL_TILE), lambda j, i: (i, 0, j)),
        out_shape=jax.ShapeDtypeStruct((b, SEQ, SC_W), BF16),
        scratch_shapes=[_conv_scratch(COL_TILE)],
        compiler_params=_params("arbitrary", "arbitrary"),
        name="shortconv_mixer",
    )(hn, w_in, w_in, w_in, conv_w)


def _na_kernel(hn_ref, wq_ref, wk_ref, wv_ref, bias_ref, o_ref, q_ref, k_ref, v_ref):
    hn = hn_ref[...]
    q = jnp.dot(hn, wq_ref[...].astype(BF16), preferred_element_type=F32)
    q_ref[...] = (q * (NA_HEAD_DIM ** -0.5)).astype(BF16)
    k_ref[...] = jnp.dot(hn, wk_ref[...].astype(BF16), preferred_element_type=F32).astype(BF16)
    v_ref[...] = jnp.dot(hn, wv_ref[...].astype(BF16), preferred_element_type=F32).astype(BF16)
    gw = NA_GROUP * NA_HEAD_DIM
    same_head = (lax.broadcasted_iota(jnp.int32, (gw, gw), 0) // NA_HEAD_DIM
                 == lax.broadcasted_iota(jnp.int32, (gw, gw), 1) // NA_HEAD_DIM)

    def row_body(r, carry):
        w0 = jnp.clip(r - NA_WIN_ROWS // 2, 0, NA_ROWS - NA_WIN_ROWS)
        off = w0 - r + (NA_WIN_ROWS - 1)
        q0 = pl.multiple_of(r * GRID_W, GRID_W)
        k0 = pl.multiple_of(w0 * GRID_W, GRID_W)
        q_row = q_ref[pl.ds(q0, GRID_W), :]
        q_heads = jnp.where(same_head, jnp.concatenate([q_row] * NA_GROUP, axis=0), 0)
        s = lax.dot_general(q_heads, k_ref[pl.ds(k0, NA_KEYS), :], (((1,), (1,)), ((), ())),
                            preferred_element_type=F32)
        bias = jnp.concatenate(
            [jnp.concatenate([bias_ref[h, off + 2 * m] for m in range(NA_WIN_ROWS // 2)], axis=1)
             for h in range(NA_GROUP)], axis=0)
        s = s + bias
        p = jnp.exp(s - jnp.max(s, axis=-1, keepdims=True))
        inv = 1.0 / jnp.sum(p, axis=-1, keepdims=True)
        pv = jnp.dot(p.astype(BF16), v_ref[pl.ds(k0, NA_KEYS), :], preferred_element_type=F32)
        pv = jnp.where(same_head, pv * inv, 0.0)
        out = pv[0:GRID_W]
        for h in range(1, NA_GROUP):
            out = out + pv[h * GRID_W:(h + 1) * GRID_W]
        o_ref[pl.ds(q0, GRID_W), :] = out.astype(BF16)
        return carry

    lax.fori_loop(0, NA_ROWS, row_body, 0, unroll=2)


def _na_bias(rpb):
    c = jnp.arange(GRID_W)
    col_start = jnp.clip(c - NA_WIN_COLS // 2, 0, GRID_W - NA_WIN_COLS)
    col_mask = (c[None, :] >= col_start[:, None]) & (c[None, :] < col_start[:, None] + NA_WIN_COLS)
    dc = jnp.clip(c[None, :] - c[:, None] + NA_WIN_COLS - 1, 0, 2 * NA_WIN_COLS - 2)
    table = jnp.take(rpb.astype(F32), dc.reshape(-1), axis=-1)
    table = table.reshape(NA_HEADS, 2 * NA_WIN_ROWS - 1, GRID_W, GRID_W)
    table = jnp.where(col_mask[None, None], table, -1e30)
    return jnp.concatenate([table[:, :-1], table[:, 1:]], axis=-1)


def _na_mixer(hn, w_in, layer, bias):
    b = hn.shape[0]
    gw = NA_GROUP * NA_HEAD_DIM
    ng = NA_W // gw
    wcol = _w_in_cols(layer, gw, 3 * HY_W // gw)
    return pl.pallas_call(
        _na_kernel,
        grid=(ng, b),
        in_specs=[pl.BlockSpec((None, SEQ, D_MODEL), lambda j, i: (i, 0, 0)),
                  wcol(0, ng), wcol(1, ng), wcol(2, ng),
                  pl.BlockSpec((NA_GROUP, 2 * NA_WIN_ROWS - 2, GRID_W, 2 * GRID_W),
                               lambda j, i: (j, 0, 0, 0))],
        out_specs=pl.BlockSpec((None, SEQ, gw), lambda j, i: (i, 0, j)),
        out_shape=jax.ShapeDtypeStruct((b, SEQ, NA_W), BF16),
        scratch_shapes=[pltpu.VMEM((SEQ, gw), BF16) for _ in range(3)],
        compiler_params=_params("arbitrary", "arbitrary"),
        name="na_mixer",
    )(hn, w_in, w_in, w_in, bias)


def _merge_kernel(hn_ref, x_ref, ya_ref, yb_ref, yc_ref, wg_ref, gb_ref, wb_ref, wo_ref, g_ref, o_ref):
    hn = hn_ref[...]
    merged = None
    for i, y_ref in enumerate((ya_ref, yb_ref, yc_ref)):
        pre = jnp.dot(hn, wg_ref[:, i * D_MODEL:(i + 1) * D_MODEL], preferred_element_type=F32)
        gate = jax.nn.sigmoid(pre + gb_ref[i:i + 1, :])
        term = gate * jnp.dot(y_ref[...], wb_ref[i], preferred_element_type=F32)
        merged = term if merged is None else merged + term
    out = jnp.dot(merged.astype(BF16), wo_ref[...], preferred_element_type=F32)
    o_ref[...] = x_ref[...] + _rms(out, g_ref[...])


def _merge(hn2d, x2d, ya, yb, yc, w_gate, gate_bias, w_branch, w_out, g):
    n = x2d.shape[0]
    tm = ROW_TILE
    once = pl.Buffered(1)
    rows = lambda w: pl.BlockSpec((tm, w), lambda i: (i, 0))
    return pl.pallas_call(
        _merge_kernel,
        grid=(n // tm,),
        in_specs=[rows(D_MODEL), rows(D_MODEL), rows(HY_W), rows(NA_W), rows(SC_W),
                  pl.BlockSpec((D_MODEL, N_BRANCH * D_MODEL), lambda i: (0, 0), pipeline_mode=once),
                  pl.BlockSpec((N_BRANCH, D_MODEL), lambda i: (0, 0)),
                  pl.BlockSpec((N_BRANCH, HY_W, D_MODEL), lambda i: (0, 0, 0), pipeline_mode=once),
                  pl.BlockSpec((D_MODEL, D_MODEL), lambda i: (0, 0), pipeline_mode=once),
                  pl.BlockSpec((1, D_MODEL), lambda i: (0, 0))],
        out_specs=rows(D_MODEL),
        out_shape=jax.ShapeDtypeStruct((n, D_MODEL), F32),
        compiler_params=_params("arbitrary"),
        name="merge",
    )(hn2d, x2d, ya, yb, yc, w_gate, gate_bias, w_branch, w_out, g)


def _kv_kernel(m_ref, g_ref, w_ref, o_ref):
    mn = _rms(m_ref[...], g_ref[...]).astype(BF16)
    o_ref[...] = jnp.dot(mn, w_ref[...], preferred_element_type=F32).astype(BF16)


def _mem_kv(mem, g, wkv):
    b = mem.shape[0]
    return pl.pallas_call(
        _kv_kernel,
        grid=(b,),
        in_specs=[pl.BlockSpec((None, N_MEM, D_MODEL), lambda i: (i, 0, 0)),
                  pl.BlockSpec((1, D_MODEL), lambda i: (0, 0)),
                  pl.BlockSpec((D_MODEL, 2 * D_MODEL), lambda i: (0, 0))],
        out_specs=pl.BlockSpec((None, N_MEM, 2 * D_MODEL), lambda i: (i, 0, 0)),
        out_shape=jax.ShapeDtypeStruct((b, N_MEM, 2 * D_MODEL), BF16),
        compiler_params=_params("arbitrary"),
        name="mem_kv",
    )(mem, g, wkv)


def _xattn_kernel(x_ref, kv_ref, wq_ref, wo_ref, gq_ref, go_ref, gn_ref, o_ref, hn_ref):
    x = x_ref[...]
    h = _rms(x, gq_ref[...]).astype(BF16)
    q = (jnp.dot(h, wq_ref[...], preferred_element_type=F32) * (XA_HEAD_DIM ** -0.5)).astype(BF16)
    heads = []
    for i in range(XA_HEADS):
        sl = slice(i * XA_HEAD_DIM, (i + 1) * XA_HEAD_DIM)
        km = kv_ref[:, sl]
        vm = kv_ref[:, D_MODEL + i * XA_HEAD_DIM:D_MODEL + (i + 1) * XA_HEAD_DIM]
        s = lax.dot_general(q[:, sl], km, (((1,), (1,)), ((), ())), preferred_element_type=F32)
        p = jnp.exp(s - jnp.max(s, axis=-1, keepdims=True))
        den = jnp.sum(p, axis=-1, keepdims=True)
        heads.append((jnp.dot(p.astype(BF16), vm, preferred_element_type=F32) / den).astype(BF16))
    o = jnp.dot(jnp.concatenate(heads, axis=-1), wo_ref[...], preferred_element_type=F32)
    xn = x + _rms(o, go_ref[...])
    o_ref[...] = xn
    hn_ref[...] = _rms(xn, gn_ref[...]).astype(BF16)


def _xattn(x, kv, wq, wo, gq, go, gn):
    b = x.shape[0]
    tm = ROW_TILE
    once = pl.Buffered(1)
    rows = pl.BlockSpec((None, tm, D_MODEL), lambda i, j: (i, j, 0))
    gain = pl.BlockSpec((1, D_MODEL), lambda i, j: (0, 0))
    wfull = pl.BlockSpec((D_MODEL, D_MODEL), lambda i, j: (0, 0), pipeline_mode=once)
    return pl.pallas_call(
        _xattn_kernel,
        grid=(b, SEQ // tm),
        in_specs=[rows, pl.BlockSpec((None, N_MEM, 2 * D_MODEL), lambda i, j: (i, 0, 0)),
                  wfull, wfull, gain, gain, gain],
        out_specs=(rows, rows),
        out_shape=(jax.ShapeDtypeStruct((b, SEQ, D_MODEL), F32),
                   jax.ShapeDtypeStruct((b, SEQ, D_MODEL), BF16)),
        compiler_params=_params("arbitrary", "arbitrary"),
        name="xattn",
    )(x, kv, wq, wo, gq, go, gn)


def _gelu_tanh(x):
    return 0.5 * x * (1.0 + jnp.tanh(math.sqrt(2.0 / math.pi) * (x + 0.044715 * (x * x * x))))


def _ffn_kernel(hn_ref, wg_ref, wv_ref, cg_ref, cv_ref, wd_ref, o_ref, padg_ref, padv_ref):
    hn = hn_ref[...]
    ug = jnp.dot(hn, wg_ref[...].astype(BF16), preferred_element_type=F32)
    uv = jnp.dot(hn, wv_ref[...].astype(BF16), preferred_element_type=F32)
    ug = _dwconv3_padded(ug, cg_ref, padg_ref)
    uv = _dwconv3_padded(uv, cv_ref, padv_ref)
    act = (_gelu_tanh(ug) * uv).astype(BF16)
    part = jnp.dot(act, wd_ref[...].astype(BF16), preferred_element_type=F32)

    @pl.when(pl.program_id(1) == 0)
    def _():
        o_ref[...] = part

    @pl.when(pl.program_id(1) != 0)
    def _():
        o_ref[...] += part


def _ffn(hn, w_up, w_conv, w_down, layer):
    b = hn.shape[0]
    nk = D_FF // COL_TILE
    return pl.pallas_call(
        _ffn_kernel,
        grid=(b, nk),
        in_specs=[pl.BlockSpec((None, SEQ, D_MODEL), lambda i, k: (i, 0, 0)),
                  pl.BlockSpec((None, D_MODEL, COL_TILE), lambda i, k: (layer, 0, k)),
                  pl.BlockSpec((None, D_MODEL, COL_TILE), lambda i, k: (layer, 0, nk + k)),
                  pl.BlockSpec((3, COL_TILE), lambda i, k: (0, k)),
                  pl.BlockSpec((3, COL_TILE), lambda i, k: (0, nk + k)),
                  pl.BlockSpec((None, COL_TILE, D_MODEL), lambda i, k: (layer, k, 0))],
        out_specs=pl.BlockSpec((None, SEQ, D_MODEL), lambda i, k: (i, 0, 0)),
        out_shape=jax.ShapeDtypeStruct((b, SEQ, D_MODEL), F32),
        scratch_shapes=[_conv_scratch(COL_TILE), _conv_scratch(COL_TILE)],
        compiler_params=_params("arbitrary", "arbitrary"),
        name="ffn",
    )(hn, w_up, w_up, w_conv, w_conv, w_down)


def _residual_kernel(x_ref, f_ref, g_ref, gn_ref, o_ref, hn_ref):
    xn = x_ref[...] + _rms(f_ref[...], g_ref[...])
    o_ref[...] = xn
    hn_ref[...] = _rms(xn, gn_ref[...]).astype(BF16)


def _residual(x2d, f2d, g, gn):
    n = x2d.shape[0]
    tm = 1024
    rows = pl.BlockSpec((tm, D_MODEL), lambda i: (i, 0))
    gain = pl.BlockSpec((1, D_MODEL), lambda i: (0, 0))
    return pl.pallas_call(
        _residual_kernel,
        grid=(n // tm,),
        in_specs=[rows, rows, gain, gain],
        out_specs=(rows, rows),
        out_shape=(jax.ShapeDtypeStruct((n, D_MODEL), F32),
                   jax.ShapeDtypeStruct((n, D_MODEL), BF16)),
        compiler_params=_params("arbitrary"),
        name="residual",
    )(x2d, f2d, g, gn)


def _dft_matrices():
    k = jnp.arange(SEQ, dtype=jnp.int32)
    ang = ((k[:, None] * k[None, :]) % FFT_N).astype(F32) * (2.0 * math.pi / FFT_N)
    return jnp.cos(ang).astype(BF16), jnp.sin(ang).astype(BF16)


def kernel(x, mem, norm_gains, mem_norm, w_in, gate_bias, hy_short_w, hy_w1, hy_b1, hy_w2, hy_b2,
           hy_w3, hy_freq, hy_bias, na_rpb, sc_conv_w, w_branch, w_out, xa_wq, xa_wkv, xa_wo,
           ffn_up, ffn_conv, ffn_down):
    b, l, d = x.shape
    depth = w_in.shape[0]
    assert (l, d) == (SEQ, D_MODEL) and mem.shape[1:] == (N_MEM, D_MODEL)
    n = b * l
    dft_c, dft_s = _dft_matrices()
    gains = norm_gains.astype(F32)
    x2d = x.reshape(n, d)
    hn = _prenorm(x2d, gains[0, 0][None])
    for i in range(depth):
        g = gains[i]
        w_hyena = w_in[i, :, :3 * HY_W].astype(BF16)
        w_gate = w_in[i, :, 3 * HY_W + 3 * NA_W + 3 * SC_W:].astype(BF16)
        kr, ki, kn = _hyena_filters(dft_c, dft_s, hy_w1[i], hy_b1[i], hy_w2[i], hy_b2[i], hy_w3[i],
                                    hy_freq[i])
        hn3 = hn.reshape(b, l, d)
        ya = _hyena_mixer(hn3, w_hyena, hy_short_w[i].astype(F32), dft_c, dft_s, kr, ki, kn,
                          hy_bias[i].astype(F32))
        yb = _na_mixer(hn3, w_in, i, _na_bias(na_rpb[i]))
        yc = _shortconv_mixer(hn3, w_in, i, sc_conv_w[i].astype(F32))
        x2d = _merge(hn, x2d, ya.reshape(n, HY_W), yb.reshape(n, NA_W), yc.reshape(n, SC_W), w_gate,
                     gate_bias[i].astype(F32), w_branch[i].astype(BF16), w_out[i].astype(BF16), g[1][None])
        kv = _mem_kv(mem, mem_norm[i].astype(F32)[None], xa_wkv[i].astype(BF16))
        x3, hn2 = _xattn(x2d.reshape(b, l, d), kv, xa_wq[i].astype(BF16), xa_wo[i].astype(BF16),
                         g[2][None], g[3][None], g[4][None])
        f = _ffn(hn2, ffn_up, ffn_conv[i].astype(F32), ffn_down, i)
        g_next = gains[i + 1, 0] if i + 1 < depth else g[0]
        x2d, hn = _residual(x3.reshape(n, d), f.reshape(n, d), g[5][None], g_next[None])
    return x2d.reshape(b, l, d)
```

```python
import functools
import math

import jax
import jax.numpy as jnp
from jax import lax
from jax.experimental import pallas as pl
from jax.experimental.pallas import tpu as pltpu

D_MODEL = 1024
SEQ = 2048
N_MEM = 256
GRID_W = 64
HY_W = 512
NA_HEADS = 8
NA_HEAD_DIM = 64
NA_W = NA_HEADS * NA_HEAD_DIM
NA_WIN_ROWS = 8
NA_WIN_COLS = 16
SC_W = 512
XA_HEADS = 4
XA_HEAD_DIM = D_MODEL // XA_HEADS
D_FF = 2816
HY_ORDER = 2
HY_EMB = 33
HY_HIDDEN = 64
HY_FAST_DECAY = 0.3
HY_SLOW_DECAY = 1.5
HY_TARGET = 1e-2
N_BRANCH = 3
EPS = 1e-6

FFT_N = 2 * SEQ
NA_ROWS = SEQ // GRID_W
NA_KEYS = NA_WIN_ROWS * GRID_W
NA_GROUP = 4
COL_TILE = 256
ROW_TILE = 512
CONV_PAD = 8
DFT_SPLIT = 64
FFN_CHUNK = 512
FFN_HALO = 16
VMEM_LIMIT = 60 * 1024 * 1024

BF16 = jnp.bfloat16
F32 = jnp.float32


def _params(*sem):
    return pltpu.CompilerParams(dimension_semantics=sem, vmem_limit_bytes=VMEM_LIMIT)


def _rms(xf, g):
    ms = jnp.mean(xf * xf, axis=-1, keepdims=True)
    return xf * lax.rsqrt(ms + EPS) * g


def _dwconv3(u, w_ref):
    n = u.shape[0]
    zeros = jnp.zeros((CONV_PAD, u.shape[1]), F32)
    padded = jnp.concatenate([zeros, u, zeros], axis=0)
    m = n + 2 * CONV_PAD
    prev = pltpu.roll(padded, 1, 0)[CONV_PAD:CONV_PAD + n]
    nxt = pltpu.roll(padded, m - 1, 0)[CONV_PAD:CONV_PAD + n]
    return prev * w_ref[0:1, :] + u * w_ref[1:2, :] + nxt * w_ref[2:3, :]


def _prenorm_kernel(x_ref, g_ref, o_ref):
    o_ref[...] = _rms(x_ref[...], g_ref[...]).astype(BF16)


def _prenorm(x2d, g):
    n = x2d.shape[0]
    tm = 1024
    return pl.pallas_call(
        _prenorm_kernel,
        grid=(n // tm,),
        in_specs=[pl.BlockSpec((tm, D_MODEL), lambda i: (i, 0)),
                  pl.BlockSpec((1, D_MODEL), lambda i: (0, 0))],
        out_specs=pl.BlockSpec((tm, D_MODEL), lambda i: (i, 0)),
        out_shape=jax.ShapeDtypeStruct((n, D_MODEL), BF16),
        compiler_params=_params("arbitrary"),
        name="prenorm",
    )(x2d, g)


def _filter_mlp_kernel(z_ref, w1_ref, b1_ref, w2_ref, b2_ref, w3_ref, f_ref, t_ref, dl_ref,
                       hs_ref, hd_ref):
    hp = lax.Precision.HIGHEST
    h = jnp.sin(f_ref[0:1, :] * (jnp.dot(z_ref[...], w1_ref[...], precision=hp) + b1_ref[...]))
    h = jnp.sin(f_ref[1:2, :] * (jnp.dot(h, w2_ref[...], precision=hp) + b2_ref[...]))
    decay = jnp.exp(-t_ref[...] * dl_ref[...])
    row = lax.broadcasted_iota(jnp.int32, (SEQ, HY_W), 0)
    for o in range(HY_ORDER):
        c_f = o * HY_W
        c_b = HY_ORDER * HY_W + o * HY_W
        hf = jnp.dot(h, w3_ref[:, c_f:c_f + HY_W], precision=hp) * decay
        hb = jnp.dot(h, w3_ref[:, c_b:c_b + HY_W], precision=hp) * decay
        hb = jnp.where(row == 0, 0.0, hb)
        hs_ref[:, c_f:c_f + HY_W] = (hf + hb).astype(BF16)
        hd_ref[:, c_f:c_f + HY_W] = (hb - hf).astype(BF16)


def _filter_dft_kernel(c_ref, s_ref, hs_ref, hd_ref, kr_ref, ki_ref, kn_ref):
    row = lax.broadcasted_iota(jnp.int32, (SEQ, COL_TILE), 0)
    wk = jnp.where(row == 0, 1.0 / FFT_N, 2.0 / FFT_N)
    hs = hs_ref[...]
    kr_ref[...] = jnp.dot(c_ref[...], hs, preferred_element_type=F32) * wk
    ki_ref[...] = jnp.dot(s_ref[...], hd_ref[...], preferred_element_type=F32) * wk
    sign = jnp.where((row & 1) == 0, 1.0, -1.0)
    kn_ref[...] = jnp.sum(hs.astype(F32) * sign, axis=0, keepdims=True) * (1.0 / FFT_N)


def _hyena_filters(dft_c, dft_s, w1, b1, w2, b2, w3, freq):
    t = jnp.linspace(0.0, 1.0, SEQ, dtype=F32)[:, None]
    bands = (HY_EMB - 1) // 2
    w = 2.0 * math.pi * jnp.arange(SEQ, dtype=F32)[:, None] / SEQ
    f = jnp.linspace(1e-4, bands - 1, bands, dtype=F32)[None, :]
    z = jnp.concatenate([t, jnp.cos(f * w), -jnp.sin(f * w)], axis=-1)
    z = jnp.pad(z, ((0, 0), (0, HY_HIDDEN - HY_EMB)))
    w1p = jnp.pad(w1.astype(F32), ((0, HY_HIDDEN - HY_EMB), (0, 0)))
    deltas = jnp.abs(jnp.linspace(math.log(HY_TARGET) / HY_SLOW_DECAY,
                                  math.log(HY_TARGET) / HY_FAST_DECAY, HY_W, dtype=F32))[None, :]
    width = HY_ORDER * HY_W
    hs, hd = pl.pallas_call(
        _filter_mlp_kernel,
        out_shape=(jax.ShapeDtypeStruct((SEQ, width), BF16),
                   jax.ShapeDtypeStruct((SEQ, width), BF16)),
        compiler_params=pltpu.CompilerParams(vmem_limit_bytes=VMEM_LIMIT),
        name="hyena_filter_mlp",
    )(z, w1p, b1[None].astype(F32), w2.astype(F32), b2[None].astype(F32), w3.astype(F32),
      freq.astype(F32), t, deltas)
    nt = width // COL_TILE
    full = pl.BlockSpec((SEQ, SEQ), lambda j: (0, 0))
    col = pl.BlockSpec((SEQ, COL_TILE), lambda j: (0, j))
    return pl.pallas_call(
        _filter_dft_kernel,
        grid=(nt,),
        in_specs=[full, full, col, col],
        out_specs=(col, col, pl.BlockSpec((1, COL_TILE), lambda j: (0, j))),
        out_shape=(jax.ShapeDtypeStruct((SEQ, width), F32),
                   jax.ShapeDtypeStruct((SEQ, width), F32),
                   jax.ShapeDtypeStruct((1, width), F32)),
        compiler_params=_params("arbitrary"),
        name="hyena_filter_dft",
    )(dft_c, dft_s, hs, hd)


def _hyena_kernel(hn_ref, wv_ref, w1_ref, w2_ref, sv_ref, s1_ref, s2_ref, c_ref, s_ref,
                  kr0_ref, ki0_ref, kn0_ref, kr1_ref, ki1_ref, kn1_ref, bias_ref,
                  o_ref, z32_ref, zb_ref, yr_ref, yi_ref, x1_ref, x2_ref):
    hn = hn_ref[...]
    z32_ref[...] = _dwconv3(jnp.dot(hn, wv_ref[...], preferred_element_type=F32), sv_ref)
    x1_ref[...] = _dwconv3(jnp.dot(hn, w1_ref[...], preferred_element_type=F32), s1_ref)
    x2_ref[...] = _dwconv3(jnp.dot(hn, w2_ref[...], preferred_element_type=F32), s2_ref)
    row = lax.broadcasted_iota(jnp.int32, (SEQ, COL_TILE), 0)
    sign = jnp.where((row & 1) == 0, 1.0, -1.0)

    def long_conv(kr_ref, ki_ref, kn_ref, o):
        z = z32_ref[...]
        zb_ref[...] = z.astype(BF16)
        zb = zb_ref[...]
        vr = jnp.dot(c_ref[...], zb, preferred_element_type=F32)
        va = jnp.dot(s_ref[...], zb, preferred_element_type=F32)
        kr = kr_ref[...]
        ki = ki_ref[...]
        yr_ref[...] = (vr * kr + va * ki).astype(BF16)
        yi_ref[...] = (vr * ki - va * kr).astype(BF16)
        vn = jnp.sum(zb.astype(F32) * sign, axis=0, keepdims=True)
        y = (jnp.dot(c_ref[...], yr_ref[...], preferred_element_type=F32)
             - jnp.dot(s_ref[...], yi_ref[...], preferred_element_type=F32)
             + sign * (vn * kn_ref[...]))
        return y + z * bias_ref[o:o + 1, :]

    z32_ref[...] = x1_ref[...] * long_conv(kr0_ref, ki0_ref, kn0_ref, 0)
    o_ref[...] = (x2_ref[...] * long_conv(kr1_ref, ki1_ref, kn1_ref, 1)).astype(BF16)


def _hyena_mixer(hn, w_in, short_w, dft_c, dft_s, kr, ki, kn, bias):
    b = hn.shape[0]
    nt = HY_W // COL_TILE
    once = pl.Buffered(1)

    def wcol(k):
        return pl.BlockSpec((D_MODEL, COL_TILE), lambda j, i, k=k: (0, k * nt + j))

    def scol(k):
        return pl.BlockSpec((3, COL_TILE), lambda j, i, k=k: (0, k * nt + j))

    def kcol(o, rows):
        return pl.BlockSpec((rows, COL_TILE), lambda j, i, o=o: (0, o * nt + j), pipeline_mode=once)

    full = pl.BlockSpec((SEQ, SEQ), lambda j, i: (0, 0), pipeline_mode=once)
    return pl.pallas_call(
        _hyena_kernel,
        grid=(nt, b),
        in_specs=[pl.BlockSpec((None, SEQ, D_MODEL), lambda j, i: (i, 0, 0)),
                  wcol(0), wcol(1), wcol(2), scol(0), scol(1), scol(2), full, full,
                  kcol(0, SEQ), kcol(0, SEQ), kcol(0, 1), kcol(1, SEQ), kcol(1, SEQ), kcol(1, 1),
                  pl.BlockSpec((HY_ORDER, COL_TILE), lambda j, i: (0, j))],
        out_specs=pl.BlockSpec((None, SEQ, COL_TILE), lambda j, i: (i, 0, j)),
        out_shape=jax.ShapeDtypeStruct((b, SEQ, HY_W), BF16),
        scratch_shapes=[pltpu.VMEM((SEQ, COL_TILE), F32), pltpu.VMEM((SEQ, COL_TILE), BF16),
                        pltpu.VMEM((SEQ, COL_TILE), BF16), pltpu.VMEM((SEQ, COL_TILE), BF16),
                        pltpu.VMEM((SEQ, COL_TILE), F32), pltpu.VMEM((SEQ, COL_TILE), F32)],
        compiler_params=_params("arbitrary", "arbitrary"),
        name="hyena_mixer",
    )(hn, w_in, w_in, w_in, short_w, short_w, short_w, dft_c, dft_s,
      kr, ki, kn, kr, ki, kn, bias)


def _shortconv_kernel(hn_ref, wb_ref, wc_ref, wx_ref, cw_ref, o_ref):
    hn = hn_ref[...]
    bg = jnp.dot(hn, wb_ref[...].astype(BF16), preferred_element_type=F32)
    cg = jnp.dot(hn, wc_ref[...].astype(BF16), preferred_element_type=F32)
    xi = jnp.dot(hn, wx_ref[...].astype(BF16), preferred_element_type=F32)
    o_ref[...] = (bg * _dwconv3(cg * xi, cw_ref)).astype(BF16)


def _w_in_cols(layer, width, first):
    return lambda k, nt: pl.BlockSpec((None, D_MODEL, width),
                                      lambda j, i: (layer, 0, first + k * nt + j))


def _shortconv_mixer(hn, w_in, layer, conv_w):
    b = hn.shape[0]
    nt = SC_W // COL_TILE
    wcol = _w_in_cols(layer, COL_TILE, (3 * HY_W + 3 * NA_W) // COL_TILE)
    return pl.pallas_call(
        _shortconv_kernel,
        grid=(nt, b),
        in_specs=[pl.BlockSpec((None, SEQ, D_MODEL), lambda j, i: (i, 0, 0)),
                  wcol(0, nt), wcol(1, nt), wcol(2, nt),
                  pl.BlockSpec((3, COL_TILE), lambda j, i: (0, j))],
        out_specs=pl.BlockSpec((None, SEQ, COL_TILE), lambda j, i: (i, 0, j)),
        out_shape=jax.ShapeDtypeStruct((b, SEQ, SC_W), BF16),
        compiler_params=_params("arbitrary", "arbitrary"),
        name="shortconv_mixer",
    )(hn, w_in, w_in, w_in, conv_w)


def _na_kernel(hn_ref, wq_ref, wk_ref, wv_ref, bias_ref, o_ref, q_ref, k_ref, v_ref):
    hn = hn_ref[...]
    q = jnp.dot(hn, wq_ref[...].astype(BF16), preferred_element_type=F32)
    q_ref[...] = (q * (NA_HEAD_DIM ** -0.5)).astype(BF16)
    k_ref[...] = jnp.dot(hn, wk_ref[...].astype(BF16), preferred_element_type=F32).astype(BF16)
    v_ref[...] = jnp.dot(hn, wv_ref[...].astype(BF16), preferred_element_type=F32).astype(BF16)
    gw = NA_GROUP * NA_HEAD_DIM
    same_head = (lax.broadcasted_iota(jnp.int32, (gw, gw), 0) // NA_HEAD_DIM
                 == lax.broadcasted_iota(jnp.int32, (gw, gw), 1) // NA_HEAD_DIM)

    def row_body(r, carry):
        w0 = jnp.clip(r - NA_WIN_ROWS // 2, 0, NA_ROWS - NA_WIN_ROWS)
        off = w0 - r + (NA_WIN_ROWS - 1)
        q0 = pl.multiple_of(r * GRID_W, GRID_W)
        k0 = pl.multiple_of(w0 * GRID_W, GRID_W)
        q_row = q_ref[pl.ds(q0, GRID_W), :]
        q_heads = jnp.where(same_head, jnp.concatenate([q_row] * NA_GROUP, axis=0), 0)
        s = lax.dot_general(q_heads, k_ref[pl.ds(k0, NA_KEYS), :], (((1,), (1,)), ((), ())),
                            preferred_element_type=F32)
        bias = jnp.concatenate(
            [jnp.concatenate([bias_ref[h, off + 2 * m] for m in range(NA_WIN_ROWS // 2)], axis=1)
             for h in range(NA_GROUP)], axis=0)
        s = s + bias
        p = jnp.exp(s - jnp.max(s, axis=-1, keepdims=True))
        inv = 1.0 / jnp.sum(p, axis=-1, keepdims=True)
        pv = jnp.dot(p.astype(BF16), v_ref[pl.ds(k0, NA_KEYS), :], preferred_element_type=F32)
        pv = jnp.where(same_head, pv * inv, 0.0)
        out = pv[0:GRID_W]
        for h in range(1, NA_GROUP):
            out = out + pv[h * GRID_W:(h + 1) * GRID_W]
        o_ref[pl.ds(q0, GRID_W), :] = out.astype(BF16)
        return carry

    lax.fori_loop(0, NA_ROWS, row_body, 0, unroll=4)


def _na_bias(rpb):
    c = jnp.arange(GRID_W)
    col_start = jnp.clip(c - NA_WIN_COLS // 2, 0, GRID_W - NA_WIN_COLS)
    col_mask = (c[None, :] >= col_start[:, None]) & (c[None, :] < col_start[:, None] + NA_WIN_COLS)
    dc = jnp.clip(c[None, :] - c[:, None] + NA_WIN_COLS - 1, 0, 2 * NA_WIN_COLS - 2)
    pick = (dc[None] == jnp.arange(2 * NA_WIN_COLS - 1)[:, None, None]).astype(F32)
    table = jnp.einsum("hrd,dqc->hrqc", rpb.astype(F32), pick, precision=lax.Precision.HIGHEST)
    table = table + jnp.where(col_mask, 0.0, -1e30)[None, None]
    return jnp.concatenate([table[:, :-1], table[:, 1:]], axis=-1)


def _na_mixer(hn, w_in, layer, bias):
    b = hn.shape[0]
    gw = NA_GROUP * NA_HEAD_DIM
    ng = NA_W // gw
    wcol = _w_in_cols(layer, gw, 3 * HY_W // gw)
    return pl.pallas_call(
        _na_kernel,
        grid=(ng, b),
        in_specs=[pl.BlockSpec((None, SEQ, D_MODEL), lambda j, i: (i, 0, 0)),
                  wcol(0, ng), wcol(1, ng), wcol(2, ng),
                  pl.BlockSpec((NA_GROUP, 2 * NA_WIN_ROWS - 2, GRID_W, 2 * GRID_W),
                               lambda j, i: (j, 0, 0, 0))],
        out_specs=pl.BlockSpec((None, SEQ, gw), lambda j, i: (i, 0, j)),
        out_shape=jax.ShapeDtypeStruct((b, SEQ, NA_W), BF16),
        scratch_shapes=[pltpu.VMEM((SEQ, gw), BF16) for _ in range(3)],
        compiler_params=_params("arbitrary", "arbitrary"),
        name="na_mixer",
    )(hn, w_in, w_in, w_in, bias)


def _merge_kernel(hn_ref, x_ref, ya_ref, yb_ref, yc_ref, wg_ref, gb_ref, wb_ref, wo_ref, g_ref, o_ref):
    hn = hn_ref[...]
    merged = None
    for i, y_ref in enumerate((ya_ref, yb_ref, yc_ref)):
        pre = jnp.dot(hn, wg_ref[:, i * D_MODEL:(i + 1) * D_MODEL], preferred_element_type=F32)
        gate = jax.nn.sigmoid(pre + gb_ref[i:i + 1, :])
        term = gate * jnp.dot(y_ref[...], wb_ref[i], preferred_element_type=F32)
        merged = term if merged is None else merged + term
    out = jnp.dot(merged.astype(BF16), wo_ref[...], preferred_element_type=F32)
    o_ref[...] = x_ref[...] + _rms(out, g_ref[...])


def _merge(hn2d, x2d, ya, yb, yc, w_gate, gate_bias, w_branch, w_out, g):
    n = x2d.shape[0]
    tm = ROW_TILE
    once = pl.Buffered(1)
    rows = lambda w: pl.BlockSpec((tm, w), lambda i: (i, 0))
    return pl.pallas_call(
        _merge_kernel,
        grid=(n // tm,),
        in_specs=[rows(D_MODEL), rows(D_MODEL), rows(HY_W), rows(NA_W), rows(SC_W),
                  pl.BlockSpec((D_MODEL, N_BRANCH * D_MODEL), lambda i: (0, 0), pipeline_mode=once),
                  pl.BlockSpec((N_BRANCH, D_MODEL), lambda i: (0, 0)),
                  pl.BlockSpec((N_BRANCH, HY_W, D_MODEL), lambda i: (0, 0, 0), pipeline_mode=once),
                  pl.BlockSpec((D_MODEL, D_MODEL), lambda i: (0, 0), pipeline_mode=once),
                  pl.BlockSpec((1, D_MODEL), lambda i: (0, 0))],
        out_specs=rows(D_MODEL),
        out_shape=jax.ShapeDtypeStruct((n, D_MODEL), F32),
        compiler_params=_params("arbitrary"),
        name="merge",
    )(hn2d, x2d, ya, yb, yc, w_gate, gate_bias, w_branch, w_out, g)


def _kv_kernel(m_ref, g_ref, w_ref, o_ref):
    mn = _rms(m_ref[...], g_ref[...]).astype(BF16)
    o_ref[...] = jnp.dot(mn, w_ref[...], preferred_element_type=F32).astype(BF16)


def _mem_kv(mem, g, wkv):
    b = mem.shape[0]
    return pl.pallas_call(
        _kv_kernel,
        grid=(b,),
        in_specs=[pl.BlockSpec((None, N_MEM, D_MODEL), lambda i: (i, 0, 0)),
                  pl.BlockSpec((1, D_MODEL), lambda i: (0, 0)),
                  pl.BlockSpec((D_MODEL, 2 * D_MODEL), lambda i: (0, 0))],
        out_specs=pl.BlockSpec((None, N_MEM, 2 * D_MODEL), lambda i: (i, 0, 0)),
        out_shape=jax.ShapeDtypeStruct((b, N_MEM, 2 * D_MODEL), BF16),
        compiler_params=_params("arbitrary"),
        name="mem_kv",
    )(mem, g, wkv)


def _xattn_kernel(x_ref, kv_ref, wq_ref, wo_ref, gq_ref, go_ref, gn_ref, o_ref, hn_ref):
    x = x_ref[...]
    h = _rms(x, gq_ref[...]).astype(BF16)
    q = (jnp.dot(h, wq_ref[...], preferred_element_type=F32) * (XA_HEAD_DIM ** -0.5)).astype(BF16)
    heads = []
    for i in range(XA_HEADS):
        sl = slice(i * XA_HEAD_DIM, (i + 1) * XA_HEAD_DIM)
        km = kv_ref[:, sl]
        vm = kv_ref[:, D_MODEL + i * XA_HEAD_DIM:D_MODEL + (i + 1) * XA_HEAD_DIM]
        s = lax.dot_general(q[:, sl], km, (((1,), (1,)), ((), ())), preferred_element_type=F32)
        p = jnp.exp(s - jnp.max(s, axis=-1, keepdims=True))
        den = jnp.sum(p, axis=-1, keepdims=True)
        heads.append((jnp.dot(p.astype(BF16), vm, preferred_element_type=F32) / den).astype(BF16))
    o = jnp.dot(jnp.concatenate(heads, axis=-1), wo_ref[...], preferred_element_type=F32)
    xn = x + _rms(o, go_ref[...])
    o_ref[...] = xn
    hn_ref[...] = _rms(xn, gn_ref[...]).astype(BF16)


def _xattn(x, kv, wq, wo, gq, go, gn):
    b = x.shape[0]
    tm = ROW_TILE
    once = pl.Buffered(1)
    rows = pl.BlockSpec((None, tm, D_MODEL), lambda i, j: (i, j, 0))
    gain = pl.BlockSpec((1, D_MODEL), lambda i, j: (0, 0))
    wfull = pl.BlockSpec((D_MODEL, D_MODEL), lambda i, j: (0, 0), pipeline_mode=once)
    return pl.pallas_call(
        _xattn_kernel,
        grid=(b, SEQ // tm),
        in_specs=[rows, pl.BlockSpec((None, N_MEM, 2 * D_MODEL), lambda i, j: (i, 0, 0)),
                  wfull, wfull, gain, gain, gain],
        out_specs=(rows, rows),
        out_shape=(jax.ShapeDtypeStruct((b, SEQ, D_MODEL), F32),
                   jax.ShapeDtypeStruct((b, SEQ, D_MODEL), BF16)),
        compiler_params=_params("arbitrary", "arbitrary"),
        name="xattn",
    )(x, kv, wq, wo, gq, go, gn)


def _gelu_tanh(x):
    c = math.sqrt(2.0 / math.pi)
    half = 0.5 * x
    return half + half * jnp.tanh(x * (c + (c * 0.044715) * (x * x)))


def _ffn_kernel(hn_ref, wg_ref, wv_ref, cg_ref, cv_ref, wd_ref, o_ref):
    wg = wg_ref[...].astype(BF16)
    wv = wv_ref[...].astype(BF16)
    wd = wd_ref[...].astype(BF16)

    @pl.when(pl.program_id(1) == 0)
    def _():
        o_ref[...] = jnp.zeros_like(o_ref)

    for r0 in range(0, SEQ, FFN_CHUNK):
        lo = max(r0 - FFN_HALO, 0)
        hi = min(r0 + FFN_CHUNK + FFN_HALO, SEQ)
        hn = hn_ref[lo:hi, :]
        ug = _dwconv3(jnp.dot(hn, wg, preferred_element_type=F32), cg_ref)
        uv = _dwconv3(jnp.dot(hn, wv, preferred_element_type=F32), cv_ref)
        act = (_gelu_tanh(ug) * uv)[r0 - lo:r0 - lo + FFN_CHUNK].astype(BF16)
        o_ref[r0:r0 + FFN_CHUNK, :] += jnp.dot(act, wd, preferred_element_type=F32)


def _ffn(hn, w_up, w_conv, w_down, layer):
    b = hn.shape[0]
    nk = D_FF // COL_TILE
    return pl.pallas_call(
        _ffn_kernel,
        grid=(b, nk),
        in_specs=[pl.BlockSpec((None, SEQ, D_MODEL), lambda i, k: (i, 0, 0)),
                  pl.BlockSpec((None, D_MODEL, COL_TILE), lambda i, k: (layer, 0, k)),
                  pl.BlockSpec((None, D_MODEL, COL_TILE), lambda i, k: (layer, 0, nk + k)),
                  pl.BlockSpec((3, COL_TILE), lambda i, k: (0, k)),
                  pl.BlockSpec((3, COL_TILE), lambda i, k: (0, nk + k)),
                  pl.BlockSpec((None, COL_TILE, D_MODEL), lambda i, k: (layer, k, 0))],
        out_specs=pl.BlockSpec((None, SEQ, D_MODEL), lambda i, k: (i, 0, 0)),
        out_shape=jax.ShapeDtypeStruct((b, SEQ, D_MODEL), F32),
        compiler_params=_params("arbitrary", "arbitrary"),
        name="ffn",
    )(hn, w_up, w_up, w_conv, w_conv, w_down)


def _residual_kernel(x_ref, f_ref, g_ref, gn_ref, o_ref, hn_ref):
    xn = x_ref[...] + _rms(f_ref[...], g_ref[...])
    o_ref[...] = xn
    hn_ref[...] = _rms(xn, gn_ref[...]).astype(BF16)


def _residual_last_kernel(x_ref, f_ref, g_ref, o_ref):
    o_ref[...] = x_ref[...] + _rms(f_ref[...], g_ref[...])


def _residual(x2d, f2d, g, gn):
    n = x2d.shape[0]
    tm = 1024
    rows = pl.BlockSpec((tm, D_MODEL), lambda i: (i, 0))
    gain = pl.BlockSpec((1, D_MODEL), lambda i: (0, 0))
    x_shape = jax.ShapeDtypeStruct((n, D_MODEL), F32)
    if gn is None:
        return pl.pallas_call(
            _residual_last_kernel,
            grid=(n // tm,),
            in_specs=[rows, rows, gain],
            out_specs=rows,
            out_shape=x_shape,
            compiler_params=_params("arbitrary"),
            name="residual_last",
        )(x2d, f2d, g), None
    return pl.pallas_call(
        _residual_kernel,
        grid=(n // tm,),
        in_specs=[rows, rows, gain, gain],
        out_specs=(rows, rows),
        out_shape=(x_shape, jax.ShapeDtypeStruct((n, D_MODEL), BF16)),
        compiler_params=_params("arbitrary"),
        name="residual",
    )(x2d, f2d, g, gn)


def _dft_matrices():
    n = jnp.arange(SEQ, dtype=jnp.int32)[None, :]
    a = jnp.arange(SEQ // DFT_SPLIT, dtype=jnp.int32)[:, None]
    b = jnp.arange(DFT_SPLIT, dtype=jnp.int32)[:, None]
    ang_a = ((DFT_SPLIT * a * n) % FFT_N).astype(F32) * (2.0 * math.pi / FFT_N)
    ang_b = ((b * n) % FFT_N).astype(F32) * (2.0 * math.pi / FFT_N)
    ca, sa = jnp.cos(ang_a)[:, None, :], jnp.sin(ang_a)[:, None, :]
    cb, sb = jnp.cos(ang_b)[None, :, :], jnp.sin(ang_b)[None, :, :]
    cos = (ca * cb - sa * sb).reshape(SEQ, SEQ)
    sin = (sa * cb + ca * sb).reshape(SEQ, SEQ)
    return cos.astype(BF16), sin.astype(BF16)


def kernel(x, mem, norm_gains, mem_norm, w_in, gate_bias, hy_short_w, hy_w1, hy_b1, hy_w2, hy_b2,
           hy_w3, hy_freq, hy_bias, na_rpb, sc_conv_w, w_branch, w_out, xa_wq, xa_wkv, xa_wo,
           ffn_up, ffn_conv, ffn_down):
    b, l, d = x.shape
    depth = w_in.shape[0]
    assert (l, d) == (SEQ, D_MODEL) and mem.shape[1:] == (N_MEM, D_MODEL)
    n = b * l
    dft_c, dft_s = _dft_matrices()
    gains = norm_gains.astype(F32)
    x2d = x.reshape(n, d)
    hn = _prenorm(x2d, gains[0, 0][None])
    for i in range(depth):
        g = gains[i]
        w_hyena = w_in[i, :, :3 * HY_W].astype(BF16)
        w_gate = w_in[i, :, 3 * HY_W + 3 * NA_W + 3 * SC_W:].astype(BF16)
        kr, ki, kn = _hyena_filters(dft_c, dft_s, hy_w1[i], hy_b1[i], hy_w2[i], hy_b2[i], hy_w3[i],
                                    hy_freq[i])
        hn3 = hn.reshape(b, l, d)
        ya = _hyena_mixer(hn3, w_hyena, hy_short_w[i].astype(F32), dft_c, dft_s, kr, ki, kn,
                          hy_bias[i].astype(F32))
        yb = _na_mixer(hn3, w_in, i, _na_bias(na_rpb[i]))
        yc = _shortconv_mixer(hn3, w_in, i, sc_conv_w[i].astype(F32))
        x2d = _merge(hn, x2d, ya.reshape(n, HY_W), yb.reshape(n, NA_W), yc.reshape(n, SC_W), w_gate,
                     gate_bias[i].astype(F32), w_branch[i].astype(BF16), w_out[i].astype(BF16), g[1][None])
        kv = _mem_kv(mem, mem_norm[i].astype(F32)[None], xa_wkv[i].astype(BF16))
        x3, hn2 = _xattn(x2d.reshape(b, l, d), kv, xa_wq[i].astype(BF16), xa_wo[i].astype(BF16),
                         g[2][None], g[3][None], g[4][None])
        f = _ffn(hn2, ffn_up, ffn_conv[i].astype(F32), ffn_down, i)
        g_next = gains[i + 1, 0][None] if i + 1 < depth else None
        x2d, hn = _residual(x3.reshape(n, d), f.reshape(n, d), g[5][None], g_next)
    return x2d.reshape(b, l, d)
```

```python
import functools
import math

import jax
import jax.numpy as jnp
from jax import lax
from jax.experimental import pallas as pl
from jax.experimental.pallas import tpu as pltpu

D_MODEL = 1024
SEQ = 2048
N_MEM = 256
GRID_W = 64
HY_W = 512
NA_HEADS = 8
NA_HEAD_DIM = 64
NA_W = NA_HEADS * NA_HEAD_DIM
NA_WIN_ROWS = 8
NA_WIN_COLS = 16
SC_W = 512
XA_HEADS = 4
XA_HEAD_DIM = D_MODEL // XA_HEADS
D_FF = 2816
HY_ORDER = 2
HY_EMB = 33
HY_HIDDEN = 64
HY_FAST_DECAY = 0.3
HY_SLOW_DECAY = 1.5
HY_TARGET = 1e-2
N_BRANCH = 3
EPS = 1e-6

FFT_N = 2 * SEQ
FFT_RADIX = 4
FFT_SUB = FFT_N // FFT_RADIX
NA_ROWS = SEQ // GRID_W
NA_KEYS = NA_WIN_ROWS * GRID_W
NA_GROUP = 4
COL_TILE = 256
ROW_TILE = 512
MERGE_ROWS = 2 * ROW_TILE
CONV_PAD = 8
DFT_SPLIT = 64
FFN_CHUNK = 512
FFN_HALO = 16
VMEM_LIMIT = 60 * 1024 * 1024

BF16 = jnp.bfloat16
F32 = jnp.float32


def _params(*sem):
    return pltpu.CompilerParams(dimension_semantics=sem, vmem_limit_bytes=VMEM_LIMIT)


def _rms(xf, g):
    ms = jnp.mean(xf * xf, axis=-1, keepdims=True)
    return xf * lax.rsqrt(ms + EPS) * g


def _dwconv3(u, w_ref):
    n = u.shape[0]
    zeros = jnp.zeros((CONV_PAD, u.shape[1]), F32)
    padded = jnp.concatenate([zeros, u, zeros], axis=0)
    m = n + 2 * CONV_PAD
    prev = pltpu.roll(padded, 1, 0)[CONV_PAD:CONV_PAD + n]
    nxt = pltpu.roll(padded, m - 1, 0)[CONV_PAD:CONV_PAD + n]
    return prev * w_ref[0:1, :] + u * w_ref[1:2, :] + nxt * w_ref[2:3, :]


def _prenorm_kernel(x_ref, g_ref, o_ref):
    o_ref[...] = _rms(x_ref[...], g_ref[...]).astype(BF16)


def _prenorm(x2d, g):
    n = x2d.shape[0]
    tm = 1024
    return pl.pallas_call(
        _prenorm_kernel,
        grid=(n // tm,),
        in_specs=[pl.BlockSpec((tm, D_MODEL), lambda i: (i, 0)),
                  pl.BlockSpec((1, D_MODEL), lambda i: (0, 0))],
        out_specs=pl.BlockSpec((tm, D_MODEL), lambda i: (i, 0)),
        out_shape=jax.ShapeDtypeStruct((n, D_MODEL), BF16),
        compiler_params=_params("arbitrary"),
        name="prenorm",
    )(x2d, g)


def _filter_mlp_kernel(z_ref, w1_ref, b1_ref, w2_ref, b2_ref, w3_ref, f_ref, t_ref, dl_ref,
                       hs_ref, hd_ref):
    hp = lax.Precision.HIGHEST
    h = jnp.sin(f_ref[0:1, :] * (jnp.dot(z_ref[...], w1_ref[...], precision=hp) + b1_ref[...]))
    h = jnp.sin(f_ref[1:2, :] * (jnp.dot(h, w2_ref[...], precision=hp) + b2_ref[...]))
    decay = jnp.exp(-t_ref[...] * dl_ref[...])
    row = lax.broadcasted_iota(jnp.int32, (SEQ, HY_W), 0)
    for o in range(HY_ORDER):
        c_f = o * HY_W
        c_b = HY_ORDER * HY_W + o * HY_W
        hf = jnp.dot(h, w3_ref[:, c_f:c_f + HY_W], precision=hp) * decay
        hb = jnp.dot(h, w3_ref[:, c_b:c_b + HY_W], precision=hp) * decay
        hb = jnp.where(row == 0, 0.0, hb)
        hs_ref[:, c_f:c_f + HY_W] = (hf + hb).astype(BF16)
        hd_ref[:, c_f:c_f + HY_W] = (hb - hf).astype(BF16)


def _filter_dft_kernel(c_ref, s_ref, hs_ref, hd_ref, kr_ref, ki_ref):
    kr_ref[...] = jnp.dot(c_ref[...], hs_ref[...], preferred_element_type=F32) * (2.0 / FFT_N)
    ki_ref[...] = jnp.dot(s_ref[...], hd_ref[...], preferred_element_type=F32) * (2.0 / FFT_N)


def _hyena_filters(dft_c, dft_s, w1, b1, w2, b2, w3, freq):
    t = jnp.linspace(0.0, 1.0, SEQ, dtype=F32)[:, None]
    bands = (HY_EMB - 1) // 2
    w = 2.0 * math.pi * jnp.arange(SEQ, dtype=F32)[:, None] / SEQ
    f = jnp.linspace(1e-4, bands - 1, bands, dtype=F32)[None, :]
    z = jnp.concatenate([t, jnp.cos(f * w), -jnp.sin(f * w)], axis=-1)
    z = jnp.pad(z, ((0, 0), (0, HY_HIDDEN - HY_EMB)))
    w1p = jnp.pad(w1.astype(F32), ((0, HY_HIDDEN - HY_EMB), (0, 0)))
    deltas = jnp.abs(jnp.linspace(math.log(HY_TARGET) / HY_SLOW_DECAY,
                                  math.log(HY_TARGET) / HY_FAST_DECAY, HY_W, dtype=F32))[None, :]
    width = HY_ORDER * HY_W
    hs, hd = pl.pallas_call(
        _filter_mlp_kernel,
        out_shape=(jax.ShapeDtypeStruct((SEQ, width), BF16),
                   jax.ShapeDtypeStruct((SEQ, width), BF16)),
        compiler_params=pltpu.CompilerParams(vmem_limit_bytes=VMEM_LIMIT),
        name="hyena_filter_mlp",
    )(z, w1p, b1[None].astype(F32), w2.astype(F32), b2[None].astype(F32), w3.astype(F32),
      freq.astype(F32), t, deltas)
    nt = width // COL_TILE
    full = pl.BlockSpec((SEQ, SEQ), lambda j: (0, 0))
    col = pl.BlockSpec((SEQ, COL_TILE), lambda j: (0, j))
    kr, ki = pl.pallas_call(
        _filter_dft_kernel,
        grid=(nt,),
        in_specs=[full, full, col, col],
        out_specs=(col, col),
        out_shape=(jax.ShapeDtypeStruct((SEQ, width), F32),
                   jax.ShapeDtypeStruct((SEQ, width), F32)),
        compiler_params=_params("arbitrary"),
        name="hyena_filter_dft",
    )(dft_c, dft_s, hs, hd)

    def families(k):
        q = FFT_SUB // 2
        return jnp.stack([k[0:q], k[2 * q:3 * q], k[q:2 * q][::-1], k[3 * q:4 * q][::-1]])

    return families(kr), families(ki)


def _hyena_kernel(hn_ref, wv_ref, w1_ref, w2_ref, sv_ref, s1_ref, s2_ref, cf_ref, sf_ref, ct_ref,
                  st_ref, tw_ref, kr0_ref, ki0_ref, kr1_ref, ki1_ref, bias_ref, o_ref):
    radix = FFT_RADIX
    zero_row = jnp.zeros((CONV_PAD, COL_TILE), F32)

    def project(w_ref):
        w = w_ref[...]
        return [jnp.dot(hn_ref[:, r * D_MODEL:(r + 1) * D_MODEL], w, preferred_element_type=F32)
                for r in range(radix)]

    def short_conv(u, w_ref):
        n = u[0].shape[0]
        prev_wrap = pltpu.roll(jnp.concatenate([u[-1], zero_row], axis=0), 1, 0)[:n]
        next_wrap = pltpu.roll(jnp.concatenate([zero_row, u[0]], axis=0), n + CONV_PAD - 1, 0)[CONV_PAD:]
        prev = [prev_wrap] + u[:-1]
        nxt = u[1:] + [next_wrap]
        return [prev[r] * w_ref[0:1, :] + u[r] * w_ref[1:2, :] + nxt[r] * w_ref[2:3, :]
                for r in range(radix)]

    def cmul(ar, ai, br, bi):
        return ar * br - ai * bi, ar * bi + ai * br

    def long_conv(x, kr_ref, ki_ref):
        cf, sf = cf_ref[...], sf_ref[...]
        xb = [v.astype(BF16) for v in x]
        tr = [jnp.dot(cf, v, preferred_element_type=F32) for v in xb]
        ti = [-jnp.dot(sf, v, preferred_element_type=F32) for v in xb]
        for r in range(1, radix):
            tr[r], ti[r] = cmul(tr[r], ti[r], tw_ref[r - 1], -tw_ref[radix - 2 + r])
        ar, ai = tr[0] + tr[2], ti[0] + ti[2]
        br, bi = tr[0] - tr[2], ti[0] - ti[2]
        cr, ci = tr[1] + tr[3], ti[1] + ti[3]
        dr, di = tr[1] - tr[3], ti[1] - ti[3]
        fam = [(ar + cr, ai + ci), (br + di, bi - dr), (br - di, -bi - dr), (ar - cr, ci - ai)]
        y = [cmul(fr, fi, kr_ref[f], ki_ref[f]) for f, (fr, fi) in enumerate(fam)]
        er, ei = y[0][0] + y[3][0], y[0][1] - y[3][1]
        fr, fi = y[0][0] - y[3][0], y[0][1] + y[3][1]
        gr, gi = y[1][0] + y[2][0], y[1][1] - y[2][1]
        hr, hi = y[1][0] - y[2][0], y[1][1] + y[2][1]
        p = [(er + gr, ei + gi), (fr - hi, fi + hr), (er - gr, ei - gi), (fr + hi, fi - hr)]
        out = []
        for r in range(radix):
            pr, pi_ = p[r]
            if r:
                pr, pi_ = cmul(pr, pi_, tw_ref[r - 1], tw_ref[radix - 2 + r])
            out.append(jnp.dot(ct_ref[...], pr.astype(BF16), preferred_element_type=F32)
                       - jnp.dot(st_ref[...], pi_.astype(BF16), preferred_element_type=F32))
        return out

    v = short_conv(project(wv_ref), sv_ref)
    x1 = short_conv(project(w1_ref), s1_ref)
    x2 = short_conv(project(w2_ref), s2_ref)
    y = long_conv(v, kr0_ref, ki0_ref)
    z = [x1[r] * (y[r] + v[r] * bias_ref[0:1, :]) for r in range(radix)]
    y = long_conv(z, kr1_ref, ki1_ref)
    for r in range(radix):
        out = x2[r] * (y[r] + z[r] * bias_ref[1:2, :])
        o_ref[:, r * COL_TILE:(r + 1) * COL_TILE] = out.astype(BF16)


def _hyena_mixer(hn, w_hy, short_w, tables, kr, ki, bias, tile):
    b = hn.shape[0]
    nt = HY_W // COL_TILE
    rows = SEQ // FFT_RADIX
    once = pl.Buffered(1)
    cf, sf, ct, st, tw = tables

    def fixed(shape, index):
        return pl.BlockSpec(shape, lambda i: index, pipeline_mode=once)

    def wcol(k):
        return fixed((D_MODEL, COL_TILE), (0, k * nt + tile))

    def scol(k):
        return fixed((3, COL_TILE), (0, k * nt + tile))

    def kfam(o):
        return fixed((FFT_RADIX, rows, COL_TILE), (0, 0, o * nt + tile))

    small = fixed((rows, rows), (0, 0))
    out = pl.pallas_call(
        _hyena_kernel,
        grid=(b,),
        in_specs=[pl.BlockSpec((None, rows, FFT_RADIX * D_MODEL), lambda i: (i, 0, 0)),
                  wcol(0), wcol(1), wcol(2), scol(0), scol(1), scol(2),
                  small, small, small, small,
                  fixed((2 * (FFT_RADIX - 1), rows, COL_TILE), (0, 0, 0)),
                  kfam(0), kfam(0), kfam(1), kfam(1),
                  fixed((HY_ORDER, COL_TILE), (0, tile))],
        out_specs=pl.BlockSpec((None, rows, FFT_RADIX * COL_TILE), lambda i: (i, 0, 0)),
        out_shape=jax.ShapeDtypeStruct((b, rows, FFT_RADIX * COL_TILE), BF16),
        compiler_params=_params("arbitrary"),
        name="hyena_mixer",
    )(hn.reshape(b, rows, FFT_RADIX * D_MODEL), w_hy, w_hy, w_hy, short_w, short_w, short_w,
      cf, sf, ct, st, tw, kr, ki, kr, ki, bias)
    return out.reshape(b * SEQ, COL_TILE)


def _shortconv_kernel(hn_ref, wb_ref, wc_ref, wx_ref, cw_ref, o_ref):
    hn = hn_ref[...]
    bg = jnp.dot(hn, wb_ref[...].astype(BF16), preferred_element_type=F32)
    cg = jnp.dot(hn, wc_ref[...].astype(BF16), preferred_element_type=F32)
    xi = jnp.dot(hn, wx_ref[...].astype(BF16), preferred_element_type=F32)
    o_ref[...] = (bg * _dwconv3(cg * xi, cw_ref)).astype(BF16)


def _w_in_cols(layer, width, first):
    return lambda k, nt: pl.BlockSpec((None, D_MODEL, width),
                                      lambda j, i: (layer, 0, first + k * nt + j))


def _shortconv_mixer(hn, w_in, layer, conv_w):
    b = hn.shape[0]
    nt = SC_W // COL_TILE
    wcol = _w_in_cols(layer, COL_TILE, (3 * HY_W + 3 * NA_W) // COL_TILE)
    return pl.pallas_call(
        _shortconv_kernel,
        grid=(nt, b),
        in_specs=[pl.BlockSpec((None, SEQ, D_MODEL), lambda j, i: (i, 0, 0)),
                  wcol(0, nt), wcol(1, nt), wcol(2, nt),
                  pl.BlockSpec((3, COL_TILE), lambda j, i: (0, j))],
        out_specs=pl.BlockSpec((None, SEQ, COL_TILE), lambda j, i: (i, 0, j)),
        out_shape=jax.ShapeDtypeStruct((b, SEQ, SC_W), BF16),
        compiler_params=_params("arbitrary", "arbitrary"),
        name="shortconv_mixer",
    )(hn, w_in, w_in, w_in, conv_w)


def _na_kernel(hn_ref, wq_ref, wk_ref, wv_ref, bias_ref, o_ref, q_ref, k_ref, v_ref):
    hn = hn_ref[...]
    q = jnp.dot(hn, wq_ref[...].astype(BF16), preferred_element_type=F32)
    q_ref[...] = (q * (NA_HEAD_DIM ** -0.5)).astype(BF16)
    k_ref[...] = jnp.dot(hn, wk_ref[...].astype(BF16), preferred_element_type=F32).astype(BF16)
    v_ref[...] = jnp.dot(hn, wv_ref[...].astype(BF16), preferred_element_type=F32).astype(BF16)
    gw = NA_GROUP * NA_HEAD_DIM
    same_head = (lax.broadcasted_iota(jnp.int32, (gw, gw), 0) // NA_HEAD_DIM
                 == lax.broadcasted_iota(jnp.int32, (gw, gw), 1) // NA_HEAD_DIM)

    def row_body(r, carry):
        w0 = jnp.clip(r - NA_WIN_ROWS // 2, 0, NA_ROWS - NA_WIN_ROWS)
        off = w0 - r + (NA_WIN_ROWS - 1)
        q0 = pl.multiple_of(r * GRID_W, GRID_W)
        k0 = pl.multiple_of(w0 * GRID_W, GRID_W)
        q_row = q_ref[pl.ds(q0, GRID_W), :]
        q_heads = jnp.where(same_head, jnp.concatenate([q_row] * NA_GROUP, axis=0), 0)
        s = lax.dot_general(q_heads, k_ref[pl.ds(k0, NA_KEYS), :], (((1,), (1,)), ((), ())),
                            preferred_element_type=F32)
        bias = jnp.concatenate(
            [jnp.concatenate([bias_ref[h, off + 2 * m] for m in range(NA_WIN_ROWS // 2)], axis=1)
             for h in range(NA_GROUP)], axis=0)
        s = s + bias
        p = jnp.exp(s - jnp.max(s, axis=-1, keepdims=True))
        inv = 1.0 / jnp.sum(p, axis=-1, keepdims=True)
        pv = jnp.dot(p.astype(BF16), v_ref[pl.ds(k0, NA_KEYS), :], preferred_element_type=F32)
        pv = jnp.where(same_head, pv * inv, 0.0)
        out = pv[0:GRID_W]
        for h in range(1, NA_GROUP):
            out = out + pv[h * GRID_W:(h + 1) * GRID_W]
        o_ref[pl.ds(q0, GRID_W), :] = out.astype(BF16)
        return carry

    lax.fori_loop(0, NA_ROWS, row_body, 0, unroll=4)


def _na_bias(rpb):
    c = jnp.arange(GRID_W)
    col_start = jnp.clip(c - NA_WIN_COLS // 2, 0, GRID_W - NA_WIN_COLS)
    col_mask = (c[None, :] >= col_start[:, None]) & (c[None, :] < col_start[:, None] + NA_WIN_COLS)
    dc = jnp.clip(c[None, :] - c[:, None] + NA_WIN_COLS - 1, 0, 2 * NA_WIN_COLS - 2)
    pick = (dc[None] == jnp.arange(2 * NA_WIN_COLS - 1)[:, None, None]).astype(F32)
    table = jnp.einsum("hrd,dqc->hrqc", rpb.astype(F32), pick, precision=lax.Precision.HIGHEST)
    table = table + jnp.where(col_mask, 0.0, -1e30)[None, None]
    return jnp.concatenate([table[:, :-1], table[:, 1:]], axis=-1)


def _na_mixer(hn, w_in, layer, bias):
    b = hn.shape[0]
    gw = NA_GROUP * NA_HEAD_DIM
    ng = NA_W // gw
    wcol = _w_in_cols(layer, gw, 3 * HY_W // gw)
    return pl.pallas_call(
        _na_kernel,
        grid=(ng, b),
        in_specs=[pl.BlockSpec((None, SEQ, D_MODEL), lambda j, i: (i, 0, 0)),
                  wcol(0, ng), wcol(1, ng), wcol(2, ng),
                  pl.BlockSpec((NA_GROUP, 2 * NA_WIN_ROWS - 2, GRID_W, 2 * GRID_W),
                               lambda j, i: (j, 0, 0, 0))],
        out_specs=pl.BlockSpec((None, SEQ, gw), lambda j, i: (i, 0, j)),
        out_shape=jax.ShapeDtypeStruct((b, SEQ, NA_W), BF16),
        scratch_shapes=[pltpu.VMEM((SEQ, gw), BF16) for _ in range(3)],
        compiler_params=_params("arbitrary", "arbitrary"),
        name="na_mixer",
    )(hn, w_in, w_in, w_in, bias)


def _merge_kernel(hn_ref, x_ref, ya0_ref, ya1_ref, yb_ref, yc_ref, wg_ref, gb_ref, wb_ref, wo_ref,
                  g_ref, o_ref):
    def gated_sum(rows):
        hn = hn_ref[rows, :]
        ya = jnp.concatenate([ya0_ref[rows, :], ya1_ref[rows, :]], axis=1)
        merged = None
        for i, y in enumerate((ya, yb_ref[rows, :], yc_ref[rows, :])):
            pre = jnp.dot(hn, wg_ref[:, i * D_MODEL:(i + 1) * D_MODEL], preferred_element_type=F32)
            gate = jax.nn.sigmoid(pre + gb_ref[i:i + 1, :])
            term = gate * jnp.dot(y, wb_ref[i], preferred_element_type=F32)
            merged = term if merged is None else merged + term
        return merged.astype(BF16)

    chunks = [slice(r, r + ROW_TILE) for r in range(0, MERGE_ROWS, ROW_TILE)]
    merged = [gated_sum(rows) for rows in chunks]
    for rows, m in zip(chunks, merged):
        out = jnp.dot(m, wo_ref[...], preferred_element_type=F32)
        o_ref[rows, :] = x_ref[rows, :] + _rms(out, g_ref[...])


def _merge(hn2d, x2d, ya0, ya1, yb, yc, w_gate, gate_bias, w_branch, w_out, g):
    n = x2d.shape[0]
    tm = MERGE_ROWS
    once = pl.Buffered(1)
    rows = lambda w: pl.BlockSpec((tm, w), lambda i: (i, 0))
    return pl.pallas_call(
        _merge_kernel,
        grid=(n // tm,),
        in_specs=[rows(D_MODEL), rows(D_MODEL), rows(COL_TILE), rows(COL_TILE), rows(NA_W), rows(SC_W),
                  pl.BlockSpec((D_MODEL, N_BRANCH * D_MODEL), lambda i: (0, 0), pipeline_mode=once),
                  pl.BlockSpec((N_BRANCH, D_MODEL), lambda i: (0, 0)),
                  pl.BlockSpec((N_BRANCH, HY_W, D_MODEL), lambda i: (0, 0, 0), pipeline_mode=once),
                  pl.BlockSpec((D_MODEL, D_MODEL), lambda i: (0, 0), pipeline_mode=once),
                  pl.BlockSpec((1, D_MODEL), lambda i: (0, 0))],
        out_specs=rows(D_MODEL),
        out_shape=jax.ShapeDtypeStruct((n, D_MODEL), F32),
        compiler_params=_params("arbitrary"),
        name="merge",
    )(hn2d, x2d, ya0, ya1, yb, yc, w_gate, gate_bias, w_branch, w_out, g)


def _kv_kernel(m_ref, g_ref, w_ref, o_ref):
    mn = _rms(m_ref[...], g_ref[...]).astype(BF16)
    o_ref[...] = jnp.dot(mn, w_ref[...], preferred_element_type=F32).astype(BF16)


def _mem_kv(mem, g, wkv):
    b = mem.shape[0]
    return pl.pallas_call(
        _kv_kernel,
        grid=(b,),
        in_specs=[pl.BlockSpec((None, N_MEM, D_MODEL), lambda i: (i, 0, 0)),
                  pl.BlockSpec((1, D_MODEL), lambda i: (0, 0)),
                  pl.BlockSpec((D_MODEL, 2 * D_MODEL), lambda i: (0, 0))],
        out_specs=pl.BlockSpec((None, N_MEM, 2 * D_MODEL), lambda i: (i, 0, 0)),
        out_shape=jax.ShapeDtypeStruct((b, N_MEM, 2 * D_MODEL), BF16),
        compiler_params=_params("arbitrary"),
        name="mem_kv",
    )(mem, g, wkv)


def _xattn_kernel(x_ref, kv_ref, wq_ref, wo_ref, gq_ref, go_ref, gn_ref, o_ref, hn_ref):
    x = x_ref[...]
    h = _rms(x, gq_ref[...]).astype(BF16)
    q = (jnp.dot(h, wq_ref[...], preferred_element_type=F32) * (XA_HEAD_DIM ** -0.5)).astype(BF16)
    heads = []
    for i in range(XA_HEADS):
        sl = slice(i * XA_HEAD_DIM, (i + 1) * XA_HEAD_DIM)
        km = kv_ref[:, sl]
        vm = kv_ref[:, D_MODEL + i * XA_HEAD_DIM:D_MODEL + (i + 1) * XA_HEAD_DIM]
        s = lax.dot_general(q[:, sl], km, (((1,), (1,)), ((), ())), preferred_element_type=F32)
        p = jnp.exp(s - jnp.max(s, axis=-1, keepdims=True))
        den = jnp.sum(p, axis=-1, keepdims=True)
        heads.append((jnp.dot(p.astype(BF16), vm, preferred_element_type=F32) / den).astype(BF16))
    o = jnp.dot(jnp.concatenate(heads, axis=-1), wo_ref[...], preferred_element_type=F32)
    xn = x + _rms(o, go_ref[...])
    o_ref[...] = xn
    hn_ref[...] = _rms(xn, gn_ref[...]).astype(BF16)


def _xattn(x, kv, wq, wo, gq, go, gn):
    b = x.shape[0]
    tm = ROW_TILE
    once = pl.Buffered(1)
    rows = pl.BlockSpec((None, tm, D_MODEL), lambda i, j: (i, j, 0))
    gain = pl.BlockSpec((1, D_MODEL), lambda i, j: (0, 0))
    wfull = pl.BlockSpec((D_MODEL, D_MODEL), lambda i, j: (0, 0), pipeline_mode=once)
    return pl.pallas_call(
        _xattn_kernel,
        grid=(b, SEQ // tm),
        in_specs=[rows, pl.BlockSpec((None, N_MEM, 2 * D_MODEL), lambda i, j: (i, 0, 0)),
                  wfull, wfull, gain, gain, gain],
        out_specs=(rows, rows),
        out_shape=(jax.ShapeDtypeStruct((b, SEQ, D_MODEL), F32),
                   jax.ShapeDtypeStruct((b, SEQ, D_MODEL), BF16)),
        compiler_params=_params("arbitrary", "arbitrary"),
        name="xattn",
    )(x, kv, wq, wo, gq, go, gn)


def _gelu_tanh(x):
    c = math.sqrt(2.0 / math.pi)
    half = 0.5 * x
    return half + half * jnp.tanh(x * (c + (c * 0.044715) * (x * x)))


def _ffn_kernel(hn_ref, wg_ref, wv_ref, cg_ref, cv_ref, wd_ref, o_ref):
    wg = wg_ref[...].astype(BF16)
    wv = wv_ref[...].astype(BF16)
    wd = wd_ref[...].astype(BF16)

    @pl.when(pl.program_id(1) == 0)
    def _():
        o_ref[...] = jnp.zeros_like(o_ref)

    def up(r0):
        lo = max(r0 - FFN_HALO, 0)
        hi = min(r0 + FFN_CHUNK + FFN_HALO, SEQ)
        hn = hn_ref[lo:hi, :]
        return (r0, r0 - lo, jnp.dot(hn, wg, preferred_element_type=F32),
                jnp.dot(hn, wv, preferred_element_type=F32))

    def down(r0, skip, ug, uv):
        act = _gelu_tanh(_dwconv3(ug, cg_ref)) * _dwconv3(uv, cv_ref)
        act = act[skip:skip + FFN_CHUNK].astype(BF16)
        o_ref[r0:r0 + FFN_CHUNK, :] += jnp.dot(act, wd, preferred_element_type=F32)

    pending = None
    for r0 in range(0, SEQ, FFN_CHUNK):
        current = up(r0)
        if pending is not None:
            down(*pending)
        pending = current
    down(*pending)


def _ffn(hn, w_up, w_conv, w_down, layer):
    b = hn.shape[0]
    nk = D_FF // COL_TILE
    return pl.pallas_call(
        _ffn_kernel,
        grid=(b, nk),
        in_specs=[pl.BlockSpec((None, SEQ, D_MODEL), lambda i, k: (i, 0, 0)),
                  pl.BlockSpec((None, D_MODEL, COL_TILE), lambda i, k: (layer, 0, k)),
                  pl.BlockSpec((None, D_MODEL, COL_TILE), lambda i, k: (layer, 0, nk + k)),
                  pl.BlockSpec((3, COL_TILE), lambda i, k: (0, k)),
                  pl.BlockSpec((3, COL_TILE), lambda i, k: (0, nk + k)),
                  pl.BlockSpec((None, COL_TILE, D_MODEL), lambda i, k: (layer, k, 0))],
        out_specs=pl.BlockSpec((None, SEQ, D_MODEL), lambda i, k: (i, 0, 0)),
        out_shape=jax.ShapeDtypeStruct((b, SEQ, D_MODEL), F32),
        compiler_params=_params("arbitrary", "arbitrary"),
        name="ffn",
    )(hn, w_up, w_up, w_conv, w_conv, w_down)


def _residual_kernel(x_ref, f_ref, g_ref, gn_ref, o_ref, hn_ref):
    xn = x_ref[...] + _rms(f_ref[...], g_ref[...])
    o_ref[...] = xn
    hn_ref[...] = _rms(xn, gn_ref[...]).astype(BF16)


def _residual_last_kernel(x_ref, f_ref, g_ref, o_ref):
    o_ref[...] = x_ref[...] + _rms(f_ref[...], g_ref[...])


def _residual(x2d, f2d, g, gn):
    n = x2d.shape[0]
    tm = 1024
    rows = pl.BlockSpec((tm, D_MODEL), lambda i: (i, 0))
    gain = pl.BlockSpec((1, D_MODEL), lambda i: (0, 0))
    x_shape = jax.ShapeDtypeStruct((n, D_MODEL), F32)
    if gn is None:
        return pl.pallas_call(
            _residual_last_kernel,
            grid=(n // tm,),
            in_specs=[rows, rows, gain],
            out_specs=rows,
            out_shape=x_shape,
            compiler_params=_params("arbitrary"),
            name="residual_last",
        )(x2d, f2d, g), None
    return pl.pallas_call(
        _residual_kernel,
        grid=(n // tm,),
        in_specs=[rows, rows, gain, gain],
        out_specs=(rows, rows),
        out_shape=(x_shape, jax.ShapeDtypeStruct((n, D_MODEL), BF16)),
        compiler_params=_params("arbitrary"),
        name="residual",
    )(x2d, f2d, g, gn)


def _angle_tables(num, den):
    ang = (num % den).astype(F32) * (2.0 * math.pi / den)
    return jnp.cos(ang), jnp.sin(ang)


def _dft_tables():
    j = jnp.arange(SEQ, dtype=jnp.int32)[None, :]
    a = jnp.arange(SEQ // DFT_SPLIT, dtype=jnp.int32)[:, None]
    b = jnp.arange(DFT_SPLIT, dtype=jnp.int32)[:, None]
    ca, sa = _angle_tables(DFT_SPLIT * a * j, FFT_N)
    cb, sb = _angle_tables((2 * b + 1) * j, 2 * FFT_N)
    ca, sa, cb, sb = ca[:, None, :], sa[:, None, :], cb[None], sb[None]
    big_c = (ca * cb - sa * sb).reshape(SEQ, SEQ).astype(BF16)
    big_s = (sa * cb + ca * sb).reshape(SEQ, SEQ).astype(BF16)
    rows = SEQ // FFT_RADIX
    kappa = jnp.arange(rows, dtype=jnp.int32)[:, None]
    m = jnp.arange(rows, dtype=jnp.int32)[None, :]
    cf, sf = _angle_tables((2 * kappa + 1) * m, 2 * FFT_SUB)
    r = jnp.arange(1, FFT_RADIX, dtype=jnp.int32)[:, None]
    tc, ts = _angle_tables((2 * kappa.T + 1) * r, 2 * FFT_N)
    tw = jnp.broadcast_to(jnp.concatenate([tc, ts])[:, :, None], (2 * (FFT_RADIX - 1), rows, COL_TILE))
    small = (cf.astype(BF16), sf.astype(BF16), cf.T.astype(BF16), sf.T.astype(BF16), tw)
    return big_c, big_s, small


def kernel(x, mem, norm_gains, mem_norm, w_in, gate_bias, hy_short_w, hy_w1, hy_b1, hy_w2, hy_b2,
           hy_w3, hy_freq, hy_bias, na_rpb, sc_conv_w, w_branch, w_out, xa_wq, xa_wkv, xa_wo,
           ffn_up, ffn_conv, ffn_down):
    b, l, d = x.shape
    depth = w_in.shape[0]
    assert (l, d) == (SEQ, D_MODEL) and mem.shape[1:] == (N_MEM, D_MODEL)
    n = b * l
    dft_c, dft_s, conv_tables = _dft_tables()
    gains = norm_gains.astype(F32)
    x2d = x.reshape(n, d)
    hn = _prenorm(x2d, gains[0, 0][None])
    for i in range(depth):
        g = gains[i]
        w_hyena = w_in[i, :, :3 * HY_W].astype(BF16)
        w_gate = w_in[i, :, 3 * HY_W + 3 * NA_W + 3 * SC_W:].astype(BF16)
        kr, ki = _hyena_filters(dft_c, dft_s, hy_w1[i], hy_b1[i], hy_w2[i], hy_b2[i], hy_w3[i],
                                hy_freq[i])
        hn3 = hn.reshape(b, l, d)
        ya = [_hyena_mixer(hn3, w_hyena, hy_short_w[i].astype(F32), conv_tables, kr, ki,
                           hy_bias[i].astype(F32), tile) for tile in range(HY_W // COL_TILE)]
        yb = _na_mixer(hn3, w_in, i, _na_bias(na_rpb[i]))
        yc = _shortconv_mixer(hn3, w_in, i, sc_conv_w[i].astype(F32))
        x2d = _merge(hn, x2d, ya[0], ya[1], yb.reshape(n, NA_W), yc.reshape(n, SC_W), w_gate,
                     gate_bias[i].astype(F32), w_branch[i].astype(BF16), w_out[i].astype(BF16), g[1][None])
        kv = _mem_kv(mem, mem_norm[i].astype(F32)[None], xa_wkv[i].astype(BF16))
        x3, hn2 = _xattn(x2d.reshape(b, l, d), kv, xa_wq[i].astype(BF16), xa_wo[i].astype(BF16),
                         g[2][None], g[3][None], g[4][None])
        f = _ffn(hn2, ffn_up, ffn_conv[i].astype(F32), ffn_down, i)
        g_next = gains[i + 1, 0][None] if i + 1 < depth else None
        x2d, hn = _residual(x3.reshape(n, d), f.reshape(n, d), g[5][None], g_next)
    return x2d.reshape(b, l, d)
```

```python
import functools
import math

import jax
import jax.numpy as jnp
from jax import lax
from jax.experimental import pallas as pl
from jax.experimental.pallas import tpu as pltpu

D_MODEL = 1024
SEQ = 2048
N_MEM = 256
GRID_W = 64
HY_W = 512
NA_HEADS = 8
NA_HEAD_DIM = 64
NA_W = NA_HEADS * NA_HEAD_DIM
NA_WIN_ROWS = 8
NA_WIN_COLS = 16
SC_W = 512
XA_HEADS = 4
XA_HEAD_DIM = D_MODEL // XA_HEADS
D_FF = 2816
HY_ORDER = 2
HY_EMB = 33
HY_HIDDEN = 64
HY_FAST_DECAY = 0.3
HY_SLOW_DECAY = 1.5
HY_TARGET = 1e-2
N_BRANCH = 3
EPS = 1e-6

FFT_N = 2 * SEQ
FFT_RADIX = 4
FFT_SUB = FFT_N // FFT_RADIX
NA_ROWS = SEQ // GRID_W
NA_KEYS = NA_WIN_ROWS * GRID_W
NA_GROUP = 4
NA_ROW_GROUP = 4
LANES = 128
COL_TILE = 256
ROW_TILE = 512
MERGE_ROWS = 2 * ROW_TILE
XA_ROWS = 2 * ROW_TILE
CONV_PAD = 8
DFT_SPLIT = 64
CONV_CHUNK = 512
CONV_HALO = 16
VMEM_LIMIT = 60 * 1024 * 1024

BF16 = jnp.bfloat16
F32 = jnp.float32


def _params(*sem):
    return pltpu.CompilerParams(dimension_semantics=sem, vmem_limit_bytes=VMEM_LIMIT)


def _rms(xf, g):
    ms = jnp.mean(xf * xf, axis=-1, keepdims=True)
    return xf * lax.rsqrt(ms + EPS) * g


def _dwconv3(u, w_ref):
    n = u.shape[0]
    zeros = jnp.zeros((CONV_PAD, u.shape[1]), F32)
    padded = jnp.concatenate([zeros, u, zeros], axis=0)
    m = n + 2 * CONV_PAD
    prev = pltpu.roll(padded, 1, 0)[CONV_PAD:CONV_PAD + n]
    nxt = pltpu.roll(padded, m - 1, 0)[CONV_PAD:CONV_PAD + n]
    return prev * w_ref[0:1, :] + u * w_ref[1:2, :] + nxt * w_ref[2:3, :]


def _prenorm_kernel(x_ref, g_ref, o_ref):
    o_ref[...] = _rms(x_ref[...], g_ref[...]).astype(BF16)


def _prenorm(x2d, g):
    n = x2d.shape[0]
    tm = 1024
    return pl.pallas_call(
        _prenorm_kernel,
        grid=(n // tm,),
        in_specs=[pl.BlockSpec((tm, D_MODEL), lambda i: (i, 0)),
                  pl.BlockSpec((1, D_MODEL), lambda i: (0, 0))],
        out_specs=pl.BlockSpec((tm, D_MODEL), lambda i: (i, 0)),
        out_shape=jax.ShapeDtypeStruct((n, D_MODEL), BF16),
        compiler_params=_params("arbitrary"),
        name="prenorm",
    )(x2d, g)


def _filter_mlp_kernel(z_ref, w1_ref, b1_ref, w2_ref, b2_ref, w3_ref, f_ref, t_ref, dl_ref,
                       hs_ref, hd_ref):
    hp = lax.Precision.HIGHEST
    h = jnp.sin(f_ref[0:1, :] * (jnp.dot(z_ref[...], w1_ref[...], precision=hp) + b1_ref[...]))
    h = jnp.sin(f_ref[1:2, :] * (jnp.dot(h, w2_ref[...], precision=hp) + b2_ref[...]))
    decay = jnp.exp(-t_ref[...] * dl_ref[...])
    row = lax.broadcasted_iota(jnp.int32, (SEQ, HY_W), 0)
    for o in range(HY_ORDER):
        c_f = o * HY_W
        c_b = HY_ORDER * HY_W + o * HY_W
        hf = jnp.dot(h, w3_ref[:, c_f:c_f + HY_W], precision=hp) * decay
        hb = jnp.dot(h, w3_ref[:, c_b:c_b + HY_W], precision=hp) * decay
        hb = jnp.where(row == 0, 0.0, hb)
        hs_ref[:, c_f:c_f + HY_W] = (hf + hb).astype(BF16)
        hd_ref[:, c_f:c_f + HY_W] = (hb - hf).astype(BF16)


def _filter_dft_kernel(c_ref, s_ref, hs_ref, hd_ref, kr_ref, ki_ref):
    kr_ref[...] = jnp.dot(c_ref[...], hs_ref[...], preferred_element_type=F32) * (2.0 / FFT_N)
    ki_ref[...] = jnp.dot(s_ref[...], hd_ref[...], preferred_element_type=F32) * (2.0 / FFT_N)


def _hyena_filters(dft_c, dft_s, w1, b1, w2, b2, w3, freq):
    t = jnp.linspace(0.0, 1.0, SEQ, dtype=F32)[:, None]
    bands = (HY_EMB - 1) // 2
    w = 2.0 * math.pi * jnp.arange(SEQ, dtype=F32)[:, None] / SEQ
    f = jnp.linspace(1e-4, bands - 1, bands, dtype=F32)[None, :]
    z = jnp.concatenate([t, jnp.cos(f * w), -jnp.sin(f * w)], axis=-1)
    z = jnp.pad(z, ((0, 0), (0, HY_HIDDEN - HY_EMB)))
    w1p = jnp.pad(w1.astype(F32), ((0, HY_HIDDEN - HY_EMB), (0, 0)))
    deltas = jnp.abs(jnp.linspace(math.log(HY_TARGET) / HY_SLOW_DECAY,
                                  math.log(HY_TARGET) / HY_FAST_DECAY, HY_W, dtype=F32))[None, :]
    width = HY_ORDER * HY_W
    hs, hd = pl.pallas_call(
        _filter_mlp_kernel,
        out_shape=(jax.ShapeDtypeStruct((SEQ, width), BF16),
                   jax.ShapeDtypeStruct((SEQ, width), BF16)),
        compiler_params=pltpu.CompilerParams(vmem_limit_bytes=VMEM_LIMIT),
        name="hyena_filter_mlp",
    )(z, w1p, b1[None].astype(F32), w2.astype(F32), b2[None].astype(F32), w3.astype(F32),
      freq.astype(F32), t, deltas)
    nt = width // COL_TILE
    full = pl.BlockSpec((SEQ, SEQ), lambda j: (0, 0))
    col = pl.BlockSpec((SEQ, COL_TILE), lambda j: (0, j))
    kr, ki = pl.pallas_call(
        _filter_dft_kernel,
        grid=(nt,),
        in_specs=[full, full, col, col],
        out_specs=(col, col),
        out_shape=(jax.ShapeDtypeStruct((SEQ, width), F32),
                   jax.ShapeDtypeStruct((SEQ, width), F32)),
        compiler_params=_params("arbitrary"),
        name="hyena_filter_dft",
    )(dft_c, dft_s, hs, hd)

    fam_shape = (FFT_RADIX, SEQ // FFT_RADIX, width)
    return kr.reshape(fam_shape), ki.reshape(fam_shape)


def _hyena_kernel(hn_ref, wv_ref, w1_ref, w2_ref, sv_ref, s1_ref, s2_ref, cf_ref, sf_ref, ct_ref,
                  st_ref, tw_ref, kr0_ref, ki0_ref, kr1_ref, ki1_ref, bias_ref, o_ref,
                  slab_v_ref, slab_1_ref, slab_2_ref, slab_o_ref):
    radix = FFT_RADIX
    rows = SEQ // radix
    n_slab = COL_TILE // LANES
    zero_row = jnp.zeros((CONV_PAD, COL_TILE), F32)

    def project(w_ref, slab_ref):
        u = jnp.dot(hn_ref[...], w_ref[...], preferred_element_type=F32)
        for j in range(n_slab):
            slab_ref[j] = u[:, j * LANES:(j + 1) * LANES]
        return [jnp.concatenate([slab_ref[j, pl.ds(r, rows, stride=radix), :] for j in range(n_slab)],
                                axis=1) for r in range(radix)]

    def short_conv(u, w_ref):
        n = u[0].shape[0]
        prev_wrap = pltpu.roll(jnp.concatenate([u[-1], zero_row], axis=0), 1, 0)[:n]
        next_wrap = pltpu.roll(jnp.concatenate([zero_row, u[0]], axis=0), n + CONV_PAD - 1, 0)[CONV_PAD:]
        prev = [prev_wrap] + u[:-1]
        nxt = u[1:] + [next_wrap]
        return [prev[r] * w_ref[0:1, :] + u[r] * w_ref[1:2, :] + nxt[r] * w_ref[2:3, :]
                for r in range(radix)]

    def cmul(ar, ai, br, bi):
        return ar * br - ai * bi, ar * bi + ai * br

    def long_conv(x, kr_ref, ki_ref):
        cf, sf = cf_ref[...], sf_ref[...]
        xb = [v.astype(BF16) for v in x]
        tr = [jnp.dot(cf, v, preferred_element_type=F32) for v in xb]
        ti = [-jnp.dot(sf, v, preferred_element_type=F32) for v in xb]
        for r in range(1, radix):
            tr[r], ti[r] = cmul(tr[r], ti[r], tw_ref[r - 1], -tw_ref[radix - 2 + r])
        ar, ai = tr[0] + tr[2], ti[0] + ti[2]
        br, bi = tr[0] - tr[2], ti[0] - ti[2]
        cr, ci = tr[1] + tr[3], ti[1] + ti[3]
        dr, di = tr[1] - tr[3], ti[1] - ti[3]
        fam = [(ar + cr, ai + ci), (br + di, bi - dr), (br - di, -bi - dr), (ar - cr, ci - ai)]
        y = [cmul(fr, fi, kr_ref[f], ki_ref[f]) for f, (fr, fi) in enumerate(fam)]
        er, ei = y[0][0] + y[3][0], y[0][1] - y[3][1]
        fr, fi = y[0][0] - y[3][0], y[0][1] + y[3][1]
        gr, gi = y[1][0] + y[2][0], y[1][1] - y[2][1]
        hr, hi = y[1][0] - y[2][0], y[1][1] + y[2][1]
        p = [(er + gr, ei + gi), (fr - hi, fi + hr), (er - gr, ei - gi), (fr + hi, fi - hr)]
        out = []
        for r in range(radix):
            pr, pi_ = p[r]
            if r:
                pr, pi_ = cmul(pr, pi_, tw_ref[r - 1], tw_ref[radix - 2 + r])
            out.append(jnp.dot(ct_ref[...], pr.astype(BF16), preferred_element_type=F32)
                       - jnp.dot(st_ref[...], pi_.astype(BF16), preferred_element_type=F32))
        return out

    v = short_conv(project(wv_ref, slab_v_ref), sv_ref)
    x1 = short_conv(project(w1_ref, slab_1_ref), s1_ref)
    x2 = short_conv(project(w2_ref, slab_2_ref), s2_ref)
    y = long_conv(v, kr0_ref, ki0_ref)
    z = [x1[r] * (y[r] + v[r] * bias_ref[0:1, :]) for r in range(radix)]
    y = long_conv(z, kr1_ref, ki1_ref)
    for r in range(radix):
        out = x2[r] * (y[r] + z[r] * bias_ref[1:2, :])
        for j in range(n_slab):
            slab_o_ref[j, pl.ds(r, rows, stride=radix), :] = out[:, j * LANES:(j + 1) * LANES]
    o_ref[...] = jnp.concatenate([slab_o_ref[j] for j in range(n_slab)], axis=1).astype(BF16)


def _hyena_mixer(hn, w_hy, short_w, tables, kr, ki, bias, tile):
    b = hn.shape[0]
    nt = HY_W // COL_TILE
    rows = SEQ // FFT_RADIX
    once = pl.Buffered(1)
    cf, sf, ct, st, tw = tables

    def fixed(shape, index):
        return pl.BlockSpec(shape, lambda i: index, pipeline_mode=once)

    def wcol(k):
        return fixed((D_MODEL, COL_TILE), (0, k * nt + tile))

    def scol(k):
        return fixed((3, COL_TILE), (0, k * nt + tile))

    def kfam(o):
        return fixed((FFT_RADIX, rows, COL_TILE), (0, 0, o * nt + tile))

    small = fixed((rows, rows), (0, 0))
    slab = pltpu.VMEM((COL_TILE // LANES, SEQ, LANES), F32)
    out = pl.pallas_call(
        _hyena_kernel,
        grid=(b,),
        in_specs=[pl.BlockSpec((None, SEQ, D_MODEL), lambda i: (i, 0, 0)),
                  wcol(0), wcol(1), wcol(2), scol(0), scol(1), scol(2),
                  small, small, small, small,
                  fixed((2 * (FFT_RADIX - 1), rows, COL_TILE), (0, 0, 0)),
                  kfam(0), kfam(0), kfam(1), kfam(1),
                  fixed((HY_ORDER, COL_TILE), (0, tile))],
        out_specs=pl.BlockSpec((None, SEQ, COL_TILE), lambda i: (i, 0, 0)),
        out_shape=jax.ShapeDtypeStruct((b, SEQ, COL_TILE), BF16),
        scratch_shapes=[slab, slab, slab, slab],
        compiler_params=_params("arbitrary"),
        name="hyena_mixer",
    )(hn, w_hy, w_hy, w_hy, short_w, short_w, short_w, cf, sf, ct, st, tw, kr, ki, kr, ki, bias)
    return out.reshape(b * SEQ, COL_TILE)


def _shortconv_kernel(hn_ref, wb_ref, wc_ref, wx_ref, cw_ref, o_ref):
    wb = wb_ref[...].astype(BF16)
    wc = wc_ref[...].astype(BF16)
    wx = wx_ref[...].astype(BF16)

    def project(r0):
        lo = max(r0 - CONV_HALO, 0)
        hi = min(r0 + CONV_CHUNK + CONV_HALO, SEQ)
        hn = hn_ref[lo:hi, :]
        return (r0, r0 - lo, jnp.dot(hn, wb, preferred_element_type=F32),
                jnp.dot(hn, wc, preferred_element_type=F32), jnp.dot(hn, wx, preferred_element_type=F32))

    def finish(r0, skip, bg, cg, xi):
        out = bg * _dwconv3(cg * xi, cw_ref)
        o_ref[r0:r0 + CONV_CHUNK, :] = out[skip:skip + CONV_CHUNK].astype(BF16)

    pending = None
    for r0 in range(0, SEQ, CONV_CHUNK):
        current = project(r0)
        if pending is not None:
            finish(*pending)
        pending = current
    finish(*pending)


def _w_in_cols(layer, width, first):
    return lambda k, nt: pl.BlockSpec((None, D_MODEL, width),
                                      lambda j, i: (layer, 0, first + k * nt + j))


def _shortconv_mixer(hn, w_in, layer, conv_w):
    b = hn.shape[0]
    nt = SC_W // COL_TILE
    wcol = _w_in_cols(layer, COL_TILE, (3 * HY_W + 3 * NA_W) // COL_TILE)
    return pl.pallas_call(
        _shortconv_kernel,
        grid=(nt, b),
        in_specs=[pl.BlockSpec((None, SEQ, D_MODEL), lambda j, i: (i, 0, 0)),
                  wcol(0, nt), wcol(1, nt), wcol(2, nt),
                  pl.BlockSpec((3, COL_TILE), lambda j, i: (0, j))],
        out_specs=pl.BlockSpec((None, SEQ, COL_TILE), lambda j, i: (i, 0, j)),
        out_shape=jax.ShapeDtypeStruct((b, SEQ, SC_W), BF16),
        compiler_params=_params("arbitrary", "arbitrary"),
        name="shortconv_mixer",
    )(hn, w_in, w_in, w_in, conv_w)


def _na_kernel(hn_ref, wq_ref, wk_ref, wv_ref, bias_ref, o_ref, q_ref, k_ref, v_ref):
    hn = hn_ref[...]
    q = jnp.dot(hn, wq_ref[...].astype(BF16), preferred_element_type=F32)
    q_ref[...] = (q * (NA_HEAD_DIM ** -0.5)).astype(BF16)
    k_ref[...] = jnp.dot(hn, wk_ref[...].astype(BF16), preferred_element_type=F32).astype(BF16)
    v_ref[...] = jnp.dot(hn, wv_ref[...].astype(BF16), preferred_element_type=F32).astype(BF16)
    gw = NA_GROUP * NA_HEAD_DIM
    same_head = (lax.broadcasted_iota(jnp.int32, (gw, gw), 0) // NA_HEAD_DIM
                 == lax.broadcasted_iota(jnp.int32, (gw, gw), 1) // NA_HEAD_DIM)

    def scores(r):
        w0 = jnp.clip(r - NA_WIN_ROWS // 2, 0, NA_ROWS - NA_WIN_ROWS)
        off = w0 - r + (NA_WIN_ROWS - 1)
        q0 = pl.multiple_of(r * GRID_W, GRID_W)
        k0 = pl.multiple_of(w0 * GRID_W, GRID_W)
        q_row = q_ref[pl.ds(q0, GRID_W), :]
        q_heads = jnp.where(same_head, jnp.concatenate([q_row] * NA_GROUP, axis=0), 0)
        s = lax.dot_general(q_heads, k_ref[pl.ds(k0, NA_KEYS), :], (((1,), (1,)), ((), ())),
                            preferred_element_type=F32)
        bias = jnp.concatenate(
            [jnp.concatenate([bias_ref[h, off + 2 * m] for m in range(NA_WIN_ROWS // 2)], axis=1)
             for h in range(NA_GROUP)], axis=0)
        return q0, k0, s + bias

    def attend(q0, k0, s):
        p = jnp.exp(s - jnp.max(s, axis=-1, keepdims=True))
        inv = 1.0 / jnp.sum(p, axis=-1, keepdims=True)
        pv = jnp.dot(p.astype(BF16), v_ref[pl.ds(k0, NA_KEYS), :], preferred_element_type=F32)
        pv = jnp.where(same_head, pv * inv, 0.0)
        out = pv[0:GRID_W]
        for h in range(1, NA_GROUP):
            out = out + pv[h * GRID_W:(h + 1) * GRID_W]
        o_ref[pl.ds(q0, GRID_W), :] = out.astype(BF16)

    def rows_body(i, carry):
        pending = [scores(i * NA_ROW_GROUP + t) for t in range(NA_ROW_GROUP)]
        for args in pending:
            attend(*args)
        return carry

    lax.fori_loop(0, NA_ROWS // NA_ROW_GROUP, rows_body, 0)


def _na_bias(rpb):
    c = jnp.arange(GRID_W)
    col_start = jnp.clip(c - NA_WIN_COLS // 2, 0, GRID_W - NA_WIN_COLS)
    col_mask = (c[None, :] >= col_start[:, None]) & (c[None, :] < col_start[:, None] + NA_WIN_COLS)
    dc = jnp.clip(c[None, :] - c[:, None] + NA_WIN_COLS - 1, 0, 2 * NA_WIN_COLS - 2)
    pick = (dc[None] == jnp.arange(2 * NA_WIN_COLS - 1)[:, None, None]).astype(F32)
    table = jnp.einsum("hrd,dqc->hrqc", rpb.astype(F32), pick, precision=lax.Precision.HIGHEST)
    table = table + jnp.where(col_mask, 0.0, -1e30)[None, None]
    return jnp.concatenate([table[:, :-1], table[:, 1:]], axis=-1)


def _na_mixer(hn, w_in, layer, bias):
    b = hn.shape[0]
    gw = NA_GROUP * NA_HEAD_DIM
    ng = NA_W // gw
    wcol = _w_in_cols(layer, gw, 3 * HY_W // gw)
    return pl.pallas_call(
        _na_kernel,
        grid=(ng, b),
        in_specs=[pl.BlockSpec((None, SEQ, D_MODEL), lambda j, i: (i, 0, 0)),
                  wcol(0, ng), wcol(1, ng), wcol(2, ng),
                  pl.BlockSpec((NA_GROUP, 2 * NA_WIN_ROWS - 2, GRID_W, 2 * GRID_W),
                               lambda j, i: (j, 0, 0, 0))],
        out_specs=pl.BlockSpec((None, SEQ, gw), lambda j, i: (i, 0, j)),
        out_shape=jax.ShapeDtypeStruct((b, SEQ, NA_W), BF16),
        scratch_shapes=[pltpu.VMEM((SEQ, gw), BF16) for _ in range(3)],
        compiler_params=_params("arbitrary", "arbitrary"),
        name="na_mixer",
    )(hn, w_in, w_in, w_in, bias)


def _merge_kernel(hn_ref, x_ref, ya0_ref, ya1_ref, yb_ref, yc_ref, wg_ref, gb_ref, wb_ref, wo_ref,
                  g_ref, o_ref):
    def gated_sum(rows):
        hn = hn_ref[rows, :]
        ya = jnp.concatenate([ya0_ref[rows, :], ya1_ref[rows, :]], axis=1)
        merged = None
        for i, y in enumerate((ya, yb_ref[rows, :], yc_ref[rows, :])):
            pre = jnp.dot(hn, wg_ref[:, i * D_MODEL:(i + 1) * D_MODEL], preferred_element_type=F32)
            gate = jax.nn.sigmoid(pre + gb_ref[i:i + 1, :])
            term = gate * jnp.dot(y, wb_ref[i], preferred_element_type=F32)
            merged = term if merged is None else merged + term
        return merged.astype(BF16)

    chunks = [slice(r, r + ROW_TILE) for r in range(0, MERGE_ROWS, ROW_TILE)]
    merged = [gated_sum(rows) for rows in chunks]
    for rows, m in zip(chunks, merged):
        out = jnp.dot(m, wo_ref[...], preferred_element_type=F32)
        o_ref[rows, :] = x_ref[rows, :] + _rms(out, g_ref[...])


def _merge(hn2d, x2d, ya0, ya1, yb, yc, w_gate, gate_bias, w_branch, w_out, g):
    n = x2d.shape[0]
    tm = MERGE_ROWS
    once = pl.Buffered(1)
    rows = lambda w: pl.BlockSpec((tm, w), lambda i: (i, 0))
    return pl.pallas_call(
        _merge_kernel,
        grid=(n // tm,),
        in_specs=[rows(D_MODEL), rows(D_MODEL), rows(COL_TILE), rows(COL_TILE), rows(NA_W), rows(SC_W),
                  pl.BlockSpec((D_MODEL, N_BRANCH * D_MODEL), lambda i: (0, 0), pipeline_mode=once),
                  pl.BlockSpec((N_BRANCH, D_MODEL), lambda i: (0, 0)),
                  pl.BlockSpec((N_BRANCH, HY_W, D_MODEL), lambda i: (0, 0, 0), pipeline_mode=once),
                  pl.BlockSpec((D_MODEL, D_MODEL), lambda i: (0, 0), pipeline_mode=once),
                  pl.BlockSpec((1, D_MODEL), lambda i: (0, 0))],
        out_specs=rows(D_MODEL),
        out_shape=jax.ShapeDtypeStruct((n, D_MODEL), F32),
        compiler_params=_params("arbitrary"),
        name="merge",
    )(hn2d, x2d, ya0, ya1, yb, yc, w_gate, gate_bias, w_branch, w_out, g)


def _kv_kernel(m_ref, g_ref, w_ref, o_ref):
    mn = _rms(m_ref[...], g_ref[...]).astype(BF16)
    o_ref[...] = jnp.dot(mn, w_ref[...], preferred_element_type=F32).astype(BF16)


def _mem_kv(mem, g, wkv):
    b = mem.shape[0]
    return pl.pallas_call(
        _kv_kernel,
        grid=(b,),
        in_specs=[pl.BlockSpec((None, N_MEM, D_MODEL), lambda i: (i, 0, 0)),
                  pl.BlockSpec((1, D_MODEL), lambda i: (0, 0)),
                  pl.BlockSpec((D_MODEL, 2 * D_MODEL), lambda i: (0, 0))],
        out_specs=pl.BlockSpec((None, N_MEM, 2 * D_MODEL), lambda i: (i, 0, 0)),
        out_shape=jax.ShapeDtypeStruct((b, N_MEM, 2 * D_MODEL), BF16),
        compiler_params=_params("arbitrary"),
        name="mem_kv",
    )(mem, g, wkv)


def _xattn_kernel(x_ref, kv_ref, wq_ref, wo_ref, gq_ref, go_ref, gn_ref, o_ref, hn_ref):
    chunks = [slice(r, r + ROW_TILE) for r in range(0, XA_ROWS, ROW_TILE)]
    head_cols = [slice(i * XA_HEAD_DIM, (i + 1) * XA_HEAD_DIM) for i in range(XA_HEADS)]

    def query(rows):
        h = _rms(x_ref[rows, :], gq_ref[...]).astype(BF16)
        q = jnp.dot(h, wq_ref[...], preferred_element_type=F32) * (XA_HEAD_DIM ** -0.5)
        return q.astype(BF16)

    def scores(q):
        return [lax.dot_general(q[:, sl], kv_ref[:, sl], (((1,), (1,)), ((), ())),
                                preferred_element_type=F32) for sl in head_cols]

    def values(s_heads):
        heads = []
        for i, s in enumerate(s_heads):
            vm = kv_ref[:, D_MODEL + i * XA_HEAD_DIM:D_MODEL + (i + 1) * XA_HEAD_DIM]
            p = jnp.exp(s - jnp.max(s, axis=-1, keepdims=True))
            den = jnp.sum(p, axis=-1, keepdims=True)
            heads.append((jnp.dot(p.astype(BF16), vm, preferred_element_type=F32) / den).astype(BF16))
        return jnp.concatenate(heads, axis=-1)

    s_all = [scores(q) for q in [query(rows) for rows in chunks]]
    attended = [values(s) for s in s_all]
    for rows, a in zip(chunks, attended):
        o = jnp.dot(a, wo_ref[...], preferred_element_type=F32)
        xn = x_ref[rows, :] + _rms(o, go_ref[...])
        o_ref[rows, :] = xn
        hn_ref[rows, :] = _rms(xn, gn_ref[...]).astype(BF16)


def _xattn(x, kv, wq, wo, gq, go, gn):
    b = x.shape[0]
    tm = XA_ROWS
    once = pl.Buffered(1)
    rows = pl.BlockSpec((None, tm, D_MODEL), lambda i, j: (i, j, 0))
    gain = pl.BlockSpec((1, D_MODEL), lambda i, j: (0, 0))
    wfull = pl.BlockSpec((D_MODEL, D_MODEL), lambda i, j: (0, 0), pipeline_mode=once)
    return pl.pallas_call(
        _xattn_kernel,
        grid=(b, SEQ // tm),
        in_specs=[rows, pl.BlockSpec((None, N_MEM, 2 * D_MODEL), lambda i, j: (i, 0, 0)),
                  wfull, wfull, gain, gain, gain],
        out_specs=(rows, rows),
        out_shape=(jax.ShapeDtypeStruct((b, SEQ, D_MODEL), F32),
                   jax.ShapeDtypeStruct((b, SEQ, D_MODEL), BF16)),
        compiler_params=_params("arbitrary", "arbitrary"),
        name="xattn",
    )(x, kv, wq, wo, gq, go, gn)


def _gelu_tanh(x):
    c = math.sqrt(2.0 / math.pi)
    half = 0.5 * x
    return half + half * jnp.tanh(x * (c + (c * 0.044715) * (x * x)))


def _ffn_kernel(hn_ref, wg_ref, wv_ref, cg_ref, cv_ref, wd_ref, o_ref):
    wg = wg_ref[...].astype(BF16)
    wv = wv_ref[...].astype(BF16)
    wd = wd_ref[...].astype(BF16)

    @pl.when(pl.program_id(1) == 0)
    def _():
        o_ref[...] = jnp.zeros_like(o_ref)

    def up(r0):
        lo = max(r0 - CONV_HALO, 0)
        hi = min(r0 + CONV_CHUNK + CONV_HALO, SEQ)
        hn = hn_ref[lo:hi, :]
        return (r0, r0 - lo, jnp.dot(hn, wg, preferred_element_type=F32),
                jnp.dot(hn, wv, preferred_element_type=F32))

    def down(r0, skip, ug, uv):
        act = _gelu_tanh(_dwconv3(ug, cg_ref)) * _dwconv3(uv, cv_ref)
        act = act[skip:skip + CONV_CHUNK].astype(BF16)
        o_ref[r0:r0 + CONV_CHUNK, :] += jnp.dot(act, wd, preferred_element_type=F32)

    pending = None
    for r0 in range(0, SEQ, CONV_CHUNK):
        current = up(r0)
        if pending is not None:
            down(*pending)
        pending = current
    down(*pending)


def _ffn(hn, w_up, w_conv, w_down, layer):
    b = hn.shape[0]
    nk = D_FF // COL_TILE
    return pl.pallas_call(
        _ffn_kernel,
        grid=(b, nk),
        in_specs=[pl.BlockSpec((None, SEQ, D_MODEL), lambda i, k: (i, 0, 0)),
                  pl.BlockSpec((None, D_MODEL, COL_TILE), lambda i, k: (layer, 0, k)),
                  pl.BlockSpec((None, D_MODEL, COL_TILE), lambda i, k: (layer, 0, nk + k)),
                  pl.BlockSpec((3, COL_TILE), lambda i, k: (0, k)),
                  pl.BlockSpec((3, COL_TILE), lambda i, k: (0, nk + k)),
                  pl.BlockSpec((None, COL_TILE, D_MODEL), lambda i, k: (layer, k, 0))],
        out_specs=pl.BlockSpec((None, SEQ, D_MODEL), lambda i, k: (i, 0, 0)),
        out_shape=jax.ShapeDtypeStruct((b, SEQ, D_MODEL), F32),
        compiler_params=_params("arbitrary", "arbitrary"),
        name="ffn",
    )(hn, w_up, w_up, w_conv, w_conv, w_down)


def _residual_kernel(x_ref, f_ref, g_ref, gn_ref, o_ref, hn_ref):
    xn = x_ref[...] + _rms(f_ref[...], g_ref[...])
    o_ref[...] = xn
    hn_ref[...] = _rms(xn, gn_ref[...]).astype(BF16)


def _residual_last_kernel(x_ref, f_ref, g_ref, o_ref):
    o_ref[...] = x_ref[...] + _rms(f_ref[...], g_ref[...])


def _residual(x2d, f2d, g, gn):
    n = x2d.shape[0]
    tm = 1024
    rows = pl.BlockSpec((tm, D_MODEL), lambda i: (i, 0))
    gain = pl.BlockSpec((1, D_MODEL), lambda i: (0, 0))
    x_shape = jax.ShapeDtypeStruct((n, D_MODEL), F32)
    if gn is None:
        return pl.pallas_call(
            _residual_last_kernel,
            grid=(n // tm,),
            in_specs=[rows, rows, gain],
            out_specs=rows,
            out_shape=x_shape,
            compiler_params=_params("arbitrary"),
            name="residual_last",
        )(x2d, f2d, g), None
    return pl.pallas_call(
        _residual_kernel,
        grid=(n // tm,),
        in_specs=[rows, rows, gain, gain],
        out_specs=(rows, rows),
        out_shape=(x_shape, jax.ShapeDtypeStruct((n, D_MODEL), BF16)),
        compiler_params=_params("arbitrary"),
        name="residual",
    )(x2d, f2d, g, gn)


def _angle_tables(num, den):
    ang = (num % den).astype(F32) * (2.0 * math.pi / den)
    return jnp.cos(ang), jnp.sin(ang)


def _dft_tables():
    j = jnp.arange(SEQ, dtype=jnp.int32)
    rows = SEQ // FFT_RADIX
    q = rows // DFT_SPLIT
    up = jnp.arange(q, dtype=jnp.int32)
    a = jnp.concatenate([up, 2 * q + up, 2 * q - 1 - up, 4 * q - 1 - up]).reshape(2, 2 * q, 1, 1)
    b_up = jnp.arange(DFT_SPLIT, dtype=jnp.int32)
    b = jnp.stack([b_up, DFT_SPLIT - 1 - b_up]).reshape(2, 1, DFT_SPLIT, 1)
    ca, sa = _angle_tables(DFT_SPLIT * a * j, FFT_N)
    cb, sb = _angle_tables((2 * b + 1) * j, 2 * FFT_N)
    big_c = (ca * cb - sa * sb).reshape(SEQ, SEQ).astype(BF16)
    big_s = (sa * cb + ca * sb).reshape(SEQ, SEQ).astype(BF16)
    kappa = jnp.arange(rows, dtype=jnp.int32)[:, None]
    m = jnp.arange(rows, dtype=jnp.int32)[None, :]
    cf, sf = _angle_tables((2 * kappa + 1) * m, 2 * FFT_SUB)
    r = jnp.arange(1, FFT_RADIX, dtype=jnp.int32)[:, None]
    tc, ts = _angle_tables((2 * kappa.T + 1) * r, 2 * FFT_N)
    tw = jnp.broadcast_to(jnp.concatenate([tc, ts])[:, :, None], (2 * (FFT_RADIX - 1), rows, COL_TILE))
    small = (cf.astype(BF16), sf.astype(BF16), cf.T.astype(BF16), sf.T.astype(BF16), tw)
    return big_c, big_s, small


def kernel(x, mem, norm_gains, mem_norm, w_in, gate_bias, hy_short_w, hy_w1, hy_b1, hy_w2, hy_b2,
           hy_w3, hy_freq, hy_bias, na_rpb, sc_conv_w, w_branch, w_out, xa_wq, xa_wkv, xa_wo,
           ffn_up, ffn_conv, ffn_down):
    b, l, d = x.shape
    depth = w_in.shape[0]
    assert (l, d) == (SEQ, D_MODEL) and mem.shape[1:] == (N_MEM, D_MODEL)
    n = b * l
    dft_c, dft_s, conv_tables = _dft_tables()
    gains = norm_gains.astype(F32)
    x2d = x.reshape(n, d)
    hn = _prenorm(x2d, gains[0, 0][None])
    for i in range(depth):
        g = gains[i]
        w_hyena = w_in[i, :, :3 * HY_W].astype(BF16)
        w_gate = w_in[i, :, 3 * HY_W + 3 * NA_W + 3 * SC_W:].astype(BF16)
        kr, ki = _hyena_filters(dft_c, dft_s, hy_w1[i], hy_b1[i], hy_w2[i], hy_b2[i], hy_w3[i],
                                hy_freq[i])
        hn3 = hn.reshape(b, l, d)
        ya = [_hyena_mixer(hn3, w_hyena, hy_short_w[i].astype(F32), conv_tables, kr, ki,
                           hy_bias[i].astype(F32), tile) for tile in range(HY_W // COL_TILE)]
        yb = _na_mixer(hn3, w_in, i, _na_bias(na_rpb[i]))
        yc = _shortconv_mixer(hn3, w_in, i, sc_conv_w[i].astype(F32))
        x2d = _merge(hn, x2d, ya[0], ya[1], yb.reshape(n, NA_W), yc.reshape(n, SC_W), w_gate,
                     gate_bias[i].astype(F32), w_branch[i].astype(BF16), w_out[i].astype(BF16), g[1][None])
        kv = _mem_kv(mem, mem_norm[i].astype(F32)[None], xa_wkv[i].astype(BF16))
        x3, hn2 = _xattn(x2d.reshape(b, l, d), kv, xa_wq[i].astype(BF16), xa_wo[i].astype(BF16),
                         g[2][None], g[3][None], g[4][None])
        f = _ffn(hn2, ffn_up, ffn_conv[i].astype(F32), ffn_down, i)
        g_next = gains[i + 1, 0][None] if i + 1 < depth else None
        x2d, hn = _residual(x3.reshape(n, d), f.reshape(n, d), g[5][None], g_next)
    return x2d.reshape(b, l, d)
```

```python
import functools
import math

import jax
import jax.numpy as jnp
from jax import lax
from jax.experimental import pallas as pl
from jax.experimental.pallas import tpu as pltpu

D_MODEL = 1024
SEQ = 2048
N_MEM = 256
GRID_W = 64
HY_W = 512
NA_HEADS = 8
NA_HEAD_DIM = 64
NA_W = NA_HEADS * NA_HEAD_DIM
NA_WIN_ROWS = 8
NA_WIN_COLS = 16
SC_W = 512
XA_HEADS = 4
XA_HEAD_DIM = D_MODEL // XA_HEADS
D_FF = 2816
HY_ORDER = 2
HY_EMB = 33
HY_HIDDEN = 64
HY_FAST_DECAY = 0.3
HY_SLOW_DECAY = 1.5
HY_TARGET = 1e-2
N_BRANCH = 3
EPS = 1e-6

FFT_N = 2 * SEQ
FFT_RADIX = 4
FFT_SUB = FFT_N // FFT_RADIX
NA_ROWS = SEQ // GRID_W
NA_KEYS = NA_WIN_ROWS * GRID_W
NA_GROUP = 4
NA_ROW_GROUP = 4
LANES = 128
COL_TILE = 256
ROW_TILE = 512
MERGE_ROWS = 2 * ROW_TILE
XA_ROWS = 2 * ROW_TILE
CONV_PAD = 8
DFT_SPLIT = 64
CONV_CHUNK = 512
FFN_SPLIT = 2
CONV_HALO = 16
VMEM_LIMIT = 60 * 1024 * 1024

BF16 = jnp.bfloat16
F32 = jnp.float32


def _params(*sem):
    return pltpu.CompilerParams(dimension_semantics=sem, vmem_limit_bytes=VMEM_LIMIT)


def _rms(xf, g):
    ms = jnp.mean(xf * xf, axis=-1, keepdims=True)
    return xf * lax.rsqrt(ms + EPS) * g


def _dwconv3(u, w_ref):
    n = u.shape[0]
    zeros = jnp.zeros((CONV_PAD, u.shape[1]), F32)
    padded = jnp.concatenate([zeros, u, zeros], axis=0)
    m = n + 2 * CONV_PAD
    prev = pltpu.roll(padded, 1, 0)[CONV_PAD:CONV_PAD + n]
    nxt = pltpu.roll(padded, m - 1, 0)[CONV_PAD:CONV_PAD + n]
    return prev * w_ref[0:1, :] + u * w_ref[1:2, :] + nxt * w_ref[2:3, :]


def _prenorm_kernel(x_ref, g_ref, o_ref):
    o_ref[...] = _rms(x_ref[...], g_ref[...]).astype(BF16)


def _prenorm(x2d, g):
    n = x2d.shape[0]
    tm = 1024
    return pl.pallas_call(
        _prenorm_kernel,
        grid=(n // tm,),
        in_specs=[pl.BlockSpec((tm, D_MODEL), lambda i: (i, 0)),
                  pl.BlockSpec((1, D_MODEL), lambda i: (0, 0))],
        out_specs=pl.BlockSpec((tm, D_MODEL), lambda i: (i, 0)),
        out_shape=jax.ShapeDtypeStruct((n, D_MODEL), BF16),
        compiler_params=_params("arbitrary"),
        name="prenorm",
    )(x2d, g)


def _filter_mlp_kernel(z_ref, w1_ref, b1_ref, w2_ref, b2_ref, w3_ref, f_ref, t_ref, dl_ref,
                       hs_ref, hd_ref):
    hp = lax.Precision.HIGHEST
    h = jnp.sin(f_ref[0:1, :] * (jnp.dot(z_ref[...], w1_ref[...], precision=hp) + b1_ref[...]))
    h = jnp.sin(f_ref[1:2, :] * (jnp.dot(h, w2_ref[...], precision=hp) + b2_ref[...]))
    decay = jnp.exp(-t_ref[...] * dl_ref[...])
    row = lax.broadcasted_iota(jnp.int32, (SEQ, HY_W), 0)
    for o in range(HY_ORDER):
        c_f = o * HY_W
        c_b = HY_ORDER * HY_W + o * HY_W
        hf = jnp.dot(h, w3_ref[:, c_f:c_f + HY_W], precision=hp) * decay
        hb = jnp.dot(h, w3_ref[:, c_b:c_b + HY_W], precision=hp) * decay
        hb = jnp.where(row == 0, 0.0, hb)
        hs_ref[:, c_f:c_f + HY_W] = (hf + hb).astype(BF16)
        hd_ref[:, c_f:c_f + HY_W] = (hb - hf).astype(BF16)


def _filter_dft_kernel(c_ref, s_ref, hs_ref, hd_ref, kr_ref, ki_ref):
    kr_ref[...] = jnp.dot(c_ref[...], hs_ref[...], preferred_element_type=F32) * (2.0 / FFT_N)
    ki_ref[...] = jnp.dot(s_ref[...], hd_ref[...], preferred_element_type=F32) * (2.0 / FFT_N)


def _hyena_filters(dft_c, dft_s, w1, b1, w2, b2, w3, freq):
    t = jnp.linspace(0.0, 1.0, SEQ, dtype=F32)[:, None]
    bands = (HY_EMB - 1) // 2
    w = 2.0 * math.pi * jnp.arange(SEQ, dtype=F32)[:, None] / SEQ
    f = jnp.linspace(1e-4, bands - 1, bands, dtype=F32)[None, :]
    z = jnp.concatenate([t, jnp.cos(f * w), -jnp.sin(f * w)], axis=-1)
    z = jnp.pad(z, ((0, 0), (0, HY_HIDDEN - HY_EMB)))
    w1p = jnp.pad(w1.astype(F32), ((0, HY_HIDDEN - HY_EMB), (0, 0)))
    deltas = jnp.abs(jnp.linspace(math.log(HY_TARGET) / HY_SLOW_DECAY,
                                  math.log(HY_TARGET) / HY_FAST_DECAY, HY_W, dtype=F32))[None, :]
    width = HY_ORDER * HY_W
    hs, hd = pl.pallas_call(
        _filter_mlp_kernel,
        out_shape=(jax.ShapeDtypeStruct((SEQ, width), BF16),
                   jax.ShapeDtypeStruct((SEQ, width), BF16)),
        compiler_params=pltpu.CompilerParams(vmem_limit_bytes=VMEM_LIMIT),
        name="hyena_filter_mlp",
    )(z, w1p, b1[None].astype(F32), w2.astype(F32), b2[None].astype(F32), w3.astype(F32),
      freq.astype(F32), t, deltas)
    nt = width // COL_TILE
    full = pl.BlockSpec((SEQ, SEQ), lambda j: (0, 0))
    col = pl.BlockSpec((SEQ, COL_TILE), lambda j: (0, j))
    kr, ki = pl.pallas_call(
        _filter_dft_kernel,
        grid=(nt,),
        in_specs=[full, full, col, col],
        out_specs=(col, col),
        out_shape=(jax.ShapeDtypeStruct((SEQ, width), F32),
                   jax.ShapeDtypeStruct((SEQ, width), F32)),
        compiler_params=_params("arbitrary"),
        name="hyena_filter_dft",
    )(dft_c, dft_s, hs, hd)

    fam_shape = (FFT_RADIX, SEQ // FFT_RADIX, width)
    return kr.reshape(fam_shape), ki.reshape(fam_shape)


def _hyena_kernel(hn_ref, wv_ref, w1_ref, w2_ref, sv_ref, s1_ref, s2_ref, cf_ref, sf_ref, ct_ref,
                  st_ref, tw_ref, kr0_ref, ki0_ref, kr1_ref, ki1_ref, bias_ref, o_ref,
                  slab_v_ref, slab_1_ref, slab_2_ref, slab_o_ref):
    radix = FFT_RADIX
    rows = SEQ // radix
    n_slab = COL_TILE // LANES
    zero_row = jnp.zeros((CONV_PAD, COL_TILE), F32)

    def project(w_ref, slab_ref):
        u = jnp.dot(hn_ref[...], w_ref[...], preferred_element_type=F32)
        for j in range(n_slab):
            slab_ref[j] = u[:, j * LANES:(j + 1) * LANES]
        return [jnp.concatenate([slab_ref[j, pl.ds(r, rows, stride=radix), :] for j in range(n_slab)],
                                axis=1) for r in range(radix)]

    def short_conv(u, w_ref):
        n = u[0].shape[0]
        prev_wrap = pltpu.roll(jnp.concatenate([u[-1], zero_row], axis=0), 1, 0)[:n]
        next_wrap = pltpu.roll(jnp.concatenate([zero_row, u[0]], axis=0), n + CONV_PAD - 1, 0)[CONV_PAD:]
        prev = [prev_wrap] + u[:-1]
        nxt = u[1:] + [next_wrap]
        return [prev[r] * w_ref[0:1, :] + u[r] * w_ref[1:2, :] + nxt[r] * w_ref[2:3, :]
                for r in range(radix)]

    def cmul(ar, ai, br, bi):
        return ar * br - ai * bi, ar * bi + ai * br

    def long_conv(x, kr_ref, ki_ref):
        cf, sf = cf_ref[...], sf_ref[...]
        xb = [v.astype(BF16) for v in x]
        tr = [jnp.dot(cf, v, preferred_element_type=F32) for v in xb]
        ti = [-jnp.dot(sf, v, preferred_element_type=F32) for v in xb]
        for r in range(1, radix):
            tr[r], ti[r] = cmul(tr[r], ti[r], tw_ref[r - 1], -tw_ref[radix - 2 + r])
        ar, ai = tr[0] + tr[2], ti[0] + ti[2]
        br, bi = tr[0] - tr[2], ti[0] - ti[2]
        cr, ci = tr[1] + tr[3], ti[1] + ti[3]
        dr, di = tr[1] - tr[3], ti[1] - ti[3]
        fam = [(ar + cr, ai + ci), (br + di, bi - dr), (br - di, -bi - dr), (ar - cr, ci - ai)]
        y = [cmul(fr, fi, kr_ref[f], ki_ref[f]) for f, (fr, fi) in enumerate(fam)]
        er, ei = y[0][0] + y[3][0], y[0][1] - y[3][1]
        fr, fi = y[0][0] - y[3][0], y[0][1] + y[3][1]
        gr, gi = y[1][0] + y[2][0], y[1][1] - y[2][1]
        hr, hi = y[1][0] - y[2][0], y[1][1] + y[2][1]
        p = [(er + gr, ei + gi), (fr - hi, fi + hr), (er - gr, ei - gi), (fr + hi, fi - hr)]
        out = []
        for r in range(radix):
            pr, pi_ = p[r]
            if r:
                pr, pi_ = cmul(pr, pi_, tw_ref[r - 1], tw_ref[radix - 2 + r])
            out.append(jnp.dot(ct_ref[...], pr.astype(BF16), preferred_element_type=F32)
                       - jnp.dot(st_ref[...], pi_.astype(BF16), preferred_element_type=F32))
        return out

    v = short_conv(project(wv_ref, slab_v_ref), sv_ref)
    x1 = short_conv(project(w1_ref, slab_1_ref), s1_ref)
    x2 = short_conv(project(w2_ref, slab_2_ref), s2_ref)
    y = long_conv(v, kr0_ref, ki0_ref)
    z = [x1[r] * (y[r] + v[r] * bias_ref[0:1, :]) for r in range(radix)]
    y = long_conv(z, kr1_ref, ki1_ref)
    for r in range(radix):
        out = x2[r] * (y[r] + z[r] * bias_ref[1:2, :])
        for j in range(n_slab):
            slab_o_ref[j, pl.ds(r, rows, stride=radix), :] = out[:, j * LANES:(j + 1) * LANES]
    o_ref[...] = jnp.concatenate([slab_o_ref[j] for j in range(n_slab)], axis=1).astype(BF16)


def _hyena_mixer(hn, w_hy, short_w, tables, kr, ki, bias, tile):
    b = hn.shape[0]
    nt = HY_W // COL_TILE
    rows = SEQ // FFT_RADIX
    once = pl.Buffered(1)
    cf, sf, ct, st, tw = tables

    def fixed(shape, index):
        return pl.BlockSpec(shape, lambda i: index, pipeline_mode=once)

    def wcol(k):
        return fixed((D_MODEL, COL_TILE), (0, k * nt + tile))

    def scol(k):
        return fixed((3, COL_TILE), (0, k * nt + tile))

    def kfam(o):
        return fixed((FFT_RADIX, rows, COL_TILE), (0, 0, o * nt + tile))

    small = fixed((rows, rows), (0, 0))
    slab = pltpu.VMEM((COL_TILE // LANES, SEQ, LANES), F32)
    out = pl.pallas_call(
        _hyena_kernel,
        grid=(b,),
        in_specs=[pl.BlockSpec((None, SEQ, D_MODEL), lambda i: (i, 0, 0)),
                  wcol(0), wcol(1), wcol(2), scol(0), scol(1), scol(2),
                  small, small, small, small,
                  fixed((2 * (FFT_RADIX - 1), rows, COL_TILE), (0, 0, 0)),
                  kfam(0), kfam(0), kfam(1), kfam(1),
                  fixed((HY_ORDER, COL_TILE), (0, tile))],
        out_specs=pl.BlockSpec((None, SEQ, COL_TILE), lambda i: (i, 0, 0)),
        out_shape=jax.ShapeDtypeStruct((b, SEQ, COL_TILE), BF16),
        scratch_shapes=[slab, slab, slab, slab],
        compiler_params=_params("arbitrary"),
        name="hyena_mixer",
    )(hn, w_hy, w_hy, w_hy, short_w, short_w, short_w, cf, sf, ct, st, tw, kr, ki, kr, ki, bias)
    return out.reshape(b * SEQ, COL_TILE)


def _shortconv_kernel(hn_ref, wb_ref, wc_ref, wx_ref, cw_ref, o_ref):
    wb = wb_ref[...].astype(BF16)
    wc = wc_ref[...].astype(BF16)
    wx = wx_ref[...].astype(BF16)

    def project(r0):
        lo = max(r0 - CONV_HALO, 0)
        hi = min(r0 + CONV_CHUNK + CONV_HALO, SEQ)
        hn = hn_ref[lo:hi, :]
        return (r0, r0 - lo, jnp.dot(hn, wb, preferred_element_type=F32),
                jnp.dot(hn, wc, preferred_element_type=F32), jnp.dot(hn, wx, preferred_element_type=F32))

    def finish(r0, skip, bg, cg, xi):
        out = bg * _dwconv3(cg * xi, cw_ref)
        o_ref[r0:r0 + CONV_CHUNK, :] = out[skip:skip + CONV_CHUNK].astype(BF16)

    pending = None
    for r0 in range(0, SEQ, CONV_CHUNK):
        current = project(r0)
        if pending is not None:
            finish(*pending)
        pending = current
    finish(*pending)


def _w_in_cols(layer, width, first):
    return lambda k, nt: pl.BlockSpec((None, D_MODEL, width),
                                      lambda j, i: (layer, 0, first + k * nt + j))


def _shortconv_mixer(hn, w_in, layer, conv_w):
    b = hn.shape[0]
    nt = SC_W // COL_TILE
    wcol = _w_in_cols(layer, COL_TILE, (3 * HY_W + 3 * NA_W) // COL_TILE)
    return pl.pallas_call(
        _shortconv_kernel,
        grid=(nt, b),
        in_specs=[pl.BlockSpec((None, SEQ, D_MODEL), lambda j, i: (i, 0, 0)),
                  wcol(0, nt), wcol(1, nt), wcol(2, nt),
                  pl.BlockSpec((3, COL_TILE), lambda j, i: (0, j))],
        out_specs=pl.BlockSpec((None, SEQ, COL_TILE), lambda j, i: (i, 0, j)),
        out_shape=jax.ShapeDtypeStruct((b, SEQ, SC_W), BF16),
        compiler_params=_params("arbitrary", "arbitrary"),
        name="shortconv_mixer",
    )(hn, w_in, w_in, w_in, conv_w)


def _na_kernel(hn_ref, wq_ref, wk_ref, wv_ref, bias_ref, o_ref, q_ref, k_ref, v_ref, s_ref):
    hn = hn_ref[...]
    q = jnp.dot(hn, wq_ref[...].astype(BF16), preferred_element_type=F32)
    q_ref[...] = (q * (NA_HEAD_DIM ** -0.5)).astype(BF16)
    k_ref[...] = jnp.dot(hn, wk_ref[...].astype(BF16), preferred_element_type=F32).astype(BF16)
    v_ref[...] = jnp.dot(hn, wv_ref[...].astype(BF16), preferred_element_type=F32).astype(BF16)
    gw = NA_GROUP * NA_HEAD_DIM
    same_head = (lax.broadcasted_iota(jnp.int32, (gw, gw), 0) // NA_HEAD_DIM
                 == lax.broadcasted_iota(jnp.int32, (gw, gw), 1) // NA_HEAD_DIM)

    def scores(r):
        w0 = jnp.clip(r - NA_WIN_ROWS // 2, 0, NA_ROWS - NA_WIN_ROWS)
        off = w0 - r + (NA_WIN_ROWS - 1)
        q0 = pl.multiple_of(r * GRID_W, GRID_W)
        k0 = pl.multiple_of(w0 * GRID_W, GRID_W)
        q_row = q_ref[pl.ds(q0, GRID_W), :]
        q_heads = jnp.where(same_head, jnp.concatenate([q_row] * NA_GROUP, axis=0), 0)
        s = lax.dot_general(q_heads, k_ref[pl.ds(k0, NA_KEYS), :], (((1,), (1,)), ((), ())),
                            preferred_element_type=F32)
        bias = jnp.concatenate(
            [jnp.concatenate([bias_ref[h, off + 2 * m] for m in range(NA_WIN_ROWS // 2)], axis=1)
             for h in range(NA_GROUP)], axis=0)
        return s + bias

    def attend(r, s):
        w0 = jnp.clip(r - NA_WIN_ROWS // 2, 0, NA_ROWS - NA_WIN_ROWS)
        q0 = pl.multiple_of(r * GRID_W, GRID_W)
        k0 = pl.multiple_of(w0 * GRID_W, GRID_W)
        p = jnp.exp(s - jnp.max(s, axis=-1, keepdims=True))
        inv = 1.0 / jnp.sum(p, axis=-1, keepdims=True)
        pv = jnp.dot(p.astype(BF16), v_ref[pl.ds(k0, NA_KEYS), :], preferred_element_type=F32)
        pv = jnp.where(same_head, pv * inv, 0.0)
        out = pv[0:GRID_W]
        for h in range(1, NA_GROUP):
            out = out + pv[h * GRID_W:(h + 1) * GRID_W]
        o_ref[pl.ds(q0, GRID_W), :] = out.astype(BF16)

    groups = NA_ROWS // NA_ROW_GROUP
    for t in range(NA_ROW_GROUP):
        s_ref[t] = scores(jnp.int32(t))

    def rows_body(i, carry):
        for t in range(NA_ROW_GROUP):
            s_next = scores(i * NA_ROW_GROUP + t)
            attend((i - 1) * NA_ROW_GROUP + t, s_ref[t])
            s_ref[t] = s_next
        return carry

    lax.fori_loop(1, groups, rows_body, 0)
    for t in range(NA_ROW_GROUP):
        attend(jnp.int32((groups - 1) * NA_ROW_GROUP + t), s_ref[t])


def _na_bias(rpb):
    c = jnp.arange(GRID_W)
    col_start = jnp.clip(c - NA_WIN_COLS // 2, 0, GRID_W - NA_WIN_COLS)
    col_mask = (c[None, :] >= col_start[:, None]) & (c[None, :] < col_start[:, None] + NA_WIN_COLS)
    dc = jnp.clip(c[None, :] - c[:, None] + NA_WIN_COLS - 1, 0, 2 * NA_WIN_COLS - 2)
    pick = (dc[None] == jnp.arange(2 * NA_WIN_COLS - 1)[:, None, None]).astype(F32)
    table = jnp.einsum("hrd,dqc->hrqc", rpb.astype(F32), pick, precision=lax.Precision.HIGHEST)
    table = table + jnp.where(col_mask, 0.0, -1e30)[None, None]
    return jnp.concatenate([table[:, :-1], table[:, 1:]], axis=-1)


def _na_mixer(hn, w_in, layer, bias):
    b = hn.shape[0]
    gw = NA_GROUP * NA_HEAD_DIM
    ng = NA_W // gw
    wcol = _w_in_cols(layer, gw, 3 * HY_W // gw)
    return pl.pallas_call(
        _na_kernel,
        grid=(ng, b),
        in_specs=[pl.BlockSpec((None, SEQ, D_MODEL), lambda j, i: (i, 0, 0)),
                  wcol(0, ng), wcol(1, ng), wcol(2, ng),
                  pl.BlockSpec((NA_GROUP, 2 * NA_WIN_ROWS - 2, GRID_W, 2 * GRID_W),
                               lambda j, i: (j, 0, 0, 0))],
        out_specs=pl.BlockSpec((None, SEQ, gw), lambda j, i: (i, 0, j)),
        out_shape=jax.ShapeDtypeStruct((b, SEQ, NA_W), BF16),
        scratch_shapes=[pltpu.VMEM((SEQ, gw), BF16) for _ in range(3)]
        + [pltpu.VMEM((NA_ROW_GROUP, gw, NA_KEYS), F32)],
        compiler_params=_params("arbitrary", "arbitrary"),
        name="na_mixer",
    )(hn, w_in, w_in, w_in, bias)


def _merge_kernel(hn_ref, x_ref, ya0_ref, ya1_ref, yb_ref, yc_ref, wg_ref, gb_ref, wb_ref, wo_ref,
                  g_ref, o_ref):
    def gated_sum(rows):
        hn = hn_ref[rows, :]
        ya = jnp.concatenate([ya0_ref[rows, :], ya1_ref[rows, :]], axis=1)
        merged = None
        for i, y in enumerate((ya, yb_ref[rows, :], yc_ref[rows, :])):
            pre = jnp.dot(hn, wg_ref[:, i * D_MODEL:(i + 1) * D_MODEL], preferred_element_type=F32)
            gate = jax.nn.sigmoid(pre + gb_ref[i:i + 1, :])
            term = gate * jnp.dot(y, wb_ref[i], preferred_element_type=F32)
            merged = term if merged is None else merged + term
        return merged.astype(BF16)

    chunks = [slice(r, r + ROW_TILE) for r in range(0, MERGE_ROWS, ROW_TILE)]
    merged = [gated_sum(rows) for rows in chunks]
    for rows, m in zip(chunks, merged):
        out = jnp.dot(m, wo_ref[...], preferred_element_type=F32)
        o_ref[rows, :] = x_ref[rows, :] + _rms(out, g_ref[...])


def _merge(hn2d, x2d, ya0, ya1, yb, yc, w_gate, gate_bias, w_branch, w_out, g):
    n = x2d.shape[0]
    tm = MERGE_ROWS
    once = pl.Buffered(1)
    rows = lambda w: pl.BlockSpec((tm, w), lambda i: (i, 0))
    return pl.pallas_call(
        _merge_kernel,
        grid=(n // tm,),
        in_specs=[rows(D_MODEL), rows(D_MODEL), rows(COL_TILE), rows(COL_TILE), rows(NA_W), rows(SC_W),
                  pl.BlockSpec((D_MODEL, N_BRANCH * D_MODEL), lambda i: (0, 0), pipeline_mode=once),
                  pl.BlockSpec((N_BRANCH, D_MODEL), lambda i: (0, 0)),
                  pl.BlockSpec((N_BRANCH, HY_W, D_MODEL), lambda i: (0, 0, 0), pipeline_mode=once),
                  pl.BlockSpec((D_MODEL, D_MODEL), lambda i: (0, 0), pipeline_mode=once),
                  pl.BlockSpec((1, D_MODEL), lambda i: (0, 0))],
        out_specs=rows(D_MODEL),
        out_shape=jax.ShapeDtypeStruct((n, D_MODEL), F32),
        compiler_params=_params("arbitrary"),
        name="merge",
    )(hn2d, x2d, ya0, ya1, yb, yc, w_gate, gate_bias, w_branch, w_out, g)


def _kv_kernel(m_ref, g_ref, w_ref, o_ref):
    mn = _rms(m_ref[...], g_ref[...]).astype(BF16)
    o_ref[...] = jnp.dot(mn, w_ref[...], preferred_element_type=F32).astype(BF16)


def _mem_kv(mem, g, wkv):
    b = mem.shape[0]
    return pl.pallas_call(
        _kv_kernel,
        grid=(b,),
        in_specs=[pl.BlockSpec((None, N_MEM, D_MODEL), lambda i: (i, 0, 0)),
                  pl.BlockSpec((1, D_MODEL), lambda i: (0, 0)),
                  pl.BlockSpec((D_MODEL, 2 * D_MODEL), lambda i: (0, 0))],
        out_specs=pl.BlockSpec((None, N_MEM, 2 * D_MODEL), lambda i: (i, 0, 0)),
        out_shape=jax.ShapeDtypeStruct((b, N_MEM, 2 * D_MODEL), BF16),
        compiler_params=_params("arbitrary"),
        name="mem_kv",
    )(mem, g, wkv)


def _xattn_kernel(x_ref, kv_ref, wq_ref, wo_ref, gq_ref, go_ref, gn_ref, o_ref, hn_ref):
    chunks = [slice(r, r + ROW_TILE) for r in range(0, XA_ROWS, ROW_TILE)]
    head_cols = [slice(i * XA_HEAD_DIM, (i + 1) * XA_HEAD_DIM) for i in range(XA_HEADS)]

    def query(rows):
        h = _rms(x_ref[rows, :], gq_ref[...]).astype(BF16)
        q = jnp.dot(h, wq_ref[...], preferred_element_type=F32) * (XA_HEAD_DIM ** -0.5)
        return q.astype(BF16)

    def scores(q):
        return [lax.dot_general(q[:, sl], kv_ref[:, sl], (((1,), (1,)), ((), ())),
                                preferred_element_type=F32) for sl in head_cols]

    def values(s_heads):
        heads = []
        for i, s in enumerate(s_heads):
            vm = kv_ref[:, D_MODEL + i * XA_HEAD_DIM:D_MODEL + (i + 1) * XA_HEAD_DIM]
            p = jnp.exp(s - jnp.max(s, axis=-1, keepdims=True))
            den = jnp.sum(p, axis=-1, keepdims=True)
            heads.append((jnp.dot(p.astype(BF16), vm, preferred_element_type=F32) / den).astype(BF16))
        return jnp.concatenate(heads, axis=-1)

    s_all = [scores(q) for q in [query(rows) for rows in chunks]]
    attended = [values(s) for s in s_all]
    for rows, a in zip(chunks, attended):
        o = jnp.dot(a, wo_ref[...], preferred_element_type=F32)
        xn = x_ref[rows, :] + _rms(o, go_ref[...])
        o_ref[rows, :] = xn
        hn_ref[rows, :] = _rms(xn, gn_ref[...]).astype(BF16)


def _xattn(x, kv, wq, wo, gq, go, gn):
    b = x.shape[0]
    tm = XA_ROWS
    once = pl.Buffered(1)
    rows = pl.BlockSpec((None, tm, D_MODEL), lambda i, j: (i, j, 0))
    gain = pl.BlockSpec((1, D_MODEL), lambda i, j: (0, 0))
    wfull = pl.BlockSpec((D_MODEL, D_MODEL), lambda i, j: (0, 0), pipeline_mode=once)
    return pl.pallas_call(
        _xattn_kernel,
        grid=(b, SEQ // tm),
        in_specs=[rows, pl.BlockSpec((None, N_MEM, 2 * D_MODEL), lambda i, j: (i, 0, 0)),
                  wfull, wfull, gain, gain, gain],
        out_specs=(rows, rows),
        out_shape=(jax.ShapeDtypeStruct((b, SEQ, D_MODEL), F32),
                   jax.ShapeDtypeStruct((b, SEQ, D_MODEL), BF16)),
        compiler_params=_params("arbitrary", "arbitrary"),
        name="xattn",
    )(x, kv, wq, wo, gq, go, gn)


def _gelu_tanh(x):
    c = math.sqrt(2.0 / math.pi)
    half = 0.5 * x
    return half + half * jnp.tanh(x * (c + (c * 0.044715) * (x * x)))


def _ffn_kernel(hn_ref, wg_ref, wv_ref, cg_ref, cv_ref, wd_ref, o_ref):
    wg = wg_ref[...]
    wv = wv_ref[...]
    wd = wd_ref[...]

    @pl.when(pl.program_id(1) == 0)
    def _():
        o_ref[...] = jnp.zeros_like(o_ref)

    def up(r0):
        lo = max(r0 - CONV_HALO, 0)
        hi = min(r0 + CONV_CHUNK + CONV_HALO, SEQ)
        hn = hn_ref[lo:hi, :]
        return (r0, r0 - lo, jnp.dot(hn, wg, preferred_element_type=F32),
                jnp.dot(hn, wv, preferred_element_type=F32))

    def down(r0, skip, ug, uv):
        act = _gelu_tanh(_dwconv3(ug, cg_ref)) * _dwconv3(uv, cv_ref)
        act = act[skip:skip + CONV_CHUNK].astype(BF16)
        o_ref[r0:r0 + CONV_CHUNK, :] += jnp.dot(act, wd, preferred_element_type=F32)

    pending = None
    for r0 in range(0, SEQ, CONV_CHUNK):
        current = up(r0)
        if pending is not None:
            down(*pending)
        pending = current
    down(*pending)


def _ffn(hn, w_up, w_conv, w_down):
    b = hn.shape[0]
    tile = D_FF // FFN_SPLIT
    once = pl.Buffered(1)
    return pl.pallas_call(
        _ffn_kernel,
        grid=(b, FFN_SPLIT),
        in_specs=[pl.BlockSpec((None, SEQ, D_MODEL), lambda i, k: (i, 0, 0), pipeline_mode=once),
                  pl.BlockSpec((D_MODEL, tile), lambda i, k: (0, k)),
                  pl.BlockSpec((D_MODEL, tile), lambda i, k: (0, FFN_SPLIT + k)),
                  pl.BlockSpec((3, tile), lambda i, k: (0, k)),
                  pl.BlockSpec((3, tile), lambda i, k: (0, FFN_SPLIT + k)),
                  pl.BlockSpec((tile, D_MODEL), lambda i, k: (k, 0))],
        out_specs=pl.BlockSpec((None, SEQ, D_MODEL), lambda i, k: (i, 0, 0), pipeline_mode=once),
        out_shape=jax.ShapeDtypeStruct((b, SEQ, D_MODEL), F32),
        compiler_params=_params("arbitrary", "arbitrary"),
        name="ffn",
    )(hn, w_up, w_up, w_conv, w_conv, w_down)


def _residual_kernel(x_ref, f_ref, g_ref, gn_ref, o_ref, hn_ref):
    xn = x_ref[...] + _rms(f_ref[...], g_ref[...])
    o_ref[...] = xn
    hn_ref[...] = _rms(xn, gn_ref[...]).astype(BF16)


def _residual_last_kernel(x_ref, f_ref, g_ref, o_ref):
    o_ref[...] = x_ref[...] + _rms(f_ref[...], g_ref[...])


def _residual(x2d, f2d, g, gn):
    n = x2d.shape[0]
    tm = 1024
    rows = pl.BlockSpec((tm, D_MODEL), lambda i: (i, 0))
    gain = pl.BlockSpec((1, D_MODEL), lambda i: (0, 0))
    x_shape = jax.ShapeDtypeStruct((n, D_MODEL), F32)
    if gn is None:
        return pl.pallas_call(
            _residual_last_kernel,
            grid=(n // tm,),
            in_specs=[rows, rows, gain],
            out_specs=rows,
            out_shape=x_shape,
            compiler_params=_params("arbitrary"),
            name="residual_last",
        )(x2d, f2d, g), None
    return pl.pallas_call(
        _residual_kernel,
        grid=(n // tm,),
        in_specs=[rows, rows, gain, gain],
        out_specs=(rows, rows),
        out_shape=(x_shape, jax.ShapeDtypeStruct((n, D_MODEL), BF16)),
        compiler_params=_params("arbitrary"),
        name="residual",
    )(x2d, f2d, g, gn)


def _angle_tables(num, den):
    ang = (num % den).astype(F32) * (2.0 * math.pi / den)
    return jnp.cos(ang), jnp.sin(ang)


def _dft_tables():
    j = jnp.arange(SEQ, dtype=jnp.int32)
    rows = SEQ // FFT_RADIX
    q = rows // DFT_SPLIT
    up = jnp.arange(q, dtype=jnp.int32)
    a = jnp.concatenate([up, 2 * q + up, 2 * q - 1 - up, 4 * q - 1 - up]).reshape(2, 2 * q, 1, 1)
    b_up = jnp.arange(DFT_SPLIT, dtype=jnp.int32)
    b = jnp.stack([b_up, DFT_SPLIT - 1 - b_up]).reshape(2, 1, DFT_SPLIT, 1)
    ca, sa = _angle_tables(DFT_SPLIT * a * j, FFT_N)
    cb, sb = _angle_tables((2 * b + 1) * j, 2 * FFT_N)
    big_c = (ca * cb - sa * sb).reshape(SEQ, SEQ).astype(BF16)
    big_s = (sa * cb + ca * sb).reshape(SEQ, SEQ).astype(BF16)
    kappa = jnp.arange(rows, dtype=jnp.int32)[:, None]
    m = jnp.arange(rows, dtype=jnp.int32)[None, :]
    cf, sf = _angle_tables((2 * kappa + 1) * m, 2 * FFT_SUB)
    r = jnp.arange(1, FFT_RADIX, dtype=jnp.int32)[:, None]
    tc, ts = _angle_tables((2 * kappa.T + 1) * r, 2 * FFT_N)
    tw = jnp.broadcast_to(jnp.concatenate([tc, ts])[:, :, None], (2 * (FFT_RADIX - 1), rows, COL_TILE))
    small = (cf.astype(BF16), sf.astype(BF16), cf.T.astype(BF16), sf.T.astype(BF16), tw)
    return big_c, big_s, small


def kernel(x, mem, norm_gains, mem_norm, w_in, gate_bias, hy_short_w, hy_w1, hy_b1, hy_w2, hy_b2,
           hy_w3, hy_freq, hy_bias, na_rpb, sc_conv_w, w_branch, w_out, xa_wq, xa_wkv, xa_wo,
           ffn_up, ffn_conv, ffn_down):
    b, l, d = x.shape
    depth = w_in.shape[0]
    assert (l, d) == (SEQ, D_MODEL) and mem.shape[1:] == (N_MEM, D_MODEL)
    n = b * l
    dft_c, dft_s, conv_tables = _dft_tables()
    gains = norm_gains.astype(F32)
    x2d = x.reshape(n, d)
    hn = _prenorm(x2d, gains[0, 0][None])
    for i in range(depth):
        g = gains[i]
        w_hyena = w_in[i, :, :3 * HY_W].astype(BF16)
        w_gate = w_in[i, :, 3 * HY_W + 3 * NA_W + 3 * SC_W:].astype(BF16)
        kr, ki = _hyena_filters(dft_c, dft_s, hy_w1[i], hy_b1[i], hy_w2[i], hy_b2[i], hy_w3[i],
                                hy_freq[i])
        hn3 = hn.reshape(b, l, d)
        ya = [_hyena_mixer(hn3, w_hyena, hy_short_w[i].astype(F32), conv_tables, kr, ki,
                           hy_bias[i].astype(F32), tile) for tile in range(HY_W // COL_TILE)]
        yb = _na_mixer(hn3, w_in, i, _na_bias(na_rpb[i]))
        yc = _shortconv_mixer(hn3, w_in, i, sc_conv_w[i].astype(F32))
        x2d = _merge(hn, x2d, ya[0], ya[1], yb.reshape(n, NA_W), yc.reshape(n, SC_W), w_gate,
                     gate_bias[i].astype(F32), w_branch[i].astype(BF16), w_out[i].astype(BF16), g[1][None])
        kv = _mem_kv(mem, mem_norm[i].astype(F32)[None], xa_wkv[i].astype(BF16))
        x3, hn2 = _xattn(x2d.reshape(b, l, d), kv, xa_wq[i].astype(BF16), xa_wo[i].astype(BF16),
                         g[2][None], g[3][None], g[4][None])
        f = _ffn(hn2, ffn_up[i].astype(BF16), ffn_conv[i].astype(F32), ffn_down[i].astype(BF16))
        g_next = gains[i + 1, 0][None] if i + 1 < depth else None
        x2d, hn = _residual(x3.reshape(n, d), f.reshape(n, d), g[5][None], g_next)
    return x2d.reshape(b, l, d)
```

```python
import functools
import math

import jax
import jax.numpy as jnp
from jax import lax
from jax.experimental import pallas as pl
from jax.experimental.pallas import tpu as pltpu

D_MODEL = 1024
SEQ = 2048
N_MEM = 256
GRID_W = 64
HY_W = 512
NA_HEADS = 8
NA_HEAD_DIM = 64
NA_W = NA_HEADS * NA_HEAD_DIM
NA_WIN_ROWS = 8
NA_WIN_COLS = 16
SC_W = 512
XA_HEADS = 4
XA_HEAD_DIM = D_MODEL // XA_HEADS
D_FF = 2816
HY_ORDER = 2
HY_EMB = 33
HY_HIDDEN = 64
HY_FAST_DECAY = 0.3
HY_SLOW_DECAY = 1.5
HY_TARGET = 1e-2
N_BRANCH = 3
EPS = 1e-6

FFT_N = 2 * SEQ
FFT_RADIX = 4
FFT_SUB = FFT_N // FFT_RADIX
NA_ROWS = SEQ // GRID_W
NA_KEYS = NA_WIN_ROWS * GRID_W
NA_GROUP = 4
NA_ROW_GROUP = 4
LANES = 128
COL_TILE = 256
ROW_TILE = 512
MERGE_ROWS = 2 * ROW_TILE
XA_ROWS = 2 * ROW_TILE
CONV_PAD = 8
DFT_SPLIT = 64
CONV_CHUNK = 512
FFN_CUTS = (0, 6 * COL_TILE, D_FF)
CONV_HALO = 16
VMEM_LIMIT = 60 * 1024 * 1024

BF16 = jnp.bfloat16
F32 = jnp.float32


def _params(*sem):
    return pltpu.CompilerParams(dimension_semantics=sem, vmem_limit_bytes=VMEM_LIMIT)


def _rms(xf, g):
    ms = jnp.mean(xf * xf, axis=-1, keepdims=True)
    return xf * lax.rsqrt(ms + EPS) * g


def _dwconv3(u, w_ref):
    n = u.shape[0]
    zeros = jnp.zeros((CONV_PAD, u.shape[1]), F32)
    padded = jnp.concatenate([zeros, u, zeros], axis=0)
    m = n + 2 * CONV_PAD
    prev = pltpu.roll(padded, 1, 0)[CONV_PAD:CONV_PAD + n]
    nxt = pltpu.roll(padded, m - 1, 0)[CONV_PAD:CONV_PAD + n]
    return prev * w_ref[0:1, :] + u * w_ref[1:2, :] + nxt * w_ref[2:3, :]


def _prenorm_kernel(x_ref, g_ref, o_ref):
    o_ref[...] = _rms(x_ref[...], g_ref[...]).astype(BF16)


def _prenorm(x2d, g):
    n = x2d.shape[0]
    tm = 1024
    return pl.pallas_call(
        _prenorm_kernel,
        grid=(n // tm,),
        in_specs=[pl.BlockSpec((tm, D_MODEL), lambda i: (i, 0)),
                  pl.BlockSpec((1, D_MODEL), lambda i: (0, 0))],
        out_specs=pl.BlockSpec((tm, D_MODEL), lambda i: (i, 0)),
        out_shape=jax.ShapeDtypeStruct((n, D_MODEL), BF16),
        compiler_params=_params("arbitrary"),
        name="prenorm",
    )(x2d, g)


def _filter_mlp_kernel(z_ref, w1_ref, b1_ref, w2_ref, b2_ref, w3_ref, f_ref, t_ref, dl_ref,
                       hs_ref, hd_ref):
    hp = lax.Precision.HIGHEST
    h = jnp.sin(f_ref[0:1, :] * (jnp.dot(z_ref[...], w1_ref[...], precision=hp) + b1_ref[...]))
    h = jnp.sin(f_ref[1:2, :] * (jnp.dot(h, w2_ref[...], precision=hp) + b2_ref[...]))
    decay = jnp.exp(-t_ref[...] * dl_ref[...])
    row = lax.broadcasted_iota(jnp.int32, (SEQ, HY_W), 0)
    for o in range(HY_ORDER):
        c_f = o * HY_W
        c_b = HY_ORDER * HY_W + o * HY_W
        hf = jnp.dot(h, w3_ref[:, c_f:c_f + HY_W], precision=hp) * decay
        hb = jnp.dot(h, w3_ref[:, c_b:c_b + HY_W], precision=hp) * decay
        hb = jnp.where(row == 0, 0.0, hb)
        hs_ref[:, c_f:c_f + HY_W] = (hf + hb).astype(BF16)
        hd_ref[:, c_f:c_f + HY_W] = (hb - hf).astype(BF16)


def _filter_dft_kernel(c_ref, s_ref, hs_ref, hd_ref, kr_ref, ki_ref):
    kr_ref[...] = jnp.dot(c_ref[...], hs_ref[...], preferred_element_type=F32) * (2.0 / FFT_N)
    ki_ref[...] = jnp.dot(s_ref[...], hd_ref[...], preferred_element_type=F32) * (2.0 / FFT_N)


def _hyena_filters(dft_c, dft_s, w1, b1, w2, b2, w3, freq):
    t = jnp.linspace(0.0, 1.0, SEQ, dtype=F32)[:, None]
    bands = (HY_EMB - 1) // 2
    w = 2.0 * math.pi * jnp.arange(SEQ, dtype=F32)[:, None] / SEQ
    f = jnp.linspace(1e-4, bands - 1, bands, dtype=F32)[None, :]
    z = jnp.concatenate([t, jnp.cos(f * w), -jnp.sin(f * w)], axis=-1)
    z = jnp.pad(z, ((0, 0), (0, HY_HIDDEN - HY_EMB)))
    w1p = jnp.pad(w1.astype(F32), ((0, HY_HIDDEN - HY_EMB), (0, 0)))
    deltas = jnp.abs(jnp.linspace(math.log(HY_TARGET) / HY_SLOW_DECAY,
                                  math.log(HY_TARGET) / HY_FAST_DECAY, HY_W, dtype=F32))[None, :]
    width = HY_ORDER * HY_W
    hs, hd = pl.pallas_call(
        _filter_mlp_kernel,
        out_shape=(jax.ShapeDtypeStruct((SEQ, width), BF16),
                   jax.ShapeDtypeStruct((SEQ, width), BF16)),
        compiler_params=pltpu.CompilerParams(vmem_limit_bytes=VMEM_LIMIT),
        name="hyena_filter_mlp",
    )(z, w1p, b1[None].astype(F32), w2.astype(F32), b2[None].astype(F32), w3.astype(F32),
      freq.astype(F32), t, deltas)
    nt = width // COL_TILE
    full = pl.BlockSpec((SEQ, SEQ), lambda j: (0, 0))
    col = pl.BlockSpec((SEQ, COL_TILE), lambda j: (0, j))
    kr, ki = pl.pallas_call(
        _filter_dft_kernel,
        grid=(nt,),
        in_specs=[full, full, col, col],
        out_specs=(col, col),
        out_shape=(jax.ShapeDtypeStruct((SEQ, width), F32),
                   jax.ShapeDtypeStruct((SEQ, width), F32)),
        compiler_params=_params("arbitrary"),
        name="hyena_filter_dft",
    )(dft_c, dft_s, hs, hd)

    fam_shape = (FFT_RADIX, SEQ // FFT_RADIX, width)
    return kr.reshape(fam_shape), ki.reshape(fam_shape)


def _hyena_kernel(hn_ref, wv_ref, w1_ref, w2_ref, sv_ref, s1_ref, s2_ref, cf_ref, sf_ref, ct_ref,
                  st_ref, kr0_ref, ki0_ref, kr1_ref, ki1_ref, bias_ref, o_ref,
                  slab_v_ref, slab_1_ref, slab_2_ref, slab_o_ref):
    radix = FFT_RADIX
    rows = SEQ // radix
    n_slab = COL_TILE // LANES
    zero_row = jnp.zeros((CONV_PAD, COL_TILE), F32)

    def project(w_ref, slab_ref):
        u = jnp.dot(hn_ref[...], w_ref[...], preferred_element_type=F32)
        for j in range(n_slab):
            slab_ref[j] = u[:, j * LANES:(j + 1) * LANES]
        return [jnp.concatenate([slab_ref[j, pl.ds(r, rows, stride=radix), :] for j in range(n_slab)],
                                axis=1) for r in range(radix)]

    def short_conv(u, w_ref):
        n = u[0].shape[0]
        prev_wrap = pltpu.roll(jnp.concatenate([u[-1], zero_row], axis=0), 1, 0)[:n]
        next_wrap = pltpu.roll(jnp.concatenate([zero_row, u[0]], axis=0), n + CONV_PAD - 1, 0)[CONV_PAD:]
        prev = [prev_wrap] + u[:-1]
        nxt = u[1:] + [next_wrap]
        return [prev[r] * w_ref[0:1, :] + u[r] * w_ref[1:2, :] + nxt[r] * w_ref[2:3, :]
                for r in range(radix)]

    def cmul(ar, ai, br, bi):
        return ar * br - ai * bi, ar * bi + ai * br

    def transform(x):
        xb = [v.astype(BF16) for v in x]
        tr = [jnp.dot(cf_ref[r], xb[r], preferred_element_type=F32) for r in range(radix)]
        ti = [-jnp.dot(sf_ref[r], xb[r], preferred_element_type=F32) for r in range(radix)]
        return tr, ti

    def filter_and_invert(tr, ti, kr_ref, ki_ref):
        ar, ai = tr[0] + tr[2], ti[0] + ti[2]
        br, bi = tr[0] - tr[2], ti[0] - ti[2]
        cr, ci = tr[1] + tr[3], ti[1] + ti[3]
        dr, di = tr[1] - tr[3], ti[1] - ti[3]
        fam = [(ar + cr, ai + ci), (br + di, bi - dr), (br - di, -bi - dr), (ar - cr, ci - ai)]
        y = [cmul(fr, fi, kr_ref[f], ki_ref[f]) for f, (fr, fi) in enumerate(fam)]
        er, ei = y[0][0] + y[3][0], y[0][1] - y[3][1]
        fr, fi = y[0][0] - y[3][0], y[0][1] + y[3][1]
        gr, gi = y[1][0] + y[2][0], y[1][1] - y[2][1]
        hr, hi = y[1][0] - y[2][0], y[1][1] + y[2][1]
        p = [(er + gr, ei + gi), (fr - hi, fi + hr), (er - gr, ei - gi), (fr + hi, fi - hr)]
        out = []
        for r in range(radix):
            pr, pi_ = p[r]
            out.append(jnp.dot(ct_ref[r], pr.astype(BF16), preferred_element_type=F32)
                       - jnp.dot(st_ref[r], pi_.astype(BF16), preferred_element_type=F32))
        return out

    v = short_conv(project(wv_ref, slab_v_ref), sv_ref)
    spectrum = transform(v)
    x1 = short_conv(project(w1_ref, slab_1_ref), s1_ref)
    x2 = short_conv(project(w2_ref, slab_2_ref), s2_ref)
    y = filter_and_invert(*spectrum, kr0_ref, ki0_ref)
    z = [x1[r] * (y[r] + v[r] * bias_ref[0:1, :]) for r in range(radix)]
    y = filter_and_invert(*transform(z), kr1_ref, ki1_ref)
    for r in range(radix):
        out = x2[r] * (y[r] + z[r] * bias_ref[1:2, :])
        for j in range(n_slab):
            slab_o_ref[j, pl.ds(r, rows, stride=radix), :] = out[:, j * LANES:(j + 1) * LANES]
    o_ref[...] = jnp.concatenate([slab_o_ref[j] for j in range(n_slab)], axis=1).astype(BF16)


def _hyena_mixer(hn, w_hy, short_w, tables, kr, ki, bias, tile):
    b = hn.shape[0]
    nt = HY_W // COL_TILE
    rows = SEQ // FFT_RADIX
    once = pl.Buffered(1)
    cf, sf, ct, st = tables

    def fixed(shape, index):
        return pl.BlockSpec(shape, lambda i: index, pipeline_mode=once)

    def wcol(k):
        return fixed((D_MODEL, COL_TILE), (0, k * nt + tile))

    def scol(k):
        return fixed((3, COL_TILE), (0, k * nt + tile))

    def kfam(o):
        return fixed((FFT_RADIX, rows, COL_TILE), (0, 0, o * nt + tile))

    small = fixed((FFT_RADIX, rows, rows), (0, 0, 0))
    slab = pltpu.VMEM((COL_TILE // LANES, SEQ, LANES), F32)
    out = pl.pallas_call(
        _hyena_kernel,
        grid=(b,),
        in_specs=[pl.BlockSpec((None, SEQ, D_MODEL), lambda i: (i, 0, 0)),
                  wcol(0), wcol(1), wcol(2), scol(0), scol(1), scol(2),
                  small, small, small, small,
                  kfam(0), kfam(0), kfam(1), kfam(1),
                  fixed((HY_ORDER, COL_TILE), (0, tile))],
        out_specs=pl.BlockSpec((None, SEQ, COL_TILE), lambda i: (i, 0, 0)),
        out_shape=jax.ShapeDtypeStruct((b, SEQ, COL_TILE), BF16),
        scratch_shapes=[slab, slab, slab, slab],
        compiler_params=_params("arbitrary"),
        name="hyena_mixer",
    )(hn, w_hy, w_hy, w_hy, short_w, short_w, short_w, cf, sf, ct, st, kr, ki, kr, ki, bias)
    return out.reshape(b * SEQ, COL_TILE)


def _shortconv_kernel(hn_ref, wb_ref, wc_ref, wx_ref, cw_ref, o_ref):
    wb = wb_ref[...].astype(BF16)
    wc = wc_ref[...].astype(BF16)
    wx = wx_ref[...].astype(BF16)

    def project(r0):
        lo = max(r0 - CONV_HALO, 0)
        hi = min(r0 + CONV_CHUNK + CONV_HALO, SEQ)
        hn = hn_ref[lo:hi, :]
        return (r0, r0 - lo, jnp.dot(hn, wb, preferred_element_type=F32),
                jnp.dot(hn, wc, preferred_element_type=F32), jnp.dot(hn, wx, preferred_element_type=F32))

    def finish(r0, skip, bg, cg, xi):
        out = bg * _dwconv3(cg * xi, cw_ref)
        o_ref[r0:r0 + CONV_CHUNK, :] = out[skip:skip + CONV_CHUNK].astype(BF16)

    pending = None
    for r0 in range(0, SEQ, CONV_CHUNK):
        current = project(r0)
        if pending is not None:
            finish(*pending)
        pending = current
    finish(*pending)


def _w_in_cols(layer, width, first):
    return lambda k, nt: pl.BlockSpec((None, D_MODEL, width),
                                      lambda j, i: (layer, 0, first + k * nt + j))


def _shortconv_mixer(hn, w_in, layer, conv_w):
    b = hn.shape[0]
    nt = SC_W // COL_TILE
    wcol = _w_in_cols(layer, COL_TILE, (3 * HY_W + 3 * NA_W) // COL_TILE)
    return pl.pallas_call(
        _shortconv_kernel,
        grid=(nt, b),
        in_specs=[pl.BlockSpec((None, SEQ, D_MODEL), lambda j, i: (i, 0, 0)),
                  wcol(0, nt), wcol(1, nt), wcol(2, nt),
                  pl.BlockSpec((3, COL_TILE), lambda j, i: (0, j))],
        out_specs=pl.BlockSpec((None, SEQ, COL_TILE), lambda j, i: (i, 0, j)),
        out_shape=jax.ShapeDtypeStruct((b, SEQ, SC_W), BF16),
        compiler_params=_params("arbitrary", "arbitrary"),
        name="shortconv_mixer",
    )(hn, w_in, w_in, w_in, conv_w)


def _na_kernel(hn_ref, wq_ref, wk_ref, wv_ref, bias_ref, o_ref, q_ref, k_ref, v_ref, s_ref):
    hn = hn_ref[...]
    q = jnp.dot(hn, wq_ref[...].astype(BF16), preferred_element_type=F32)
    q_ref[...] = (q * (NA_HEAD_DIM ** -0.5)).astype(BF16)
    k_ref[...] = jnp.dot(hn, wk_ref[...].astype(BF16), preferred_element_type=F32).astype(BF16)
    v_ref[...] = jnp.dot(hn, wv_ref[...].astype(BF16), preferred_element_type=F32).astype(BF16)
    gw = NA_GROUP * NA_HEAD_DIM
    same_head = (lax.broadcasted_iota(jnp.int32, (gw, gw), 0) // NA_HEAD_DIM
                 == lax.broadcasted_iota(jnp.int32, (gw, gw), 1) // NA_HEAD_DIM)

    def scores(r):
        w0 = jnp.clip(r - NA_WIN_ROWS // 2, 0, NA_ROWS - NA_WIN_ROWS)
        off = w0 - r + (NA_WIN_ROWS - 1)
        q0 = pl.multiple_of(r * GRID_W, GRID_W)
        k0 = pl.multiple_of(w0 * GRID_W, GRID_W)
        q_row = q_ref[pl.ds(q0, GRID_W), :]
        q_heads = jnp.where(same_head, jnp.concatenate([q_row] * NA_GROUP, axis=0), 0)
        s = lax.dot_general(q_heads, k_ref[pl.ds(k0, NA_KEYS), :], (((1,), (1,)), ((), ())),
                            preferred_element_type=F32)
        bias = jnp.concatenate(
            [jnp.concatenate([bias_ref[h, off + 2 * m] for m in range(NA_WIN_ROWS // 2)], axis=1)
             for h in range(NA_GROUP)], axis=0)
        return s + bias

    def attend(r, s):
        w0 = jnp.clip(r - NA_WIN_ROWS // 2, 0, NA_ROWS - NA_WIN_ROWS)
        q0 = pl.multiple_of(r * GRID_W, GRID_W)
        k0 = pl.multiple_of(w0 * GRID_W, GRID_W)
        p = jnp.exp(s - jnp.max(s, axis=-1, keepdims=True))
        inv = 1.0 / jnp.sum(p, axis=-1, keepdims=True)
        pv = jnp.dot(p.astype(BF16), v_ref[pl.ds(k0, NA_KEYS), :], preferred_element_type=F32)
        pv = jnp.where(same_head, pv * inv, 0.0)
        out = pv[0:GRID_W]
        for h in range(1, NA_GROUP):
            out = out + pv[h * GRID_W:(h + 1) * GRID_W]
        o_ref[pl.ds(q0, GRID_W), :] = out.astype(BF16)

    groups = NA_ROWS // NA_ROW_GROUP
    for t in range(NA_ROW_GROUP):
        s_ref[t] = scores(jnp.int32(t))

    def rows_body(i, carry):
        for t in range(NA_ROW_GROUP):
            s_next = scores(i * NA_ROW_GROUP + t)
            attend((i - 1) * NA_ROW_GROUP + t, s_ref[t])
            s_ref[t] = s_next
        return carry

    lax.fori_loop(1, groups, rows_body, 0)
    for t in range(NA_ROW_GROUP):
        attend(jnp.int32((groups - 1) * NA_ROW_GROUP + t), s_ref[t])


def _na_bias(rpb):
    c = jnp.arange(GRID_W)
    col_start = jnp.clip(c - NA_WIN_COLS // 2, 0, GRID_W - NA_WIN_COLS)
    col_mask = (c[None, :] >= col_start[:, None]) & (c[None, :] < col_start[:, None] + NA_WIN_COLS)
    dc = jnp.clip(c[None, :] - c[:, None] + NA_WIN_COLS - 1, 0, 2 * NA_WIN_COLS - 2)
    pick = (dc[None] == jnp.arange(2 * NA_WIN_COLS - 1)[:, None, None]).astype(F32)
    table = jnp.einsum("hrd,dqc->hrqc", rpb.astype(F32), pick, precision=lax.Precision.HIGHEST)
    table = table + jnp.where(col_mask, 0.0, -1e30)[None, None]
    return jnp.concatenate([table[:, :-1], table[:, 1:]], axis=-1)


def _na_mixer(hn, w_in, layer, bias):
    b = hn.shape[0]
    gw = NA_GROUP * NA_HEAD_DIM
    ng = NA_W // gw
    wcol = _w_in_cols(layer, gw, 3 * HY_W // gw)
    return pl.pallas_call(
        _na_kernel,
        grid=(ng, b),
        in_specs=[pl.BlockSpec((None, SEQ, D_MODEL), lambda j, i: (i, 0, 0)),
                  wcol(0, ng), wcol(1, ng), wcol(2, ng),
                  pl.BlockSpec((NA_GROUP, 2 * NA_WIN_ROWS - 2, GRID_W, 2 * GRID_W),
                               lambda j, i: (j, 0, 0, 0))],
        out_specs=pl.BlockSpec((None, SEQ, gw), lambda j, i: (i, 0, j)),
        out_shape=jax.ShapeDtypeStruct((b, SEQ, NA_W), BF16),
        scratch_shapes=[pltpu.VMEM((SEQ, gw), BF16) for _ in range(3)]
        + [pltpu.VMEM((NA_ROW_GROUP, gw, NA_KEYS), F32)],
        compiler_params=_params("arbitrary", "arbitrary"),
        name="na_mixer",
    )(hn, w_in, w_in, w_in, bias)


def _merge_kernel(hn_ref, x_ref, ya0_ref, ya1_ref, yb_ref, yc_ref, wg_ref, gb_ref, wb_ref, wo_ref,
                  g_ref, o_ref):
    def gated_sum(rows):
        hn = hn_ref[rows, :]
        ya = jnp.concatenate([ya0_ref[rows, :], ya1_ref[rows, :]], axis=1)
        merged = None
        for i, y in enumerate((ya, yb_ref[rows, :], yc_ref[rows, :])):
            pre = jnp.dot(hn, wg_ref[:, i * D_MODEL:(i + 1) * D_MODEL], preferred_element_type=F32)
            gate = jax.nn.sigmoid(pre + gb_ref[i:i + 1, :])
            term = gate * jnp.dot(y, wb_ref[i], preferred_element_type=F32)
            merged = term if merged is None else merged + term
        return merged.astype(BF16)

    chunks = [slice(r, r + ROW_TILE) for r in range(0, MERGE_ROWS, ROW_TILE)]
    merged = [gated_sum(rows) for rows in chunks]
    for rows, m in zip(chunks, merged):
        out = jnp.dot(m, wo_ref[...], preferred_element_type=F32)
        o_ref[rows, :] = x_ref[rows, :] + _rms(out, g_ref[...])


def _merge(hn2d, x2d, ya0, ya1, yb, yc, w_gate, gate_bias, w_branch, w_out, g):
    n = x2d.shape[0]
    tm = MERGE_ROWS
    once = pl.Buffered(1)
    rows = lambda w: pl.BlockSpec((tm, w), lambda i: (i, 0))
    return pl.pallas_call(
        _merge_kernel,
        grid=(n // tm,),
        in_specs=[rows(D_MODEL), rows(D_MODEL), rows(COL_TILE), rows(COL_TILE), rows(NA_W), rows(SC_W),
                  pl.BlockSpec((D_MODEL, N_BRANCH * D_MODEL), lambda i: (0, 0), pipeline_mode=once),
                  pl.BlockSpec((N_BRANCH, D_MODEL), lambda i: (0, 0)),
                  pl.BlockSpec((N_BRANCH, HY_W, D_MODEL), lambda i: (0, 0, 0), pipeline_mode=once),
                  pl.BlockSpec((D_MODEL, D_MODEL), lambda i: (0, 0), pipeline_mode=once),
                  pl.BlockSpec((1, D_MODEL), lambda i: (0, 0))],
        out_specs=rows(D_MODEL),
        out_shape=jax.ShapeDtypeStruct((n, D_MODEL), F32),
        compiler_params=_params("arbitrary"),
        name="merge",
    )(hn2d, x2d, ya0, ya1, yb, yc, w_gate, gate_bias, w_branch, w_out, g)


def _kv_kernel(m_ref, g_ref, w_ref, o_ref):
    mn = _rms(m_ref[...], g_ref[...]).astype(BF16)
    o_ref[...] = jnp.dot(mn, w_ref[...], preferred_element_type=F32).astype(BF16)


def _mem_kv(mem, g, wkv):
    b = mem.shape[0]
    return pl.pallas_call(
        _kv_kernel,
        grid=(b,),
        in_specs=[pl.BlockSpec((None, N_MEM, D_MODEL), lambda i: (i, 0, 0)),
                  pl.BlockSpec((1, D_MODEL), lambda i: (0, 0)),
                  pl.BlockSpec((D_MODEL, 2 * D_MODEL), lambda i: (0, 0))],
        out_specs=pl.BlockSpec((None, N_MEM, 2 * D_MODEL), lambda i: (i, 0, 0)),
        out_shape=jax.ShapeDtypeStruct((b, N_MEM, 2 * D_MODEL), BF16),
        compiler_params=_params("arbitrary"),
        name="mem_kv",
    )(mem, g, wkv)


def _xattn_kernel(x_ref, kv_ref, wq_ref, wo_ref, gq_ref, go_ref, gn_ref, o_ref, hn_ref):
    chunks = [slice(r, r + ROW_TILE) for r in range(0, XA_ROWS, ROW_TILE)]
    head_cols = [slice(i * XA_HEAD_DIM, (i + 1) * XA_HEAD_DIM) for i in range(XA_HEADS)]

    def query(rows):
        h = _rms(x_ref[rows, :], gq_ref[...]).astype(BF16)
        q = jnp.dot(h, wq_ref[...], preferred_element_type=F32) * (XA_HEAD_DIM ** -0.5)
        return q.astype(BF16)

    def scores(q):
        return [lax.dot_general(q[:, sl], kv_ref[:, sl], (((1,), (1,)), ((), ())),
                                preferred_element_type=F32) for sl in head_cols]

    def values(s_heads):
        heads = []
        for i, s in enumerate(s_heads):
            vm = kv_ref[:, D_MODEL + i * XA_HEAD_DIM:D_MODEL + (i + 1) * XA_HEAD_DIM]
            p = jnp.exp(s - jnp.max(s, axis=-1, keepdims=True))
            den = jnp.sum(p, axis=-1, keepdims=True)
            heads.append((jnp.dot(p.astype(BF16), vm, preferred_element_type=F32) / den).astype(BF16))
        return jnp.concatenate(heads, axis=-1)

    s_all = [scores(q) for q in [query(rows) for rows in chunks]]
    attended = [values(s) for s in s_all]
    for rows, a in zip(chunks, attended):
        o = jnp.dot(a, wo_ref[...], preferred_element_type=F32)
        xn = x_ref[rows, :] + _rms(o, go_ref[...])
        o_ref[rows, :] = xn
        hn_ref[rows, :] = _rms(xn, gn_ref[...]).astype(BF16)


def _xattn(x, kv, wq, wo, gq, go, gn):
    b = x.shape[0]
    tm = XA_ROWS
    once = pl.Buffered(1)
    rows = pl.BlockSpec((None, tm, D_MODEL), lambda i, j: (i, j, 0))
    gain = pl.BlockSpec((1, D_MODEL), lambda i, j: (0, 0))
    wfull = pl.BlockSpec((D_MODEL, D_MODEL), lambda i, j: (0, 0), pipeline_mode=once)
    return pl.pallas_call(
        _xattn_kernel,
        grid=(b, SEQ // tm),
        in_specs=[rows, pl.BlockSpec((None, N_MEM, 2 * D_MODEL), lambda i, j: (i, 0, 0)),
                  wfull, wfull, gain, gain, gain],
        out_specs=(rows, rows),
        out_shape=(jax.ShapeDtypeStruct((b, SEQ, D_MODEL), F32),
                   jax.ShapeDtypeStruct((b, SEQ, D_MODEL), BF16)),
        compiler_params=_params("arbitrary", "arbitrary"),
        name="xattn",
    )(x, kv, wq, wo, gq, go, gn)


def _gelu_tanh(x):
    c = math.sqrt(2.0 / math.pi)
    half = 0.5 * x
    return half + half * jnp.tanh(x * (c + (c * 0.044715) * (x * x)))


def _ffn_kernel(hn_ref, wu_ref, cw_ref, wd_ref, o_ref):
    def up(r0, c0, c1):
        lo = max(r0 - CONV_HALO, 0)
        hi = min(r0 + CONV_CHUNK + CONV_HALO, SEQ)
        hn = hn_ref[lo:hi, :]
        ug = jnp.dot(hn, wu_ref[:, c0:c1], preferred_element_type=F32)
        uv = jnp.dot(hn, wu_ref[:, D_FF + c0:D_FF + c1], preferred_element_type=F32)
        return r0, r0 - lo, c0, c1, ug, uv

    def down(r0, skip, c0, c1, ug, uv):
        act = (_gelu_tanh(_dwconv3(ug, cw_ref.at[:, c0:c1]))
               * _dwconv3(uv, cw_ref.at[:, D_FF + c0:D_FF + c1]))
        act = act[skip:skip + CONV_CHUNK].astype(BF16)
        part = jnp.dot(act, wd_ref[c0:c1, :], preferred_element_type=F32)
        if c0 == 0:
            o_ref[r0:r0 + CONV_CHUNK, :] = part
        else:
            o_ref[r0:r0 + CONV_CHUNK, :] += part

    pending = None
    for r0 in range(0, SEQ, CONV_CHUNK):
        for c0, c1 in zip(FFN_CUTS[:-1], FFN_CUTS[1:]):
            current = up(r0, c0, c1)
            if pending is not None:
                down(*pending)
            pending = current
    down(*pending)


def _ffn(hn, w_up, w_conv, w_down):
    b = hn.shape[0]
    once = pl.Buffered(1)
    return pl.pallas_call(
        _ffn_kernel,
        grid=(b,),
        in_specs=[pl.BlockSpec((None, SEQ, D_MODEL), lambda i: (i, 0, 0)),
                  pl.BlockSpec((D_MODEL, 2 * D_FF), lambda i: (0, 0), pipeline_mode=once),
                  pl.BlockSpec((3, 2 * D_FF), lambda i: (0, 0), pipeline_mode=once),
                  pl.BlockSpec((D_FF, D_MODEL), lambda i: (0, 0), pipeline_mode=once)],
        out_specs=pl.BlockSpec((None, SEQ, D_MODEL), lambda i: (i, 0, 0), pipeline_mode=once),
        out_shape=jax.ShapeDtypeStruct((b, SEQ, D_MODEL), F32),
        compiler_params=_params("arbitrary"),
        name="ffn",
    )(hn, w_up, w_conv, w_down)


def _residual_kernel(x_ref, f_ref, g_ref, gn_ref, o_ref, hn_ref):
    xn = x_ref[...] + _rms(f_ref[...], g_ref[...])
    o_ref[...] = xn
    hn_ref[...] = _rms(xn, gn_ref[...]).astype(BF16)


def _residual_last_kernel(x_ref, f_ref, g_ref, o_ref):
    o_ref[...] = x_ref[...] + _rms(f_ref[...], g_ref[...])


def _residual(x2d, f2d, g, gn):
    n = x2d.shape[0]
    tm = 1024
    rows = pl.BlockSpec((tm, D_MODEL), lambda i: (i, 0))
    gain = pl.BlockSpec((1, D_MODEL), lambda i: (0, 0))
    x_shape = jax.ShapeDtypeStruct((n, D_MODEL), F32)
    if gn is None:
        return pl.pallas_call(
            _residual_last_kernel,
            grid=(n // tm,),
            in_specs=[rows, rows, gain],
            out_specs=rows,
            out_shape=x_shape,
            compiler_params=_params("arbitrary"),
            name="residual_last",
        )(x2d, f2d, g), None
    return pl.pallas_call(
        _residual_kernel,
        grid=(n // tm,),
        in_specs=[rows, rows, gain, gain],
        out_specs=(rows, rows),
        out_shape=(x_shape, jax.ShapeDtypeStruct((n, D_MODEL), BF16)),
        compiler_params=_params("arbitrary"),
        name="residual",
    )(x2d, f2d, g, gn)


def _angle_tables(num, den):
    ang = (num % den).astype(F32) * (2.0 * math.pi / den)
    return jnp.cos(ang), jnp.sin(ang)


def _dft_tables():
    j = jnp.arange(SEQ, dtype=jnp.int32)
    rows = SEQ // FFT_RADIX
    q = rows // DFT_SPLIT
    up = jnp.arange(q, dtype=jnp.int32)
    a = jnp.concatenate([up, 2 * q + up, 2 * q - 1 - up, 4 * q - 1 - up]).reshape(2, 2 * q, 1, 1)
    b_up = jnp.arange(DFT_SPLIT, dtype=jnp.int32)
    b = jnp.stack([b_up, DFT_SPLIT - 1 - b_up]).reshape(2, 1, DFT_SPLIT, 1)
    ca, sa = _angle_tables(DFT_SPLIT * a * j, FFT_N)
    cb, sb = _angle_tables((2 * b + 1) * j, 2 * FFT_N)
    big_c = (ca * cb - sa * sb).reshape(SEQ, SEQ).astype(BF16)
    big_s = (sa * cb + ca * sb).reshape(SEQ, SEQ).astype(BF16)
    kappa = jnp.arange(rows, dtype=jnp.int32)[None, :, None]
    m = jnp.arange(rows, dtype=jnp.int32)[None, None, :]
    r = jnp.arange(FFT_RADIX, dtype=jnp.int32)[:, None, None]
    cf, sf = _angle_tables((2 * kappa + 1) * (FFT_RADIX * m + r), 2 * FFT_N)
    cf, sf = cf.astype(BF16), sf.astype(BF16)
    small = (cf, sf, cf.transpose(0, 2, 1), sf.transpose(0, 2, 1))
    return big_c, big_s, small


def kernel(x, mem, norm_gains, mem_norm, w_in, gate_bias, hy_short_w, hy_w1, hy_b1, hy_w2, hy_b2,
           hy_w3, hy_freq, hy_bias, na_rpb, sc_conv_w, w_branch, w_out, xa_wq, xa_wkv, xa_wo,
           ffn_up, ffn_conv, ffn_down):
    b, l, d = x.shape
    depth = w_in.shape[0]
    assert (l, d) == (SEQ, D_MODEL) and mem.shape[1:] == (N_MEM, D_MODEL)
    n = b * l
    dft_c, dft_s, conv_tables = _dft_tables()
    gains = norm_gains.astype(F32)
    x2d = x.reshape(n, d)
    hn = _prenorm(x2d, gains[0, 0][None])
    for i in range(depth):
        g = gains[i]
        w_hyena = w_in[i, :, :3 * HY_W].astype(BF16)
        w_gate = w_in[i, :, 3 * HY_W + 3 * NA_W + 3 * SC_W:].astype(BF16)
        kr, ki = _hyena_filters(dft_c, dft_s, hy_w1[i], hy_b1[i], hy_w2[i], hy_b2[i], hy_w3[i],
                                hy_freq[i])
        hn3 = hn.reshape(b, l, d)
        ya = [_hyena_mixer(hn3, w_hyena, hy_short_w[i].astype(F32), conv_tables, kr, ki,
                           hy_bias[i].astype(F32), tile) for tile in range(HY_W // COL_TILE)]
        yb = _na_mixer(hn3, w_in, i, _na_bias(na_rpb[i]))
        yc = _shortconv_mixer(hn3, w_in, i, sc_conv_w[i].astype(F32))
        x2d = _merge(hn, x2d, ya[0], ya[1], yb.reshape(n, NA_W), yc.reshape(n, SC_W), w_gate,
                     gate_bias[i].astype(F32), w_branch[i].astype(BF16), w_out[i].astype(BF16), g[1][None])
        kv = _mem_kv(mem, mem_norm[i].astype(F32)[None], xa_wkv[i].astype(BF16))
        x3, hn2 = _xattn(x2d.reshape(b, l, d), kv, xa_wq[i].astype(BF16), xa_wo[i].astype(BF16),
                         g[2][None], g[3][None], g[4][None])
        f = _ffn(hn2, ffn_up[i].astype(BF16), ffn_conv[i].astype(F32), ffn_down[i].astype(BF16))
        g_next = gains[i + 1, 0][None] if i + 1 < depth else None
        x2d, hn = _residual(x3.reshape(n, d), f.reshape(n, d), g[5][None], g_next)
    return x2d.reshape(b, l, d)
```

```python
import functools
import math

import jax
import jax.numpy as jnp
from jax import lax
from jax.experimental import pallas as pl
from jax.experimental.pallas import tpu as pltpu

D_MODEL = 1024
SEQ = 2048
N_MEM = 256
GRID_W = 64
HY_W = 512
NA_HEADS = 8
NA_HEAD_DIM = 64
NA_W = NA_HEADS * NA_HEAD_DIM
NA_WIN_ROWS = 8
NA_WIN_COLS = 16
SC_W = 512
XA_HEADS = 4
XA_HEAD_DIM = D_MODEL // XA_HEADS
D_FF = 2816
HY_ORDER = 2
HY_EMB = 33
HY_HIDDEN = 64
HY_FAST_DECAY = 0.3
HY_SLOW_DECAY = 1.5
HY_TARGET = 1e-2
N_BRANCH = 3
EPS = 1e-6

FFT_N = 2 * SEQ
FFT_RADIX = 4
FFT_SUB = FFT_N // FFT_RADIX
NA_ROWS = SEQ // GRID_W
NA_KEYS = NA_WIN_ROWS * GRID_W
NA_GROUP = 4
NA_ROW_GROUP = 4
LANES = 128
COL_TILE = 256
ROW_TILE = 512
MERGE_ROWS = 2 * ROW_TILE
XA_ROWS = 2 * ROW_TILE
CONV_PAD = 8
DFT_SPLIT = 64
CONV_CHUNK = 512
FFN_CUTS = (0, 6 * COL_TILE, D_FF)
CONV_HALO = 16
VMEM_LIMIT = 60 * 1024 * 1024

BF16 = jnp.bfloat16
F32 = jnp.float32


def _params(*sem):
    return pltpu.CompilerParams(dimension_semantics=sem, vmem_limit_bytes=VMEM_LIMIT)


def _rms(xf, g):
    ms = jnp.mean(xf * xf, axis=-1, keepdims=True)
    return xf * lax.rsqrt(ms + EPS) * g


def _dwconv3(u, w_ref):
    n = u.shape[0]
    zeros = jnp.zeros((CONV_PAD, u.shape[1]), F32)
    padded = jnp.concatenate([zeros, u, zeros], axis=0)
    m = n + 2 * CONV_PAD
    prev = pltpu.roll(padded, 1, 0)[CONV_PAD:CONV_PAD + n]
    nxt = pltpu.roll(padded, m - 1, 0)[CONV_PAD:CONV_PAD + n]
    return prev * w_ref[0:1, :] + u * w_ref[1:2, :] + nxt * w_ref[2:3, :]


def _prenorm_kernel(x_ref, g_ref, o_ref):
    o_ref[...] = _rms(x_ref[...], g_ref[...]).astype(BF16)


def _prenorm(x2d, g):
    n = x2d.shape[0]
    tm = 1024
    return pl.pallas_call(
        _prenorm_kernel,
        grid=(n // tm,),
        in_specs=[pl.BlockSpec((tm, D_MODEL), lambda i: (i, 0)),
                  pl.BlockSpec((1, D_MODEL), lambda i: (0, 0))],
        out_specs=pl.BlockSpec((tm, D_MODEL), lambda i: (i, 0)),
        out_shape=jax.ShapeDtypeStruct((n, D_MODEL), BF16),
        compiler_params=_params("arbitrary"),
        name="prenorm",
    )(x2d, g)


def _filter_mlp_kernel(z_ref, w1_ref, b1_ref, w2_ref, b2_ref, w3_ref, f_ref, t_ref, dl_ref,
                       hs_ref, hd_ref):
    hp = lax.Precision.HIGHEST
    h = jnp.sin(f_ref[0:1, :] * (jnp.dot(z_ref[...], w1_ref[...], precision=hp) + b1_ref[...]))
    h = jnp.sin(f_ref[1:2, :] * (jnp.dot(h, w2_ref[...], precision=hp) + b2_ref[...]))
    decay = jnp.exp(-t_ref[...] * dl_ref[...])
    row = lax.broadcasted_iota(jnp.int32, (SEQ, HY_W), 0)
    h_hi = h.astype(BF16)
    h_lo = (h - h_hi.astype(F32)).astype(BF16)

    def out_layer(cols):
        w = w3_ref[:, cols]
        w_hi = w.astype(BF16)
        w_lo = (w - w_hi.astype(F32)).astype(BF16)
        return (jnp.dot(h_hi, w_hi, preferred_element_type=F32)
                + (jnp.dot(h_hi, w_lo, preferred_element_type=F32)
                   + jnp.dot(h_lo, w_hi, preferred_element_type=F32)))

    for o in range(HY_ORDER):
        c_f = o * HY_W
        c_b = HY_ORDER * HY_W + o * HY_W
        hf = out_layer(slice(c_f, c_f + HY_W)) * decay
        hb = out_layer(slice(c_b, c_b + HY_W)) * decay
        hb = jnp.where(row == 0, 0.0, hb)
        hs_ref[:, c_f:c_f + HY_W] = (hf + hb).astype(BF16)
        hd_ref[:, c_f:c_f + HY_W] = (hb - hf).astype(BF16)


def _filter_dft_kernel(c_ref, s_ref, hs_ref, hd_ref, kr_ref, ki_ref):
    kr_ref[...] = jnp.dot(c_ref[...], hs_ref[...], preferred_element_type=F32) * (2.0 / FFT_N)
    ki_ref[...] = jnp.dot(s_ref[...], hd_ref[...], preferred_element_type=F32) * (2.0 / FFT_N)


def _hyena_filters(dft_c, dft_s, w1, b1, w2, b2, w3, freq):
    t = jnp.linspace(0.0, 1.0, SEQ, dtype=F32)[:, None]
    bands = (HY_EMB - 1) // 2
    w = 2.0 * math.pi * jnp.arange(SEQ, dtype=F32)[:, None] / SEQ
    f = jnp.linspace(1e-4, bands - 1, bands, dtype=F32)[None, :]
    z = jnp.concatenate([t, jnp.cos(f * w), -jnp.sin(f * w)], axis=-1)
    z = jnp.pad(z, ((0, 0), (0, HY_HIDDEN - HY_EMB)))
    w1p = jnp.pad(w1.astype(F32), ((0, HY_HIDDEN - HY_EMB), (0, 0)))
    deltas = jnp.abs(jnp.linspace(math.log(HY_TARGET) / HY_SLOW_DECAY,
                                  math.log(HY_TARGET) / HY_FAST_DECAY, HY_W, dtype=F32))[None, :]
    width = HY_ORDER * HY_W
    hs, hd = pl.pallas_call(
        _filter_mlp_kernel,
        out_shape=(jax.ShapeDtypeStruct((SEQ, width), BF16),
                   jax.ShapeDtypeStruct((SEQ, width), BF16)),
        compiler_params=pltpu.CompilerParams(vmem_limit_bytes=VMEM_LIMIT),
        name="hyena_filter_mlp",
    )(z, w1p, b1[None].astype(F32), w2.astype(F32), b2[None].astype(F32), w3.astype(F32),
      freq.astype(F32), t, deltas)
    nt = width // COL_TILE
    full = pl.BlockSpec((SEQ, SEQ), lambda j: (0, 0))
    col = pl.BlockSpec((SEQ, COL_TILE), lambda j: (0, j))
    kr, ki = pl.pallas_call(
        _filter_dft_kernel,
        grid=(nt,),
        in_specs=[full, full, col, col],
        out_specs=(col, col),
        out_shape=(jax.ShapeDtypeStruct((SEQ, width), F32),
                   jax.ShapeDtypeStruct((SEQ, width), F32)),
        compiler_params=_params("arbitrary"),
        name="hyena_filter_dft",
    )(dft_c, dft_s, hs, hd)

    fam_shape = (FFT_RADIX, SEQ // FFT_RADIX, width)
    return kr.reshape(fam_shape), ki.reshape(fam_shape)


def _hyena_kernel(hn_ref, wv_ref, w1_ref, w2_ref, sv_ref, s1_ref, s2_ref, cf_ref, sf_ref, ct_ref,
                  st_ref, kr0_ref, ki0_ref, kr1_ref, ki1_ref, bias_ref, o_ref,
                  slab_v_ref, slab_1_ref, slab_2_ref, slab_o_ref):
    radix = FFT_RADIX
    rows = SEQ // radix
    n_slab = COL_TILE // LANES
    zero_row = jnp.zeros((CONV_PAD, COL_TILE), F32)

    def project(w_ref, slab_ref):
        half = SEQ // 2
        for top in (0, half):
            u = jnp.dot(hn_ref[top:top + half, :], w_ref[...], preferred_element_type=F32)
            for j in range(n_slab):
                slab_ref[j, top:top + half, :] = u[:, j * LANES:(j + 1) * LANES]
        return [jnp.concatenate([slab_ref[j, pl.ds(r, rows, stride=radix), :] for j in range(n_slab)],
                                axis=1) for r in range(radix)]

    def short_conv(u, w_ref):
        n = u[0].shape[0]
        prev_wrap = pltpu.roll(jnp.concatenate([u[-1], zero_row], axis=0), 1, 0)[:n]
        next_wrap = pltpu.roll(jnp.concatenate([zero_row, u[0]], axis=0), n + CONV_PAD - 1, 0)[CONV_PAD:]
        prev = [prev_wrap] + u[:-1]
        nxt = u[1:] + [next_wrap]
        return [prev[r] * w_ref[0:1, :] + u[r] * w_ref[1:2, :] + nxt[r] * w_ref[2:3, :]
                for r in range(radix)]

    def cmul(ar, ai, br, bi):
        return ar * br - ai * bi, ar * bi + ai * br

    def transform(x):
        xb = [v.astype(BF16) for v in x]
        tr = [jnp.dot(cf_ref[r], xb[r], preferred_element_type=F32) for r in range(radix)]
        ti = [-jnp.dot(sf_ref[r], xb[r], preferred_element_type=F32) for r in range(radix)]
        return tr, ti

    def filter_and_invert(tr, ti, kr_ref, ki_ref):
        ar, ai = tr[0] + tr[2], ti[0] + ti[2]
        br, bi = tr[0] - tr[2], ti[0] - ti[2]
        cr, ci = tr[1] + tr[3], ti[1] + ti[3]
        dr, di = tr[1] - tr[3], ti[1] - ti[3]
        fam = [(ar + cr, ai + ci), (br + di, bi - dr), (br - di, -bi - dr), (ar - cr, ci - ai)]
        y = [cmul(fr, fi, kr_ref[f], ki_ref[f]) for f, (fr, fi) in enumerate(fam)]
        er, ei = y[0][0] + y[3][0], y[0][1] - y[3][1]
        fr, fi = y[0][0] - y[3][0], y[0][1] + y[3][1]
        gr, gi = y[1][0] + y[2][0], y[1][1] - y[2][1]
        hr, hi = y[1][0] - y[2][0], y[1][1] + y[2][1]
        p = [(er + gr, ei + gi), (fr - hi, fi + hr), (er - gr, ei - gi), (fr + hi, fi - hr)]
        out = []
        for r in range(radix):
            pr, pi_ = p[r]
            out.append(jnp.dot(ct_ref[r], pr.astype(BF16), preferred_element_type=F32)
                       - jnp.dot(st_ref[r], pi_.astype(BF16), preferred_element_type=F32))
        return out

    v = short_conv(project(wv_ref, slab_v_ref), sv_ref)
    spectrum = transform(v)
    x1 = short_conv(project(w1_ref, slab_1_ref), s1_ref)
    x2 = short_conv(project(w2_ref, slab_2_ref), s2_ref)
    y = filter_and_invert(*spectrum, kr0_ref, ki0_ref)
    z = [x1[r] * (y[r] + v[r] * bias_ref[0:1, :]) for r in range(radix)]
    y = filter_and_invert(*transform(z), kr1_ref, ki1_ref)
    for r in range(radix):
        out = x2[r] * (y[r] + z[r] * bias_ref[1:2, :])
        for j in range(n_slab):
            slab_o_ref[j, pl.ds(r, rows, stride=radix), :] = out[:, j * LANES:(j + 1) * LANES]
    o_ref[...] = jnp.concatenate([slab_o_ref[j] for j in range(n_slab)], axis=1).astype(BF16)


def _hyena_mixer(hn, w_hy, short_w, tables, kr, ki, bias, tile):
    b = hn.shape[0]
    nt = HY_W // COL_TILE
    rows = SEQ // FFT_RADIX
    once = pl.Buffered(1)
    cf, sf, ct, st = tables

    def fixed(shape, index):
        return pl.BlockSpec(shape, lambda i: index, pipeline_mode=once)

    def wcol(k):
        return fixed((D_MODEL, COL_TILE), (0, k * nt + tile))

    def scol(k):
        return fixed((3, COL_TILE), (0, k * nt + tile))

    def kfam(o):
        return fixed((FFT_RADIX, rows, COL_TILE), (0, 0, o * nt + tile))

    small = fixed((FFT_RADIX, rows, rows), (0, 0, 0))
    slab = pltpu.VMEM((COL_TILE // LANES, SEQ, LANES), F32)
    out = pl.pallas_call(
        _hyena_kernel,
        grid=(b,),
        in_specs=[pl.BlockSpec((None, SEQ, D_MODEL), lambda i: (i, 0, 0)),
                  wcol(0), wcol(1), wcol(2), scol(0), scol(1), scol(2),
                  small, small, small, small,
                  kfam(0), kfam(0), kfam(1), kfam(1),
                  fixed((HY_ORDER, COL_TILE), (0, tile))],
        out_specs=pl.BlockSpec((None, SEQ, COL_TILE), lambda i: (i, 0, 0)),
        out_shape=jax.ShapeDtypeStruct((b, SEQ, COL_TILE), BF16),
        scratch_shapes=[slab, slab, slab, slab],
        compiler_params=_params("arbitrary"),
        name="hyena_mixer",
    )(hn, w_hy, w_hy, w_hy, short_w, short_w, short_w, cf, sf, ct, st, kr, ki, kr, ki, bias)
    return out.reshape(b * SEQ, COL_TILE)


def _shortconv_kernel(hn_ref, wb_ref, wc_ref, wx_ref, cw_ref, o_ref):
    wb = wb_ref[...].astype(BF16)
    wc = wc_ref[...].astype(BF16)
    wx = wx_ref[...].astype(BF16)

    def project(r0):
        lo = max(r0 - CONV_HALO, 0)
        hi = min(r0 + CONV_CHUNK + CONV_HALO, SEQ)
        hn = hn_ref[lo:hi, :]
        return (r0, r0 - lo, jnp.dot(hn, wb, preferred_element_type=F32),
                jnp.dot(hn, wc, preferred_element_type=F32), jnp.dot(hn, wx, preferred_element_type=F32))

    def finish(r0, skip, bg, cg, xi):
        out = bg * _dwconv3(cg * xi, cw_ref)
        o_ref[r0:r0 + CONV_CHUNK, :] = out[skip:skip + CONV_CHUNK].astype(BF16)

    pending = None
    for r0 in range(0, SEQ, CONV_CHUNK):
        current = project(r0)
        if pending is not None:
            finish(*pending)
        pending = current
    finish(*pending)


def _w_in_cols(layer, width, first):
    return lambda k, nt: pl.BlockSpec((None, D_MODEL, width),
                                      lambda j, i: (layer, 0, first + k * nt + j))


def _shortconv_mixer(hn, w_in, layer, conv_w):
    b = hn.shape[0]
    nt = SC_W // COL_TILE
    wcol = _w_in_cols(layer, COL_TILE, (3 * HY_W + 3 * NA_W) // COL_TILE)
    return pl.pallas_call(
        _shortconv_kernel,
        grid=(nt, b),
        in_specs=[pl.BlockSpec((None, SEQ, D_MODEL), lambda j, i: (i, 0, 0)),
                  wcol(0, nt), wcol(1, nt), wcol(2, nt),
                  pl.BlockSpec((3, COL_TILE), lambda j, i: (0, j))],
        out_specs=pl.BlockSpec((None, SEQ, COL_TILE), lambda j, i: (i, 0, j)),
        out_shape=jax.ShapeDtypeStruct((b, SEQ, SC_W), BF16),
        compiler_params=_params("arbitrary", "arbitrary"),
        name="shortconv_mixer",
    )(hn, w_in, w_in, w_in, conv_w)


def _na_kernel(hn_ref, wq_ref, wk_ref, wv_ref, bias_ref, o_ref, q_ref, k_ref, v_ref, s_ref):
    hn = hn_ref[...]
    q = jnp.dot(hn, wq_ref[...].astype(BF16), preferred_element_type=F32)
    q_ref[...] = (q * (NA_HEAD_DIM ** -0.5)).astype(BF16)
    k_ref[...] = jnp.dot(hn, wk_ref[...].astype(BF16), preferred_element_type=F32).astype(BF16)
    v_ref[...] = jnp.dot(hn, wv_ref[...].astype(BF16), preferred_element_type=F32).astype(BF16)
    gw = NA_GROUP * NA_HEAD_DIM
    same_head = (lax.broadcasted_iota(jnp.int32, (gw, gw), 0) // NA_HEAD_DIM
                 == lax.broadcasted_iota(jnp.int32, (gw, gw), 1) // NA_HEAD_DIM)

    def scores(r):
        w0 = jnp.clip(r - NA_WIN_ROWS // 2, 0, NA_ROWS - NA_WIN_ROWS)
        off = w0 - r + (NA_WIN_ROWS - 1)
        q0 = pl.multiple_of(r * GRID_W, GRID_W)
        k0 = pl.multiple_of(w0 * GRID_W, GRID_W)
        q_row = q_ref[pl.ds(q0, GRID_W), :]
        q_heads = jnp.where(same_head, jnp.concatenate([q_row] * NA_GROUP, axis=0), 0)
        s = lax.dot_general(q_heads, k_ref[pl.ds(k0, NA_KEYS), :], (((1,), (1,)), ((), ())),
                            preferred_element_type=F32)
        bias = jnp.concatenate(
            [jnp.concatenate([bias_ref[h, off + 2 * m] for m in range(NA_WIN_ROWS // 2)], axis=1)
             for h in range(NA_GROUP)], axis=0)
        return s + bias

    def attend(r, s):
        w0 = jnp.clip(r - NA_WIN_ROWS // 2, 0, NA_ROWS - NA_WIN_ROWS)
        q0 = pl.multiple_of(r * GRID_W, GRID_W)
        k0 = pl.multiple_of(w0 * GRID_W, GRID_W)
        p = jnp.exp(s - jnp.max(s, axis=-1, keepdims=True))
        inv = 1.0 / jnp.sum(p, axis=-1, keepdims=True)
        pv = jnp.dot(p.astype(BF16), v_ref[pl.ds(k0, NA_KEYS), :], preferred_element_type=F32)
        pv = jnp.where(same_head, pv * inv, 0.0)
        out = pv[0:GRID_W]
        for h in range(1, NA_GROUP):
            out = out + pv[h * GRID_W:(h + 1) * GRID_W]
        o_ref[pl.ds(q0, GRID_W), :] = out.astype(BF16)

    groups = NA_ROWS // NA_ROW_GROUP
    for t in range(NA_ROW_GROUP):
        s_ref[t] = scores(jnp.int32(t))

    def rows_body(i, carry):
        for t in range(NA_ROW_GROUP):
            s_next = scores(i * NA_ROW_GROUP + t)
            attend((i - 1) * NA_ROW_GROUP + t, s_ref[t])
            s_ref[t] = s_next
        return carry

    lax.fori_loop(1, groups, rows_body, 0)
    for t in range(NA_ROW_GROUP):
        attend(jnp.int32((groups - 1) * NA_ROW_GROUP + t), s_ref[t])


def _na_bias(rpb):
    c = jnp.arange(GRID_W)
    col_start = jnp.clip(c - NA_WIN_COLS // 2, 0, GRID_W - NA_WIN_COLS)
    col_mask = (c[None, :] >= col_start[:, None]) & (c[None, :] < col_start[:, None] + NA_WIN_COLS)
    dc = jnp.clip(c[None, :] - c[:, None] + NA_WIN_COLS - 1, 0, 2 * NA_WIN_COLS - 2)
    pick = (dc[None] == jnp.arange(2 * NA_WIN_COLS - 1)[:, None, None]).astype(F32)
    table = jnp.einsum("hrd,dqc->hrqc", rpb.astype(F32), pick, precision=lax.Precision.HIGHEST)
    table = table + jnp.where(col_mask, 0.0, -1e30)[None, None]
    return jnp.concatenate([table[:, :-1], table[:, 1:]], axis=-1)


def _na_mixer(hn, w_in, layer, bias):
    b = hn.shape[0]
    gw = NA_GROUP * NA_HEAD_DIM
    ng = NA_W // gw
    wcol = _w_in_cols(layer, gw, 3 * HY_W // gw)
    return pl.pallas_call(
        _na_kernel,
        grid=(ng, b),
        in_specs=[pl.BlockSpec((None, SEQ, D_MODEL), lambda j, i: (i, 0, 0)),
                  wcol(0, ng), wcol(1, ng), wcol(2, ng),
                  pl.BlockSpec((NA_GROUP, 2 * NA_WIN_ROWS - 2, GRID_W, 2 * GRID_W),
                               lambda j, i: (j, 0, 0, 0))],
        out_specs=pl.BlockSpec((None, SEQ, gw), lambda j, i: (i, 0, j)),
        out_shape=jax.ShapeDtypeStruct((b, SEQ, NA_W), BF16),
        scratch_shapes=[pltpu.VMEM((SEQ, gw), BF16) for _ in range(3)]
        + [pltpu.VMEM((NA_ROW_GROUP, gw, NA_KEYS), F32)],
        compiler_params=_params("arbitrary", "arbitrary"),
        name="na_mixer",
    )(hn, w_in, w_in, w_in, bias)


def _merge_kernel(hn_ref, x_ref, ya0_ref, ya1_ref, yb_ref, yc_ref, wg_ref, gb_ref, wb_ref, wo_ref,
                  g_ref, o_ref):
    def gated_sum(rows):
        hn = hn_ref[rows, :]
        ya = jnp.concatenate([ya0_ref[rows, :], ya1_ref[rows, :]], axis=1)
        merged = None
        for i, y in enumerate((ya, yb_ref[rows, :], yc_ref[rows, :])):
            pre = jnp.dot(hn, wg_ref[:, i * D_MODEL:(i + 1) * D_MODEL], preferred_element_type=F32)
            gate = jax.nn.sigmoid(pre + gb_ref[i:i + 1, :])
            term = gate * jnp.dot(y, wb_ref[i], preferred_element_type=F32)
            merged = term if merged is None else merged + term
        return merged.astype(BF16)

    chunks = [slice(r, r + ROW_TILE) for r in range(0, MERGE_ROWS, ROW_TILE)]
    merged = [gated_sum(rows) for rows in chunks]
    for rows, m in zip(chunks, merged):
        out = jnp.dot(m, wo_ref[...], preferred_element_type=F32)
        o_ref[rows, :] = x_ref[rows, :] + _rms(out, g_ref[...])


def _merge(hn2d, x2d, ya0, ya1, yb, yc, w_gate, gate_bias, w_branch, w_out, g):
    n = x2d.shape[0]
    tm = MERGE_ROWS
    once = pl.Buffered(1)
    rows = lambda w: pl.BlockSpec((tm, w), lambda i: (i, 0))
    return pl.pallas_call(
        _merge_kernel,
        grid=(n // tm,),
        in_specs=[rows(D_MODEL), rows(D_MODEL), rows(COL_TILE), rows(COL_TILE), rows(NA_W), rows(SC_W),
                  pl.BlockSpec((D_MODEL, N_BRANCH * D_MODEL), lambda i: (0, 0), pipeline_mode=once),
                  pl.BlockSpec((N_BRANCH, D_MODEL), lambda i: (0, 0)),
                  pl.BlockSpec((N_BRANCH, HY_W, D_MODEL), lambda i: (0, 0, 0), pipeline_mode=once),
                  pl.BlockSpec((D_MODEL, D_MODEL), lambda i: (0, 0), pipeline_mode=once),
                  pl.BlockSpec((1, D_MODEL), lambda i: (0, 0))],
        out_specs=rows(D_MODEL),
        out_shape=jax.ShapeDtypeStruct((n, D_MODEL), F32),
        compiler_params=_params("arbitrary"),
        name="merge",
    )(hn2d, x2d, ya0, ya1, yb, yc, w_gate, gate_bias, w_branch, w_out, g)


def _kv_kernel(m_ref, g_ref, w_ref, o_ref):
    mn = _rms(m_ref[...], g_ref[...]).astype(BF16)
    o_ref[...] = jnp.dot(mn, w_ref[...], preferred_element_type=F32).astype(BF16)


def _mem_kv(mem, g, wkv):
    b = mem.shape[0]
    return pl.pallas_call(
        _kv_kernel,
        grid=(b,),
        in_specs=[pl.BlockSpec((None, N_MEM, D_MODEL), lambda i: (i, 0, 0)),
                  pl.BlockSpec((1, D_MODEL), lambda i: (0, 0)),
                  pl.BlockSpec((D_MODEL, 2 * D_MODEL), lambda i: (0, 0))],
        out_specs=pl.BlockSpec((None, N_MEM, 2 * D_MODEL), lambda i: (i, 0, 0)),
        out_shape=jax.ShapeDtypeStruct((b, N_MEM, 2 * D_MODEL), BF16),
        compiler_params=_params("arbitrary"),
        name="mem_kv",
    )(mem, g, wkv)


def _xattn_kernel(x_ref, kv_ref, wq_ref, wo_ref, gq_ref, go_ref, gn_ref, o_ref, hn_ref):
    chunks = [slice(r, r + ROW_TILE) for r in range(0, XA_ROWS, ROW_TILE)]
    head_cols = [slice(i * XA_HEAD_DIM, (i + 1) * XA_HEAD_DIM) for i in range(XA_HEADS)]

    def query(rows):
        h = _rms(x_ref[rows, :], gq_ref[...]).astype(BF16)
        q = jnp.dot(h, wq_ref[...], preferred_element_type=F32) * (XA_HEAD_DIM ** -0.5)
        return q.astype(BF16)

    def scores(q):
        return [lax.dot_general(q[:, sl], kv_ref[:, sl], (((1,), (1,)), ((), ())),
                                preferred_element_type=F32) for sl in head_cols]

    def values(s_heads):
        heads = []
        for i, s in enumerate(s_heads):
            vm = kv_ref[:, D_MODEL + i * XA_HEAD_DIM:D_MODEL + (i + 1) * XA_HEAD_DIM]
            p = jnp.exp(s - jnp.max(s, axis=-1, keepdims=True))
            den = jnp.sum(p, axis=-1, keepdims=True)
            heads.append((jnp.dot(p.astype(BF16), vm, preferred_element_type=F32) / den).astype(BF16))
        return jnp.concatenate(heads, axis=-1)

    s_all = [scores(q) for q in [query(rows) for rows in chunks]]
    attended = [values(s) for s in s_all]
    for rows, a in zip(chunks, attended):
        o = jnp.dot(a, wo_ref[...], preferred_element_type=F32)
        xn = x_ref[rows, :] + _rms(o, go_ref[...])
        o_ref[rows, :] = xn
        hn_ref[rows, :] = _rms(xn, gn_ref[...]).astype(BF16)


def _xattn(x, kv, wq, wo, gq, go, gn):
    b = x.shape[0]
    tm = XA_ROWS
    once = pl.Buffered(1)
    rows = pl.BlockSpec((None, tm, D_MODEL), lambda i, j: (i, j, 0))
    gain = pl.BlockSpec((1, D_MODEL), lambda i, j: (0, 0))
    wfull = pl.BlockSpec((D_MODEL, D_MODEL), lambda i, j: (0, 0), pipeline_mode=once)
    return pl.pallas_call(
        _xattn_kernel,
        grid=(b, SEQ // tm),
        in_specs=[rows, pl.BlockSpec((None, N_MEM, 2 * D_MODEL), lambda i, j: (i, 0, 0)),
                  wfull, wfull, gain, gain, gain],
        out_specs=(rows, rows),
        out_shape=(jax.ShapeDtypeStruct((b, SEQ, D_MODEL), F32),
                   jax.ShapeDtypeStruct((b, SEQ, D_MODEL), BF16)),
        compiler_params=_params("arbitrary", "arbitrary"),
        name="xattn",
    )(x, kv, wq, wo, gq, go, gn)


def _gelu_tanh(x):
    c = math.sqrt(2.0 / math.pi)
    half = 0.5 * x
    return half + half * jnp.tanh(x * (c + (c * 0.044715) * (x * x)))


def _ffn_kernel(hn_ref, wu_ref, cw_ref, wd_ref, o_ref):
    def up(r0, c0, c1):
        lo = max(r0 - CONV_HALO, 0)
        hi = min(r0 + CONV_CHUNK + CONV_HALO, SEQ)
        hn = hn_ref[lo:hi, :]
        ug = jnp.dot(hn, wu_ref[:, c0:c1], preferred_element_type=F32)
        uv = jnp.dot(hn, wu_ref[:, D_FF + c0:D_FF + c1], preferred_element_type=F32)
        return r0, r0 - lo, c0, c1, ug, uv

    def down(r0, skip, c0, c1, ug, uv):
        act = (_gelu_tanh(_dwconv3(ug, cw_ref.at[:, c0:c1]))
               * _dwconv3(uv, cw_ref.at[:, D_FF + c0:D_FF + c1]))
        act = act[skip:skip + CONV_CHUNK].astype(BF16)
        part = jnp.dot(act, wd_ref[c0:c1, :], preferred_element_type=F32)
        if c0 == 0:
            o_ref[r0:r0 + CONV_CHUNK, :] = part
        else:
            o_ref[r0:r0 + CONV_CHUNK, :] += part

    pending = None
    for r0 in range(0, SEQ, CONV_CHUNK):
        for c0, c1 in zip(FFN_CUTS[:-1], FFN_CUTS[1:]):
            current = up(r0, c0, c1)
            if pending is not None:
                down(*pending)
            pending = current
    down(*pending)


def _ffn(hn, w_up, w_conv, w_down):
    b = hn.shape[0]
    once = pl.Buffered(1)
    return pl.pallas_call(
        _ffn_kernel,
        grid=(b,),
        in_specs=[pl.BlockSpec((None, SEQ, D_MODEL), lambda i: (i, 0, 0)),
                  pl.BlockSpec((D_MODEL, 2 * D_FF), lambda i: (0, 0), pipeline_mode=once),
                  pl.BlockSpec((3, 2 * D_FF), lambda i: (0, 0), pipeline_mode=once),
                  pl.BlockSpec((D_FF, D_MODEL), lambda i: (0, 0), pipeline_mode=once)],
        out_specs=pl.BlockSpec((None, SEQ, D_MODEL), lambda i: (i, 0, 0), pipeline_mode=once),
        out_shape=jax.ShapeDtypeStruct((b, SEQ, D_MODEL), F32),
        compiler_params=_params("arbitrary"),
        name="ffn",
    )(hn, w_up, w_conv, w_down)


def _residual_kernel(x_ref, f_ref, g_ref, gn_ref, o_ref, hn_ref):
    xn = x_ref[...] + _rms(f_ref[...], g_ref[...])
    o_ref[...] = xn
    hn_ref[...] = _rms(xn, gn_ref[...]).astype(BF16)


def _residual_last_kernel(x_ref, f_ref, g_ref, o_ref):
    o_ref[...] = x_ref[...] + _rms(f_ref[...], g_ref[...])


def _residual(x2d, f2d, g, gn):
    n = x2d.shape[0]
    tm = 1024
    rows = pl.BlockSpec((tm, D_MODEL), lambda i: (i, 0))
    gain = pl.BlockSpec((1, D_MODEL), lambda i: (0, 0))
    x_shape = jax.ShapeDtypeStruct((n, D_MODEL), F32)
    if gn is None:
        return pl.pallas_call(
            _residual_last_kernel,
            grid=(n // tm,),
            in_specs=[rows, rows, gain],
            out_specs=rows,
            out_shape=x_shape,
            compiler_params=_params("arbitrary"),
            name="residual_last",
        )(x2d, f2d, g), None
    return pl.pallas_call(
        _residual_kernel,
        grid=(n // tm,),
        in_specs=[rows, rows, gain, gain],
        out_specs=(rows, rows),
        out_shape=(x_shape, jax.ShapeDtypeStruct((n, D_MODEL), BF16)),
        compiler_params=_params("arbitrary"),
        name="residual",
    )(x2d, f2d, g, gn)


def _angle_tables(num, den):
    ang = (num % den).astype(F32) * (2.0 * math.pi / den)
    return jnp.cos(ang), jnp.sin(ang)


def _dft_tables():
    j = jnp.arange(SEQ, dtype=jnp.int32)
    rows = SEQ // FFT_RADIX
    q = rows // DFT_SPLIT
    up = jnp.arange(q, dtype=jnp.int32)
    a = jnp.concatenate([up, 2 * q + up, 2 * q - 1 - up, 4 * q - 1 - up]).reshape(2, 2 * q, 1, 1)
    b_up = jnp.arange(DFT_SPLIT, dtype=jnp.int32)
    b = jnp.stack([b_up, DFT_SPLIT - 1 - b_up]).reshape(2, 1, DFT_SPLIT, 1)
    ca, sa = _angle_tables(DFT_SPLIT * a * j, FFT_N)
    cb, sb = _angle_tables((2 * b + 1) * j, 2 * FFT_N)
    big_c = (ca * cb - sa * sb).reshape(SEQ, SEQ).astype(BF16)
    big_s = (sa * cb + ca * sb).reshape(SEQ, SEQ).astype(BF16)
    kappa = jnp.arange(rows, dtype=jnp.int32)[None, :, None]
    m = jnp.arange(rows, dtype=jnp.int32)[None, None, :]
    r = jnp.arange(FFT_RADIX, dtype=jnp.int32)[:, None, None]
    c0, s0 = _angle_tables((2 * kappa + 1) * m, 2 * FFT_SUB)
    cr, sr = _angle_tables((2 * kappa + 1) * r, 2 * FFT_N)
    cf = (c0 * cr - s0 * sr).astype(BF16)
    sf = (s0 * cr + c0 * sr).astype(BF16)
    small = (cf, sf, cf.transpose(0, 2, 1), sf.transpose(0, 2, 1))
    return big_c, big_s, small


def kernel(x, mem, norm_gains, mem_norm, w_in, gate_bias, hy_short_w, hy_w1, hy_b1, hy_w2, hy_b2,
           hy_w3, hy_freq, hy_bias, na_rpb, sc_conv_w, w_branch, w_out, xa_wq, xa_wkv, xa_wo,
           ffn_up, ffn_conv, ffn_down):
    b, l, d = x.shape
    depth = w_in.shape[0]
    assert (l, d) == (SEQ, D_MODEL) and mem.shape[1:] == (N_MEM, D_MODEL)
    n = b * l
    dft_c, dft_s, conv_tables = _dft_tables()
    gains = norm_gains.astype(F32)
    x2d = x.reshape(n, d)
    hn = _prenorm(x2d, gains[0, 0][None])
    for i in range(depth):
        g = gains[i]
        w_hyena = w_in[i, :, :3 * HY_W].astype(BF16)
        w_gate = w_in[i, :, 3 * HY_W + 3 * NA_W + 3 * SC_W:].astype(BF16)
        kr, ki = _hyena_filters(dft_c, dft_s, hy_w1[i], hy_b1[i], hy_w2[i], hy_b2[i], hy_w3[i],
                                hy_freq[i])
        hn3 = hn.reshape(b, l, d)
        ya = [_hyena_mixer(hn3, w_hyena, hy_short_w[i].astype(F32), conv_tables, kr, ki,
                           hy_bias[i].astype(F32), tile) for tile in range(HY_W // COL_TILE)]
        yb = _na_mixer(hn3, w_in, i, _na_bias(na_rpb[i]))
        yc = _shortconv_mixer(hn3, w_in, i, sc_conv_w[i].astype(F32))
        x2d = _merge(hn, x2d, ya[0], ya[1], yb.reshape(n, NA_W), yc.reshape(n, SC_W), w_gate,
                     gate_bias[i].astype(F32), w_branch[i].astype(BF16), w_out[i].astype(BF16), g[1][None])
        kv = _mem_kv(mem, mem_norm[i].astype(F32)[None], xa_wkv[i].astype(BF16))
        x3, hn2 = _xattn(x2d.reshape(b, l, d), kv, xa_wq[i].astype(BF16), xa_wo[i].astype(BF16),
                         g[2][None], g[3][None], g[4][None])
        f = _ffn(hn2, ffn_up[i].astype(BF16), ffn_conv[i].astype(F32), ffn_down[i].astype(BF16))
        g_next = gains[i + 1, 0][None] if i + 1 < depth else None
        x2d, hn = _residual(x3.reshape(n, d), f.reshape(n, d), g[5][None], g_next)
    return x2d.reshape(b, l, d)
```

```python
import math

import jax
import jax.numpy as jnp
from jax import lax
from jax.experimental import pallas as pl
from jax.experimental.pallas import tpu as pltpu

D_MODEL = 1024
SEQ = 2048
N_MEM = 256
GRID_W = 64
HY_W = 512
NA_HEADS = 8
NA_HEAD_DIM = 64
NA_W = NA_HEADS * NA_HEAD_DIM
NA_WIN_ROWS = 8
NA_WIN_COLS = 16
SC_W = 512
XA_HEADS = 4
XA_HEAD_DIM = D_MODEL // XA_HEADS
D_FF = 2816
HY_ORDER = 2
HY_EMB = 33
HY_HIDDEN = 64
HY_FAST_DECAY = 0.3
HY_SLOW_DECAY = 1.5
HY_TARGET = 1e-2
N_BRANCH = 3
EPS = 1e-6

FFT_N = 2 * SEQ
FFT_RADIX = 4
FFT_SUB = FFT_N // FFT_RADIX
NA_ROWS = SEQ // GRID_W
NA_KEYS = NA_WIN_ROWS * GRID_W
NA_GROUP = 4
NA_ROW_GROUP = 4
LANES = 128
COL_TILE = 256
ROW_TILE = 512
MERGE_ROWS = 2 * ROW_TILE
XA_ROWS = 2 * ROW_TILE
CAST_ROWS = 256
CONV_PAD = 8
DFT_SPLIT = 64
CONV_CHUNK = 512
FFN_CUTS = (0, 6 * COL_TILE, D_FF)
CONV_HALO = 16
VMEM_LIMIT = 60 * 1024 * 1024

BF16 = jnp.bfloat16
F32 = jnp.float32


def _params(*sem):
    return pltpu.CompilerParams(dimension_semantics=sem, vmem_limit_bytes=VMEM_LIMIT)


def _gain_spec(index):
    return pl.BlockSpec((None, 1, D_MODEL), lambda *_: (index, 0, 0))


def _layer_spec(arr, layer, width=None, col=0):
    _, r, c = arr.shape
    return pl.BlockSpec((None, r, c if width is None else width), lambda *_: (layer, 0, col))


def _rms(xf, g):
    ms = jnp.mean(xf * xf, axis=-1, keepdims=True)
    return xf * lax.rsqrt(ms + EPS) * g


def _dwconv3(u, w_ref):
    n = u.shape[0]
    zeros = jnp.zeros((CONV_PAD, u.shape[1]), F32)
    padded = jnp.concatenate([zeros, u, zeros], axis=0)
    m = n + 2 * CONV_PAD
    prev = pltpu.roll(padded, 1, 0)[CONV_PAD:CONV_PAD + n]
    nxt = pltpu.roll(padded, m - 1, 0)[CONV_PAD:CONV_PAD + n]
    return prev * w_ref[0:1, :] + u * w_ref[1:2, :] + nxt * w_ref[2:3, :]


def _prenorm_kernel(x_ref, g_ref, o_ref):
    o_ref[...] = _rms(x_ref[...], g_ref[...]).astype(BF16)


def _prenorm(x2d, gains, gi):
    n = x2d.shape[0]
    tm = 1024
    return pl.pallas_call(
        _prenorm_kernel,
        grid=(n // tm,),
        in_specs=[pl.BlockSpec((tm, D_MODEL), lambda i: (i, 0)),
                  _gain_spec(gi)],
        out_specs=pl.BlockSpec((tm, D_MODEL), lambda i: (i, 0)),
        out_shape=jax.ShapeDtypeStruct((n, D_MODEL), BF16),
        compiler_params=_params("arbitrary"),
        name="prenorm",
    )(x2d, gains)


def _cast_kernel(w_ref, o_ref):
    o_ref[...] = w_ref[...].astype(BF16)


def _cast_bf16(w, layer, col0=0, ncols=None):
    _, rows, cols = w.shape
    ncols = cols if ncols is None else ncols
    cb = math.gcd(col0, ncols) if col0 else ncols
    rb = math.gcd(rows, CAST_ROWS)
    assert cb % LANES == 0 and rb % 16 == 0
    return pl.pallas_call(
        _cast_kernel,
        grid=(rows // rb, ncols // cb),
        in_specs=[pl.BlockSpec((None, rb, cb), lambda i, j: (layer, i, col0 // cb + j))],
        out_specs=pl.BlockSpec((rb, cb), lambda i, j: (i, j)),
        out_shape=jax.ShapeDtypeStruct((rows, ncols), BF16),
        compiler_params=_params("arbitrary", "arbitrary"),
        name="cast_bf16",
    )(w)


def _filter_mlp_kernel(z_ref, w1_ref, b1_ref, w2_ref, b2_ref, w3_ref, f_ref, t_ref, dl_ref,
                       hs_ref, hd_ref):
    hp = lax.Precision.HIGHEST
    h = jnp.sin(f_ref[0:1, :] * (jnp.dot(z_ref[...], w1_ref[...], precision=hp) + b1_ref[...]))
    h = jnp.sin(f_ref[1:2, :] * (jnp.dot(h, w2_ref[...], precision=hp) + b2_ref[...]))
    decay = jnp.exp(-t_ref[...] * dl_ref[...])
    row = lax.broadcasted_iota(jnp.int32, (SEQ, HY_W), 0)
    h_hi = h.astype(BF16)
    h_lo = (h - h_hi.astype(F32)).astype(BF16)

    def out_layer(cols):
        w = w3_ref[:, cols]
        w_hi = w.astype(BF16)
        w_lo = (w - w_hi.astype(F32)).astype(BF16)
        return (jnp.dot(h_hi, w_hi, preferred_element_type=F32)
                + (jnp.dot(h_hi, w_lo, preferred_element_type=F32)
                   + jnp.dot(h_lo, w_hi, preferred_element_type=F32)))

    for o in range(HY_ORDER):
        c_f = o * HY_W
        c_b = HY_ORDER * HY_W + o * HY_W
        hf = out_layer(slice(c_f, c_f + HY_W)) * decay
        hb = out_layer(slice(c_b, c_b + HY_W)) * decay
        hb = jnp.where(row == 0, 0.0, hb)
        hs_ref[:, c_f:c_f + HY_W] = (hf + hb).astype(BF16)
        hd_ref[:, c_f:c_f + HY_W] = (hb - hf).astype(BF16)


def _filter_dft_kernel(c_ref, s_ref, hs_ref, hd_ref, kr_ref, ki_ref):
    kr_ref[...] = jnp.dot(c_ref[...], hs_ref[...], preferred_element_type=F32) * (2.0 / FFT_N)
    ki_ref[...] = jnp.dot(s_ref[...], hd_ref[...], preferred_element_type=F32) * (2.0 / FFT_N)


def _hyena_filters(dft_c, dft_s, w1p, b1, w2, b2, w3, freq, layer):
    t = jnp.linspace(0.0, 1.0, SEQ, dtype=F32)[:, None]
    bands = (HY_EMB - 1) // 2
    w = 2.0 * math.pi * jnp.arange(SEQ, dtype=F32)[:, None] / SEQ
    f = jnp.linspace(1e-4, bands - 1, bands, dtype=F32)[None, :]
    z = jnp.concatenate([t, jnp.cos(f * w), -jnp.sin(f * w)], axis=-1)
    z = jnp.pad(z, ((0, 0), (0, HY_HIDDEN - HY_EMB)))
    deltas = jnp.abs(jnp.linspace(math.log(HY_TARGET) / HY_SLOW_DECAY,
                                  math.log(HY_TARGET) / HY_FAST_DECAY, HY_W, dtype=F32))[None, :]
    width = HY_ORDER * HY_W
    whole = lambda a: pl.BlockSpec(a.shape, lambda i: (0,) * a.ndim)
    taps = pl.BlockSpec((SEQ, width), lambda i: (0, 0))
    hs, hd = pl.pallas_call(
        _filter_mlp_kernel,
        grid=(1,),
        in_specs=[whole(z)] + [_layer_spec(a, layer) for a in (w1p, b1, w2, b2, w3, freq)]
        + [whole(t), whole(deltas)],
        out_specs=(taps, taps),
        out_shape=(jax.ShapeDtypeStruct((SEQ, width), BF16),
                   jax.ShapeDtypeStruct((SEQ, width), BF16)),
        compiler_params=_params("arbitrary"),
        name="hyena_filter_mlp",
    )(z, w1p, b1, w2, b2, w3, freq, t, deltas)
    nt = width // COL_TILE
    full = pl.BlockSpec((SEQ, SEQ), lambda j: (0, 0))
    col = pl.BlockSpec((SEQ, COL_TILE), lambda j: (0, j))
    kr, ki = pl.pallas_call(
        _filter_dft_kernel,
        grid=(nt,),
        in_specs=[full, full, col, col],
        out_specs=(col, col),
        out_shape=(jax.ShapeDtypeStruct((SEQ, width), F32),
                   jax.ShapeDtypeStruct((SEQ, width), F32)),
        compiler_params=_params("arbitrary"),
        name="hyena_filter_dft",
    )(dft_c, dft_s, hs, hd)

    fam_shape = (FFT_RADIX, SEQ // FFT_RADIX, width)
    return kr.reshape(fam_shape), ki.reshape(fam_shape)


def _hyena_kernel(hn_ref, wv_ref, w1_ref, w2_ref, sv_ref, s1_ref, s2_ref, cf_ref, sf_ref, ct_ref,
                  st_ref, kr0_ref, ki0_ref, kr1_ref, ki1_ref, bias_ref, o_ref,
                  slab_v_ref, slab_1_ref, slab_2_ref, slab_o_ref):
    radix = FFT_RADIX
    rows = SEQ // radix
    n_slab = COL_TILE // LANES
    zero_row = jnp.zeros((CONV_PAD, COL_TILE), F32)

    def project(w_ref, slab_ref):
        half = SEQ // 2
        for top in (0, half):
            u = jnp.dot(hn_ref[top:top + half, :], w_ref[...], preferred_element_type=F32)
            for j in range(n_slab):
                slab_ref[j, top:top + half, :] = u[:, j * LANES:(j + 1) * LANES]
        return [jnp.concatenate([slab_ref[j, pl.ds(r, rows, stride=radix), :] for j in range(n_slab)],
                                axis=1) for r in range(radix)]

    def short_conv(u, w_ref):
        n = u[0].shape[0]
        prev_wrap = pltpu.roll(jnp.concatenate([u[-1], zero_row], axis=0), 1, 0)[:n]
        next_wrap = pltpu.roll(jnp.concatenate([zero_row, u[0]], axis=0), n + CONV_PAD - 1, 0)[CONV_PAD:]
        prev = [prev_wrap] + u[:-1]
        nxt = u[1:] + [next_wrap]
        return [prev[r] * w_ref[0:1, :] + u[r] * w_ref[1:2, :] + nxt[r] * w_ref[2:3, :]
                for r in range(radix)]

    def cmul(ar, ai, br, bi):
        return ar * br - ai * bi, ar * bi + ai * br

    def transform(x):
        xb = [v.astype(BF16) for v in x]
        tr = [jnp.dot(cf_ref[r], xb[r], preferred_element_type=F32) for r in range(radix)]
        ti = [-jnp.dot(sf_ref[r], xb[r], preferred_element_type=F32) for r in range(radix)]
        return tr, ti

    def filter_and_invert(tr, ti, kr_ref, ki_ref):
        ar, ai = tr[0] + tr[2], ti[0] + ti[2]
        br, bi = tr[0] - tr[2], ti[0] - ti[2]
        cr, ci = tr[1] + tr[3], ti[1] + ti[3]
        dr, di = tr[1] - tr[3], ti[1] - ti[3]
        fam = [(ar + cr, ai + ci), (br + di, bi - dr), (br - di, -bi - dr), (ar - cr, ci - ai)]
        y = [cmul(fr, fi, kr_ref[f], ki_ref[f]) for f, (fr, fi) in enumerate(fam)]
        er, ei = y[0][0] + y[3][0], y[0][1] - y[3][1]
        fr, fi = y[0][0] - y[3][0], y[0][1] + y[3][1]
        gr, gi = y[1][0] + y[2][0], y[1][1] - y[2][1]
        hr, hi = y[1][0] - y[2][0], y[1][1] + y[2][1]
        p = [(er + gr, ei + gi), (fr - hi, fi + hr), (er - gr, ei - gi), (fr + hi, fi - hr)]
        out = []
        for r in range(radix):
            pr, pi_ = p[r]
            out.append(jnp.dot(ct_ref[r], pr.astype(BF16), preferred_element_type=F32)
                       - jnp.dot(st_ref[r], pi_.astype(BF16), preferred_element_type=F32))
        return out

    v = short_conv(project(wv_ref, slab_v_ref), sv_ref)
    spectrum = transform(v)
    x1 = short_conv(project(w1_ref, slab_1_ref), s1_ref)
    x2 = short_conv(project(w2_ref, slab_2_ref), s2_ref)
    y = filter_and_invert(*spectrum, kr0_ref, ki0_ref)
    z = [x1[r] * (y[r] + v[r] * bias_ref[0:1, :]) for r in range(radix)]
    y = filter_and_invert(*transform(z), kr1_ref, ki1_ref)
    for r in range(radix):
        out = x2[r] * (y[r] + z[r] * bias_ref[1:2, :])
        for j in range(n_slab):
            slab_o_ref[j, pl.ds(r, rows, stride=radix), :] = out[:, j * LANES:(j + 1) * LANES]
    o_ref[...] = jnp.concatenate([slab_o_ref[j] for j in range(n_slab)], axis=1).astype(BF16)


def _hyena_mixer(hn, w_hy, short_w, tables, kr, ki, bias, layer, tile):
    b = hn.shape[0]
    nt = HY_W // COL_TILE
    rows = SEQ // FFT_RADIX
    once = pl.Buffered(1)
    cf, sf, ct, st = tables

    def fixed(shape, index):
        return pl.BlockSpec(shape, lambda i: index, pipeline_mode=once)

    def wcol(k):
        return fixed((D_MODEL, COL_TILE), (0, k * nt + tile))

    def scol(k):
        return _layer_spec(short_w, layer, COL_TILE, k * nt + tile)

    def kfam(o):
        return fixed((FFT_RADIX, rows, COL_TILE), (0, 0, o * nt + tile))

    small = fixed((FFT_RADIX, rows, rows), (0, 0, 0))
    slab = pltpu.VMEM((COL_TILE // LANES, SEQ, LANES), F32)
    out = pl.pallas_call(
        _hyena_kernel,
        grid=(b,),
        in_specs=[pl.BlockSpec((None, SEQ, D_MODEL), lambda i: (i, 0, 0)),
                  wcol(0), wcol(1), wcol(2), scol(0), scol(1), scol(2),
                  small, small, small, small,
                  kfam(0), kfam(0), kfam(1), kfam(1),
                  _layer_spec(bias, layer, COL_TILE, tile)],
        out_specs=pl.BlockSpec((None, SEQ, COL_TILE), lambda i: (i, 0, 0)),
        out_shape=jax.ShapeDtypeStruct((b, SEQ, COL_TILE), BF16),
        scratch_shapes=[slab, slab, slab, slab],
        compiler_params=_params("arbitrary"),
        name="hyena_mixer",
    )(hn, w_hy, w_hy, w_hy, short_w, short_w, short_w, cf, sf, ct, st, kr, ki, kr, ki, bias)
    return out.reshape(b * SEQ, COL_TILE)


def _shortconv_kernel(hn_ref, wb_ref, wc_ref, wx_ref, cw_ref, o_ref):
    wb = wb_ref[...].astype(BF16)
    wc = wc_ref[...].astype(BF16)
    wx = wx_ref[...].astype(BF16)

    def project(r0):
        lo = max(r0 - CONV_HALO, 0)
        hi = min(r0 + CONV_CHUNK + CONV_HALO, SEQ)
        hn = hn_ref[lo:hi, :]
        return (r0, r0 - lo, jnp.dot(hn, wb, preferred_element_type=F32),
                jnp.dot(hn, wc, preferred_element_type=F32), jnp.dot(hn, wx, preferred_element_type=F32))

    def finish(r0, skip, bg, cg, xi):
        out = bg * _dwconv3(cg * xi, cw_ref)
        o_ref[r0:r0 + CONV_CHUNK, :] = out[skip:skip + CONV_CHUNK].astype(BF16)

    pending = None
    for r0 in range(0, SEQ, CONV_CHUNK):
        current = project(r0)
        if pending is not None:
            finish(*pending)
        pending = current
    finish(*pending)


def _w_in_cols(layer, width, first):
    return lambda k, nt: pl.BlockSpec((None, D_MODEL, width),
                                      lambda j, i: (layer, 0, first + k * nt + j))


def _shortconv_mixer(hn, w_in, layer, conv_w):
    b = hn.shape[0]
    nt = SC_W // COL_TILE
    wcol = _w_in_cols(layer, COL_TILE, (3 * HY_W + 3 * NA_W) // COL_TILE)
    return pl.pallas_call(
        _shortconv_kernel,
        grid=(nt, b),
        in_specs=[pl.BlockSpec((None, SEQ, D_MODEL), lambda j, i: (i, 0, 0)),
                  wcol(0, nt), wcol(1, nt), wcol(2, nt),
                  pl.BlockSpec((None, 3, COL_TILE), lambda j, i: (layer, 0, j))],
        out_specs=pl.BlockSpec((None, SEQ, COL_TILE), lambda j, i: (i, 0, j)),
        out_shape=jax.ShapeDtypeStruct((b, SEQ, SC_W), BF16),
        compiler_params=_params("arbitrary", "arbitrary"),
        name="shortconv_mixer",
    )(hn, w_in, w_in, w_in, conv_w)


def _na_kernel(hn_ref, wq_ref, wk_ref, wv_ref, bias_ref, o_ref, q_ref, k_ref, v_ref, s_ref):
    hn = hn_ref[...]
    q = jnp.dot(hn, wq_ref[...].astype(BF16), preferred_element_type=F32)
    q_ref[...] = (q * (NA_HEAD_DIM ** -0.5)).astype(BF16)
    k_ref[...] = jnp.dot(hn, wk_ref[...].astype(BF16), preferred_element_type=F32).astype(BF16)
    v_ref[...] = jnp.dot(hn, wv_ref[...].astype(BF16), preferred_element_type=F32).astype(BF16)
    gw = NA_GROUP * NA_HEAD_DIM
    same_head = (lax.broadcasted_iota(jnp.int32, (gw, gw), 0) // NA_HEAD_DIM
                 == lax.broadcasted_iota(jnp.int32, (gw, gw), 1) // NA_HEAD_DIM)

    def scores(r):
        w0 = jnp.clip(r - NA_WIN_ROWS // 2, 0, NA_ROWS - NA_WIN_ROWS)
        off = w0 - r + (NA_WIN_ROWS - 1)
        q0 = pl.multiple_of(r * GRID_W, GRID_W)
        k0 = pl.multiple_of(w0 * GRID_W, GRID_W)
        q_row = q_ref[pl.ds(q0, GRID_W), :]
        q_heads = jnp.where(same_head, jnp.concatenate([q_row] * NA_GROUP, axis=0), 0)
        s = lax.dot_general(q_heads, k_ref[pl.ds(k0, NA_KEYS), :], (((1,), (1,)), ((), ())),
                            preferred_element_type=F32)
        bias = jnp.concatenate(
            [jnp.concatenate([bias_ref[h, off + 2 * m] for m in range(NA_WIN_ROWS // 2)], axis=1)
             for h in range(NA_GROUP)], axis=0)
        return s + bias

    def attend(r, s):
        w0 = jnp.clip(r - NA_WIN_ROWS // 2, 0, NA_ROWS - NA_WIN_ROWS)
        q0 = pl.multiple_of(r * GRID_W, GRID_W)
        k0 = pl.multiple_of(w0 * GRID_W, GRID_W)
        p = jnp.exp(s - jnp.max(s, axis=-1, keepdims=True))
        inv = 1.0 / jnp.sum(p, axis=-1, keepdims=True)
        pv = jnp.dot(p.astype(BF16), v_ref[pl.ds(k0, NA_KEYS), :], preferred_element_type=F32)
        pv = jnp.where(same_head, pv * inv, 0.0)
        out = pv[0:GRID_W]
        for h in range(1, NA_GROUP):
            out = out + pv[h * GRID_W:(h + 1) * GRID_W]
        o_ref[pl.ds(q0, GRID_W), :] = out.astype(BF16)

    groups = NA_ROWS // NA_ROW_GROUP
    for t in range(NA_ROW_GROUP):
        s_ref[t] = scores(jnp.int32(t))

    def rows_body(i, carry):
        for t in range(NA_ROW_GROUP):
            s_next = scores(i * NA_ROW_GROUP + t)
            attend((i - 1) * NA_ROW_GROUP + t, s_ref[t])
            s_ref[t] = s_next
        return carry

    lax.fori_loop(1, groups, rows_body, 0)
    for t in range(NA_ROW_GROUP):
        attend(jnp.int32((groups - 1) * NA_ROW_GROUP + t), s_ref[t])


def _na_bias(rpb):
    c = jnp.arange(GRID_W)
    col_start = jnp.clip(c - NA_WIN_COLS // 2, 0, GRID_W - NA_WIN_COLS)
    col_mask = (c[None, :] >= col_start[:, None]) & (c[None, :] < col_start[:, None] + NA_WIN_COLS)
    dc = jnp.clip(c[None, :] - c[:, None] + NA_WIN_COLS - 1, 0, 2 * NA_WIN_COLS - 2)
    pick = (dc[None] == jnp.arange(2 * NA_WIN_COLS - 1)[:, None, None]).astype(F32)
    table = jnp.einsum("lhrd,dqc->lhrqc", rpb.astype(F32), pick, precision=lax.Precision.HIGHEST)
    table = table + jnp.where(col_mask, 0.0, -1e30)
    return jnp.concatenate([table[:, :, :-1], table[:, :, 1:]], axis=-1)


def _na_mixer(hn, w_in, layer, bias):
    b = hn.shape[0]
    gw = NA_GROUP * NA_HEAD_DIM
    ng = NA_W // gw
    wcol = _w_in_cols(layer, gw, 3 * HY_W // gw)
    return pl.pallas_call(
        _na_kernel,
        grid=(ng, b),
        in_specs=[pl.BlockSpec((None, SEQ, D_MODEL), lambda j, i: (i, 0, 0)),
                  wcol(0, ng), wcol(1, ng), wcol(2, ng),
                  pl.BlockSpec((None, NA_GROUP, 2 * NA_WIN_ROWS - 2, GRID_W, 2 * GRID_W),
                               lambda j, i: (layer, j, 0, 0, 0))],
        out_specs=pl.BlockSpec((None, SEQ, gw), lambda j, i: (i, 0, j)),
        out_shape=jax.ShapeDtypeStruct((b, SEQ, NA_W), BF16),
        scratch_shapes=[pltpu.VMEM((SEQ, gw), BF16) for _ in range(3)]
        + [pltpu.VMEM((NA_ROW_GROUP, gw, NA_KEYS), F32)],
        compiler_params=_params("arbitrary", "arbitrary"),
        name="na_mixer",
    )(hn, w_in, w_in, w_in, bias)


def _merge_kernel(hn_ref, x_ref, ya0_ref, ya1_ref, yb_ref, yc_ref, wg_ref, gb_ref, wb_ref, wo_ref,
                  g_ref, o_ref):
    def gated_sum(rows):
        hn = hn_ref[rows, :]
        ya = jnp.concatenate([ya0_ref[rows, :], ya1_ref[rows, :]], axis=1)
        merged = None
        for i, y in enumerate((ya, yb_ref[rows, :], yc_ref[rows, :])):
            pre = jnp.dot(hn, wg_ref[:, i * D_MODEL:(i + 1) * D_MODEL], preferred_element_type=F32)
            gate = jax.nn.sigmoid(pre + gb_ref[i:i + 1, :])
            term = gate * jnp.dot(y, wb_ref[i], preferred_element_type=F32)
            merged = term if merged is None else merged + term
        return merged.astype(BF16)

    chunks = [slice(r, r + ROW_TILE) for r in range(0, MERGE_ROWS, ROW_TILE)]
    merged = [gated_sum(rows) for rows in chunks]
    for rows, m in zip(chunks, merged):
        out = jnp.dot(m, wo_ref[...], preferred_element_type=F32)
        o_ref[rows, :] = x_ref[rows, :] + _rms(out, g_ref[...])


def _merge(hn2d, x2d, ya0, ya1, yb, yc, w_gate, gate_bias, w_branch, w_out, gains, layer):
    n = x2d.shape[0]
    tm = MERGE_ROWS
    once = pl.Buffered(1)
    rows = lambda w: pl.BlockSpec((tm, w), lambda i: (i, 0))
    return pl.pallas_call(
        _merge_kernel,
        grid=(n // tm,),
        in_specs=[rows(D_MODEL), rows(D_MODEL), rows(COL_TILE), rows(COL_TILE), rows(NA_W), rows(SC_W),
                  pl.BlockSpec((D_MODEL, N_BRANCH * D_MODEL), lambda i: (0, 0), pipeline_mode=once),
                  _layer_spec(gate_bias, layer),
                  pl.BlockSpec((N_BRANCH, HY_W, D_MODEL), lambda i: (0, 0, 0), pipeline_mode=once),
                  pl.BlockSpec((D_MODEL, D_MODEL), lambda i: (0, 0), pipeline_mode=once),
                  _gain_spec(6 * layer + 1)],
        out_specs=rows(D_MODEL),
        out_shape=jax.ShapeDtypeStruct((n, D_MODEL), F32),
        compiler_params=_params("arbitrary"),
        name="merge",
    )(hn2d, x2d, ya0, ya1, yb, yc, w_gate, gate_bias, w_branch, w_out, gains)


def _kv_kernel(m_ref, g_ref, w_ref, o_ref):
    mn = _rms(m_ref[...], g_ref[...]).astype(BF16)
    o_ref[...] = jnp.dot(mn, w_ref[...], preferred_element_type=F32).astype(BF16)


def _mem_kv(mem, mem_norm, layer, wkv):
    b = mem.shape[0]
    return pl.pallas_call(
        _kv_kernel,
        grid=(b,),
        in_specs=[pl.BlockSpec((None, N_MEM, D_MODEL), lambda i: (i, 0, 0)),
                  _layer_spec(mem_norm, layer),
                  pl.BlockSpec((D_MODEL, 2 * D_MODEL), lambda i: (0, 0))],
        out_specs=pl.BlockSpec((None, N_MEM, 2 * D_MODEL), lambda i: (i, 0, 0)),
        out_shape=jax.ShapeDtypeStruct((b, N_MEM, 2 * D_MODEL), BF16),
        compiler_params=_params("arbitrary"),
        name="mem_kv",
    )(mem, mem_norm, wkv)


def _xattn_kernel(x_ref, kv_ref, wq_ref, wo_ref, gq_ref, go_ref, gn_ref, o_ref, hn_ref):
    chunks = [slice(r, r + ROW_TILE) for r in range(0, XA_ROWS, ROW_TILE)]
    head_cols = [slice(i * XA_HEAD_DIM, (i + 1) * XA_HEAD_DIM) for i in range(XA_HEADS)]

    def query(rows):
        h = _rms(x_ref[rows, :], gq_ref[...]).astype(BF16)
        q = jnp.dot(h, wq_ref[...], preferred_element_type=F32) * (XA_HEAD_DIM ** -0.5)
        return q.astype(BF16)

    def scores(q):
        return [lax.dot_general(q[:, sl], kv_ref[:, sl], (((1,), (1,)), ((), ())),
                                preferred_element_type=F32) for sl in head_cols]

    def values(s_heads):
        heads = []
        for i, s in enumerate(s_heads):
            vm = kv_ref[:, D_MODEL + i * XA_HEAD_DIM:D_MODEL + (i + 1) * XA_HEAD_DIM]
            p = jnp.exp(s - jnp.max(s, axis=-1, keepdims=True))
            den = jnp.sum(p, axis=-1, keepdims=True)
            heads.append((jnp.dot(p.astype(BF16), vm, preferred_element_type=F32) / den).astype(BF16))
        return jnp.concatenate(heads, axis=-1)

    s_all = [scores(q) for q in [query(rows) for rows in chunks]]
    attended = [values(s) for s in s_all]
    for rows, a in zip(chunks, attended):
        o = jnp.dot(a, wo_ref[...], preferred_element_type=F32)
        xn = x_ref[rows, :] + _rms(o, go_ref[...])
        o_ref[rows, :] = xn
        hn_ref[rows, :] = _rms(xn, gn_ref[...]).astype(BF16)


def _xattn(x, kv, wq, wo, gains, layer):
    b = x.shape[0]
    tm = XA_ROWS
    once = pl.Buffered(1)
    rows = pl.BlockSpec((None, tm, D_MODEL), lambda i, j: (i, j, 0))
    wfull = pl.BlockSpec((D_MODEL, D_MODEL), lambda i, j: (0, 0), pipeline_mode=once)
    return pl.pallas_call(
        _xattn_kernel,
        grid=(b, SEQ // tm),
        in_specs=[rows, pl.BlockSpec((None, N_MEM, 2 * D_MODEL), lambda i, j: (i, 0, 0)),
                  wfull, wfull, _gain_spec(6 * layer + 2), _gain_spec(6 * layer + 3),
                  _gain_spec(6 * layer + 4)],
        out_specs=(rows, rows),
        out_shape=(jax.ShapeDtypeStruct((b, SEQ, D_MODEL), F32),
                   jax.ShapeDtypeStruct((b, SEQ, D_MODEL), BF16)),
        compiler_params=_params("arbitrary", "arbitrary"),
        name="xattn",
    )(x, kv, wq, wo, gains, gains, gains)


def _gelu_tanh(x):
    c = math.sqrt(2.0 / math.pi)
    half = 0.5 * x
    return half + half * jnp.tanh(x * (c + (c * 0.044715) * (x * x)))


def _ffn_kernel(hn_ref, wu_ref, cw_ref, wd_ref, o_ref):
    def up(r0, c0, c1):
        lo = max(r0 - CONV_HALO, 0)
        hi = min(r0 + CONV_CHUNK + CONV_HALO, SEQ)
        hn = hn_ref[lo:hi, :]
        ug = jnp.dot(hn, wu_ref[:, c0:c1], preferred_element_type=F32)
        uv = jnp.dot(hn, wu_ref[:, D_FF + c0:D_FF + c1], preferred_element_type=F32)
        return r0, r0 - lo, c0, c1, ug, uv

    def down(r0, skip, c0, c1, ug, uv):
        act = (_gelu_tanh(_dwconv3(ug, cw_ref.at[:, c0:c1]))
               * _dwconv3(uv, cw_ref.at[:, D_FF + c0:D_FF + c1]))
        act = act[skip:skip + CONV_CHUNK].astype(BF16)
        part = jnp.dot(act, wd_ref[c0:c1, :], preferred_element_type=F32)
        if c0 == 0:
            o_ref[r0:r0 + CONV_CHUNK, :] = part
        else:
            o_ref[r0:r0 + CONV_CHUNK, :] += part

    pending = None
    for r0 in range(0, SEQ, CONV_CHUNK):
        for c0, c1 in zip(FFN_CUTS[:-1], FFN_CUTS[1:]):
            current = up(r0, c0, c1)
            if pending is not None:
                down(*pending)
            pending = current
    down(*pending)


def _ffn(hn, w_up, w_conv, w_down, layer):
    b = hn.shape[0]
    once = pl.Buffered(1)
    return pl.pallas_call(
        _ffn_kernel,
        grid=(b,),
        in_specs=[pl.BlockSpec((None, SEQ, D_MODEL), lambda i: (i, 0, 0)),
                  pl.BlockSpec((D_MODEL, 2 * D_FF), lambda i: (0, 0), pipeline_mode=once),
                  _layer_spec(w_conv, layer),
                  pl.BlockSpec((D_FF, D_MODEL), lambda i: (0, 0), pipeline_mode=once)],
        out_specs=pl.BlockSpec((None, SEQ, D_MODEL), lambda i: (i, 0, 0), pipeline_mode=once),
        out_shape=jax.ShapeDtypeStruct((b, SEQ, D_MODEL), F32),
        compiler_params=_params("arbitrary"),
        name="ffn",
    )(hn, w_up, w_conv, w_down)


def _residual_kernel(x_ref, f_ref, g_ref, gn_ref, o_ref, hn_ref):
    xn = x_ref[...] + _rms(f_ref[...], g_ref[...])
    o_ref[...] = xn
    hn_ref[...] = _rms(xn, gn_ref[...]).astype(BF16)


def _residual_last_kernel(x_ref, f_ref, g_ref, o_ref):
    o_ref[...] = x_ref[...] + _rms(f_ref[...], g_ref[...])


def _residual(x2d, f2d, gains, layer, last):
    n = x2d.shape[0]
    tm = 1024
    rows = pl.BlockSpec((tm, D_MODEL), lambda i: (i, 0))
    x_shape = jax.ShapeDtypeStruct((n, D_MODEL), F32)
    if last:
        return pl.pallas_call(
            _residual_last_kernel,
            grid=(n // tm,),
            in_specs=[rows, rows, _gain_spec(6 * layer + 5)],
            out_specs=rows,
            out_shape=x_shape,
            compiler_params=_params("arbitrary"),
            name="residual_last",
        )(x2d, f2d, gains), None
    return pl.pallas_call(
        _residual_kernel,
        grid=(n // tm,),
        in_specs=[rows, rows, _gain_spec(6 * layer + 5), _gain_spec(6 * (layer + 1))],
        out_specs=(rows, rows),
        out_shape=(x_shape, jax.ShapeDtypeStruct((n, D_MODEL), BF16)),
        compiler_params=_params("arbitrary"),
        name="residual",
    )(x2d, f2d, gains, gains)


def _angle_tables(num, den):
    ang = (num % den).astype(F32) * (2.0 * math.pi / den)
    return jnp.cos(ang), jnp.sin(ang)


def _dft_tables():
    j = jnp.arange(SEQ, dtype=jnp.int32)
    rows = SEQ // FFT_RADIX
    q = rows // DFT_SPLIT
    up = jnp.arange(q, dtype=jnp.int32)
    a = jnp.concatenate([up, 2 * q + up, 2 * q - 1 - up, 4 * q - 1 - up]).reshape(2, 2 * q, 1, 1)
    b_up = jnp.arange(DFT_SPLIT, dtype=jnp.int32)
    b = jnp.stack([b_up, DFT_SPLIT - 1 - b_up]).reshape(2, 1, DFT_SPLIT, 1)
    ca, sa = _angle_tables(DFT_SPLIT * a * j, FFT_N)
    cb, sb = _angle_tables((2 * b + 1) * j, 2 * FFT_N)
    big_c = (ca * cb - sa * sb).reshape(SEQ, SEQ).astype(BF16)
    big_s = (sa * cb + ca * sb).reshape(SEQ, SEQ).astype(BF16)
    kappa = jnp.arange(rows, dtype=jnp.int32)[None, :, None]
    m = jnp.arange(rows, dtype=jnp.int32)[None, None, :]
    r = jnp.arange(FFT_RADIX, dtype=jnp.int32)[:, None, None]
    c0, s0 = _angle_tables((2 * kappa + 1) * m, 2 * FFT_SUB)
    cr, sr = _angle_tables((2 * kappa + 1) * r, 2 * FFT_N)
    cf = (c0 * cr - s0 * sr).astype(BF16)
    sf = (s0 * cr + c0 * sr).astype(BF16)
    small = (cf, sf, cf.transpose(0, 2, 1), sf.transpose(0, 2, 1))
    return big_c, big_s, small


def kernel(x, mem, norm_gains, mem_norm, w_in, gate_bias, hy_short_w, hy_w1, hy_b1, hy_w2, hy_b2,
           hy_w3, hy_freq, hy_bias, na_rpb, sc_conv_w, w_branch, w_out, xa_wq, xa_wkv, xa_wo,
           ffn_up, ffn_conv, ffn_down):
    b, l, d = x.shape
    depth = w_in.shape[0]
    assert (l, d) == (SEQ, D_MODEL) and mem.shape[1:] == (N_MEM, D_MODEL)
    n = b * l
    dft_c, dft_s, conv_tables = _dft_tables()
    gains = norm_gains.astype(F32).reshape(depth * 6, 1, d)
    mem_gain = mem_norm.astype(F32).reshape(depth, 1, d)
    w1p = jnp.pad(hy_w1.astype(F32), ((0, 0), (0, HY_HIDDEN - HY_EMB), (0, 0)))
    b1 = hy_b1.astype(F32).reshape(depth, 1, HY_HIDDEN)
    b2 = hy_b2.astype(F32).reshape(depth, 1, HY_HIDDEN)
    na_bias = _na_bias(na_rpb)
    w_branch2d = w_branch.reshape(depth, N_BRANCH * HY_W, d)
    x2d = x.reshape(n, d)
    hn = _prenorm(x2d, gains, 0)
    for i in range(depth):
        w_hyena = _cast_bf16(w_in, i, 0, 3 * HY_W)
        w_gate = _cast_bf16(w_in, i, 3 * HY_W + 3 * NA_W + 3 * SC_W, N_BRANCH * D_MODEL)
        kr, ki = _hyena_filters(dft_c, dft_s, w1p, b1, hy_w2, b2, hy_w3, hy_freq, i)
        hn3 = hn.reshape(b, l, d)
        ya = [_hyena_mixer(hn3, w_hyena, hy_short_w, conv_tables, kr, ki, hy_bias, i, tile)
              for tile in range(HY_W // COL_TILE)]
        yb = _na_mixer(hn3, w_in, i, na_bias)
        yc = _shortconv_mixer(hn3, w_in, i, sc_conv_w)
        x2d = _merge(hn, x2d, ya[0], ya[1], yb.reshape(n, NA_W), yc.reshape(n, SC_W), w_gate, gate_bias,
                     _cast_bf16(w_branch2d, i).reshape(N_BRANCH, HY_W, d), _cast_bf16(w_out, i), gains, i)
        kv = _mem_kv(mem, mem_gain, i, _cast_bf16(xa_wkv, i))
        x3, hn2 = _xattn(x2d.reshape(b, l, d), kv, _cast_bf16(xa_wq, i), _cast_bf16(xa_wo, i), gains, i)
        f = _ffn(hn2, _cast_bf16(ffn_up, i), ffn_conv, _cast_bf16(ffn_down, i), i)
        x2d, hn = _residual(x3.reshape(n, d), f.reshape(n, d), gains, i, i + 1 == depth)
    return x2d.reshape(b, l, d)
```

```python
import functools
import math

import jax
import jax.numpy as jnp
from jax import lax
from jax.experimental import pallas as pl
from jax.experimental.pallas import tpu as pltpu

D_MODEL = 1024
SEQ = 2048
N_MEM = 256
GRID_W = 64
HY_W = 512
NA_HEADS = 8
NA_HEAD_DIM = 64
NA_W = NA_HEADS * NA_HEAD_DIM
NA_WIN_ROWS = 8
NA_WIN_COLS = 16
SC_W = 512
XA_HEADS = 4
XA_HEAD_DIM = D_MODEL // XA_HEADS
D_FF = 2816
HY_ORDER = 2
HY_EMB = 33
HY_HIDDEN = 64
HY_FAST_DECAY = 0.3
HY_SLOW_DECAY = 1.5
HY_TARGET = 1e-2
N_BRANCH = 3
EPS = 1e-6

FFT_N = 2 * SEQ
FFT_RADIX = 4
FFT_SUB = FFT_N // FFT_RADIX
NA_ROWS = SEQ // GRID_W
NA_KEYS = NA_WIN_ROWS * GRID_W
NA_GROUP = 4
NA_ROW_GROUP = 4
LANES = 128
COL_TILE = 256
ROW_TILE = 512
MERGE_ROWS = 2 * ROW_TILE
XA_ROWS = 2 * ROW_TILE
CAST_STEPS = 8
CONV_PAD = 8
DFT_SPLIT = 64
CONV_CHUNK = 512
FFN_CUTS = (0, 6 * COL_TILE, D_FF)
CONV_HALO = 16
VMEM_LIMIT = 60 * 1024 * 1024

BF16 = jnp.bfloat16
F32 = jnp.float32


def _params(*sem):
    return pltpu.CompilerParams(dimension_semantics=sem, vmem_limit_bytes=VMEM_LIMIT)


def _gain_spec(index):
    return pl.BlockSpec((None, 1, D_MODEL), lambda *_: (index, 0, 0))


def _layer_spec(arr, layer, width=None, col=0):
    _, r, c = arr.shape
    return pl.BlockSpec((None, r, c if width is None else width), lambda *_: (layer, 0, col))


def _rms(xf, g):
    ms = jnp.mean(xf * xf, axis=-1, keepdims=True)
    return xf * lax.rsqrt(ms + EPS) * g


def _dwconv3(u, w_ref):
    n = u.shape[0]
    zeros = jnp.zeros((CONV_PAD, u.shape[1]), F32)
    padded = jnp.concatenate([zeros, u, zeros], axis=0)
    m = n + 2 * CONV_PAD
    prev = pltpu.roll(padded, 1, 0)[CONV_PAD:CONV_PAD + n]
    nxt = pltpu.roll(padded, m - 1, 0)[CONV_PAD:CONV_PAD + n]
    return prev * w_ref[0:1, :] + u * w_ref[1:2, :] + nxt * w_ref[2:3, :]


def _prenorm_kernel(x_ref, g_ref, o_ref):
    o_ref[...] = _rms(x_ref[...], g_ref[...]).astype(BF16)


def _prenorm(x2d, gains, gi):
    n = x2d.shape[0]
    tm = 1024
    return pl.pallas_call(
        _prenorm_kernel,
        grid=(n // tm,),
        in_specs=[pl.BlockSpec((tm, D_MODEL), lambda i: (i, 0)),
                  _gain_spec(gi)],
        out_specs=pl.BlockSpec((tm, D_MODEL), lambda i: (i, 0)),
        out_shape=jax.ShapeDtypeStruct((n, D_MODEL), BF16),
        compiler_params=_params("arbitrary"),
        name="prenorm",
    )(x2d, gains)


def _cast_kernel(parts, *refs):
    n_in = sum(parts)
    w_refs, o_refs = refs[:n_in], refs[n_in:]
    k = 0
    for o_ref, n in zip(o_refs, parts):
        width = o_ref.shape[1] // n
        for p in range(n):
            o_ref[:, p * width:(p + 1) * width] = w_refs[k][...].astype(BF16)
            k += 1


def _cast_layer(layer, *weights):
    in_specs, out_specs, out_shapes, operands, parts = [], [], [], [], []
    for w, col0, ncols in weights:
        rb = w.shape[1] // CAST_STEPS
        cb = math.gcd(col0, ncols) if col0 else ncols
        assert w.shape[1] % CAST_STEPS == 0 and rb % 16 == 0 and cb % LANES == 0
        parts.append(ncols // cb)
        for p in range(ncols // cb):
            in_specs.append(pl.BlockSpec((None, rb, cb),
                                         lambda i, cblk=col0 // cb + p: (layer, i, cblk)))
            operands.append(w)
        out_specs.append(pl.BlockSpec((rb, ncols), lambda i: (i, 0)))
        out_shapes.append(jax.ShapeDtypeStruct((w.shape[1], ncols), BF16))
    return pl.pallas_call(
        functools.partial(_cast_kernel, tuple(parts)),
        grid=(CAST_STEPS,),
        in_specs=in_specs,
        out_specs=tuple(out_specs),
        out_shape=tuple(out_shapes),
        compiler_params=_params("arbitrary"),
        name="cast_bf16",
    )(*operands)


def _filter_mlp_kernel(z_ref, w1_ref, b1_ref, w2_ref, b2_ref, w3_ref, f_ref, t_ref, dl_ref,
                       hs_ref, hd_ref):
    hp = lax.Precision.HIGHEST
    h = jnp.sin(f_ref[0:1, :] * (jnp.dot(z_ref[...], w1_ref[...], precision=hp) + b1_ref[...]))
    h = jnp.sin(f_ref[1:2, :] * (jnp.dot(h, w2_ref[...], precision=hp) + b2_ref[...]))
    decay = jnp.exp(-t_ref[...] * dl_ref[...])
    row = lax.broadcasted_iota(jnp.int32, (SEQ, HY_W), 0)
    h_hi = h.astype(BF16)
    h_lo = (h - h_hi.astype(F32)).astype(BF16)

    def out_layer(cols):
        w = w3_ref[:, cols]
        w_hi = w.astype(BF16)
        w_lo = (w - w_hi.astype(F32)).astype(BF16)
        return (jnp.dot(h_hi, w_hi, preferred_element_type=F32)
                + (jnp.dot(h_hi, w_lo, preferred_element_type=F32)
                   + jnp.dot(h_lo, w_hi, preferred_element_type=F32)))

    for o in range(HY_ORDER):
        c_f = o * HY_W
        c_b = HY_ORDER * HY_W + o * HY_W
        hf = out_layer(slice(c_f, c_f + HY_W)) * decay
        hb = out_layer(slice(c_b, c_b + HY_W)) * decay
        hb = jnp.where(row == 0, 0.0, hb)
        hs_ref[:, c_f:c_f + HY_W] = (hf + hb).astype(BF16)
        hd_ref[:, c_f:c_f + HY_W] = (hb - hf).astype(BF16)


def _filter_dft_kernel(c_ref, s_ref, hs_ref, hd_ref, kr_ref, ki_ref):
    kr_ref[...] = jnp.dot(c_ref[...], hs_ref[...], preferred_element_type=F32) * (2.0 / FFT_N)
    ki_ref[...] = jnp.dot(s_ref[...], hd_ref[...], preferred_element_type=F32) * (2.0 / FFT_N)


def _hyena_filters(dft_c, dft_s, w1p, b1, w2, b2, w3, freq, layer):
    t = jnp.linspace(0.0, 1.0, SEQ, dtype=F32)[:, None]
    bands = (HY_EMB - 1) // 2
    w = 2.0 * math.pi * jnp.arange(SEQ, dtype=F32)[:, None] / SEQ
    f = jnp.linspace(1e-4, bands - 1, bands, dtype=F32)[None, :]
    z = jnp.concatenate([t, jnp.cos(f * w), -jnp.sin(f * w)], axis=-1)
    z = jnp.pad(z, ((0, 0), (0, HY_HIDDEN - HY_EMB)))
    deltas = jnp.abs(jnp.linspace(math.log(HY_TARGET) / HY_SLOW_DECAY,
                                  math.log(HY_TARGET) / HY_FAST_DECAY, HY_W, dtype=F32))[None, :]
    width = HY_ORDER * HY_W
    whole = lambda a: pl.BlockSpec(a.shape, lambda i: (0,) * a.ndim)
    taps = pl.BlockSpec((SEQ, width), lambda i: (0, 0))
    hs, hd = pl.pallas_call(
        _filter_mlp_kernel,
        grid=(1,),
        in_specs=[whole(z)] + [_layer_spec(a, layer) for a in (w1p, b1, w2, b2, w3, freq)]
        + [whole(t), whole(deltas)],
        out_specs=(taps, taps),
        out_shape=(jax.ShapeDtypeStruct((SEQ, width), BF16),
                   jax.ShapeDtypeStruct((SEQ, width), BF16)),
        compiler_params=_params("arbitrary"),
        name="hyena_filter_mlp",
    )(z, w1p, b1, w2, b2, w3, freq, t, deltas)
    nt = width // COL_TILE
    full = pl.BlockSpec((SEQ, SEQ), lambda j: (0, 0))
    col = pl.BlockSpec((SEQ, COL_TILE), lambda j: (0, j))
    kr, ki = pl.pallas_call(
        _filter_dft_kernel,
        grid=(nt,),
        in_specs=[full, full, col, col],
        out_specs=(col, col),
        out_shape=(jax.ShapeDtypeStruct((SEQ, width), F32),
                   jax.ShapeDtypeStruct((SEQ, width), F32)),
        compiler_params=_params("arbitrary"),
        name="hyena_filter_dft",
    )(dft_c, dft_s, hs, hd)

    fam_shape = (FFT_RADIX, SEQ // FFT_RADIX, width)
    return kr.reshape(fam_shape), ki.reshape(fam_shape)


def _hyena_kernel(hn_ref, wv_ref, w1_ref, w2_ref, sv_ref, s1_ref, s2_ref, cf_ref, sf_ref, ct_ref,
                  st_ref, kr0_ref, ki0_ref, kr1_ref, ki1_ref, bias_ref, o_ref,
                  slab_v_ref, slab_1_ref, slab_2_ref, slab_o_ref):
    radix = FFT_RADIX
    rows = SEQ // radix
    n_slab = COL_TILE // LANES
    zero_row = jnp.zeros((CONV_PAD, COL_TILE), F32)

    def project(w_ref, slab_ref):
        half = SEQ // 2
        for top in (0, half):
            u = jnp.dot(hn_ref[top:top + half, :], w_ref[...], preferred_element_type=F32)
            for j in range(n_slab):
                slab_ref[j, top:top + half, :] = u[:, j * LANES:(j + 1) * LANES]
        return [jnp.concatenate([slab_ref[j, pl.ds(r, rows, stride=radix), :] for j in range(n_slab)],
                                axis=1) for r in range(radix)]

    def short_conv(u, w_ref):
        n = u[0].shape[0]
        prev_wrap = pltpu.roll(jnp.concatenate([u[-1], zero_row], axis=0), 1, 0)[:n]
        next_wrap = pltpu.roll(jnp.concatenate([zero_row, u[0]], axis=0), n + CONV_PAD - 1, 0)[CONV_PAD:]
        prev = [prev_wrap] + u[:-1]
        nxt = u[1:] + [next_wrap]
        return [prev[r] * w_ref[0:1, :] + u[r] * w_ref[1:2, :] + nxt[r] * w_ref[2:3, :]
                for r in range(radix)]

    def cmul(ar, ai, br, bi):
        return ar * br - ai * bi, ar * bi + ai * br

    def transform(x):
        xb = [v.astype(BF16) for v in x]
        tr = [jnp.dot(cf_ref[r], xb[r], preferred_element_type=F32) for r in range(radix)]
        ti = [-jnp.dot(sf_ref[r], xb[r], preferred_element_type=F32) for r in range(radix)]
        return tr, ti

    def filter_and_invert(tr, ti, kr_ref, ki_ref):
        ar, ai = tr[0] + tr[2], ti[0] + ti[2]
        br, bi = tr[0] - tr[2], ti[0] - ti[2]
        cr, ci = tr[1] + tr[3], ti[1] + ti[3]
        dr, di = tr[1] - tr[3], ti[1] - ti[3]
        fam = [(ar + cr, ai + ci), (br + di, bi - dr), (br - di, -bi - dr), (ar - cr, ci - ai)]
        y = [cmul(fr, fi, kr_ref[f], ki_ref[f]) for f, (fr, fi) in enumerate(fam)]
        er, ei = y[0][0] + y[3][0], y[0][1] - y[3][1]
        fr, fi = y[0][0] - y[3][0], y[0][1] + y[3][1]
        gr, gi = y[1][0] + y[2][0], y[1][1] - y[2][1]
        hr, hi = y[1][0] - y[2][0], y[1][1] + y[2][1]
        p = [(er + gr, ei + gi), (fr - hi, fi + hr), (er - gr, ei - gi), (fr + hi, fi - hr)]
        out = []
        for r in range(radix):
            pr, pi_ = p[r]
            out.append(jnp.dot(ct_ref[r], pr.astype(BF16), preferred_element_type=F32)
                       - jnp.dot(st_ref[r], pi_.astype(BF16), preferred_element_type=F32))
        return out

    v = short_conv(project(wv_ref, slab_v_ref), sv_ref)
    spectrum = transform(v)
    x1 = short_conv(project(w1_ref, slab_1_ref), s1_ref)
    x2 = short_conv(project(w2_ref, slab_2_ref), s2_ref)
    y = filter_and_invert(*spectrum, kr0_ref, ki0_ref)
    z = [x1[r] * (y[r] + v[r] * bias_ref[0:1, :]) for r in range(radix)]
    y = filter_and_invert(*transform(z), kr1_ref, ki1_ref)
    for r in range(radix):
        out = x2[r] * (y[r] + z[r] * bias_ref[1:2, :])
        for j in range(n_slab):
            slab_o_ref[j, pl.ds(r, rows, stride=radix), :] = out[:, j * LANES:(j + 1) * LANES]
    o_ref[...] = jnp.concatenate([slab_o_ref[j] for j in range(n_slab)], axis=1).astype(BF16)


def _hyena_mixer(hn, w_hy, short_w, tables, kr, ki, bias, layer, tile):
    b = hn.shape[0]
    nt = HY_W // COL_TILE
    rows = SEQ // FFT_RADIX
    once = pl.Buffered(1)
    cf, sf, ct, st = tables

    def fixed(shape, index):
        return pl.BlockSpec(shape, lambda i: index, pipeline_mode=once)

    def wcol(k):
        return fixed((D_MODEL, COL_TILE), (0, k * nt + tile))

    def scol(k):
        return _layer_spec(short_w, layer, COL_TILE, k * nt + tile)

    def kfam(o):
        return fixed((FFT_RADIX, rows, COL_TILE), (0, 0, o * nt + tile))

    small = fixed((FFT_RADIX, rows, rows), (0, 0, 0))
    slab = pltpu.VMEM((COL_TILE // LANES, SEQ, LANES), F32)
    out = pl.pallas_call(
        _hyena_kernel,
        grid=(b,),
        in_specs=[pl.BlockSpec((None, SEQ, D_MODEL), lambda i: (i, 0, 0)),
                  wcol(0), wcol(1), wcol(2), scol(0), scol(1), scol(2),
                  small, small, small, small,
                  kfam(0), kfam(0), kfam(1), kfam(1),
                  _layer_spec(bias, layer, COL_TILE, tile)],
        out_specs=pl.BlockSpec((None, SEQ, COL_TILE), lambda i: (i, 0, 0)),
        out_shape=jax.ShapeDtypeStruct((b, SEQ, COL_TILE), BF16),
        scratch_shapes=[slab, slab, slab, slab],
        compiler_params=_params("arbitrary"),
        name="hyena_mixer",
    )(hn, w_hy, w_hy, w_hy, short_w, short_w, short_w, cf, sf, ct, st, kr, ki, kr, ki, bias)
    return out.reshape(b * SEQ, COL_TILE)


def _shortconv_kernel(hn_ref, wb_ref, wc_ref, wx_ref, cw_ref, o_ref):
    wb = wb_ref[...].astype(BF16)
    wc = wc_ref[...].astype(BF16)
    wx = wx_ref[...].astype(BF16)

    def project(r0):
        lo = max(r0 - CONV_HALO, 0)
        hi = min(r0 + CONV_CHUNK + CONV_HALO, SEQ)
        hn = hn_ref[lo:hi, :]
        return (r0, r0 - lo, jnp.dot(hn, wb, preferred_element_type=F32),
                jnp.dot(hn, wc, preferred_element_type=F32), jnp.dot(hn, wx, preferred_element_type=F32))

    def finish(r0, skip, bg, cg, xi):
        out = bg * _dwconv3(cg * xi, cw_ref)
        o_ref[r0:r0 + CONV_CHUNK, :] = out[skip:skip + CONV_CHUNK].astype(BF16)

    pending = None
    for r0 in range(0, SEQ, CONV_CHUNK):
        current = project(r0)
        if pending is not None:
            finish(*pending)
        pending = current
    finish(*pending)


def _w_in_cols(layer, width, first):
    return lambda k, nt: pl.BlockSpec((None, D_MODEL, width),
                                      lambda j, i: (layer, 0, first + k * nt + j))


def _shortconv_mixer(hn, w_in, layer, conv_w):
    b = hn.shape[0]
    nt = SC_W // COL_TILE
    wcol = _w_in_cols(layer, COL_TILE, (3 * HY_W + 3 * NA_W) // COL_TILE)
    return pl.pallas_call(
        _shortconv_kernel,
        grid=(nt, b),
        in_specs=[pl.BlockSpec((None, SEQ, D_MODEL), lambda j, i: (i, 0, 0)),
                  wcol(0, nt), wcol(1, nt), wcol(2, nt),
                  pl.BlockSpec((None, 3, COL_TILE), lambda j, i: (layer, 0, j))],
        out_specs=pl.BlockSpec((None, SEQ, COL_TILE), lambda j, i: (i, 0, j)),
        out_shape=jax.ShapeDtypeStruct((b, SEQ, SC_W), BF16),
        compiler_params=_params("arbitrary", "arbitrary"),
        name="shortconv_mixer",
    )(hn, w_in, w_in, w_in, conv_w)


def _na_kernel(hn_ref, wq_ref, wk_ref, wv_ref, bias_ref, o_ref, q_ref, k_ref, v_ref, s_ref):
    hn = hn_ref[...]
    q = jnp.dot(hn, wq_ref[...].astype(BF16), preferred_element_type=F32)
    q_ref[...] = (q * (NA_HEAD_DIM ** -0.5)).astype(BF16)
    k_ref[...] = jnp.dot(hn, wk_ref[...].astype(BF16), preferred_element_type=F32).astype(BF16)
    v_ref[...] = jnp.dot(hn, wv_ref[...].astype(BF16), preferred_element_type=F32).astype(BF16)
    gw = NA_GROUP * NA_HEAD_DIM
    same_head = (lax.broadcasted_iota(jnp.int32, (gw, gw), 0) // NA_HEAD_DIM
                 == lax.broadcasted_iota(jnp.int32, (gw, gw), 1) // NA_HEAD_DIM)

    def scores(r):
        w0 = jnp.clip(r - NA_WIN_ROWS // 2, 0, NA_ROWS - NA_WIN_ROWS)
        off = w0 - r + (NA_WIN_ROWS - 1)
        q0 = pl.multiple_of(r * GRID_W, GRID_W)
        k0 = pl.multiple_of(w0 * GRID_W, GRID_W)
        q_row = q_ref[pl.ds(q0, GRID_W), :]
        q_heads = jnp.where(same_head, jnp.concatenate([q_row] * NA_GROUP, axis=0), 0)
        s = lax.dot_general(q_heads, k_ref[pl.ds(k0, NA_KEYS), :], (((1,), (1,)), ((), ())),
                            preferred_element_type=F32)
        bias = jnp.concatenate(
            [jnp.concatenate([bias_ref[h, off + 2 * m] for m in range(NA_WIN_ROWS // 2)], axis=1)
             for h in range(NA_GROUP)], axis=0)
        return s + bias

    def attend(r, s):
        w0 = jnp.clip(r - NA_WIN_ROWS // 2, 0, NA_ROWS - NA_WIN_ROWS)
        q0 = pl.multiple_of(r * GRID_W, GRID_W)
        k0 = pl.multiple_of(w0 * GRID_W, GRID_W)
        p = jnp.exp(s - jnp.max(s, axis=-1, keepdims=True))
        inv = 1.0 / jnp.sum(p, axis=-1, keepdims=True)
        pv = jnp.dot(p.astype(BF16), v_ref[pl.ds(k0, NA_KEYS), :], preferred_element_type=F32)
        pv = jnp.where(same_head, pv * inv, 0.0)
        out = pv[0:GRID_W]
        for h in range(1, NA_GROUP):
            out = out + pv[h * GRID_W:(h + 1) * GRID_W]
        o_ref[pl.ds(q0, GRID_W), :] = out.astype(BF16)

    groups = NA_ROWS // NA_ROW_GROUP
    for t in range(NA_ROW_GROUP):
        s_ref[t] = scores(jnp.int32(t))

    def rows_body(i, carry):
        for t in range(NA_ROW_GROUP):
            s_next = scores(i * NA_ROW_GROUP + t)
            attend((i - 1) * NA_ROW_GROUP + t, s_ref[t])
            s_ref[t] = s_next
        return carry

    lax.fori_loop(1, groups, rows_body, 0)
    for t in range(NA_ROW_GROUP):
        attend(jnp.int32((groups - 1) * NA_ROW_GROUP + t), s_ref[t])


def _na_bias(rpb):
    c = jnp.arange(GRID_W)
    col_start = jnp.clip(c - NA_WIN_COLS // 2, 0, GRID_W - NA_WIN_COLS)
    col_mask = (c[None, :] >= col_start[:, None]) & (c[None, :] < col_start[:, None] + NA_WIN_COLS)
    dc = jnp.clip(c[None, :] - c[:, None] + NA_WIN_COLS - 1, 0, 2 * NA_WIN_COLS - 2)
    pick = (dc[None] == jnp.arange(2 * NA_WIN_COLS - 1)[:, None, None]).astype(F32)
    table = jnp.einsum("lhrd,dqc->lhrqc", rpb.astype(F32), pick, precision=lax.Precision.HIGHEST)
    table = table + jnp.where(col_mask, 0.0, -1e30)
    return jnp.concatenate([table[:, :, :-1], table[:, :, 1:]], axis=-1)


def _na_mixer(hn, w_in, layer, bias):
    b = hn.shape[0]
    gw = NA_GROUP * NA_HEAD_DIM
    ng = NA_W // gw
    wcol = _w_in_cols(layer, gw, 3 * HY_W // gw)
    return pl.pallas_call(
        _na_kernel,
        grid=(ng, b),
        in_specs=[pl.BlockSpec((None, SEQ, D_MODEL), lambda j, i: (i, 0, 0)),
                  wcol(0, ng), wcol(1, ng), wcol(2, ng),
                  pl.BlockSpec((None, NA_GROUP, 2 * NA_WIN_ROWS - 2, GRID_W, 2 * GRID_W),
                               lambda j, i: (layer, j, 0, 0, 0))],
        out_specs=pl.BlockSpec((None, SEQ, gw), lambda j, i: (i, 0, j)),
        out_shape=jax.ShapeDtypeStruct((b, SEQ, NA_W), BF16),
        scratch_shapes=[pltpu.VMEM((SEQ, gw), BF16) for _ in range(3)]
        + [pltpu.VMEM((NA_ROW_GROUP, gw, NA_KEYS), F32)],
        compiler_params=_params("arbitrary", "arbitrary"),
        name="na_mixer",
    )(hn, w_in, w_in, w_in, bias)


def _merge_kernel(hn_ref, x_ref, ya0_ref, ya1_ref, yb_ref, yc_ref, wg_ref, gb_ref, wb_ref, wo_ref,
                  g_ref, o_ref):
    def gated_sum(rows):
        hn = hn_ref[rows, :]
        ya = jnp.concatenate([ya0_ref[rows, :], ya1_ref[rows, :]], axis=1)
        merged = None
        for i, y in enumerate((ya, yb_ref[rows, :], yc_ref[rows, :])):
            pre = jnp.dot(hn, wg_ref[:, i * D_MODEL:(i + 1) * D_MODEL], preferred_element_type=F32)
            gate = jax.nn.sigmoid(pre + gb_ref[i:i + 1, :])
            term = gate * jnp.dot(y, wb_ref[i], preferred_element_type=F32)
            merged = term if merged is None else merged + term
        return merged.astype(BF16)

    chunks = [slice(r, r + ROW_TILE) for r in range(0, MERGE_ROWS, ROW_TILE)]
    merged = [gated_sum(rows) for rows in chunks]
    for rows, m in zip(chunks, merged):
        out = jnp.dot(m, wo_ref[...], preferred_element_type=F32)
        o_ref[rows, :] = x_ref[rows, :] + _rms(out, g_ref[...])


def _merge(hn2d, x2d, ya0, ya1, yb, yc, w_gate, gate_bias, w_branch, w_out, gains, layer):
    n = x2d.shape[0]
    tm = MERGE_ROWS
    once = pl.Buffered(1)
    rows = lambda w: pl.BlockSpec((tm, w), lambda i: (i, 0))
    return pl.pallas_call(
        _merge_kernel,
        grid=(n // tm,),
        in_specs=[rows(D_MODEL), rows(D_MODEL), rows(COL_TILE), rows(COL_TILE), rows(NA_W), rows(SC_W),
                  pl.BlockSpec((D_MODEL, N_BRANCH * D_MODEL), lambda i: (0, 0), pipeline_mode=once),
                  _layer_spec(gate_bias, layer),
                  pl.BlockSpec((N_BRANCH, HY_W, D_MODEL), lambda i: (0, 0, 0), pipeline_mode=once),
                  pl.BlockSpec((D_MODEL, D_MODEL), lambda i: (0, 0), pipeline_mode=once),
                  _gain_spec(6 * layer + 1)],
        out_specs=rows(D_MODEL),
        out_shape=jax.ShapeDtypeStruct((n, D_MODEL), F32),
        compiler_params=_params("arbitrary"),
        name="merge",
    )(hn2d, x2d, ya0, ya1, yb, yc, w_gate, gate_bias, w_branch, w_out, gains)


def _kv_kernel(m_ref, g_ref, w_ref, o_ref):
    mn = _rms(m_ref[...], g_ref[...]).astype(BF16)
    o_ref[...] = jnp.dot(mn, w_ref[...], preferred_element_type=F32).astype(BF16)


def _mem_kv(mem, mem_norm, layer, wkv):
    b = mem.shape[0]
    return pl.pallas_call(
        _kv_kernel,
        grid=(b,),
        in_specs=[pl.BlockSpec((None, N_MEM, D_MODEL), lambda i: (i, 0, 0)),
                  _layer_spec(mem_norm, layer),
                  pl.BlockSpec((D_MODEL, 2 * D_MODEL), lambda i: (0, 0))],
        out_specs=pl.BlockSpec((None, N_MEM, 2 * D_MODEL), lambda i: (i, 0, 0)),
        out_shape=jax.ShapeDtypeStruct((b, N_MEM, 2 * D_MODEL), BF16),
        compiler_params=_params("arbitrary"),
        name="mem_kv",
    )(mem, mem_norm, wkv)


def _xattn_kernel(x_ref, kv_ref, wq_ref, wo_ref, gq_ref, go_ref, gn_ref, o_ref, hn_ref):
    chunks = [slice(r, r + ROW_TILE) for r in range(0, XA_ROWS, ROW_TILE)]
    head_cols = [slice(i * XA_HEAD_DIM, (i + 1) * XA_HEAD_DIM) for i in range(XA_HEADS)]

    def query(rows):
        h = _rms(x_ref[rows, :], gq_ref[...]).astype(BF16)
        q = jnp.dot(h, wq_ref[...], preferred_element_type=F32) * (XA_HEAD_DIM ** -0.5)
        return q.astype(BF16)

    def scores(q):
        return [lax.dot_general(q[:, sl], kv_ref[:, sl], (((1,), (1,)), ((), ())),
                                preferred_element_type=F32) for sl in head_cols]

    def values(s_heads):
        heads = []
        for i, s in enumerate(s_heads):
            vm = kv_ref[:, D_MODEL + i * XA_HEAD_DIM:D_MODEL + (i + 1) * XA_HEAD_DIM]
            p = jnp.exp(s - jnp.max(s, axis=-1, keepdims=True))
            den = jnp.sum(p, axis=-1, keepdims=True)
            heads.append((jnp.dot(p.astype(BF16), vm, preferred_element_type=F32) / den).astype(BF16))
        return jnp.concatenate(heads, axis=-1)

    s_all = [scores(q) for q in [query(rows) for rows in chunks]]
    attended = [values(s) for s in s_all]
    for rows, a in zip(chunks, attended):
        o = jnp.dot(a, wo_ref[...], preferred_element_type=F32)
        xn = x_ref[rows, :] + _rms(o, go_ref[...])
        o_ref[rows, :] = xn
        hn_ref[rows, :] = _rms(xn, gn_ref[...]).astype(BF16)


def _xattn(x, kv, wq, wo, gains, layer):
    b = x.shape[0]
    tm = XA_ROWS
    once = pl.Buffered(1)
    rows = pl.BlockSpec((None, tm, D_MODEL), lambda i, j: (i, j, 0))
    wfull = pl.BlockSpec((D_MODEL, D_MODEL), lambda i, j: (0, 0), pipeline_mode=once)
    return pl.pallas_call(
        _xattn_kernel,
        grid=(b, SEQ // tm),
        in_specs=[rows, pl.BlockSpec((None, N_MEM, 2 * D_MODEL), lambda i, j: (i, 0, 0)),
                  wfull, wfull, _gain_spec(6 * layer + 2), _gain_spec(6 * layer + 3),
                  _gain_spec(6 * layer + 4)],
        out_specs=(rows, rows),
        out_shape=(jax.ShapeDtypeStruct((b, SEQ, D_MODEL), F32),
                   jax.ShapeDtypeStruct((b, SEQ, D_MODEL), BF16)),
        compiler_params=_params("arbitrary", "arbitrary"),
        name="xattn",
    )(x, kv, wq, wo, gains, gains, gains)


def _gelu_tanh(x):
    c = math.sqrt(2.0 / math.pi)
    half = 0.5 * x
    return half + half * jnp.tanh(x * (c + (c * 0.044715) * (x * x)))


def _ffn_kernel(hn_ref, wu_ref, cw_ref, wd_ref, o_ref):
    def up(r0, c0, c1):
        lo = max(r0 - CONV_HALO, 0)
        hi = min(r0 + CONV_CHUNK + CONV_HALO, SEQ)
        hn = hn_ref[lo:hi, :]
        ug = jnp.dot(hn, wu_ref[:, c0:c1], preferred_element_type=F32)
        uv = jnp.dot(hn, wu_ref[:, D_FF + c0:D_FF + c1], preferred_element_type=F32)
        return r0, r0 - lo, c0, c1, ug, uv

    def down(r0, skip, c0, c1, ug, uv):
        act = (_gelu_tanh(_dwconv3(ug, cw_ref.at[:, c0:c1]))
               * _dwconv3(uv, cw_ref.at[:, D_FF + c0:D_FF + c1]))
        act = act[skip:skip + CONV_CHUNK].astype(BF16)
        part = jnp.dot(act, wd_ref[c0:c1, :], preferred_element_type=F32)
        if c0 == 0:
            o_ref[r0:r0 + CONV_CHUNK, :] = part
        else:
            o_ref[r0:r0 + CONV_CHUNK, :] += part

    pending = None
    for r0 in range(0, SEQ, CONV_CHUNK):
        for c0, c1 in zip(FFN_CUTS[:-1], FFN_CUTS[1:]):
            current = up(r0, c0, c1)
            if pending is not None:
                down(*pending)
            pending = current
    down(*pending)


def _ffn(hn, w_up, w_conv, w_down, layer):
    b = hn.shape[0]
    once = pl.Buffered(1)
    return pl.pallas_call(
        _ffn_kernel,
        grid=(b,),
        in_specs=[pl.BlockSpec((None, SEQ, D_MODEL), lambda i: (i, 0, 0)),
                  pl.BlockSpec((D_MODEL, 2 * D_FF), lambda i: (0, 0), pipeline_mode=once),
                  _layer_spec(w_conv, layer),
                  pl.BlockSpec((D_FF, D_MODEL), lambda i: (0, 0), pipeline_mode=once)],
        out_specs=pl.BlockSpec((None, SEQ, D_MODEL), lambda i: (i, 0, 0), pipeline_mode=once),
        out_shape=jax.ShapeDtypeStruct((b, SEQ, D_MODEL), F32),
        compiler_params=_params("arbitrary"),
        name="ffn",
    )(hn, w_up, w_conv, w_down)


def _residual_kernel(x_ref, f_ref, g_ref, gn_ref, o_ref, hn_ref):
    xn = x_ref[...] + _rms(f_ref[...], g_ref[...])
    o_ref[...] = xn
    hn_ref[...] = _rms(xn, gn_ref[...]).astype(BF16)


def _residual_last_kernel(x_ref, f_ref, g_ref, o_ref):
    o_ref[...] = x_ref[...] + _rms(f_ref[...], g_ref[...])


def _residual(x2d, f2d, gains, layer, last):
    n = x2d.shape[0]
    tm = 1024
    rows = pl.BlockSpec((tm, D_MODEL), lambda i: (i, 0))
    x_shape = jax.ShapeDtypeStruct((n, D_MODEL), F32)
    if last:
        return pl.pallas_call(
            _residual_last_kernel,
            grid=(n // tm,),
            in_specs=[rows, rows, _gain_spec(6 * layer + 5)],
            out_specs=rows,
            out_shape=x_shape,
            compiler_params=_params("arbitrary"),
            name="residual_last",
        )(x2d, f2d, gains), None
    return pl.pallas_call(
        _residual_kernel,
        grid=(n // tm,),
        in_specs=[rows, rows, _gain_spec(6 * layer + 5), _gain_spec(6 * (layer + 1))],
        out_specs=(rows, rows),
        out_shape=(x_shape, jax.ShapeDtypeStruct((n, D_MODEL), BF16)),
        compiler_params=_params("arbitrary"),
        name="residual",
    )(x2d, f2d, gains, gains)


def _angle_tables(num, den):
    ang = (num % den).astype(F32) * (2.0 * math.pi / den)
    return jnp.cos(ang), jnp.sin(ang)


def _dft_tables():
    j = jnp.arange(SEQ, dtype=jnp.int32)
    rows = SEQ // FFT_RADIX
    q = rows // DFT_SPLIT
    up = jnp.arange(q, dtype=jnp.int32)
    a = jnp.concatenate([up, 2 * q + up, 2 * q - 1 - up, 4 * q - 1 - up]).reshape(2, 2 * q, 1, 1)
    b_up = jnp.arange(DFT_SPLIT, dtype=jnp.int32)
    b = jnp.stack([b_up, DFT_SPLIT - 1 - b_up]).reshape(2, 1, DFT_SPLIT, 1)
    ca, sa = _angle_tables(DFT_SPLIT * a * j, FFT_N)
    cb, sb = _angle_tables((2 * b + 1) * j, 2 * FFT_N)
    big_c = (ca * cb - sa * sb).reshape(SEQ, SEQ).astype(BF16)
    big_s = (sa * cb + ca * sb).reshape(SEQ, SEQ).astype(BF16)
    kappa = jnp.arange(rows, dtype=jnp.int32)[None, :, None]
    m = jnp.arange(rows, dtype=jnp.int32)[None, None, :]
    r = jnp.arange(FFT_RADIX, dtype=jnp.int32)[:, None, None]
    c0, s0 = _angle_tables((2 * kappa + 1) * m, 2 * FFT_SUB)
    cr, sr = _angle_tables((2 * kappa + 1) * r, 2 * FFT_N)
    cf = (c0 * cr - s0 * sr).astype(BF16)
    sf = (s0 * cr + c0 * sr).astype(BF16)
    small = (cf, sf, cf.transpose(0, 2, 1), sf.transpose(0, 2, 1))
    return big_c, big_s, small


def kernel(x, mem, norm_gains, mem_norm, w_in, gate_bias, hy_short_w, hy_w1, hy_b1, hy_w2, hy_b2,
           hy_w3, hy_freq, hy_bias, na_rpb, sc_conv_w, w_branch, w_out, xa_wq, xa_wkv, xa_wo,
           ffn_up, ffn_conv, ffn_down):
    b, l, d = x.shape
    depth = w_in.shape[0]
    assert (l, d) == (SEQ, D_MODEL) and mem.shape[1:] == (N_MEM, D_MODEL)
    n = b * l
    dft_c, dft_s, conv_tables = _dft_tables()
    gains = norm_gains.astype(F32).reshape(depth * 6, 1, d)
    mem_gain = mem_norm.astype(F32).reshape(depth, 1, d)
    w1p = jnp.pad(hy_w1.astype(F32), ((0, 0), (0, HY_HIDDEN - HY_EMB), (0, 0)))
    b1 = hy_b1.astype(F32).reshape(depth, 1, HY_HIDDEN)
    b2 = hy_b2.astype(F32).reshape(depth, 1, HY_HIDDEN)
    na_bias = _na_bias(na_rpb)
    w_branch2d = w_branch.reshape(depth, N_BRANCH * HY_W, d)
    x2d = x.reshape(n, d)
    hn = _prenorm(x2d, gains, 0)
    for i in range(depth):
        w_hyena, w_gate, w_br, w_o, wq, wkv, wo, w_up, w_down = _cast_layer(
            i, (w_in, 0, 3 * HY_W), (w_in, 3 * HY_W + 3 * NA_W + 3 * SC_W, N_BRANCH * D_MODEL),
            (w_branch2d, 0, d), (w_out, 0, d), (xa_wq, 0, d), (xa_wkv, 0, 2 * d), (xa_wo, 0, d),
            (ffn_up, 0, 2 * D_FF), (ffn_down, 0, d))
        kr, ki = _hyena_filters(dft_c, dft_s, w1p, b1, hy_w2, b2, hy_w3, hy_freq, i)
        hn3 = hn.reshape(b, l, d)
        ya = [_hyena_mixer(hn3, w_hyena, hy_short_w, conv_tables, kr, ki, hy_bias, i, tile)
              for tile in range(HY_W // COL_TILE)]
        yb = _na_mixer(hn3, w_in, i, na_bias)
        yc = _shortconv_mixer(hn3, w_in, i, sc_conv_w)
        x2d = _merge(hn, x2d, ya[0], ya[1], yb.reshape(n, NA_W), yc.reshape(n, SC_W), w_gate, gate_bias,
                     w_br.reshape(N_BRANCH, HY_W, d), w_o, gains, i)
        kv = _mem_kv(mem, mem_gain, i, wkv)
        x3, hn2 = _xattn(x2d.reshape(b, l, d), kv, wq, wo, gains, i)
        f = _ffn(hn2, w_up, ffn_conv, w_down, i)
        x2d, hn = _residual(x3.reshape(n, d), f.reshape(n, d), gains, i, i + 1 == depth)
    return x2d.reshape(b, l, d)
```

```python
import functools
import math

import jax
import jax.numpy as jnp
from jax import lax
from jax.experimental import pallas as pl
from jax.experimental.pallas import tpu as pltpu

D_MODEL = 1024
SEQ = 2048
N_MEM = 256
GRID_W = 64
HY_W = 512
NA_HEADS = 8
NA_HEAD_DIM = 64
NA_W = NA_HEADS * NA_HEAD_DIM
NA_WIN_ROWS = 8
NA_WIN_COLS = 16
SC_W = 512
XA_HEADS = 4
XA_HEAD_DIM = D_MODEL // XA_HEADS
D_FF = 2816
HY_ORDER = 2
HY_EMB = 33
HY_HIDDEN = 64
HY_FAST_DECAY = 0.3
HY_SLOW_DECAY = 1.5
HY_TARGET = 1e-2
N_BRANCH = 3
EPS = 1e-6

FFT_N = 2 * SEQ
FFT_RADIX = 4
FFT_SUB = FFT_N // FFT_RADIX
NA_ROWS = SEQ // GRID_W
NA_KEYS = NA_WIN_ROWS * GRID_W
NA_GROUP = 4
NA_ROW_GROUP = 2
LANES = 128
COL_TILE = 256
ROW_TILE = 512
MERGE_ROWS = 2 * ROW_TILE
XA_ROWS = 2 * ROW_TILE
XA_CHUNK = 256
CAST_STEPS = 8
CONV_PAD = 8
DFT_SPLIT = 64
CONV_CHUNK = 512
FFN_CUTS = (0, 6 * COL_TILE, D_FF)
CONV_HALO = 16
VMEM_LIMIT = 60 * 1024 * 1024

BF16 = jnp.bfloat16
F32 = jnp.float32


def _params(*sem):
    return pltpu.CompilerParams(dimension_semantics=sem, vmem_limit_bytes=VMEM_LIMIT)


def _gain_spec(index):
    return pl.BlockSpec((None, 1, D_MODEL), lambda *_: (index, 0, 0))


def _layer_spec(arr, layer, width=None, col=0):
    _, r, c = arr.shape
    return pl.BlockSpec((None, r, c if width is None else width), lambda *_: (layer, 0, col))


def _rms(xf, g):
    ms = jnp.mean(xf * xf, axis=-1, keepdims=True)
    return xf * lax.rsqrt(ms + EPS) * g


def _dwconv3(u, w_ref):
    n = u.shape[0]
    zeros = jnp.zeros((CONV_PAD, u.shape[1]), F32)
    padded = jnp.concatenate([zeros, u, zeros], axis=0)
    m = n + 2 * CONV_PAD
    prev = pltpu.roll(padded, 1, 0)[CONV_PAD:CONV_PAD + n]
    nxt = pltpu.roll(padded, m - 1, 0)[CONV_PAD:CONV_PAD + n]
    return prev * w_ref[0:1, :] + u * w_ref[1:2, :] + nxt * w_ref[2:3, :]


def _prenorm_kernel(x_ref, g_ref, o_ref):
    o_ref[...] = _rms(x_ref[...], g_ref[...]).astype(BF16)


def _prenorm(x2d, gains, gi):
    n = x2d.shape[0]
    tm = 1024
    return pl.pallas_call(
        _prenorm_kernel,
        grid=(n // tm,),
        in_specs=[pl.BlockSpec((tm, D_MODEL), lambda i: (i, 0)),
                  _gain_spec(gi)],
        out_specs=pl.BlockSpec((tm, D_MODEL), lambda i: (i, 0)),
        out_shape=jax.ShapeDtypeStruct((n, D_MODEL), BF16),
        compiler_params=_params("arbitrary"),
        name="prenorm",
    )(x2d, gains)


def _cast_kernel(parts, *refs):
    n_in = sum(parts)
    w_refs, o_refs = refs[:n_in], refs[n_in:]
    k = 0
    for o_ref, n in zip(o_refs, parts):
        width = o_ref.shape[1] // n
        for p in range(n):
            o_ref[:, p * width:(p + 1) * width] = w_refs[k][...].astype(BF16)
            k += 1


def _cast_layer(layer, *weights):
    in_specs, out_specs, out_shapes, operands, parts = [], [], [], [], []
    for w, col0, ncols in weights:
        rb = w.shape[1] // CAST_STEPS
        cb = math.gcd(col0, ncols) if col0 else ncols
        assert w.shape[1] % CAST_STEPS == 0 and rb % 16 == 0 and cb % LANES == 0
        parts.append(ncols // cb)
        for p in range(ncols // cb):
            in_specs.append(pl.BlockSpec((None, rb, cb),
                                         lambda i, cblk=col0 // cb + p: (layer, i, cblk)))
            operands.append(w)
        out_specs.append(pl.BlockSpec((rb, ncols), lambda i: (i, 0)))
        out_shapes.append(jax.ShapeDtypeStruct((w.shape[1], ncols), BF16))
    return pl.pallas_call(
        functools.partial(_cast_kernel, tuple(parts)),
        grid=(CAST_STEPS,),
        in_specs=in_specs,
        out_specs=tuple(out_specs),
        out_shape=tuple(out_shapes),
        compiler_params=_params("arbitrary"),
        name="cast_bf16",
    )(*operands)


def _filter_mlp_kernel(z_ref, w1_ref, b1_ref, w2_ref, b2_ref, w3_ref, f_ref, t_ref, dl_ref,
                       hs_ref, hd_ref):
    hp = lax.Precision.HIGHEST
    h = jnp.sin(f_ref[0:1, :] * (jnp.dot(z_ref[...], w1_ref[...], precision=hp) + b1_ref[...]))
    h = jnp.sin(f_ref[1:2, :] * (jnp.dot(h, w2_ref[...], precision=hp) + b2_ref[...]))
    decay = jnp.exp(-t_ref[...] * dl_ref[...])
    row = lax.broadcasted_iota(jnp.int32, (SEQ, HY_W), 0)
    h_hi = h.astype(BF16)
    h_lo = (h - h_hi.astype(F32)).astype(BF16)

    def out_layer(cols):
        w = w3_ref[:, cols]
        w_hi = w.astype(BF16)
        w_lo = (w - w_hi.astype(F32)).astype(BF16)
        return (jnp.dot(h_hi, w_hi, preferred_element_type=F32)
                + (jnp.dot(h_hi, w_lo, preferred_element_type=F32)
                   + jnp.dot(h_lo, w_hi, preferred_element_type=F32)))

    for o in range(HY_ORDER):
        c_f = o * HY_W
        c_b = HY_ORDER * HY_W + o * HY_W
        hf = out_layer(slice(c_f, c_f + HY_W)) * decay
        hb = out_layer(slice(c_b, c_b + HY_W)) * decay
        hb = jnp.where(row == 0, 0.0, hb)
        hs_ref[:, c_f:c_f + HY_W] = (hf + hb).astype(BF16)
        hd_ref[:, c_f:c_f + HY_W] = (hb - hf).astype(BF16)


def _filter_dft_kernel(c_ref, s_ref, hs_ref, hd_ref, kr_ref, ki_ref):
    kr_ref[...] = jnp.dot(c_ref[...], hs_ref[...], preferred_element_type=F32) * (2.0 / FFT_N)
    ki_ref[...] = jnp.dot(s_ref[...], hd_ref[...], preferred_element_type=F32) * (2.0 / FFT_N)


def _hyena_filters(dft_c, dft_s, w1p, b1, w2, b2, w3, freq, layer):
    t = jnp.linspace(0.0, 1.0, SEQ, dtype=F32)[:, None]
    bands = (HY_EMB - 1) // 2
    w = 2.0 * math.pi * jnp.arange(SEQ, dtype=F32)[:, None] / SEQ
    f = jnp.linspace(1e-4, bands - 1, bands, dtype=F32)[None, :]
    z = jnp.concatenate([t, jnp.cos(f * w), -jnp.sin(f * w)], axis=-1)
    z = jnp.pad(z, ((0, 0), (0, HY_HIDDEN - HY_EMB)))
    deltas = jnp.abs(jnp.linspace(math.log(HY_TARGET) / HY_SLOW_DECAY,
                                  math.log(HY_TARGET) / HY_FAST_DECAY, HY_W, dtype=F32))[None, :]
    width = HY_ORDER * HY_W
    whole = lambda a: pl.BlockSpec(a.shape, lambda i: (0,) * a.ndim)
    taps = pl.BlockSpec((SEQ, width), lambda i: (0, 0))
    hs, hd = pl.pallas_call(
        _filter_mlp_kernel,
        grid=(1,),
        in_specs=[whole(z)] + [_layer_spec(a, layer) for a in (w1p, b1, w2, b2, w3, freq)]
        + [whole(t), whole(deltas)],
        out_specs=(taps, taps),
        out_shape=(jax.ShapeDtypeStruct((SEQ, width), BF16),
                   jax.ShapeDtypeStruct((SEQ, width), BF16)),
        compiler_params=_params("arbitrary"),
        name="hyena_filter_mlp",
    )(z, w1p, b1, w2, b2, w3, freq, t, deltas)
    nt = width // COL_TILE
    full = pl.BlockSpec((SEQ, SEQ), lambda j: (0, 0))
    col = pl.BlockSpec((SEQ, COL_TILE), lambda j: (0, j))
    kr, ki = pl.pallas_call(
        _filter_dft_kernel,
        grid=(nt,),
        in_specs=[full, full, col, col],
        out_specs=(col, col),
        out_shape=(jax.ShapeDtypeStruct((SEQ, width), F32),
                   jax.ShapeDtypeStruct((SEQ, width), F32)),
        compiler_params=_params("arbitrary"),
        name="hyena_filter_dft",
    )(dft_c, dft_s, hs, hd)

    fam_shape = (FFT_RADIX, SEQ // FFT_RADIX, width)
    return kr.reshape(fam_shape), ki.reshape(fam_shape)


def _hyena_kernel(hn_ref, wv_ref, w1_ref, w2_ref, sv_ref, s1_ref, s2_ref, cf_ref, sf_ref, ct_ref,
                  st_ref, kr0_ref, ki0_ref, kr1_ref, ki1_ref, bias_ref, o_ref,
                  slab_v_ref, slab_1_ref, slab_2_ref, slab_o_ref):
    radix = FFT_RADIX
    rows = SEQ // radix
    n_slab = COL_TILE // LANES
    zero_row = jnp.zeros((CONV_PAD, COL_TILE), F32)

    def project(w_ref, slab_ref):
        half = SEQ // 2
        for top in (0, half):
            u = jnp.dot(hn_ref[top:top + half, :], w_ref[...], preferred_element_type=F32)
            for j in range(n_slab):
                slab_ref[j, top:top + half, :] = u[:, j * LANES:(j + 1) * LANES]
        return [jnp.concatenate([slab_ref[j, pl.ds(r, rows, stride=radix), :] for j in range(n_slab)],
                                axis=1) for r in range(radix)]

    def short_conv(u, w_ref):
        n = u[0].shape[0]
        prev_wrap = pltpu.roll(jnp.concatenate([u[-1], zero_row], axis=0), 1, 0)[:n]
        next_wrap = pltpu.roll(jnp.concatenate([zero_row, u[0]], axis=0), n + CONV_PAD - 1, 0)[CONV_PAD:]
        prev = [prev_wrap] + u[:-1]
        nxt = u[1:] + [next_wrap]
        return [prev[r] * w_ref[0:1, :] + u[r] * w_ref[1:2, :] + nxt[r] * w_ref[2:3, :]
                for r in range(radix)]

    def cmul(ar, ai, br, bi):
        return ar * br - ai * bi, ar * bi + ai * br

    def transform(x):
        xb = [v.astype(BF16) for v in x]
        tr = [jnp.dot(cf_ref[r], xb[r], preferred_element_type=F32) for r in range(radix)]
        ti = [-jnp.dot(sf_ref[r], xb[r], preferred_element_type=F32) for r in range(radix)]
        return tr, ti

    def filter_and_invert(tr, ti, kr_ref, ki_ref):
        ar, ai = tr[0] + tr[2], ti[0] + ti[2]
        br, bi = tr[0] - tr[2], ti[0] - ti[2]
        cr, ci = tr[1] + tr[3], ti[1] + ti[3]
        dr, di = tr[1] - tr[3], ti[1] - ti[3]
        fam = [(ar + cr, ai + ci), (br + di, bi - dr), (br - di, -bi - dr), (ar - cr, ci - ai)]
        y = [cmul(fr, fi, kr_ref[f], ki_ref[f]) for f, (fr, fi) in enumerate(fam)]
        er, ei = y[0][0] + y[3][0], y[0][1] - y[3][1]
        fr, fi = y[0][0] - y[3][0], y[0][1] + y[3][1]
        gr, gi = y[1][0] + y[2][0], y[1][1] - y[2][1]
        hr, hi = y[1][0] - y[2][0], y[1][1] + y[2][1]
        p = [(er + gr, ei + gi), (fr - hi, fi + hr), (er - gr, ei - gi), (fr + hi, fi - hr)]
        out = []
        for r in range(radix):
            pr, pi_ = p[r]
            out.append(jnp.dot(ct_ref[r], pr.astype(BF16), preferred_element_type=F32)
                       - jnp.dot(st_ref[r], pi_.astype(BF16), preferred_element_type=F32))
        return out

    v = short_conv(project(wv_ref, slab_v_ref), sv_ref)
    spectrum = transform(v)
    x1 = short_conv(project(w1_ref, slab_1_ref), s1_ref)
    x2 = short_conv(project(w2_ref, slab_2_ref), s2_ref)
    y = filter_and_invert(*spectrum, kr0_ref, ki0_ref)
    z = [x1[r] * (y[r] + v[r] * bias_ref[0:1, :]) for r in range(radix)]
    y = filter_and_invert(*transform(z), kr1_ref, ki1_ref)
    for r in range(radix):
        out = x2[r] * (y[r] + z[r] * bias_ref[1:2, :])
        for j in range(n_slab):
            slab_o_ref[j, pl.ds(r, rows, stride=radix), :] = out[:, j * LANES:(j + 1) * LANES]
    o_ref[...] = jnp.concatenate([slab_o_ref[j] for j in range(n_slab)], axis=1).astype(BF16)


def _hyena_mixer(hn, w_hy, short_w, tables, kr, ki, bias, layer, tile):
    b = hn.shape[0]
    nt = HY_W // COL_TILE
    rows = SEQ // FFT_RADIX
    once = pl.Buffered(1)
    cf, sf, ct, st = tables

    def fixed(shape, index):
        return pl.BlockSpec(shape, lambda i: index, pipeline_mode=once)

    def wcol(k):
        return fixed((D_MODEL, COL_TILE), (0, k * nt + tile))

    def scol(k):
        return _layer_spec(short_w, layer, COL_TILE, k * nt + tile)

    def kfam(o):
        return fixed((FFT_RADIX, rows, COL_TILE), (0, 0, o * nt + tile))

    small = fixed((FFT_RADIX, rows, rows), (0, 0, 0))
    slab = pltpu.VMEM((COL_TILE // LANES, SEQ, LANES), F32)
    out = pl.pallas_call(
        _hyena_kernel,
        grid=(b,),
        in_specs=[pl.BlockSpec((None, SEQ, D_MODEL), lambda i: (i, 0, 0)),
                  wcol(0), wcol(1), wcol(2), scol(0), scol(1), scol(2),
                  small, small, small, small,
                  kfam(0), kfam(0), kfam(1), kfam(1),
                  _layer_spec(bias, layer, COL_TILE, tile)],
        out_specs=pl.BlockSpec((None, SEQ, COL_TILE), lambda i: (i, 0, 0)),
        out_shape=jax.ShapeDtypeStruct((b, SEQ, COL_TILE), BF16),
        scratch_shapes=[slab, slab, slab, slab],
        compiler_params=_params("arbitrary"),
        name="hyena_mixer",
    )(hn, w_hy, w_hy, w_hy, short_w, short_w, short_w, cf, sf, ct, st, kr, ki, kr, ki, bias)
    return out.reshape(b * SEQ, COL_TILE)


def _shortconv_kernel(hn_ref, wb_ref, wc_ref, wx_ref, cw_ref, o_ref):
    wb = wb_ref[...].astype(BF16)
    wc = wc_ref[...].astype(BF16)
    wx = wx_ref[...].astype(BF16)

    def project(r0):
        lo = max(r0 - CONV_HALO, 0)
        hi = min(r0 + CONV_CHUNK + CONV_HALO, SEQ)
        hn = hn_ref[lo:hi, :]
        return (r0, r0 - lo, jnp.dot(hn, wb, preferred_element_type=F32),
                jnp.dot(hn, wc, preferred_element_type=F32), jnp.dot(hn, wx, preferred_element_type=F32))

    def finish(r0, skip, bg, cg, xi):
        out = bg * _dwconv3(cg * xi, cw_ref)
        o_ref[r0:r0 + CONV_CHUNK, :] = out[skip:skip + CONV_CHUNK].astype(BF16)

    pending = None
    for r0 in range(0, SEQ, CONV_CHUNK):
        current = project(r0)
        if pending is not None:
            finish(*pending)
        pending = current
    finish(*pending)


def _w_in_cols(layer, width, first):
    return lambda k, nt: pl.BlockSpec((None, D_MODEL, width),
                                      lambda j, i: (layer, 0, first + k * nt + j))


def _shortconv_mixer(hn, w_in, layer, conv_w):
    b = hn.shape[0]
    nt = SC_W // COL_TILE
    wcol = _w_in_cols(layer, COL_TILE, (3 * HY_W + 3 * NA_W) // COL_TILE)
    return pl.pallas_call(
        _shortconv_kernel,
        grid=(nt, b),
        in_specs=[pl.BlockSpec((None, SEQ, D_MODEL), lambda j, i: (i, 0, 0)),
                  wcol(0, nt), wcol(1, nt), wcol(2, nt),
                  pl.BlockSpec((None, 3, COL_TILE), lambda j, i: (layer, 0, j))],
        out_specs=pl.BlockSpec((None, SEQ, COL_TILE), lambda j, i: (i, 0, j)),
        out_shape=jax.ShapeDtypeStruct((b, SEQ, SC_W), BF16),
        compiler_params=_params("arbitrary", "arbitrary"),
        name="shortconv_mixer",
    )(hn, w_in, w_in, w_in, conv_w)


def _na_kernel(hn_ref, wq_ref, wk_ref, wv_ref, bias_ref, o_ref, q_ref, k_ref, v_ref, s_ref):
    hn = hn_ref[...]
    q = jnp.dot(hn, wq_ref[...].astype(BF16), preferred_element_type=F32)
    q_ref[...] = (q * (NA_HEAD_DIM ** -0.5)).astype(BF16)
    k_ref[...] = jnp.dot(hn, wk_ref[...].astype(BF16), preferred_element_type=F32).astype(BF16)
    v_ref[...] = jnp.dot(hn, wv_ref[...].astype(BF16), preferred_element_type=F32).astype(BF16)
    gw = NA_GROUP * NA_HEAD_DIM
    same_head = (lax.broadcasted_iota(jnp.int32, (gw, gw), 0) // NA_HEAD_DIM
                 == lax.broadcasted_iota(jnp.int32, (gw, gw), 1) // NA_HEAD_DIM)

    def scores(r):
        w0 = jnp.clip(r - NA_WIN_ROWS // 2, 0, NA_ROWS - NA_WIN_ROWS)
        off = w0 - r + (NA_WIN_ROWS - 1)
        q0 = pl.multiple_of(r * GRID_W, GRID_W)
        k0 = pl.multiple_of(w0 * GRID_W, GRID_W)
        q_row = q_ref[pl.ds(q0, GRID_W), :]
        q_heads = jnp.where(same_head, jnp.concatenate([q_row] * NA_GROUP, axis=0), 0)
        s = lax.dot_general(q_heads, k_ref[pl.ds(k0, NA_KEYS), :], (((1,), (1,)), ((), ())),
                            preferred_element_type=F32)
        bias = jnp.concatenate(
            [jnp.concatenate([bias_ref[h, off + 2 * m] for m in range(NA_WIN_ROWS // 2)], axis=1)
             for h in range(NA_GROUP)], axis=0)
        return s + bias

    def attend(r, s):
        w0 = jnp.clip(r - NA_WIN_ROWS // 2, 0, NA_ROWS - NA_WIN_ROWS)
        q0 = pl.multiple_of(r * GRID_W, GRID_W)
        k0 = pl.multiple_of(w0 * GRID_W, GRID_W)
        p = jnp.exp(s - jnp.max(s, axis=-1, keepdims=True))
        inv = 1.0 / jnp.sum(p, axis=-1, keepdims=True)
        pv = jnp.dot(p.astype(BF16), v_ref[pl.ds(k0, NA_KEYS), :], preferred_element_type=F32)
        pv = jnp.where(same_head, pv * inv, 0.0)
        out = pv[0:GRID_W]
        for h in range(1, NA_GROUP):
            out = out + pv[h * GRID_W:(h + 1) * GRID_W]
        o_ref[pl.ds(q0, GRID_W), :] = out.astype(BF16)

    groups = NA_ROWS // NA_ROW_GROUP
    for t in range(NA_ROW_GROUP):
        s_ref[t] = scores(jnp.int32(t))

    def rows_body(i, carry):
        for t in range(NA_ROW_GROUP):
            s_next = scores(i * NA_ROW_GROUP + t)
            attend((i - 1) * NA_ROW_GROUP + t, s_ref[t])
            s_ref[t] = s_next
        return carry

    lax.fori_loop(1, groups, rows_body, 0)
    for t in range(NA_ROW_GROUP):
        attend(jnp.int32((groups - 1) * NA_ROW_GROUP + t), s_ref[t])


def _na_bias(rpb):
    c = jnp.arange(GRID_W)
    col_start = jnp.clip(c - NA_WIN_COLS // 2, 0, GRID_W - NA_WIN_COLS)
    col_mask = (c[None, :] >= col_start[:, None]) & (c[None, :] < col_start[:, None] + NA_WIN_COLS)
    dc = jnp.clip(c[None, :] - c[:, None] + NA_WIN_COLS - 1, 0, 2 * NA_WIN_COLS - 2)
    pick = (dc[None] == jnp.arange(2 * NA_WIN_COLS - 1)[:, None, None]).astype(F32)
    table = jnp.einsum("lhrd,dqc->lhrqc", rpb.astype(F32), pick, precision=lax.Precision.HIGHEST)
    table = table + jnp.where(col_mask, 0.0, -1e30)
    return jnp.concatenate([table[:, :, :-1], table[:, :, 1:]], axis=-1)


def _na_mixer(hn, w_in, layer, bias):
    b = hn.shape[0]
    gw = NA_GROUP * NA_HEAD_DIM
    ng = NA_W // gw
    wcol = _w_in_cols(layer, gw, 3 * HY_W // gw)
    return pl.pallas_call(
        _na_kernel,
        grid=(ng, b),
        in_specs=[pl.BlockSpec((None, SEQ, D_MODEL), lambda j, i: (i, 0, 0)),
                  wcol(0, ng), wcol(1, ng), wcol(2, ng),
                  pl.BlockSpec((None, NA_GROUP, 2 * NA_WIN_ROWS - 2, GRID_W, 2 * GRID_W),
                               lambda j, i: (layer, j, 0, 0, 0))],
        out_specs=pl.BlockSpec((None, SEQ, gw), lambda j, i: (i, 0, j)),
        out_shape=jax.ShapeDtypeStruct((b, SEQ, NA_W), BF16),
        scratch_shapes=[pltpu.VMEM((SEQ, gw), BF16) for _ in range(3)]
        + [pltpu.VMEM((NA_ROW_GROUP, gw, NA_KEYS), F32)],
        compiler_params=_params("arbitrary", "arbitrary"),
        name="na_mixer",
    )(hn, w_in, w_in, w_in, bias)


def _merge_kernel(hn_ref, x_ref, ya0_ref, ya1_ref, yb_ref, yc_ref, wg_ref, gb_ref, wb_ref, wo_ref,
                  g_ref, o_ref):
    def gated_sum(rows):
        hn = hn_ref[rows, :]
        ya = jnp.concatenate([ya0_ref[rows, :], ya1_ref[rows, :]], axis=1)
        merged = None
        for i, y in enumerate((ya, yb_ref[rows, :], yc_ref[rows, :])):
            pre = jnp.dot(hn, wg_ref[:, i * D_MODEL:(i + 1) * D_MODEL], preferred_element_type=F32)
            gate = jax.nn.sigmoid(pre + gb_ref[i:i + 1, :])
            term = gate * jnp.dot(y, wb_ref[i], preferred_element_type=F32)
            merged = term if merged is None else merged + term
        return merged.astype(BF16)

    chunks = [slice(r, r + ROW_TILE) for r in range(0, MERGE_ROWS, ROW_TILE)]
    merged = [gated_sum(rows) for rows in chunks]
    for rows, m in zip(chunks, merged):
        out = jnp.dot(m, wo_ref[...], preferred_element_type=F32)
        o_ref[rows, :] = x_ref[rows, :] + _rms(out, g_ref[...])


def _merge(hn2d, x2d, ya0, ya1, yb, yc, w_gate, gate_bias, w_branch, w_out, gains, layer):
    n = x2d.shape[0]
    tm = MERGE_ROWS
    once = pl.Buffered(1)
    rows = lambda w: pl.BlockSpec((tm, w), lambda i: (i, 0))
    return pl.pallas_call(
        _merge_kernel,
        grid=(n // tm,),
        in_specs=[rows(D_MODEL), rows(D_MODEL), rows(COL_TILE), rows(COL_TILE), rows(NA_W), rows(SC_W),
                  pl.BlockSpec((D_MODEL, N_BRANCH * D_MODEL), lambda i: (0, 0), pipeline_mode=once),
                  _layer_spec(gate_bias, layer),
                  pl.BlockSpec((N_BRANCH, HY_W, D_MODEL), lambda i: (0, 0, 0), pipeline_mode=once),
                  pl.BlockSpec((D_MODEL, D_MODEL), lambda i: (0, 0), pipeline_mode=once),
                  _gain_spec(6 * layer + 1)],
        out_specs=rows(D_MODEL),
        out_shape=jax.ShapeDtypeStruct((n, D_MODEL), F32),
        compiler_params=_params("arbitrary"),
        name="merge",
    )(hn2d, x2d, ya0, ya1, yb, yc, w_gate, gate_bias, w_branch, w_out, gains)


def _kv_kernel(m_ref, g_ref, w_ref, o_ref):
    mn = _rms(m_ref[...], g_ref[...]).astype(BF16)
    o_ref[...] = jnp.dot(mn, w_ref[...], preferred_element_type=F32).astype(BF16)


def _mem_kv(mem, mem_norm, layer, wkv):
    b = mem.shape[0]
    return pl.pallas_call(
        _kv_kernel,
        grid=(b,),
        in_specs=[pl.BlockSpec((None, N_MEM, D_MODEL), lambda i: (i, 0, 0)),
                  _layer_spec(mem_norm, layer),
                  pl.BlockSpec((D_MODEL, 2 * D_MODEL), lambda i: (0, 0))],
        out_specs=pl.BlockSpec((None, N_MEM, 2 * D_MODEL), lambda i: (i, 0, 0)),
        out_shape=jax.ShapeDtypeStruct((b, N_MEM, 2 * D_MODEL), BF16),
        compiler_params=_params("arbitrary"),
        name="mem_kv",
    )(mem, mem_norm, wkv)


def _xattn_kernel(x_ref, kv_ref, wq_ref, wo_ref, gq_ref, go_ref, gn_ref, o_ref, hn_ref):
    chunks = [slice(r, r + XA_CHUNK) for r in range(0, XA_ROWS, XA_CHUNK)]
    head_cols = [slice(i * XA_HEAD_DIM, (i + 1) * XA_HEAD_DIM) for i in range(XA_HEADS)]

    def query(rows):
        h = _rms(x_ref[rows, :], gq_ref[...]).astype(BF16)
        q = jnp.dot(h, wq_ref[...], preferred_element_type=F32) * (XA_HEAD_DIM ** -0.5)
        return q.astype(BF16)

    def scores(q):
        return [lax.dot_general(q[:, sl], kv_ref[:, sl], (((1,), (1,)), ((), ())),
                                preferred_element_type=F32) for sl in head_cols]

    def values(s_heads):
        heads = []
        for i, s in enumerate(s_heads):
            vm = kv_ref[:, D_MODEL + i * XA_HEAD_DIM:D_MODEL + (i + 1) * XA_HEAD_DIM]
            p = jnp.exp(s - jnp.max(s, axis=-1, keepdims=True))
            den = jnp.sum(p, axis=-1, keepdims=True)
            heads.append((jnp.dot(p.astype(BF16), vm, preferred_element_type=F32) / den).astype(BF16))
        return jnp.concatenate(heads, axis=-1)

    s_all = [scores(q) for q in [query(rows) for rows in chunks]]
    attended = [values(s) for s in s_all]
    for rows, a in zip(chunks, attended):
        o = jnp.dot(a, wo_ref[...], preferred_element_type=F32)
        xn = x_ref[rows, :] + _rms(o, go_ref[...])
        o_ref[rows, :] = xn
        hn_ref[rows, :] = _rms(xn, gn_ref[...]).astype(BF16)


def _xattn(x, kv, wq, wo, gains, layer):
    b = x.shape[0]
    tm = XA_ROWS
    once = pl.Buffered(1)
    rows = pl.BlockSpec((None, tm, D_MODEL), lambda i, j: (i, j, 0))
    wfull = pl.BlockSpec((D_MODEL, D_MODEL), lambda i, j: (0, 0), pipeline_mode=once)
    return pl.pallas_call(
        _xattn_kernel,
        grid=(b, SEQ // tm),
        in_specs=[rows, pl.BlockSpec((None, N_MEM, 2 * D_MODEL), lambda i, j: (i, 0, 0)),
                  wfull, wfull, _gain_spec(6 * layer + 2), _gain_spec(6 * layer + 3),
                  _gain_spec(6 * layer + 4)],
        out_specs=(rows, rows),
        out_shape=(jax.ShapeDtypeStruct((b, SEQ, D_MODEL), F32),
                   jax.ShapeDtypeStruct((b, SEQ, D_MODEL), BF16)),
        compiler_params=_params("arbitrary", "arbitrary"),
        name="xattn",
    )(x, kv, wq, wo, gains, gains, gains)


def _gelu_tanh(x):
    c = math.sqrt(2.0 / math.pi)
    half = 0.5 * x
    return half + half * jnp.tanh(x * (c + (c * 0.044715) * (x * x)))


def _ffn_kernel(hn_ref, wu_ref, cw_ref, wd_ref, o_ref):
    def up(r0, c0, c1):
        lo = max(r0 - CONV_HALO, 0)
        hi = min(r0 + CONV_CHUNK + CONV_HALO, SEQ)
        hn = hn_ref[lo:hi, :]
        ug = jnp.dot(hn, wu_ref[:, c0:c1], preferred_element_type=F32)
        uv = jnp.dot(hn, wu_ref[:, D_FF + c0:D_FF + c1], preferred_element_type=F32)
        return r0, r0 - lo, c0, c1, ug, uv

    def down(r0, skip, c0, c1, ug, uv):
        act = (_gelu_tanh(_dwconv3(ug, cw_ref.at[:, c0:c1]))
               * _dwconv3(uv, cw_ref.at[:, D_FF + c0:D_FF + c1]))
        act = act[skip:skip + CONV_CHUNK].astype(BF16)
        part = jnp.dot(act, wd_ref[c0:c1, :], preferred_element_type=F32)
        if c0 == 0:
            o_ref[r0:r0 + CONV_CHUNK, :] = part
        else:
            o_ref[r0:r0 + CONV_CHUNK, :] += part

    pending = None
    for r0 in range(0, SEQ, CONV_CHUNK):
        for c0, c1 in zip(FFN_CUTS[:-1], FFN_CUTS[1:]):
            current = up(r0, c0, c1)
            if pending is not None:
                down(*pending)
            pending = current
    down(*pending)


def _ffn(hn, w_up, w_conv, w_down, layer):
    b = hn.shape[0]
    once = pl.Buffered(1)
    return pl.pallas_call(
        _ffn_kernel,
        grid=(b,),
        in_specs=[pl.BlockSpec((None, SEQ, D_MODEL), lambda i: (i, 0, 0)),
                  pl.BlockSpec((D_MODEL, 2 * D_FF), lambda i: (0, 0), pipeline_mode=once),
                  _layer_spec(w_conv, layer),
                  pl.BlockSpec((D_FF, D_MODEL), lambda i: (0, 0), pipeline_mode=once)],
        out_specs=pl.BlockSpec((None, SEQ, D_MODEL), lambda i: (i, 0, 0), pipeline_mode=once),
        out_shape=jax.ShapeDtypeStruct((b, SEQ, D_MODEL), F32),
        compiler_params=_params("arbitrary"),
        name="ffn",
    )(hn, w_up, w_conv, w_down)


def _residual_kernel(x_ref, f_ref, g_ref, gn_ref, o_ref, hn_ref):
    xn = x_ref[...] + _rms(f_ref[...], g_ref[...])
    o_ref[...] = xn
    hn_ref[...] = _rms(xn, gn_ref[...]).astype(BF16)


def _residual_last_kernel(x_ref, f_ref, g_ref, o_ref):
    o_ref[...] = x_ref[...] + _rms(f_ref[...], g_ref[...])


def _residual(x2d, f2d, gains, layer, last):
    n = x2d.shape[0]
    tm = 1024
    rows = pl.BlockSpec((tm, D_MODEL), lambda i: (i, 0))
    x_shape = jax.ShapeDtypeStruct((n, D_MODEL), F32)
    if last:
        return pl.pallas_call(
            _residual_last_kernel,
            grid=(n // tm,),
            in_specs=[rows, rows, _gain_spec(6 * layer + 5)],
            out_specs=rows,
            out_shape=x_shape,
            compiler_params=_params("arbitrary"),
            name="residual_last",
        )(x2d, f2d, gains), None
    return pl.pallas_call(
        _residual_kernel,
        grid=(n // tm,),
        in_specs=[rows, rows, _gain_spec(6 * layer + 5), _gain_spec(6 * (layer + 1))],
        out_specs=(rows, rows),
        out_shape=(x_shape, jax.ShapeDtypeStruct((n, D_MODEL), BF16)),
        compiler_params=_params("arbitrary"),
        name="residual",
    )(x2d, f2d, gains, gains)


def _angle_tables(num, den):
    ang = (num % den).astype(F32) * (2.0 * math.pi / den)
    return jnp.cos(ang), jnp.sin(ang)


def _dft_tables():
    j = jnp.arange(SEQ, dtype=jnp.int32)
    rows = SEQ // FFT_RADIX
    q = rows // DFT_SPLIT
    up = jnp.arange(q, dtype=jnp.int32)
    a = jnp.concatenate([up, 2 * q + up, 2 * q - 1 - up, 4 * q - 1 - up]).reshape(2, 2 * q, 1, 1)
    b_up = jnp.arange(DFT_SPLIT, dtype=jnp.int32)
    b = jnp.stack([b_up, DFT_SPLIT - 1 - b_up]).reshape(2, 1, DFT_SPLIT, 1)
    ca, sa = _angle_tables(DFT_SPLIT * a * j, FFT_N)
    cb, sb = _angle_tables((2 * b + 1) * j, 2 * FFT_N)
    big_c = (ca * cb - sa * sb).reshape(SEQ, SEQ).astype(BF16)
    big_s = (sa * cb + ca * sb).reshape(SEQ, SEQ).astype(BF16)
    kappa = jnp.arange(rows, dtype=jnp.int32)[None, :, None]
    m = jnp.arange(rows, dtype=jnp.int32)[None, None, :]
    r = jnp.arange(FFT_RADIX, dtype=jnp.int32)[:, None, None]
    c0, s0 = _angle_tables((2 * kappa + 1) * m, 2 * FFT_SUB)
    cr, sr = _angle_tables((2 * kappa + 1) * r, 2 * FFT_N)
    cf = (c0 * cr - s0 * sr).astype(BF16)
    sf = (s0 * cr + c0 * sr).astype(BF16)
    small = (cf, sf, cf.transpose(0, 2, 1), sf.transpose(0, 2, 1))
    return big_c, big_s, small


def kernel(x, mem, norm_gains, mem_norm, w_in, gate_bias, hy_short_w, hy_w1, hy_b1, hy_w2, hy_b2,
           hy_w3, hy_freq, hy_bias, na_rpb, sc_conv_w, w_branch, w_out, xa_wq, xa_wkv, xa_wo,
           ffn_up, ffn_conv, ffn_down):
    b, l, d = x.shape
    depth = w_in.shape[0]
    assert (l, d) == (SEQ, D_MODEL) and mem.shape[1:] == (N_MEM, D_MODEL)
    n = b * l
    dft_c, dft_s, conv_tables = _dft_tables()
    gains = norm_gains.astype(F32).reshape(depth * 6, 1, d)
    mem_gain = mem_norm.astype(F32).reshape(depth, 1, d)
    w1p = jnp.pad(hy_w1.astype(F32), ((0, 0), (0, HY_HIDDEN - HY_EMB), (0, 0)))
    b1 = hy_b1.astype(F32).reshape(depth, 1, HY_HIDDEN)
    b2 = hy_b2.astype(F32).reshape(depth, 1, HY_HIDDEN)
    na_bias = _na_bias(na_rpb)
    w_branch2d = w_branch.reshape(depth, N_BRANCH * HY_W, d)
    x2d = x.reshape(n, d)
    hn = _prenorm(x2d, gains, 0)
    for i in range(depth):
        w_hyena, w_gate, w_br, w_o, wq, wkv, wo, w_up, w_down = _cast_layer(
            i, (w_in, 0, 3 * HY_W), (w_in, 3 * HY_W + 3 * NA_W + 3 * SC_W, N_BRANCH * D_MODEL),
            (w_branch2d, 0, d), (w_out, 0, d), (xa_wq, 0, d), (xa_wkv, 0, 2 * d), (xa_wo, 0, d),
            (ffn_up, 0, 2 * D_FF), (ffn_down, 0, d))
        kr, ki = _hyena_filters(dft_c, dft_s, w1p, b1, hy_w2, b2, hy_w3, hy_freq, i)
        hn3 = hn.reshape(b, l, d)
        ya = [_hyena_mixer(hn3, w_hyena, hy_short_w, conv_tables, kr, ki, hy_bias, i, tile)
              for tile in range(HY_W // COL_TILE)]
        yb = _na_mixer(hn3, w_in, i, na_bias)
        yc = _shortconv_mixer(hn3, w_in, i, sc_conv_w)
        x2d = _merge(hn, x2d, ya[0], ya[1], yb.reshape(n, NA_W), yc.reshape(n, SC_W), w_gate, gate_bias,
                     w_br.reshape(N_BRANCH, HY_W, d), w_o, gains, i)
        kv = _mem_kv(mem, mem_gain, i, wkv)
        x3, hn2 = _xattn(x2d.reshape(b, l, d), kv, wq, wo, gains, i)
        f = _ffn(hn2, w_up, ffn_conv, w_down, i)
        x2d, hn = _residual(x3.reshape(n, d), f.reshape(n, d), gains, i, i + 1 == depth)
    return x2d.reshape(b, l, d)
```

```python
import functools
import math

import jax
import jax.numpy as jnp
from jax import lax
from jax.experimental import pallas as pl
from jax.experimental.pallas import tpu as pltpu

D_MODEL = 1024
SEQ = 2048
N_MEM = 256
GRID_W = 64
HY_W = 512
NA_HEADS = 8
NA_HEAD_DIM = 64
NA_W = NA_HEADS * NA_HEAD_DIM
NA_WIN_ROWS = 8
NA_WIN_COLS = 16
SC_W = 512
XA_HEADS = 4
XA_HEAD_DIM = D_MODEL // XA_HEADS
D_FF = 2816
HY_ORDER = 2
HY_EMB = 33
HY_HIDDEN = 64
HY_FAST_DECAY = 0.3
HY_SLOW_DECAY = 1.5
HY_TARGET = 1e-2
N_BRANCH = 3
EPS = 1e-6

FFT_N = 2 * SEQ
FFT_RADIX = 4
FFT_SUB = FFT_N // FFT_RADIX
NA_ROWS = SEQ // GRID_W
NA_KEYS = NA_WIN_ROWS * GRID_W
NA_GROUP = 4
NA_ROW_GROUP = 8
LANES = 128
COL_TILE = 256
ROW_TILE = 512
MERGE_ROWS = 2 * ROW_TILE
XA_ROWS = 2 * ROW_TILE
XA_CHUNK = 256
CAST_STEPS = 8
CONV_PAD = 8
DFT_SPLIT = 64
CONV_CHUNK = 512
FFN_CUTS = (0, 6 * COL_TILE, D_FF)
CONV_HALO = 16
VMEM_LIMIT = 60 * 1024 * 1024

BF16 = jnp.bfloat16
F32 = jnp.float32


def _params(*sem):
    return pltpu.CompilerParams(dimension_semantics=sem, vmem_limit_bytes=VMEM_LIMIT)


def _gain_spec(index):
    return pl.BlockSpec((None, 1, D_MODEL), lambda *_: (index, 0, 0))


def _layer_spec(arr, layer, width=None, col=0):
    _, r, c = arr.shape
    return pl.BlockSpec((None, r, c if width is None else width), lambda *_: (layer, 0, col))


def _rms(xf, g):
    ms = jnp.mean(xf * xf, axis=-1, keepdims=True)
    return xf * lax.rsqrt(ms + EPS) * g


def _dwconv3(u, w_ref):
    n = u.shape[0]
    zeros = jnp.zeros((CONV_PAD, u.shape[1]), F32)
    padded = jnp.concatenate([zeros, u, zeros], axis=0)
    m = n + 2 * CONV_PAD
    prev = pltpu.roll(padded, 1, 0)[CONV_PAD:CONV_PAD + n]
    nxt = pltpu.roll(padded, m - 1, 0)[CONV_PAD:CONV_PAD + n]
    return prev * w_ref[0:1, :] + u * w_ref[1:2, :] + nxt * w_ref[2:3, :]


def _prenorm_kernel(x_ref, g_ref, o_ref):
    o_ref[...] = _rms(x_ref[...], g_ref[...]).astype(BF16)


def _prenorm(x2d, gains, gi):
    n = x2d.shape[0]
    tm = 1024
    return pl.pallas_call(
        _prenorm_kernel,
        grid=(n // tm,),
        in_specs=[pl.BlockSpec((tm, D_MODEL), lambda i: (i, 0)),
                  _gain_spec(gi)],
        out_specs=pl.BlockSpec((tm, D_MODEL), lambda i: (i, 0)),
        out_shape=jax.ShapeDtypeStruct((n, D_MODEL), BF16),
        compiler_params=_params("arbitrary"),
        name="prenorm",
    )(x2d, gains)


def _cast_kernel(parts, *refs):
    n_in = sum(parts)
    w_refs, o_refs = refs[:n_in], refs[n_in:]
    k = 0
    for o_ref, n in zip(o_refs, parts):
        width = o_ref.shape[1] // n
        for p in range(n):
            o_ref[:, p * width:(p + 1) * width] = w_refs[k][...].astype(BF16)
            k += 1


def _cast_layer(layer, *weights):
    in_specs, out_specs, out_shapes, operands, parts = [], [], [], [], []
    for w, col0, ncols in weights:
        rb = w.shape[1] // CAST_STEPS
        cb = math.gcd(col0, ncols) if col0 else ncols
        assert w.shape[1] % CAST_STEPS == 0 and rb % 16 == 0 and cb % LANES == 0
        parts.append(ncols // cb)
        for p in range(ncols // cb):
            in_specs.append(pl.BlockSpec((None, rb, cb),
                                         lambda i, cblk=col0 // cb + p: (layer, i, cblk)))
            operands.append(w)
        out_specs.append(pl.BlockSpec((rb, ncols), lambda i: (i, 0)))
        out_shapes.append(jax.ShapeDtypeStruct((w.shape[1], ncols), BF16))
    return pl.pallas_call(
        functools.partial(_cast_kernel, tuple(parts)),
        grid=(CAST_STEPS,),
        in_specs=in_specs,
        out_specs=tuple(out_specs),
        out_shape=tuple(out_shapes),
        compiler_params=_params("arbitrary"),
        name="cast_bf16",
    )(*operands)


def _filter_mlp_kernel(z_ref, w1_ref, b1_ref, w2_ref, b2_ref, w3_ref, f_ref, t_ref, dl_ref,
                       hs_ref, hd_ref):
    hp = lax.Precision.HIGHEST
    h = jnp.sin(f_ref[0:1, :] * (jnp.dot(z_ref[...], w1_ref[...], precision=hp) + b1_ref[...]))
    h = jnp.sin(f_ref[1:2, :] * (jnp.dot(h, w2_ref[...], precision=hp) + b2_ref[...]))
    decay = jnp.exp(-t_ref[...] * dl_ref[...])
    row = lax.broadcasted_iota(jnp.int32, (SEQ, HY_W), 0)
    h_hi = h.astype(BF16)
    h_lo = (h - h_hi.astype(F32)).astype(BF16)

    def out_layer(cols):
        w = w3_ref[:, cols]
        w_hi = w.astype(BF16)
        w_lo = (w - w_hi.astype(F32)).astype(BF16)
        return (jnp.dot(h_hi, w_hi, preferred_element_type=F32)
                + (jnp.dot(h_hi, w_lo, preferred_element_type=F32)
                   + jnp.dot(h_lo, w_hi, preferred_element_type=F32)))

    for o in range(HY_ORDER):
        c_f = o * HY_W
        c_b = HY_ORDER * HY_W + o * HY_W
        hf = out_layer(slice(c_f, c_f + HY_W)) * decay
        hb = out_layer(slice(c_b, c_b + HY_W)) * decay
        hb = jnp.where(row == 0, 0.0, hb)
        hs_ref[:, c_f:c_f + HY_W] = (hf + hb).astype(BF16)
        hd_ref[:, c_f:c_f + HY_W] = (hb - hf).astype(BF16)


def _filter_dft_kernel(c_ref, s_ref, hs_ref, hd_ref, kr_ref, ki_ref):
    kr_ref[...] = jnp.dot(c_ref[...], hs_ref[...], preferred_element_type=F32) * (2.0 / FFT_N)
    ki_ref[...] = jnp.dot(s_ref[...], hd_ref[...], preferred_element_type=F32) * (2.0 / FFT_N)


def _hyena_filters(dft_c, dft_s, w1p, b1, w2, b2, w3, freq, layer):
    t = jnp.linspace(0.0, 1.0, SEQ, dtype=F32)[:, None]
    bands = (HY_EMB - 1) // 2
    w = 2.0 * math.pi * jnp.arange(SEQ, dtype=F32)[:, None] / SEQ
    f = jnp.linspace(1e-4, bands - 1, bands, dtype=F32)[None, :]
    z = jnp.concatenate([t, jnp.cos(f * w), -jnp.sin(f * w)], axis=-1)
    z = jnp.pad(z, ((0, 0), (0, HY_HIDDEN - HY_EMB)))
    deltas = jnp.abs(jnp.linspace(math.log(HY_TARGET) / HY_SLOW_DECAY,
                                  math.log(HY_TARGET) / HY_FAST_DECAY, HY_W, dtype=F32))[None, :]
    width = HY_ORDER * HY_W
    whole = lambda a: pl.BlockSpec(a.shape, lambda i: (0,) * a.ndim)
    taps = pl.BlockSpec((SEQ, width), lambda i: (0, 0))
    hs, hd = pl.pallas_call(
        _filter_mlp_kernel,
        grid=(1,),
        in_specs=[whole(z)] + [_layer_spec(a, layer) for a in (w1p, b1, w2, b2, w3, freq)]
        + [whole(t), whole(deltas)],
        out_specs=(taps, taps),
        out_shape=(jax.ShapeDtypeStruct((SEQ, width), BF16),
                   jax.ShapeDtypeStruct((SEQ, width), BF16)),
        compiler_params=_params("arbitrary"),
        name="hyena_filter_mlp",
    )(z, w1p, b1, w2, b2, w3, freq, t, deltas)
    nt = width // COL_TILE
    full = pl.BlockSpec((SEQ, SEQ), lambda j: (0, 0))
    col = pl.BlockSpec((SEQ, COL_TILE), lambda j: (0, j))
    kr, ki = pl.pallas_call(
        _filter_dft_kernel,
        grid=(nt,),
        in_specs=[full, full, col, col],
        out_specs=(col, col),
        out_shape=(jax.ShapeDtypeStruct((SEQ, width), F32),
                   jax.ShapeDtypeStruct((SEQ, width), F32)),
        compiler_params=_params("arbitrary"),
        name="hyena_filter_dft",
    )(dft_c, dft_s, hs, hd)

    fam_shape = (FFT_RADIX, SEQ // FFT_RADIX, width)
    return kr.reshape(fam_shape), ki.reshape(fam_shape)


def _hyena_kernel(hn_ref, wv_ref, w1_ref, w2_ref, sv_ref, s1_ref, s2_ref, cf_ref, sf_ref, ct_ref,
                  st_ref, kr0_ref, ki0_ref, kr1_ref, ki1_ref, bias_ref, o_ref,
                  slab_v_ref, slab_1_ref, slab_2_ref, slab_o_ref):
    radix = FFT_RADIX
    rows = SEQ // radix
    n_slab = COL_TILE // LANES
    zero_row = jnp.zeros((CONV_PAD, COL_TILE), F32)

    def project(w_ref, slab_ref):
        half = SEQ // 2
        for top in (0, half):
            u = jnp.dot(hn_ref[top:top + half, :], w_ref[...], preferred_element_type=F32)
            for j in range(n_slab):
                slab_ref[j, top:top + half, :] = u[:, j * LANES:(j + 1) * LANES]
        return [jnp.concatenate([slab_ref[j, pl.ds(r, rows, stride=radix), :] for j in range(n_slab)],
                                axis=1) for r in range(radix)]

    def short_conv(u, w_ref):
        n = u[0].shape[0]
        prev_wrap = pltpu.roll(jnp.concatenate([u[-1], zero_row], axis=0), 1, 0)[:n]
        next_wrap = pltpu.roll(jnp.concatenate([zero_row, u[0]], axis=0), n + CONV_PAD - 1, 0)[CONV_PAD:]
        prev = [prev_wrap] + u[:-1]
        nxt = u[1:] + [next_wrap]
        return [prev[r] * w_ref[0:1, :] + u[r] * w_ref[1:2, :] + nxt[r] * w_ref[2:3, :]
                for r in range(radix)]

    def cmul(ar, ai, br, bi):
        return ar * br - ai * bi, ar * bi + ai * br

    def transform(x):
        xb = [v.astype(BF16) for v in x]
        tr = [jnp.dot(cf_ref[r], xb[r], preferred_element_type=F32) for r in range(radix)]
        ti = [-jnp.dot(sf_ref[r], xb[r], preferred_element_type=F32) for r in range(radix)]
        return tr, ti

    def filter_and_invert(tr, ti, kr_ref, ki_ref):
        ar, ai = tr[0] + tr[2], ti[0] + ti[2]
        br, bi = tr[0] - tr[2], ti[0] - ti[2]
        cr, ci = tr[1] + tr[3], ti[1] + ti[3]
        dr, di = tr[1] - tr[3], ti[1] - ti[3]
        fam = [(ar + cr, ai + ci), (br + di, bi - dr), (br - di, -bi - dr), (ar - cr, ci - ai)]
        y = [cmul(fr, fi, kr_ref[f], ki_ref[f]) for f, (fr, fi) in enumerate(fam)]
        er, ei = y[0][0] + y[3][0], y[0][1] - y[3][1]
        fr, fi = y[0][0] - y[3][0], y[0][1] + y[3][1]
        gr, gi = y[1][0] + y[2][0], y[1][1] - y[2][1]
        hr, hi = y[1][0] - y[2][0], y[1][1] + y[2][1]
        p = [(er + gr, ei + gi), (fr - hi, fi + hr), (er - gr, ei - gi), (fr + hi, fi - hr)]
        out = []
        for r in range(radix):
            pr, pi_ = p[r]
            out.append(jnp.dot(ct_ref[r], pr.astype(BF16), preferred_element_type=F32)
                       - jnp.dot(st_ref[r], pi_.astype(BF16), preferred_element_type=F32))
        return out

    v = short_conv(project(wv_ref, slab_v_ref), sv_ref)
    spectrum = transform(v)
    x1 = short_conv(project(w1_ref, slab_1_ref), s1_ref)
    x2 = short_conv(project(w2_ref, slab_2_ref), s2_ref)
    y = filter_and_invert(*spectrum, kr0_ref, ki0_ref)
    z = [x1[r] * (y[r] + v[r] * bias_ref[0:1, :]) for r in range(radix)]
    y = filter_and_invert(*transform(z), kr1_ref, ki1_ref)
    for r in range(radix):
        out = x2[r] * (y[r] + z[r] * bias_ref[1:2, :])
        for j in range(n_slab):
            slab_o_ref[j, pl.ds(r, rows, stride=radix), :] = out[:, j * LANES:(j + 1) * LANES]
    o_ref[...] = jnp.concatenate([slab_o_ref[j] for j in range(n_slab)], axis=1).astype(BF16)


def _hyena_mixer(hn, w_hy, short_w, tables, kr, ki, bias, layer, tile):
    b = hn.shape[0]
    nt = HY_W // COL_TILE
    rows = SEQ // FFT_RADIX
    once = pl.Buffered(1)
    cf, sf, ct, st = tables

    def fixed(shape, index):
        return pl.BlockSpec(shape, lambda i: index, pipeline_mode=once)

    def wcol(k):
        return fixed((D_MODEL, COL_TILE), (0, k * nt + tile))

    def scol(k):
        return _layer_spec(short_w, layer, COL_TILE, k * nt + tile)

    def kfam(o):
        return fixed((FFT_RADIX, rows, COL_TILE), (0, 0, o * nt + tile))

    small = fixed((FFT_RADIX, rows, rows), (0, 0, 0))
    slab = pltpu.VMEM((COL_TILE // LANES, SEQ, LANES), F32)
    out = pl.pallas_call(
        _hyena_kernel,
        grid=(b,),
        in_specs=[pl.BlockSpec((None, SEQ, D_MODEL), lambda i: (i, 0, 0)),
                  wcol(0), wcol(1), wcol(2), scol(0), scol(1), scol(2),
                  small, small, small, small,
                  kfam(0), kfam(0), kfam(1), kfam(1),
                  _layer_spec(bias, layer, COL_TILE, tile)],
        out_specs=pl.BlockSpec((None, SEQ, COL_TILE), lambda i: (i, 0, 0)),
        out_shape=jax.ShapeDtypeStruct((b, SEQ, COL_TILE), BF16),
        scratch_shapes=[slab, slab, slab, slab],
        compiler_params=_params("arbitrary"),
        name="hyena_mixer",
    )(hn, w_hy, w_hy, w_hy, short_w, short_w, short_w, cf, sf, ct, st, kr, ki, kr, ki, bias)
    return out.reshape(b * SEQ, COL_TILE)


def _shortconv_kernel(hn_ref, wb_ref, wc_ref, wx_ref, cw_ref, o_ref):
    wb = wb_ref[...].astype(BF16)
    wc = wc_ref[...].astype(BF16)
    wx = wx_ref[...].astype(BF16)

    def project(r0):
        lo = max(r0 - CONV_HALO, 0)
        hi = min(r0 + CONV_CHUNK + CONV_HALO, SEQ)
        hn = hn_ref[lo:hi, :]
        return (r0, r0 - lo, jnp.dot(hn, wb, preferred_element_type=F32),
                jnp.dot(hn, wc, preferred_element_type=F32), jnp.dot(hn, wx, preferred_element_type=F32))

    def finish(r0, skip, bg, cg, xi):
        out = bg * _dwconv3(cg * xi, cw_ref)
        o_ref[r0:r0 + CONV_CHUNK, :] = out[skip:skip + CONV_CHUNK].astype(BF16)

    pending = None
    for r0 in range(0, SEQ, CONV_CHUNK):
        current = project(r0)
        if pending is not None:
            finish(*pending)
        pending = current
    finish(*pending)


def _w_in_cols(layer, width, first):
    return lambda k, nt: pl.BlockSpec((None, D_MODEL, width),
                                      lambda j, i: (layer, 0, first + k * nt + j))


def _shortconv_mixer(hn, w_in, layer, conv_w):
    b = hn.shape[0]
    nt = SC_W // COL_TILE
    wcol = _w_in_cols(layer, COL_TILE, (3 * HY_W + 3 * NA_W) // COL_TILE)
    return pl.pallas_call(
        _shortconv_kernel,
        grid=(nt, b),
        in_specs=[pl.BlockSpec((None, SEQ, D_MODEL), lambda j, i: (i, 0, 0)),
                  wcol(0, nt), wcol(1, nt), wcol(2, nt),
                  pl.BlockSpec((None, 3, COL_TILE), lambda j, i: (layer, 0, j))],
        out_specs=pl.BlockSpec((None, SEQ, COL_TILE), lambda j, i: (i, 0, j)),
        out_shape=jax.ShapeDtypeStruct((b, SEQ, SC_W), BF16),
        compiler_params=_params("arbitrary", "arbitrary"),
        name="shortconv_mixer",
    )(hn, w_in, w_in, w_in, conv_w)


def _na_kernel(hn_ref, wq_ref, wk_ref, wv_ref, bias_ref, o_ref, q_ref, k_ref, v_ref, s_ref):
    hn = hn_ref[...]
    q = jnp.dot(hn, wq_ref[...].astype(BF16), preferred_element_type=F32)
    q_ref[...] = (q * (NA_HEAD_DIM ** -0.5)).astype(BF16)
    k_ref[...] = jnp.dot(hn, wk_ref[...].astype(BF16), preferred_element_type=F32).astype(BF16)
    v_ref[...] = jnp.dot(hn, wv_ref[...].astype(BF16), preferred_element_type=F32).astype(BF16)
    gw = NA_GROUP * NA_HEAD_DIM
    same_head = (lax.broadcasted_iota(jnp.int32, (gw, gw), 0) // NA_HEAD_DIM
                 == lax.broadcasted_iota(jnp.int32, (gw, gw), 1) // NA_HEAD_DIM)

    def scores(r):
        w0 = jnp.clip(r - NA_WIN_ROWS // 2, 0, NA_ROWS - NA_WIN_ROWS)
        off = w0 - r + (NA_WIN_ROWS - 1)
        q0 = pl.multiple_of(r * GRID_W, GRID_W)
        k0 = pl.multiple_of(w0 * GRID_W, GRID_W)
        q_row = q_ref[pl.ds(q0, GRID_W), :]
        q_heads = jnp.where(same_head, jnp.concatenate([q_row] * NA_GROUP, axis=0), 0)
        s = lax.dot_general(q_heads, k_ref[pl.ds(k0, NA_KEYS), :], (((1,), (1,)), ((), ())),
                            preferred_element_type=F32)
        bias = jnp.concatenate(
            [jnp.concatenate([bias_ref[h, off + 2 * m] for m in range(NA_WIN_ROWS // 2)], axis=1)
             for h in range(NA_GROUP)], axis=0)
        return s + bias

    def attend(r, s):
        w0 = jnp.clip(r - NA_WIN_ROWS // 2, 0, NA_ROWS - NA_WIN_ROWS)
        q0 = pl.multiple_of(r * GRID_W, GRID_W)
        k0 = pl.multiple_of(w0 * GRID_W, GRID_W)
        p = jnp.exp(s - jnp.max(s, axis=-1, keepdims=True))
        inv = 1.0 / jnp.sum(p, axis=-1, keepdims=True)
        pv = jnp.dot(p.astype(BF16), v_ref[pl.ds(k0, NA_KEYS), :], preferred_element_type=F32)
        pv = jnp.where(same_head, pv * inv, 0.0)
        out = pv[0:GRID_W]
        for h in range(1, NA_GROUP):
            out = out + pv[h * GRID_W:(h + 1) * GRID_W]
        o_ref[pl.ds(q0, GRID_W), :] = out.astype(BF16)

    groups = NA_ROWS // NA_ROW_GROUP
    for t in range(NA_ROW_GROUP):
        s_ref[t] = scores(jnp.int32(t))

    def rows_body(i, carry):
        for t in range(NA_ROW_GROUP):
            s_next = scores(i * NA_ROW_GROUP + t)
            attend((i - 1) * NA_ROW_GROUP + t, s_ref[t])
            s_ref[t] = s_next
        return carry

    lax.fori_loop(1, groups, rows_body, 0)
    for t in range(NA_ROW_GROUP):
        attend(jnp.int32((groups - 1) * NA_ROW_GROUP + t), s_ref[t])


def _na_bias(rpb):
    c = jnp.arange(GRID_W)
    col_start = jnp.clip(c - NA_WIN_COLS // 2, 0, GRID_W - NA_WIN_COLS)
    col_mask = (c[None, :] >= col_start[:, None]) & (c[None, :] < col_start[:, None] + NA_WIN_COLS)
    dc = jnp.clip(c[None, :] - c[:, None] + NA_WIN_COLS - 1, 0, 2 * NA_WIN_COLS - 2)
    pick = (dc[None] == jnp.arange(2 * NA_WIN_COLS - 1)[:, None, None]).astype(F32)
    table = jnp.einsum("lhrd,dqc->lhrqc", rpb.astype(F32), pick, precision=lax.Precision.HIGHEST)
    table = table + jnp.where(col_mask, 0.0, -1e30)
    return jnp.concatenate([table[:, :, :-1], table[:, :, 1:]], axis=-1)


def _na_mixer(hn, w_in, layer, bias):
    b = hn.shape[0]
    gw = NA_GROUP * NA_HEAD_DIM
    ng = NA_W // gw
    wcol = _w_in_cols(layer, gw, 3 * HY_W // gw)
    return pl.pallas_call(
        _na_kernel,
        grid=(ng, b),
        in_specs=[pl.BlockSpec((None, SEQ, D_MODEL), lambda j, i: (i, 0, 0)),
                  wcol(0, ng), wcol(1, ng), wcol(2, ng),
                  pl.BlockSpec((None, NA_GROUP, 2 * NA_WIN_ROWS - 2, GRID_W, 2 * GRID_W),
                               lambda j, i: (layer, j, 0, 0, 0))],
        out_specs=pl.BlockSpec((None, SEQ, gw), lambda j, i: (i, 0, j)),
        out_shape=jax.ShapeDtypeStruct((b, SEQ, NA_W), BF16),
        scratch_shapes=[pltpu.VMEM((SEQ, gw), BF16) for _ in range(3)]
        + [pltpu.VMEM((NA_ROW_GROUP, gw, NA_KEYS), F32)],
        compiler_params=_params("arbitrary", "arbitrary"),
        name="na_mixer",
    )(hn, w_in, w_in, w_in, bias)


def _merge_kernel(hn_ref, x_ref, ya0_ref, ya1_ref, yb_ref, yc_ref, wg_ref, gb_ref, wb_ref, wo_ref,
                  g_ref, o_ref):
    def gated_sum(rows):
        hn = hn_ref[rows, :]
        ya = jnp.concatenate([ya0_ref[rows, :], ya1_ref[rows, :]], axis=1)
        merged = None
        for i, y in enumerate((ya, yb_ref[rows, :], yc_ref[rows, :])):
            pre = jnp.dot(hn, wg_ref[:, i * D_MODEL:(i + 1) * D_MODEL], preferred_element_type=F32)
            gate = jax.nn.sigmoid(pre + gb_ref[i:i + 1, :])
            term = gate * jnp.dot(y, wb_ref[i], preferred_element_type=F32)
            merged = term if merged is None else merged + term
        return merged.astype(BF16)

    chunks = [slice(r, r + ROW_TILE) for r in range(0, MERGE_ROWS, ROW_TILE)]
    merged = [gated_sum(rows) for rows in chunks]
    for rows, m in zip(chunks, merged):
        out = jnp.dot(m, wo_ref[...], preferred_element_type=F32)
        o_ref[rows, :] = x_ref[rows, :] + _rms(out, g_ref[...])


def _merge(hn2d, x2d, ya0, ya1, yb, yc, w_gate, gate_bias, w_branch, w_out, gains, layer):
    n = x2d.shape[0]
    tm = MERGE_ROWS
    once = pl.Buffered(1)
    rows = lambda w: pl.BlockSpec((tm, w), lambda i: (i, 0))
    return pl.pallas_call(
        _merge_kernel,
        grid=(n // tm,),
        in_specs=[rows(D_MODEL), rows(D_MODEL), rows(COL_TILE), rows(COL_TILE), rows(NA_W), rows(SC_W),
                  pl.BlockSpec((D_MODEL, N_BRANCH * D_MODEL), lambda i: (0, 0), pipeline_mode=once),
                  _layer_spec(gate_bias, layer),
                  pl.BlockSpec((N_BRANCH, HY_W, D_MODEL), lambda i: (0, 0, 0), pipeline_mode=once),
                  pl.BlockSpec((D_MODEL, D_MODEL), lambda i: (0, 0), pipeline_mode=once),
                  _gain_spec(6 * layer + 1)],
        out_specs=rows(D_MODEL),
        out_shape=jax.ShapeDtypeStruct((n, D_MODEL), F32),
        compiler_params=_params("arbitrary"),
        name="merge",
    )(hn2d, x2d, ya0, ya1, yb, yc, w_gate, gate_bias, w_branch, w_out, gains)


def _kv_kernel(m_ref, g_ref, w_ref, o_ref):
    mn = _rms(m_ref[...], g_ref[...]).astype(BF16)
    o_ref[...] = jnp.dot(mn, w_ref[...], preferred_element_type=F32).astype(BF16)


def _mem_kv(mem, mem_norm, layer, wkv):
    b = mem.shape[0]
    return pl.pallas_call(
        _kv_kernel,
        grid=(b,),
        in_specs=[pl.BlockSpec((None, N_MEM, D_MODEL), lambda i: (i, 0, 0)),
                  _layer_spec(mem_norm, layer),
                  pl.BlockSpec((D_MODEL, 2 * D_MODEL), lambda i: (0, 0))],
        out_specs=pl.BlockSpec((None, N_MEM, 2 * D_MODEL), lambda i: (i, 0, 0)),
        out_shape=jax.ShapeDtypeStruct((b, N_MEM, 2 * D_MODEL), BF16),
        compiler_params=_params("arbitrary"),
        name="mem_kv",
    )(mem, mem_norm, wkv)


def _xattn_kernel(x_ref, kv_ref, wq_ref, wo_ref, gq_ref, go_ref, gn_ref, o_ref, hn_ref):
    chunks = [slice(r, r + XA_CHUNK) for r in range(0, XA_ROWS, XA_CHUNK)]
    head_cols = [slice(i * XA_HEAD_DIM, (i + 1) * XA_HEAD_DIM) for i in range(XA_HEADS)]

    def query(rows):
        h = _rms(x_ref[rows, :], gq_ref[...]).astype(BF16)
        q = jnp.dot(h, wq_ref[...], preferred_element_type=F32) * (XA_HEAD_DIM ** -0.5)
        return q.astype(BF16)

    def scores(q):
        return [lax.dot_general(q[:, sl], kv_ref[:, sl], (((1,), (1,)), ((), ())),
                                preferred_element_type=F32) for sl in head_cols]

    def values(s_heads):
        heads = []
        for i, s in enumerate(s_heads):
            vm = kv_ref[:, D_MODEL + i * XA_HEAD_DIM:D_MODEL + (i + 1) * XA_HEAD_DIM]
            p = jnp.exp(s - jnp.max(s, axis=-1, keepdims=True))
            den = jnp.sum(p, axis=-1, keepdims=True)
            heads.append((jnp.dot(p.astype(BF16), vm, preferred_element_type=F32) / den).astype(BF16))
        return jnp.concatenate(heads, axis=-1)

    s_all = [scores(q) for q in [query(rows) for rows in chunks]]
    attended = [values(s) for s in s_all]
    for rows, a in zip(chunks, attended):
        o = jnp.dot(a, wo_ref[...], preferred_element_type=F32)
        xn = x_ref[rows, :] + _rms(o, go_ref[...])
        o_ref[rows, :] = xn
        hn_ref[rows, :] = _rms(xn, gn_ref[...]).astype(BF16)


def _xattn(x, kv, wq, wo, gains, layer):
    b = x.shape[0]
    tm = XA_ROWS
    once = pl.Buffered(1)
    rows = pl.BlockSpec((None, tm, D_MODEL), lambda i, j: (i, j, 0))
    wfull = pl.BlockSpec((D_MODEL, D_MODEL), lambda i, j: (0, 0), pipeline_mode=once)
    return pl.pallas_call(
        _xattn_kernel,
        grid=(b, SEQ // tm),
        in_specs=[rows, pl.BlockSpec((None, N_MEM, 2 * D_MODEL), lambda i, j: (i, 0, 0)),
                  wfull, wfull, _gain_spec(6 * layer + 2), _gain_spec(6 * layer + 3),
                  _gain_spec(6 * layer + 4)],
        out_specs=(rows, rows),
        out_shape=(jax.ShapeDtypeStruct((b, SEQ, D_MODEL), F32),
                   jax.ShapeDtypeStruct((b, SEQ, D_MODEL), BF16)),
        compiler_params=_params("arbitrary", "arbitrary"),
        name="xattn",
    )(x, kv, wq, wo, gains, gains, gains)


def _gelu_tanh(x):
    c = math.sqrt(2.0 / math.pi)
    half = 0.5 * x
    return half + half * jnp.tanh(x * (c + (c * 0.044715) * (x * x)))


def _ffn_kernel(hn_ref, wu_ref, cw_ref, wd_ref, o_ref):
    def up(r0, c0, c1):
        lo = max(r0 - CONV_HALO, 0)
        hi = min(r0 + CONV_CHUNK + CONV_HALO, SEQ)
        hn = hn_ref[lo:hi, :]
        ug = jnp.dot(hn, wu_ref[:, c0:c1], preferred_element_type=F32)
        uv = jnp.dot(hn, wu_ref[:, D_FF + c0:D_FF + c1], preferred_element_type=F32)
        return r0, r0 - lo, c0, c1, ug, uv

    def down(acc, r0, skip, c0, c1, ug, uv):
        act = (_gelu_tanh(_dwconv3(ug, cw_ref.at[:, c0:c1]))
               * _dwconv3(uv, cw_ref.at[:, D_FF + c0:D_FF + c1]))
        act = act[skip:skip + CONV_CHUNK].astype(BF16)
        part = jnp.dot(act, wd_ref[c0:c1, :], preferred_element_type=F32)
        acc = part if c0 == 0 else acc + part
        if c1 == D_FF:
            o_ref[r0:r0 + CONV_CHUNK, :] = acc.astype(o_ref.dtype)
        return acc

    pending, acc = None, None
    for r0 in range(0, SEQ, CONV_CHUNK):
        for c0, c1 in zip(FFN_CUTS[:-1], FFN_CUTS[1:]):
            current = up(r0, c0, c1)
            if pending is not None:
                acc = down(acc, *pending)
            pending = current
    down(acc, *pending)


def _ffn(hn, w_up, w_conv, w_down, layer):
    b = hn.shape[0]
    once = pl.Buffered(1)
    return pl.pallas_call(
        _ffn_kernel,
        grid=(b,),
        in_specs=[pl.BlockSpec((None, SEQ, D_MODEL), lambda i: (i, 0, 0)),
                  pl.BlockSpec((D_MODEL, 2 * D_FF), lambda i: (0, 0), pipeline_mode=once),
                  _layer_spec(w_conv, layer),
                  pl.BlockSpec((D_FF, D_MODEL), lambda i: (0, 0), pipeline_mode=once)],
        out_specs=pl.BlockSpec((None, SEQ, D_MODEL), lambda i: (i, 0, 0)),
        out_shape=jax.ShapeDtypeStruct((b, SEQ, D_MODEL), BF16),
        compiler_params=_params("arbitrary"),
        name="ffn",
    )(hn, w_up, w_conv, w_down)


def _residual_kernel(x_ref, f_ref, g_ref, gn_ref, o_ref, hn_ref):
    xn = x_ref[...] + _rms(f_ref[...].astype(F32), g_ref[...])
    o_ref[...] = xn
    hn_ref[...] = _rms(xn, gn_ref[...]).astype(BF16)


def _residual_last_kernel(x_ref, f_ref, g_ref, o_ref):
    o_ref[...] = x_ref[...] + _rms(f_ref[...].astype(F32), g_ref[...])


def _residual(x2d, f2d, gains, layer, last):
    n = x2d.shape[0]
    tm = 1024
    rows = pl.BlockSpec((tm, D_MODEL), lambda i: (i, 0))
    x_shape = jax.ShapeDtypeStruct((n, D_MODEL), F32)
    if last:
        return pl.pallas_call(
            _residual_last_kernel,
            grid=(n // tm,),
            in_specs=[rows, rows, _gain_spec(6 * layer + 5)],
            out_specs=rows,
            out_shape=x_shape,
            compiler_params=_params("arbitrary"),
            name="residual_last",
        )(x2d, f2d, gains), None
    return pl.pallas_call(
        _residual_kernel,
        grid=(n // tm,),
        in_specs=[rows, rows, _gain_spec(6 * layer + 5), _gain_spec(6 * (layer + 1))],
        out_specs=(rows, rows),
        out_shape=(x_shape, jax.ShapeDtypeStruct((n, D_MODEL), BF16)),
        compiler_params=_params("arbitrary"),
        name="residual",
    )(x2d, f2d, gains, gains)


def _angle_tables(num, den):
    ang = (num % den).astype(F32) * (2.0 * math.pi / den)
    return jnp.cos(ang), jnp.sin(ang)


def _dft_tables():
    j = jnp.arange(SEQ, dtype=jnp.int32)
    rows = SEQ // FFT_RADIX
    q = rows // DFT_SPLIT
    up = jnp.arange(q, dtype=jnp.int32)
    a = jnp.concatenate([up, 2 * q + up, 2 * q - 1 - up, 4 * q - 1 - up]).reshape(2, 2 * q, 1, 1)
    b_up = jnp.arange(DFT_SPLIT, dtype=jnp.int32)
    b = jnp.stack([b_up, DFT_SPLIT - 1 - b_up]).reshape(2, 1, DFT_SPLIT, 1)
    ca, sa = _angle_tables(DFT_SPLIT * a * j, FFT_N)
    cb, sb = _angle_tables((2 * b + 1) * j, 2 * FFT_N)
    big_c = (ca * cb - sa * sb).reshape(SEQ, SEQ).astype(BF16)
    big_s = (sa * cb + ca * sb).reshape(SEQ, SEQ).astype(BF16)
    kappa = jnp.arange(rows, dtype=jnp.int32)[None, :, None]
    m = jnp.arange(rows, dtype=jnp.int32)[None, None, :]
    r = jnp.arange(FFT_RADIX, dtype=jnp.int32)[:, None, None]
    c0, s0 = _angle_tables((2 * kappa + 1) * m, 2 * FFT_SUB)
    cr, sr = _angle_tables((2 * kappa + 1) * r, 2 * FFT_N)
    cf = (c0 * cr - s0 * sr).astype(BF16)
    sf = (s0 * cr + c0 * sr).astype(BF16)
    small = (cf, sf, cf.transpose(0, 2, 1), sf.transpose(0, 2, 1))
    return big_c, big_s, small


def kernel(x, mem, norm_gains, mem_norm, w_in, gate_bias, hy_short_w, hy_w1, hy_b1, hy_w2, hy_b2,
           hy_w3, hy_freq, hy_bias, na_rpb, sc_conv_w, w_branch, w_out, xa_wq, xa_wkv, xa_wo,
           ffn_up, ffn_conv, ffn_down):
    b, l, d = x.shape
    depth = w_in.shape[0]
    assert (l, d) == (SEQ, D_MODEL) and mem.shape[1:] == (N_MEM, D_MODEL)
    n = b * l
    dft_c, dft_s, conv_tables = _dft_tables()
    gains = norm_gains.astype(F32).reshape(depth * 6, 1, d)
    mem_gain = mem_norm.astype(F32).reshape(depth, 1, d)
    w1p = jnp.pad(hy_w1.astype(F32), ((0, 0), (0, HY_HIDDEN - HY_EMB), (0, 0)))
    b1 = hy_b1.astype(F32).reshape(depth, 1, HY_HIDDEN)
    b2 = hy_b2.astype(F32).reshape(depth, 1, HY_HIDDEN)
    na_bias = _na_bias(na_rpb)
    w_branch2d = w_branch.reshape(depth, N_BRANCH * HY_W, d)
    x2d = x.reshape(n, d)
    hn = _prenorm(x2d, gains, 0)
    for i in range(depth):
        w_hyena, w_gate, w_br, w_o, wq, wkv, wo, w_up, w_down = _cast_layer(
            i, (w_in, 0, 3 * HY_W), (w_in, 3 * HY_W + 3 * NA_W + 3 * SC_W, N_BRANCH * D_MODEL),
            (w_branch2d, 0, d), (w_out, 0, d), (xa_wq, 0, d), (xa_wkv, 0, 2 * d), (xa_wo, 0, d),
            (ffn_up, 0, 2 * D_FF), (ffn_down, 0, d))
        kr, ki = _hyena_filters(dft_c, dft_s, w1p, b1, hy_w2, b2, hy_w3, hy_freq, i)
        hn3 = hn.reshape(b, l, d)
        ya = [_hyena_mixer(hn3, w_hyena, hy_short_w, conv_tables, kr, ki, hy_bias, i, tile)
              for tile in range(HY_W // COL_TILE)]
        yb = _na_mixer(hn3, w_in, i, na_bias)
        yc = _shortconv_mixer(hn3, w_in, i, sc_conv_w)
        x2d = _merge(hn, x2d, ya[0], ya[1], yb.reshape(n, NA_W), yc.reshape(n, SC_W), w_gate, gate_bias,
                     w_br.reshape(N_BRANCH, HY_W, d), w_o, gains, i)
        kv = _mem_kv(mem, mem_gain, i, wkv)
        x3, hn2 = _xattn(x2d.reshape(b, l, d), kv, wq, wo, gains, i)
        f = _ffn(hn2, w_up, ffn_conv, w_down, i)
        x2d, hn = _residual(x3.reshape(n, d), f.reshape(n, d), gains, i, i + 1 == depth)
    return x2d.reshape(b, l, d)
```

```python
import functools
import math

import jax
import jax.numpy as jnp
from jax import lax
from jax.experimental import pallas as pl
from jax.experimental.pallas import tpu as pltpu

D_MODEL = 1024
SEQ = 2048
N_MEM = 256
GRID_W = 64
HY_W = 512
NA_HEADS = 8
NA_HEAD_DIM = 64
NA_W = NA_HEADS * NA_HEAD_DIM
NA_WIN_ROWS = 8
NA_WIN_COLS = 16
SC_W = 512
XA_HEADS = 4
XA_HEAD_DIM = D_MODEL // XA_HEADS
D_FF = 2816
HY_ORDER = 2
HY_EMB = 33
HY_HIDDEN = 64
HY_FAST_DECAY = 0.3
HY_SLOW_DECAY = 1.5
HY_TARGET = 1e-2
N_BRANCH = 3
EPS = 1e-6

FFT_N = 2 * SEQ
FFT_RADIX = 4
FFT_SUB = FFT_N // FFT_RADIX
NA_ROWS = SEQ // GRID_W
NA_KEYS = NA_WIN_ROWS * GRID_W
NA_GROUP = 4
NA_ROW_GROUP = 8
LANES = 128
COL_TILE = 256
ROW_TILE = 512
MERGE_ROWS = 2 * ROW_TILE
XA_ROWS = 2 * ROW_TILE
XA_CHUNK = 256
CAST_STEPS = 8
CONV_PAD = 8
DFT_SPLIT = 64
CONV_CHUNK = 512
FFN_CUTS = (0, 6 * COL_TILE, D_FF)
CONV_HALO = 16
VMEM_LIMIT = 60 * 1024 * 1024

BF16 = jnp.bfloat16
F32 = jnp.float32


def _params(*sem):
    return pltpu.CompilerParams(dimension_semantics=sem, vmem_limit_bytes=VMEM_LIMIT)


def _gain_spec(index):
    return pl.BlockSpec((None, 1, D_MODEL), lambda *_: (index, 0, 0))


def _layer_spec(arr, layer, width=None, col=0):
    _, r, c = arr.shape
    return pl.BlockSpec((None, r, c if width is None else width), lambda *_: (layer, 0, col))


def _rms(xf, g):
    ms = jnp.mean(xf * xf, axis=-1, keepdims=True)
    return xf * lax.rsqrt(ms + EPS) * g


def _dwconv3(u, w_ref):
    n = u.shape[0]
    zeros = jnp.zeros((CONV_PAD, u.shape[1]), F32)
    padded = jnp.concatenate([zeros, u, zeros], axis=0)
    m = n + 2 * CONV_PAD
    prev = pltpu.roll(padded, 1, 0)[CONV_PAD:CONV_PAD + n]
    nxt = pltpu.roll(padded, m - 1, 0)[CONV_PAD:CONV_PAD + n]
    return prev * w_ref[0:1, :] + u * w_ref[1:2, :] + nxt * w_ref[2:3, :]


def _prenorm_kernel(x_ref, g_ref, o_ref):
    o_ref[...] = _rms(x_ref[...], g_ref[...]).astype(BF16)


def _prenorm(x2d, gains, gi):
    n = x2d.shape[0]
    tm = 1024
    return pl.pallas_call(
        _prenorm_kernel,
        grid=(n // tm,),
        in_specs=[pl.BlockSpec((tm, D_MODEL), lambda i: (i, 0)),
                  _gain_spec(gi)],
        out_specs=pl.BlockSpec((tm, D_MODEL), lambda i: (i, 0)),
        out_shape=jax.ShapeDtypeStruct((n, D_MODEL), BF16),
        compiler_params=_params("arbitrary"),
        name="prenorm",
    )(x2d, gains)


def _cast_kernel(parts, *refs):
    n_in = sum(parts)
    w_refs, o_refs = refs[:n_in], refs[n_in:]
    k = 0
    for o_ref, n in zip(o_refs, parts):
        width = o_ref.shape[1] // n
        for p in range(n):
            o_ref[:, p * width:(p + 1) * width] = w_refs[k][...].astype(BF16)
            k += 1


def _cast_layer(layer, *weights):
    in_specs, out_specs, out_shapes, operands, parts = [], [], [], [], []
    for w, col0, ncols in weights:
        rb = w.shape[1] // CAST_STEPS
        cb = math.gcd(col0, ncols) if col0 else ncols
        assert w.shape[1] % CAST_STEPS == 0 and rb % 16 == 0 and cb % LANES == 0
        parts.append(ncols // cb)
        for p in range(ncols // cb):
            in_specs.append(pl.BlockSpec((None, rb, cb),
                                         lambda i, cblk=col0 // cb + p: (layer, i, cblk)))
            operands.append(w)
        out_specs.append(pl.BlockSpec((rb, ncols), lambda i: (i, 0)))
        out_shapes.append(jax.ShapeDtypeStruct((w.shape[1], ncols), BF16))
    return pl.pallas_call(
        functools.partial(_cast_kernel, tuple(parts)),
        grid=(CAST_STEPS,),
        in_specs=in_specs,
        out_specs=tuple(out_specs),
        out_shape=tuple(out_shapes),
        compiler_params=_params("arbitrary"),
        name="cast_bf16",
    )(*operands)


def _filter_mlp_kernel(z_ref, w1_ref, b1_ref, w2_ref, b2_ref, w3_ref, f_ref, t_ref, dl_ref,
                       hs_ref, hd_ref):
    hp = lax.Precision.HIGHEST
    h = jnp.sin(f_ref[0:1, :] * (jnp.dot(z_ref[...], w1_ref[...], precision=hp) + b1_ref[...]))
    h = jnp.sin(f_ref[1:2, :] * (jnp.dot(h, w2_ref[...], precision=hp) + b2_ref[...]))
    decay = jnp.exp(-t_ref[...] * dl_ref[...])
    row = lax.broadcasted_iota(jnp.int32, (SEQ, HY_W), 0)
    h_hi = h.astype(BF16)
    h_lo = (h - h_hi.astype(F32)).astype(BF16)

    def out_layer(cols):
        w = w3_ref[:, cols]
        w_hi = w.astype(BF16)
        w_lo = (w - w_hi.astype(F32)).astype(BF16)
        return (jnp.dot(h_hi, w_hi, preferred_element_type=F32)
                + (jnp.dot(h_hi, w_lo, preferred_element_type=F32)
                   + jnp.dot(h_lo, w_hi, preferred_element_type=F32)))

    for o in range(HY_ORDER):
        c_f = o * HY_W
        c_b = HY_ORDER * HY_W + o * HY_W
        hf = out_layer(slice(c_f, c_f + HY_W)) * decay
        hb = out_layer(slice(c_b, c_b + HY_W)) * decay
        hb = jnp.where(row == 0, 0.0, hb)
        hs_ref[:, c_f:c_f + HY_W] = (hf + hb).astype(BF16)
        hd_ref[:, c_f:c_f + HY_W] = (hb - hf).astype(BF16)


def _filter_dft_kernel(c_ref, s_ref, hs_ref, hd_ref, kr_ref, ki_ref):
    kr_ref[...] = jnp.dot(c_ref[...], hs_ref[...], preferred_element_type=F32) * (2.0 / FFT_N)
    ki_ref[...] = jnp.dot(s_ref[...], hd_ref[...], preferred_element_type=F32) * (2.0 / FFT_N)


def _hyena_filters(dft_c, dft_s, w1p, b1, w2, b2, w3, freq, layer):
    t = jnp.linspace(0.0, 1.0, SEQ, dtype=F32)[:, None]
    bands = (HY_EMB - 1) // 2
    w = 2.0 * math.pi * jnp.arange(SEQ, dtype=F32)[:, None] / SEQ
    f = jnp.linspace(1e-4, bands - 1, bands, dtype=F32)[None, :]
    z = jnp.concatenate([t, jnp.cos(f * w), -jnp.sin(f * w)], axis=-1)
    z = jnp.pad(z, ((0, 0), (0, HY_HIDDEN - HY_EMB)))
    deltas = jnp.abs(jnp.linspace(math.log(HY_TARGET) / HY_SLOW_DECAY,
                                  math.log(HY_TARGET) / HY_FAST_DECAY, HY_W, dtype=F32))[None, :]
    width = HY_ORDER * HY_W
    whole = lambda a: pl.BlockSpec(a.shape, lambda i: (0,) * a.ndim)
    taps = pl.BlockSpec((SEQ, width), lambda i: (0, 0))
    hs, hd = pl.pallas_call(
        _filter_mlp_kernel,
        grid=(1,),
        in_specs=[whole(z)] + [_layer_spec(a, layer) for a in (w1p, b1, w2, b2, w3, freq)]
        + [whole(t), whole(deltas)],
        out_specs=(taps, taps),
        out_shape=(jax.ShapeDtypeStruct((SEQ, width), BF16),
                   jax.ShapeDtypeStruct((SEQ, width), BF16)),
        compiler_params=_params("arbitrary"),
        name="hyena_filter_mlp",
    )(z, w1p, b1, w2, b2, w3, freq, t, deltas)
    nt = width // COL_TILE
    full = pl.BlockSpec((SEQ, SEQ), lambda j: (0, 0))
    col = pl.BlockSpec((SEQ, COL_TILE), lambda j: (0, j))
    kr, ki = pl.pallas_call(
        _filter_dft_kernel,
        grid=(nt,),
        in_specs=[full, full, col, col],
        out_specs=(col, col),
        out_shape=(jax.ShapeDtypeStruct((SEQ, width), F32),
                   jax.ShapeDtypeStruct((SEQ, width), F32)),
        compiler_params=_params("arbitrary"),
        name="hyena_filter_dft",
    )(dft_c, dft_s, hs, hd)

    fam_shape = (FFT_RADIX, SEQ // FFT_RADIX, width)
    return kr.reshape(fam_shape), ki.reshape(fam_shape)


def _hyena_kernel(hn_ref, wv_ref, w1_ref, w2_ref, sv_ref, s1_ref, s2_ref, cf_ref, sf_ref, ct_ref,
                  st_ref, kr0_ref, ki0_ref, kr1_ref, ki1_ref, bias_ref, o_ref,
                  slab_v_ref, slab_1_ref, slab_2_ref, slab_o_ref):
    radix = FFT_RADIX
    rows = SEQ // radix
    n_slab = COL_TILE // LANES
    zero_row = jnp.zeros((CONV_PAD, COL_TILE), F32)

    def project(w_ref, slab_ref):
        half = SEQ // 2
        for top in (0, half):
            u = jnp.dot(hn_ref[top:top + half, :], w_ref[...], preferred_element_type=F32)
            for j in range(n_slab):
                slab_ref[j, top:top + half, :] = u[:, j * LANES:(j + 1) * LANES]
        return [jnp.concatenate([slab_ref[j, pl.ds(r, rows, stride=radix), :] for j in range(n_slab)],
                                axis=1) for r in range(radix)]

    def short_conv(u, w_ref):
        n = u[0].shape[0]
        prev_wrap = pltpu.roll(jnp.concatenate([u[-1], zero_row], axis=0), 1, 0)[:n]
        next_wrap = pltpu.roll(jnp.concatenate([zero_row, u[0]], axis=0), n + CONV_PAD - 1, 0)[CONV_PAD:]
        prev = [prev_wrap] + u[:-1]
        nxt = u[1:] + [next_wrap]
        return [prev[r] * w_ref[0:1, :] + u[r] * w_ref[1:2, :] + nxt[r] * w_ref[2:3, :]
                for r in range(radix)]

    def cmul(ar, ai, br, bi):
        return ar * br - ai * bi, ar * bi + ai * br

    def transform(x):
        xb = [v.astype(BF16) for v in x]
        tr = [jnp.dot(cf_ref[r], xb[r], preferred_element_type=F32) for r in range(radix)]
        ti = [-jnp.dot(sf_ref[r], xb[r], preferred_element_type=F32) for r in range(radix)]
        return tr, ti

    def filter_and_invert(tr, ti, kr_ref, ki_ref):
        ar, ai = tr[0] + tr[2], ti[0] + ti[2]
        br, bi = tr[0] - tr[2], ti[0] - ti[2]
        cr, ci = tr[1] + tr[3], ti[1] + ti[3]
        dr, di = tr[1] - tr[3], ti[1] - ti[3]
        fam = [(ar + cr, ai + ci), (br + di, bi - dr), (br - di, -bi - dr), (ar - cr, ci - ai)]
        y = [cmul(fr, fi, kr_ref[f], ki_ref[f]) for f, (fr, fi) in enumerate(fam)]
        er, ei = y[0][0] + y[3][0], y[0][1] - y[3][1]
        fr, fi = y[0][0] - y[3][0], y[0][1] + y[3][1]
        gr, gi = y[1][0] + y[2][0], y[1][1] - y[2][1]
        hr, hi = y[1][0] - y[2][0], y[1][1] + y[2][1]
        p = [(er + gr, ei + gi), (fr - hi, fi + hr), (er - gr, ei - gi), (fr + hi, fi - hr)]
        out = []
        for r in range(radix):
            pr, pi_ = p[r]
            out.append(jnp.dot(ct_ref[r], pr.astype(BF16), preferred_element_type=F32)
                       - jnp.dot(st_ref[r], pi_.astype(BF16), preferred_element_type=F32))
        return out

    v = short_conv(project(wv_ref, slab_v_ref), sv_ref)
    spectrum = transform(v)
    x1 = short_conv(project(w1_ref, slab_1_ref), s1_ref)
    y = filter_and_invert(*spectrum, kr0_ref, ki0_ref)
    z = [x1[r] * (y[r] + v[r] * bias_ref[0:1, :]) for r in range(radix)]
    spectrum = transform(z)
    x2 = short_conv(project(w2_ref, slab_2_ref), s2_ref)
    y = filter_and_invert(*spectrum, kr1_ref, ki1_ref)
    for r in range(radix):
        out = x2[r] * (y[r] + z[r] * bias_ref[1:2, :])
        for j in range(n_slab):
            slab_o_ref[j, pl.ds(r, rows, stride=radix), :] = out[:, j * LANES:(j + 1) * LANES]
    o_ref[...] = jnp.concatenate([slab_o_ref[j] for j in range(n_slab)], axis=1).astype(BF16)


def _hyena_mixer(hn, w_hy, short_w, tables, kr, ki, bias, layer, tile):
    b = hn.shape[0]
    nt = HY_W // COL_TILE
    rows = SEQ // FFT_RADIX
    once = pl.Buffered(1)
    cf, sf, ct, st = tables

    def fixed(shape, index):
        return pl.BlockSpec(shape, lambda i: index, pipeline_mode=once)

    def wcol(k):
        return fixed((D_MODEL, COL_TILE), (0, k * nt + tile))

    def scol(k):
        return _layer_spec(short_w, layer, COL_TILE, k * nt + tile)

    def kfam(o):
        return fixed((FFT_RADIX, rows, COL_TILE), (0, 0, o * nt + tile))

    small = fixed((FFT_RADIX, rows, rows), (0, 0, 0))
    slab = pltpu.VMEM((COL_TILE // LANES, SEQ, LANES), F32)
    out = pl.pallas_call(
        _hyena_kernel,
        grid=(b,),
        in_specs=[pl.BlockSpec((None, SEQ, D_MODEL), lambda i: (i, 0, 0)),
                  wcol(0), wcol(1), wcol(2), scol(0), scol(1), scol(2),
                  small, small, small, small,
                  kfam(0), kfam(0), kfam(1), kfam(1),
                  _layer_spec(bias, layer, COL_TILE, tile)],
        out_specs=pl.BlockSpec((None, SEQ, COL_TILE), lambda i: (i, 0, 0)),
        out_shape=jax.ShapeDtypeStruct((b, SEQ, COL_TILE), BF16),
        scratch_shapes=[slab, slab, slab, slab],
        compiler_params=_params("arbitrary"),
        name="hyena_mixer",
    )(hn, w_hy, w_hy, w_hy, short_w, short_w, short_w, cf, sf, ct, st, kr, ki, kr, ki, bias)
    return out.reshape(b * SEQ, COL_TILE)


def _shortconv_kernel(hn_ref, wb_ref, wc_ref, wx_ref, cw_ref, o_ref):
    wb = wb_ref[...].astype(BF16)
    wc = wc_ref[...].astype(BF16)
    wx = wx_ref[...].astype(BF16)

    def project(r0):
        lo = max(r0 - CONV_HALO, 0)
        hi = min(r0 + CONV_CHUNK + CONV_HALO, SEQ)
        hn = hn_ref[lo:hi, :]
        return (r0, r0 - lo, jnp.dot(hn, wb, preferred_element_type=F32),
                jnp.dot(hn, wc, preferred_element_type=F32), jnp.dot(hn, wx, preferred_element_type=F32))

    def finish(r0, skip, bg, cg, xi):
        out = bg * _dwconv3(cg * xi, cw_ref)
        o_ref[r0:r0 + CONV_CHUNK, :] = out[skip:skip + CONV_CHUNK].astype(BF16)

    pending = None
    for r0 in range(0, SEQ, CONV_CHUNK):
        current = project(r0)
        if pending is not None:
            finish(*pending)
        pending = current
    finish(*pending)


def _w_in_cols(layer, width, first):
    return lambda k, nt: pl.BlockSpec((None, D_MODEL, width),
                                      lambda j, i: (layer, 0, first + k * nt + j))


def _shortconv_mixer(hn, w_in, layer, conv_w):
    b = hn.shape[0]
    nt = SC_W // COL_TILE
    wcol = _w_in_cols(layer, COL_TILE, (3 * HY_W + 3 * NA_W) // COL_TILE)
    return pl.pallas_call(
        _shortconv_kernel,
        grid=(nt, b),
        in_specs=[pl.BlockSpec((None, SEQ, D_MODEL), lambda j, i: (i, 0, 0)),
                  wcol(0, nt), wcol(1, nt), wcol(2, nt),
                  pl.BlockSpec((None, 3, COL_TILE), lambda j, i: (layer, 0, j))],
        out_specs=pl.BlockSpec((None, SEQ, COL_TILE), lambda j, i: (i, 0, j)),
        out_shape=jax.ShapeDtypeStruct((b, SEQ, SC_W), BF16),
        compiler_params=_params("arbitrary", "arbitrary"),
        name="shortconv_mixer",
    )(hn, w_in, w_in, w_in, conv_w)


def _na_kernel(hn_ref, wq_ref, wk_ref, wv_ref, bias_ref, o_ref, q_ref, k_ref, v_ref, s_ref):
    hn = hn_ref[...]
    q = jnp.dot(hn, wq_ref[...].astype(BF16), preferred_element_type=F32)
    q_ref[...] = (q * (NA_HEAD_DIM ** -0.5)).astype(BF16)
    k_ref[...] = jnp.dot(hn, wk_ref[...].astype(BF16), preferred_element_type=F32).astype(BF16)
    v_ref[...] = jnp.dot(hn, wv_ref[...].astype(BF16), preferred_element_type=F32).astype(BF16)
    gw = NA_GROUP * NA_HEAD_DIM
    same_head = (lax.broadcasted_iota(jnp.int32, (gw, gw), 0) // NA_HEAD_DIM
                 == lax.broadcasted_iota(jnp.int32, (gw, gw), 1) // NA_HEAD_DIM)

    def scores(r):
        w0 = jnp.clip(r - NA_WIN_ROWS // 2, 0, NA_ROWS - NA_WIN_ROWS)
        off = w0 - r + (NA_WIN_ROWS - 1)
        q0 = pl.multiple_of(r * GRID_W, GRID_W)
        k0 = pl.multiple_of(w0 * GRID_W, GRID_W)
        q_row = q_ref[pl.ds(q0, GRID_W), :]
        q_heads = jnp.where(same_head, jnp.concatenate([q_row] * NA_GROUP, axis=0), 0)
        s = lax.dot_general(q_heads, k_ref[pl.ds(k0, NA_KEYS), :], (((1,), (1,)), ((), ())),
                            preferred_element_type=F32)
        bias = jnp.concatenate(
            [jnp.concatenate([bias_ref[h, off + 2 * m] for m in range(NA_WIN_ROWS // 2)], axis=1)
             for h in range(NA_GROUP)], axis=0)
        return s + bias

    def attend(r, s):
        w0 = jnp.clip(r - NA_WIN_ROWS // 2, 0, NA_ROWS - NA_WIN_ROWS)
        q0 = pl.multiple_of(r * GRID_W, GRID_W)
        k0 = pl.multiple_of(w0 * GRID_W, GRID_W)
        p = jnp.exp(s - jnp.max(s, axis=-1, keepdims=True))
        inv = 1.0 / jnp.sum(p, axis=-1, keepdims=True)
        pv = jnp.dot(p.astype(BF16), v_ref[pl.ds(k0, NA_KEYS), :], preferred_element_type=F32)
        pv = jnp.where(same_head, pv * inv, 0.0)
        out = pv[0:GRID_W]
        for h in range(1, NA_GROUP):
            out = out + pv[h * GRID_W:(h + 1) * GRID_W]
        o_ref[pl.ds(q0, GRID_W), :] = out.astype(BF16)

    groups = NA_ROWS // NA_ROW_GROUP
    for t in range(NA_ROW_GROUP):
        s_ref[t] = scores(jnp.int32(t))

    def rows_body(i, carry):
        for t in range(NA_ROW_GROUP):
            s_next = scores(i * NA_ROW_GROUP + t)
            attend((i - 1) * NA_ROW_GROUP + t, s_ref[t])
            s_ref[t] = s_next
        return carry

    lax.fori_loop(1, groups, rows_body, 0)
    for t in range(NA_ROW_GROUP):
        attend(jnp.int32((groups - 1) * NA_ROW_GROUP + t), s_ref[t])


def _na_bias(rpb):
    c = jnp.arange(GRID_W)
    col_start = jnp.clip(c - NA_WIN_COLS // 2, 0, GRID_W - NA_WIN_COLS)
    col_mask = (c[None, :] >= col_start[:, None]) & (c[None, :] < col_start[:, None] + NA_WIN_COLS)
    dc = jnp.clip(c[None, :] - c[:, None] + NA_WIN_COLS - 1, 0, 2 * NA_WIN_COLS - 2)
    pick = (dc[None] == jnp.arange(2 * NA_WIN_COLS - 1)[:, None, None]).astype(F32)
    table = jnp.einsum("lhrd,dqc->lhrqc", rpb.astype(F32), pick, precision=lax.Precision.HIGHEST)
    table = table + jnp.where(col_mask, 0.0, -1e30)
    return jnp.concatenate([table[:, :, :-1], table[:, :, 1:]], axis=-1)


def _na_mixer(hn, w_in, layer, bias):
    b = hn.shape[0]
    gw = NA_GROUP * NA_HEAD_DIM
    ng = NA_W // gw
    wcol = _w_in_cols(layer, gw, 3 * HY_W // gw)
    return pl.pallas_call(
        _na_kernel,
        grid=(ng, b),
        in_specs=[pl.BlockSpec((None, SEQ, D_MODEL), lambda j, i: (i, 0, 0)),
                  wcol(0, ng), wcol(1, ng), wcol(2, ng),
                  pl.BlockSpec((None, NA_GROUP, 2 * NA_WIN_ROWS - 2, GRID_W, 2 * GRID_W),
                               lambda j, i: (layer, j, 0, 0, 0))],
        out_specs=pl.BlockSpec((None, SEQ, gw), lambda j, i: (i, 0, j)),
        out_shape=jax.ShapeDtypeStruct((b, SEQ, NA_W), BF16),
        scratch_shapes=[pltpu.VMEM((SEQ, gw), BF16) for _ in range(3)]
        + [pltpu.VMEM((NA_ROW_GROUP, gw, NA_KEYS), F32)],
        compiler_params=_params("arbitrary", "arbitrary"),
        name="na_mixer",
    )(hn, w_in, w_in, w_in, bias)


def _merge_kernel(hn_ref, x_ref, ya0_ref, ya1_ref, yb_ref, yc_ref, wg_ref, gb_ref, wb_ref, wo_ref,
                  g_ref, o_ref):
    def gated_sum(rows):
        hn = hn_ref[rows, :]
        ya = jnp.concatenate([ya0_ref[rows, :], ya1_ref[rows, :]], axis=1)
        merged = None
        for i, y in enumerate((ya, yb_ref[rows, :], yc_ref[rows, :])):
            pre = jnp.dot(hn, wg_ref[:, i * D_MODEL:(i + 1) * D_MODEL], preferred_element_type=F32)
            gate = jax.nn.sigmoid(pre + gb_ref[i:i + 1, :])
            term = gate * jnp.dot(y, wb_ref[i], preferred_element_type=F32)
            merged = term if merged is None else merged + term
        return merged.astype(BF16)

    chunks = [slice(r, r + ROW_TILE) for r in range(0, MERGE_ROWS, ROW_TILE)]
    merged = [gated_sum(rows) for rows in chunks]
    for rows, m in zip(chunks, merged):
        out = jnp.dot(m, wo_ref[...], preferred_element_type=F32)
        o_ref[rows, :] = x_ref[rows, :] + _rms(out, g_ref[...])


def _merge(hn2d, x2d, ya0, ya1, yb, yc, w_gate, gate_bias, w_branch, w_out, gains, layer):
    n = x2d.shape[0]
    tm = MERGE_ROWS
    once = pl.Buffered(1)
    rows = lambda w: pl.BlockSpec((tm, w), lambda i: (i, 0))
    return pl.pallas_call(
        _merge_kernel,
        grid=(n // tm,),
        in_specs=[rows(D_MODEL), rows(D_MODEL), rows(COL_TILE), rows(COL_TILE), rows(NA_W), rows(SC_W),
                  pl.BlockSpec((D_MODEL, N_BRANCH * D_MODEL), lambda i: (0, 0), pipeline_mode=once),
                  _layer_spec(gate_bias, layer),
                  pl.BlockSpec((N_BRANCH, HY_W, D_MODEL), lambda i: (0, 0, 0), pipeline_mode=once),
                  pl.BlockSpec((D_MODEL, D_MODEL), lambda i: (0, 0), pipeline_mode=once),
                  _gain_spec(6 * layer + 1)],
        out_specs=rows(D_MODEL),
        out_shape=jax.ShapeDtypeStruct((n, D_MODEL), F32),
        compiler_params=_params("arbitrary"),
        name="merge",
    )(hn2d, x2d, ya0, ya1, yb, yc, w_gate, gate_bias, w_branch, w_out, gains)


def _kv_kernel(m_ref, g_ref, w_ref, o_ref):
    mn = _rms(m_ref[...], g_ref[...]).astype(BF16)
    o_ref[...] = jnp.dot(mn, w_ref[...], preferred_element_type=F32).astype(BF16)


def _mem_kv(mem, mem_norm, layer, wkv):
    b = mem.shape[0]
    return pl.pallas_call(
        _kv_kernel,
        grid=(b,),
        in_specs=[pl.BlockSpec((None, N_MEM, D_MODEL), lambda i: (i, 0, 0)),
                  _layer_spec(mem_norm, layer),
                  pl.BlockSpec((D_MODEL, 2 * D_MODEL), lambda i: (0, 0))],
        out_specs=pl.BlockSpec((None, N_MEM, 2 * D_MODEL), lambda i: (i, 0, 0)),
        out_shape=jax.ShapeDtypeStruct((b, N_MEM, 2 * D_MODEL), BF16),
        compiler_params=_params("arbitrary"),
        name="mem_kv",
    )(mem, mem_norm, wkv)


def _xattn_kernel(x_ref, kv_ref, wq_ref, wo_ref, gq_ref, go_ref, gn_ref, o_ref, hn_ref):
    chunks = [slice(r, r + XA_CHUNK) for r in range(0, XA_ROWS, XA_CHUNK)]
    head_cols = [slice(i * XA_HEAD_DIM, (i + 1) * XA_HEAD_DIM) for i in range(XA_HEADS)]

    def query(rows):
        h = _rms(x_ref[rows, :], gq_ref[...]).astype(BF16)
        q = jnp.dot(h, wq_ref[...], preferred_element_type=F32) * (XA_HEAD_DIM ** -0.5)
        return q.astype(BF16)

    def scores(q):
        return [lax.dot_general(q[:, sl], kv_ref[:, sl], (((1,), (1,)), ((), ())),
                                preferred_element_type=F32) for sl in head_cols]

    def values(s_heads):
        heads = []
        for i, s in enumerate(s_heads):
            vm = kv_ref[:, D_MODEL + i * XA_HEAD_DIM:D_MODEL + (i + 1) * XA_HEAD_DIM]
            p = jnp.exp(s - jnp.max(s, axis=-1, keepdims=True))
            den = jnp.sum(p, axis=-1, keepdims=True)
            heads.append((jnp.dot(p.astype(BF16), vm, preferred_element_type=F32) / den).astype(BF16))
        return jnp.concatenate(heads, axis=-1)

    s_all = [scores(q) for q in [query(rows) for rows in chunks]]
    attended = [values(s) for s in s_all]
    for rows, a in zip(chunks, attended):
        o = jnp.dot(a, wo_ref[...], preferred_element_type=F32)
        xn = x_ref[rows, :] + _rms(o, go_ref[...])
        o_ref[rows, :] = xn
        hn_ref[rows, :] = _rms(xn, gn_ref[...]).astype(BF16)


def _xattn(x, kv, wq, wo, gains, layer):
    b = x.shape[0]
    tm = XA_ROWS
    once = pl.Buffered(1)
    rows = pl.BlockSpec((None, tm, D_MODEL), lambda i, j: (i, j, 0))
    wfull = pl.BlockSpec((D_MODEL, D_MODEL), lambda i, j: (0, 0), pipeline_mode=once)
    return pl.pallas_call(
        _xattn_kernel,
        grid=(b, SEQ // tm),
        in_specs=[rows, pl.BlockSpec((None, N_MEM, 2 * D_MODEL), lambda i, j: (i, 0, 0)),
                  wfull, wfull, _gain_spec(6 * layer + 2), _gain_spec(6 * layer + 3),
                  _gain_spec(6 * layer + 4)],
        out_specs=(rows, rows),
        out_shape=(jax.ShapeDtypeStruct((b, SEQ, D_MODEL), F32),
                   jax.ShapeDtypeStruct((b, SEQ, D_MODEL), BF16)),
        compiler_params=_params("arbitrary", "arbitrary"),
        name="xattn",
    )(x, kv, wq, wo, gains, gains, gains)


def _gelu_tanh(x):
    c = math.sqrt(2.0 / math.pi)
    half = 0.5 * x
    return half + half * jnp.tanh(x * (c + (c * 0.044715) * (x * x)))


def _ffn_kernel(hn_ref, wu_ref, cw_ref, wd_ref, o_ref):
    def up(r0, c0, c1):
        lo = max(r0 - CONV_HALO, 0)
        hi = min(r0 + CONV_CHUNK + CONV_HALO, SEQ)
        hn = hn_ref[lo:hi, :]
        ug = jnp.dot(hn, wu_ref[:, c0:c1], preferred_element_type=F32)
        uv = jnp.dot(hn, wu_ref[:, D_FF + c0:D_FF + c1], preferred_element_type=F32)
        return r0, r0 - lo, c0, c1, ug, uv

    def down(acc, r0, skip, c0, c1, ug, uv):
        act = (_gelu_tanh(_dwconv3(ug, cw_ref.at[:, c0:c1]))
               * _dwconv3(uv, cw_ref.at[:, D_FF + c0:D_FF + c1]))
        act = act[skip:skip + CONV_CHUNK].astype(BF16)
        part = jnp.dot(act, wd_ref[c0:c1, :], preferred_element_type=F32)
        acc = part if c0 == 0 else acc + part
        if c1 == D_FF:
            o_ref[r0:r0 + CONV_CHUNK, :] = acc.astype(o_ref.dtype)
        return acc

    pending, acc = None, None
    for r0 in range(0, SEQ, CONV_CHUNK):
        for c0, c1 in zip(FFN_CUTS[:-1], FFN_CUTS[1:]):
            current = up(r0, c0, c1)
            if pending is not None:
                acc = down(acc, *pending)
            pending = current
    down(acc, *pending)


def _ffn(hn, w_up, w_conv, w_down, layer):
    b = hn.shape[0]
    once = pl.Buffered(1)
    return pl.pallas_call(
        _ffn_kernel,
        grid=(b,),
        in_specs=[pl.BlockSpec((None, SEQ, D_MODEL), lambda i: (i, 0, 0)),
                  pl.BlockSpec((D_MODEL, 2 * D_FF), lambda i: (0, 0), pipeline_mode=once),
                  _layer_spec(w_conv, layer),
                  pl.BlockSpec((D_FF, D_MODEL), lambda i: (0, 0), pipeline_mode=once)],
        out_specs=pl.BlockSpec((None, SEQ, D_MODEL), lambda i: (i, 0, 0)),
        out_shape=jax.ShapeDtypeStruct((b, SEQ, D_MODEL), BF16),
        compiler_params=_params("arbitrary"),
        name="ffn",
    )(hn, w_up, w_conv, w_down)


def _residual_kernel(x_ref, f_ref, g_ref, gn_ref, o_ref, hn_ref):
    xn = x_ref[...] + _rms(f_ref[...].astype(F32), g_ref[...])
    o_ref[...] = xn
    hn_ref[...] = _rms(xn, gn_ref[...]).astype(BF16)


def _residual_last_kernel(x_ref, f_ref, g_ref, o_ref):
    o_ref[...] = x_ref[...] + _rms(f_ref[...].astype(F32), g_ref[...])


def _residual(x2d, f2d, gains, layer, last):
    n = x2d.shape[0]
    tm = 1024
    rows = pl.BlockSpec((tm, D_MODEL), lambda i: (i, 0))
    x_shape = jax.ShapeDtypeStruct((n, D_MODEL), F32)
    if last:
        return pl.pallas_call(
            _residual_last_kernel,
            grid=(n // tm,),
            in_specs=[rows, rows, _gain_spec(6 * layer + 5)],
            out_specs=rows,
            out_shape=x_shape,
            compiler_params=_params("arbitrary"),
            name="residual_last",
        )(x2d, f2d, gains), None
    return pl.pallas_call(
        _residual_kernel,
        grid=(n // tm,),
        in_specs=[rows, rows, _gain_spec(6 * layer + 5), _gain_spec(6 * (layer + 1))],
        out_specs=(rows, rows),
        out_shape=(x_shape, jax.ShapeDtypeStruct((n, D_MODEL), BF16)),
        compiler_params=_params("arbitrary"),
        name="residual",
    )(x2d, f2d, gains, gains)


def _angle_tables(num, den):
    ang = (num % den).astype(F32) * (2.0 * math.pi / den)
    return jnp.cos(ang), jnp.sin(ang)


def _dft_tables():
    j = jnp.arange(SEQ, dtype=jnp.int32)
    rows = SEQ // FFT_RADIX
    q = rows // DFT_SPLIT
    up = jnp.arange(q, dtype=jnp.int32)
    a = jnp.concatenate([up, 2 * q + up, 2 * q - 1 - up, 4 * q - 1 - up]).reshape(2, 2 * q, 1, 1)
    b_up = jnp.arange(DFT_SPLIT, dtype=jnp.int32)
    b = jnp.stack([b_up, DFT_SPLIT - 1 - b_up]).reshape(2, 1, DFT_SPLIT, 1)
    ca, sa = _angle_tables(DFT_SPLIT * a * j, FFT_N)
    cb, sb = _angle_tables((2 * b + 1) * j, 2 * FFT_N)
    big_c = (ca * cb - sa * sb).reshape(SEQ, SEQ).astype(BF16)
    big_s = (sa * cb + ca * sb).reshape(SEQ, SEQ).astype(BF16)
    kappa = jnp.arange(rows, dtype=jnp.int32)[None, :, None]
    m = jnp.arange(rows, dtype=jnp.int32)[None, None, :]
    r = jnp.arange(FFT_RADIX, dtype=jnp.int32)[:, None, None]
    c0, s0 = _angle_tables((2 * kappa + 1) * m, 2 * FFT_SUB)
    cr, sr = _angle_tables((2 * kappa + 1) * r, 2 * FFT_N)
    cf = (c0 * cr - s0 * sr).astype(BF16)
    sf = (s0 * cr + c0 * sr).astype(BF16)
    small = (cf, sf, cf.transpose(0, 2, 1), sf.transpose(0, 2, 1))
    return big_c, big_s, small


def kernel(x, mem, norm_gains, mem_norm, w_in, gate_bias, hy_short_w, hy_w1, hy_b1, hy_w2, hy_b2,
           hy_w3, hy_freq, hy_bias, na_rpb, sc_conv_w, w_branch, w_out, xa_wq, xa_wkv, xa_wo,
           ffn_up, ffn_conv, ffn_down):
    b, l, d = x.shape
    depth = w_in.shape[0]
    assert (l, d) == (SEQ, D_MODEL) and mem.shape[1:] == (N_MEM, D_MODEL)
    n = b * l
    dft_c, dft_s, conv_tables = _dft_tables()
    gains = norm_gains.astype(F32).reshape(depth * 6, 1, d)
    mem_gain = mem_norm.astype(F32).reshape(depth, 1, d)
    w1p = jnp.pad(hy_w1.astype(F32), ((0, 0), (0, HY_HIDDEN - HY_EMB), (0, 0)))
    b1 = hy_b1.astype(F32).reshape(depth, 1, HY_HIDDEN)
    b2 = hy_b2.astype(F32).reshape(depth, 1, HY_HIDDEN)
    na_bias = _na_bias(na_rpb)
    w_branch2d = w_branch.reshape(depth, N_BRANCH * HY_W, d)
    x2d = x.reshape(n, d)
    hn = _prenorm(x2d, gains, 0)
    for i in range(depth):
        w_hyena, w_gate, w_br, w_o, wq, wkv, wo, w_up, w_down = _cast_layer(
            i, (w_in, 0, 3 * HY_W), (w_in, 3 * HY_W + 3 * NA_W + 3 * SC_W, N_BRANCH * D_MODEL),
            (w_branch2d, 0, d), (w_out, 0, d), (xa_wq, 0, d), (xa_wkv, 0, 2 * d), (xa_wo, 0, d),
            (ffn_up, 0, 2 * D_FF), (ffn_down, 0, d))
        kr, ki = _hyena_filters(dft_c, dft_s, w1p, b1, hy_w2, b2, hy_w3, hy_freq, i)
        hn3 = hn.reshape(b, l, d)
        ya = [_hyena_mixer(hn3, w_hyena, hy_short_w, conv_tables, kr, ki, hy_bias, i, tile)
              for tile in range(HY_W // COL_TILE)]
        yb = _na_mixer(hn3, w_in, i, na_bias)
        yc = _shortconv_mixer(hn3, w_in, i, sc_conv_w)
        x2d = _merge(hn, x2d, ya[0], ya[1], yb.reshape(n, NA_W), yc.reshape(n, SC_W), w_gate, gate_bias,
                     w_br.reshape(N_BRANCH, HY_W, d), w_o, gains, i)
        kv = _mem_kv(mem, mem_gain, i, wkv)
        x3, hn2 = _xattn(x2d.reshape(b, l, d), kv, wq, wo, gains, i)
        f = _ffn(hn2, w_up, ffn_conv, w_down, i)
        x2d, hn = _residual(x3.reshape(n, d), f.reshape(n, d), gains, i, i + 1 == depth)
    return x2d.reshape(b, l, d)
```

```python
import functools
import math

import jax
import jax.numpy as jnp
from jax import lax
from jax.experimental import pallas as pl
from jax.experimental.pallas import tpu as pltpu

D_MODEL = 1024
SEQ = 2048
N_MEM = 256
GRID_W = 64
HY_W = 512
NA_HEADS = 8
NA_HEAD_DIM = 64
NA_W = NA_HEADS * NA_HEAD_DIM
NA_WIN_ROWS = 8
NA_WIN_COLS = 16
SC_W = 512
XA_HEADS = 4
XA_HEAD_DIM = D_MODEL // XA_HEADS
D_FF = 2816
HY_ORDER = 2
HY_EMB = 33
HY_HIDDEN = 64
HY_FAST_DECAY = 0.3
HY_SLOW_DECAY = 1.5
HY_TARGET = 1e-2
N_BRANCH = 3
EPS = 1e-6

FFT_N = 2 * SEQ
FFT_RADIX = 4
FFT_SUB = FFT_N // FFT_RADIX
NA_ROWS = SEQ // GRID_W
NA_KEYS = NA_WIN_ROWS * GRID_W
NA_GROUP = 4
NA_ROW_GROUP = 8
LANES = 128
COL_TILE = 256
ROW_TILE = 512
MERGE_ROWS = 2 * ROW_TILE
XA_ROWS = 2 * ROW_TILE
XA_CHUNK = 256
CAST_STEPS = 8
CONV_PAD = 8
CONV_CHUNK = 512
FFN_CUTS = (0, 6 * COL_TILE, D_FF)
CONV_HALO = 16
VMEM_LIMIT = 60 * 1024 * 1024

BF16 = jnp.bfloat16
F32 = jnp.float32


def _params(*sem):
    return pltpu.CompilerParams(dimension_semantics=sem, vmem_limit_bytes=VMEM_LIMIT)


def _gain_spec(index):
    return pl.BlockSpec((None, 1, D_MODEL), lambda *_: (index, 0, 0))


def _layer_spec(arr, layer, width=None, col=0):
    _, r, c = arr.shape
    return pl.BlockSpec((None, r, c if width is None else width), lambda *_: (layer, 0, col))


def _rms(xf, g):
    ms = jnp.mean(xf * xf, axis=-1, keepdims=True)
    return xf * lax.rsqrt(ms + EPS) * g


def _dwconv3(u, w_ref):
    n = u.shape[0]
    zeros = jnp.zeros((CONV_PAD, u.shape[1]), F32)
    padded = jnp.concatenate([zeros, u, zeros], axis=0)
    m = n + 2 * CONV_PAD
    prev = pltpu.roll(padded, 1, 0)[CONV_PAD:CONV_PAD + n]
    nxt = pltpu.roll(padded, m - 1, 0)[CONV_PAD:CONV_PAD + n]
    return prev * w_ref[0:1, :] + u * w_ref[1:2, :] + nxt * w_ref[2:3, :]


def _prenorm_kernel(x_ref, g_ref, o_ref):
    o_ref[...] = _rms(x_ref[...], g_ref[...]).astype(BF16)


def _prenorm(x2d, gains, gi):
    n = x2d.shape[0]
    tm = 1024
    return pl.pallas_call(
        _prenorm_kernel,
        grid=(n // tm,),
        in_specs=[pl.BlockSpec((tm, D_MODEL), lambda i: (i, 0)),
                  _gain_spec(gi)],
        out_specs=pl.BlockSpec((tm, D_MODEL), lambda i: (i, 0)),
        out_shape=jax.ShapeDtypeStruct((n, D_MODEL), BF16),
        compiler_params=_params("arbitrary"),
        name="prenorm",
    )(x2d, gains)


def _cast_kernel(parts, *refs):
    n_in = sum(parts)
    w_refs, o_refs = refs[:n_in], refs[n_in:]
    k = 0
    for o_ref, n in zip(o_refs, parts):
        width = o_ref.shape[1] // n
        for p in range(n):
            o_ref[:, p * width:(p + 1) * width] = w_refs[k][...].astype(BF16)
            k += 1


def _cast_layer(layer, *weights):
    in_specs, out_specs, out_shapes, operands, parts = [], [], [], [], []
    for w, col0, ncols in weights:
        rb = w.shape[1] // CAST_STEPS
        cb = math.gcd(col0, ncols) if col0 else ncols
        assert w.shape[1] % CAST_STEPS == 0 and rb % 16 == 0 and cb % LANES == 0
        parts.append(ncols // cb)
        for p in range(ncols // cb):
            in_specs.append(pl.BlockSpec((None, rb, cb),
                                         lambda i, cblk=col0 // cb + p: (layer, i, cblk)))
            operands.append(w)
        out_specs.append(pl.BlockSpec((rb, ncols), lambda i: (i, 0)))
        out_shapes.append(jax.ShapeDtypeStruct((w.shape[1], ncols), BF16))
    return pl.pallas_call(
        functools.partial(_cast_kernel, tuple(parts)),
        grid=(CAST_STEPS,),
        in_specs=in_specs,
        out_specs=tuple(out_specs),
        out_shape=tuple(out_shapes),
        compiler_params=_params("arbitrary"),
        name="cast_bf16",
    )(*operands)


def _filter_mlp_kernel(z_ref, w1_ref, b1_ref, w2_ref, b2_ref, w3_ref, f_ref, t_ref, dl_ref,
                       hs_ref, hd_ref):
    hp = lax.Precision.HIGHEST
    h = jnp.sin(f_ref[0:1, :] * (jnp.dot(z_ref[...], w1_ref[...], precision=hp) + b1_ref[...]))
    h = jnp.sin(f_ref[1:2, :] * (jnp.dot(h, w2_ref[...], precision=hp) + b2_ref[...]))
    decay = jnp.exp(-t_ref[...] * dl_ref[...])
    row = lax.broadcasted_iota(jnp.int32, (SEQ, HY_W), 0)
    h_hi = h.astype(BF16)
    h_lo = (h - h_hi.astype(F32)).astype(BF16)

    def out_layer(cols):
        w = w3_ref[:, cols]
        w_hi = w.astype(BF16)
        w_lo = (w - w_hi.astype(F32)).astype(BF16)
        return (jnp.dot(h_hi, w_hi, preferred_element_type=F32)
                + (jnp.dot(h_hi, w_lo, preferred_element_type=F32)
                   + jnp.dot(h_lo, w_hi, preferred_element_type=F32)))

    for o in range(HY_ORDER):
        c_f = o * HY_W
        c_b = HY_ORDER * HY_W + o * HY_W
        hf = out_layer(slice(c_f, c_f + HY_W)) * decay
        hb = out_layer(slice(c_b, c_b + HY_W)) * decay
        hb = jnp.where(row == 0, 0.0, hb)
        hs_ref[:, c_f:c_f + HY_W] = (hf + hb).astype(BF16)
        hd_ref[:, c_f:c_f + HY_W] = (hb - hf).astype(BF16)


def _filter_dft_kernel(cf_ref, sf_ref, hs_ref, hd_ref, kr_ref, ki_ref):
    rows = SEQ // FFT_RADIX
    scale = 2.0 / FFT_N
    sub = lambda ref, r: ref[r * rows:(r + 1) * rows, :]
    cos_dot = lambda ref, r: jnp.dot(cf_ref[r], sub(ref, r), preferred_element_type=F32)
    sin_dot = lambda ref, r: jnp.dot(sf_ref[r], sub(ref, r), preferred_element_type=F32)
    c = [cos_dot(hs_ref, r) for r in range(FFT_RADIX)]
    a, b, e = c[0] + c[2], c[0] - c[2], c[1] + c[3]
    d_im = sin_dot(hs_ref, 3) - sin_dot(hs_ref, 1)
    for f, val in enumerate((a + e, b + d_im, b - d_im, a - e)):
        kr_ref[f] = val * scale
    s = [sin_dot(hd_ref, r) for r in range(FFT_RADIX)]
    a, b, e = s[0] + s[2], s[0] - s[2], s[1] + s[3]
    d_re = cos_dot(hd_ref, 1) - cos_dot(hd_ref, 3)
    for f, val in enumerate((a + e, b + d_re, d_re - b, e - a)):
        ki_ref[f] = val * scale


def _hyena_filters(tables, w1p, b1, w2, b2, w3, freq, layer):
    t = jnp.linspace(0.0, 1.0, SEQ, dtype=F32)[:, None]
    bands = (HY_EMB - 1) // 2
    w = 2.0 * math.pi * jnp.arange(SEQ, dtype=F32)[:, None] / SEQ
    f = jnp.linspace(1e-4, bands - 1, bands, dtype=F32)[None, :]
    z = jnp.concatenate([t, jnp.cos(f * w), -jnp.sin(f * w)], axis=-1)
    z = jnp.pad(z, ((0, 0), (0, HY_HIDDEN - HY_EMB)))
    rows = SEQ // FFT_RADIX
    decimate = lambda a: a.reshape(rows, FFT_RADIX, -1).transpose(1, 0, 2).reshape(SEQ, -1)
    z, t = decimate(z), decimate(t)
    deltas = jnp.abs(jnp.linspace(math.log(HY_TARGET) / HY_SLOW_DECAY,
                                  math.log(HY_TARGET) / HY_FAST_DECAY, HY_W, dtype=F32))[None, :]
    width = HY_ORDER * HY_W
    whole = lambda a: pl.BlockSpec(a.shape, lambda i: (0,) * a.ndim)
    taps = pl.BlockSpec((SEQ, width), lambda i: (0, 0))
    hs, hd = pl.pallas_call(
        _filter_mlp_kernel,
        grid=(1,),
        in_specs=[whole(z)] + [_layer_spec(a, layer) for a in (w1p, b1, w2, b2, w3, freq)]
        + [whole(t), whole(deltas)],
        out_specs=(taps, taps),
        out_shape=(jax.ShapeDtypeStruct((SEQ, width), BF16),
                   jax.ShapeDtypeStruct((SEQ, width), BF16)),
        compiler_params=_params("arbitrary"),
        name="hyena_filter_mlp",
    )(z, w1p, b1, w2, b2, w3, freq, t, deltas)
    cf, sf = tables[:2]
    small = pl.BlockSpec((FFT_RADIX, rows, rows), lambda j: (0, 0, 0))
    col = pl.BlockSpec((SEQ, COL_TILE), lambda j: (0, j))
    fam = pl.BlockSpec((FFT_RADIX, rows, COL_TILE), lambda j: (0, 0, j))
    fam_shape = jax.ShapeDtypeStruct((FFT_RADIX, rows, width), F32)
    return pl.pallas_call(
        _filter_dft_kernel,
        grid=(width // COL_TILE,),
        in_specs=[small, small, col, col],
        out_specs=(fam, fam),
        out_shape=(fam_shape, fam_shape),
        compiler_params=_params("arbitrary"),
        name="hyena_filter_dft",
    )(cf, sf, hs, hd)


def _hyena_kernel(hn_ref, wv_ref, w1_ref, w2_ref, sv_ref, s1_ref, s2_ref, cf_ref, sf_ref, ct_ref,
                  st_ref, kr0_ref, ki0_ref, kr1_ref, ki1_ref, bias_ref, o_ref,
                  slab_v_ref, slab_1_ref, slab_2_ref, slab_o_ref):
    radix = FFT_RADIX
    rows = SEQ // radix
    n_slab = COL_TILE // LANES
    zero_row = jnp.zeros((CONV_PAD, COL_TILE), F32)

    def project(w_ref, slab_ref):
        half = SEQ // 2
        for top in (0, half):
            u = jnp.dot(hn_ref[top:top + half, :], w_ref[...], preferred_element_type=F32)
            for j in range(n_slab):
                slab_ref[j, top:top + half, :] = u[:, j * LANES:(j + 1) * LANES]
        return [jnp.concatenate([slab_ref[j, pl.ds(r, rows, stride=radix), :] for j in range(n_slab)],
                                axis=1) for r in range(radix)]

    def short_conv(u, w_ref):
        n = u[0].shape[0]
        prev_wrap = pltpu.roll(jnp.concatenate([u[-1], zero_row], axis=0), 1, 0)[:n]
        next_wrap = pltpu.roll(jnp.concatenate([zero_row, u[0]], axis=0), n + CONV_PAD - 1, 0)[CONV_PAD:]
        prev = [prev_wrap] + u[:-1]
        nxt = u[1:] + [next_wrap]
        return [prev[r] * w_ref[0:1, :] + u[r] * w_ref[1:2, :] + nxt[r] * w_ref[2:3, :]
                for r in range(radix)]

    def cmul(ar, ai, br, bi):
        return ar * br - ai * bi, ar * bi + ai * br

    def transform(x):
        xb = [v.astype(BF16) for v in x]
        tr = [jnp.dot(cf_ref[r], xb[r], preferred_element_type=F32) for r in range(radix)]
        ti = [-jnp.dot(sf_ref[r], xb[r], preferred_element_type=F32) for r in range(radix)]
        return tr, ti

    def filter_and_invert(tr, ti, kr_ref, ki_ref):
        ar, ai = tr[0] + tr[2], ti[0] + ti[2]
        br, bi = tr[0] - tr[2], ti[0] - ti[2]
        cr, ci = tr[1] + tr[3], ti[1] + ti[3]
        dr, di = tr[1] - tr[3], ti[1] - ti[3]
        fam = [(ar + cr, ai + ci), (br + di, bi - dr), (br - di, -bi - dr), (ar - cr, ci - ai)]
        y = [cmul(fr, fi, kr_ref[f], ki_ref[f]) for f, (fr, fi) in enumerate(fam)]
        er, ei = y[0][0] + y[3][0], y[0][1] - y[3][1]
        fr, fi = y[0][0] - y[3][0], y[0][1] + y[3][1]
        gr, gi = y[1][0] + y[2][0], y[1][1] - y[2][1]
        hr, hi = y[1][0] - y[2][0], y[1][1] + y[2][1]
        p = [(er + gr, ei + gi), (fr - hi, fi + hr), (er - gr, ei - gi), (fr + hi, fi - hr)]
        out = []
        for r in range(radix):
            pr, pi_ = p[r]
            out.append(jnp.dot(ct_ref[r], pr.astype(BF16), preferred_element_type=F32)
                       - jnp.dot(st_ref[r], pi_.astype(BF16), preferred_element_type=F32))
        return out

    v = short_conv(project(wv_ref, slab_v_ref), sv_ref)
    spectrum = transform(v)
    x1 = short_conv(project(w1_ref, slab_1_ref), s1_ref)
    y = filter_and_invert(*spectrum, kr0_ref, ki0_ref)
    z = [x1[r] * (y[r] + v[r] * bias_ref[0:1, :]) for r in range(radix)]
    spectrum = transform(z)
    x2 = short_conv(project(w2_ref, slab_2_ref), s2_ref)
    y = filter_and_invert(*spectrum, kr1_ref, ki1_ref)
    for r in range(radix):
        out = x2[r] * (y[r] + z[r] * bias_ref[1:2, :])
        for j in range(n_slab):
            slab_o_ref[j, pl.ds(r, rows, stride=radix), :] = out[:, j * LANES:(j + 1) * LANES]
    o_ref[...] = jnp.concatenate([slab_o_ref[j] for j in range(n_slab)], axis=1).astype(BF16)


def _hyena_mixer(hn, w_hy, short_w, tables, kr, ki, bias, layer, tile):
    b = hn.shape[0]
    nt = HY_W // COL_TILE
    rows = SEQ // FFT_RADIX
    once = pl.Buffered(1)
    cf, sf, ct, st = tables

    def fixed(shape, index):
        return pl.BlockSpec(shape, lambda i: index, pipeline_mode=once)

    def wcol(k):
        return fixed((D_MODEL, COL_TILE), (0, k * nt + tile))

    def scol(k):
        return _layer_spec(short_w, layer, COL_TILE, k * nt + tile)

    def kfam(o):
        return fixed((FFT_RADIX, rows, COL_TILE), (0, 0, o * nt + tile))

    small = fixed((FFT_RADIX, rows, rows), (0, 0, 0))
    slab = pltpu.VMEM((COL_TILE // LANES, SEQ, LANES), F32)
    out = pl.pallas_call(
        _hyena_kernel,
        grid=(b,),
        in_specs=[pl.BlockSpec((None, SEQ, D_MODEL), lambda i: (i, 0, 0)),
                  wcol(0), wcol(1), wcol(2), scol(0), scol(1), scol(2),
                  small, small, small, small,
                  kfam(0), kfam(0), kfam(1), kfam(1),
                  _layer_spec(bias, layer, COL_TILE, tile)],
        out_specs=pl.BlockSpec((None, SEQ, COL_TILE), lambda i: (i, 0, 0)),
        out_shape=jax.ShapeDtypeStruct((b, SEQ, COL_TILE), BF16),
        scratch_shapes=[slab, slab, slab, slab],
        compiler_params=_params("arbitrary"),
        name="hyena_mixer",
    )(hn, w_hy, w_hy, w_hy, short_w, short_w, short_w, cf, sf, ct, st, kr, ki, kr, ki, bias)
    return out.reshape(b * SEQ, COL_TILE)


def _shortconv_kernel(hn_ref, wb_ref, wc_ref, wx_ref, cw_ref, o_ref):
    wb = wb_ref[...].astype(BF16)
    wc = wc_ref[...].astype(BF16)
    wx = wx_ref[...].astype(BF16)

    def project(r0):
        lo = max(r0 - CONV_HALO, 0)
        hi = min(r0 + CONV_CHUNK + CONV_HALO, SEQ)
        hn = hn_ref[lo:hi, :]
        return (r0, r0 - lo, jnp.dot(hn, wb, preferred_element_type=F32),
                jnp.dot(hn, wc, preferred_element_type=F32), jnp.dot(hn, wx, preferred_element_type=F32))

    def finish(r0, skip, bg, cg, xi):
        out = bg * _dwconv3(cg * xi, cw_ref)
        o_ref[r0:r0 + CONV_CHUNK, :] = out[skip:skip + CONV_CHUNK].astype(BF16)

    pending = None
    for r0 in range(0, SEQ, CONV_CHUNK):
        current = project(r0)
        if pending is not None:
            finish(*pending)
        pending = current
    finish(*pending)


def _w_in_cols(layer, width, first):
    return lambda k, nt: pl.BlockSpec((None, D_MODEL, width),
                                      lambda j, i: (layer, 0, first + k * nt + j))


def _shortconv_mixer(hn, w_in, layer, conv_w):
    b = hn.shape[0]
    nt = SC_W // COL_TILE
    wcol = _w_in_cols(layer, COL_TILE, (3 * HY_W + 3 * NA_W) // COL_TILE)
    return pl.pallas_call(
        _shortconv_kernel,
        grid=(nt, b),
        in_specs=[pl.BlockSpec((None, SEQ, D_MODEL), lambda j, i: (i, 0, 0)),
                  wcol(0, nt), wcol(1, nt), wcol(2, nt),
                  pl.BlockSpec((None, 3, COL_TILE), lambda j, i: (layer, 0, j))],
        out_specs=pl.BlockSpec((None, SEQ, COL_TILE), lambda j, i: (i, 0, j)),
        out_shape=jax.ShapeDtypeStruct((b, SEQ, SC_W), BF16),
        compiler_params=_params("arbitrary", "arbitrary"),
        name="shortconv_mixer",
    )(hn, w_in, w_in, w_in, conv_w)


def _na_kernel(hn_ref, wq_ref, wk_ref, wv_ref, bias_ref, o_ref, q_ref, k_ref, v_ref, s_ref):
    hn = hn_ref[...]
    q = jnp.dot(hn, wq_ref[...].astype(BF16), preferred_element_type=F32)
    q_ref[...] = (q * (NA_HEAD_DIM ** -0.5)).astype(BF16)
    k_ref[...] = jnp.dot(hn, wk_ref[...].astype(BF16), preferred_element_type=F32).astype(BF16)
    v_ref[...] = jnp.dot(hn, wv_ref[...].astype(BF16), preferred_element_type=F32).astype(BF16)
    gw = NA_GROUP * NA_HEAD_DIM
    same_head = (lax.broadcasted_iota(jnp.int32, (gw, gw), 0) // NA_HEAD_DIM
                 == lax.broadcasted_iota(jnp.int32, (gw, gw), 1) // NA_HEAD_DIM)

    def scores(r):
        w0 = jnp.clip(r - NA_WIN_ROWS // 2, 0, NA_ROWS - NA_WIN_ROWS)
        off = w0 - r + (NA_WIN_ROWS - 1)
        q0 = pl.multiple_of(r * GRID_W, GRID_W)
        k0 = pl.multiple_of(w0 * GRID_W, GRID_W)
        q_row = q_ref[pl.ds(q0, GRID_W), :]
        q_heads = jnp.where(same_head, jnp.concatenate([q_row] * NA_GROUP, axis=0), 0)
        s = lax.dot_general(q_heads, k_ref[pl.ds(k0, NA_KEYS), :], (((1,), (1,)), ((), ())),
                            preferred_element_type=F32)
        bias = jnp.concatenate(
            [jnp.concatenate([bias_ref[h, off + 2 * m] for m in range(NA_WIN_ROWS // 2)], axis=1)
             for h in range(NA_GROUP)], axis=0)
        return s + bias

    def attend(r, s):
        w0 = jnp.clip(r - NA_WIN_ROWS // 2, 0, NA_ROWS - NA_WIN_ROWS)
        q0 = pl.multiple_of(r * GRID_W, GRID_W)
        k0 = pl.multiple_of(w0 * GRID_W, GRID_W)
        p = jnp.exp(s - jnp.max(s, axis=-1, keepdims=True))
        inv = 1.0 / jnp.sum(p, axis=-1, keepdims=True)
        pv = jnp.dot(p.astype(BF16), v_ref[pl.ds(k0, NA_KEYS), :], preferred_element_type=F32)
        pv = jnp.where(same_head, pv * inv, 0.0)
        out = pv[0:GRID_W]
        for h in range(1, NA_GROUP):
            out = out + pv[h * GRID_W:(h + 1) * GRID_W]
        o_ref[pl.ds(q0, GRID_W), :] = out.astype(BF16)

    groups = NA_ROWS // NA_ROW_GROUP
    for t in range(NA_ROW_GROUP):
        s_ref[t] = scores(jnp.int32(t))

    def rows_body(i, carry):
        for t in range(NA_ROW_GROUP):
            s_next = scores(i * NA_ROW_GROUP + t)
            attend((i - 1) * NA_ROW_GROUP + t, s_ref[t])
            s_ref[t] = s_next
        return carry

    lax.fori_loop(1, groups, rows_body, 0)
    for t in range(NA_ROW_GROUP):
        attend(jnp.int32((groups - 1) * NA_ROW_GROUP + t), s_ref[t])


def _na_bias(rpb):
    c = jnp.arange(GRID_W)
    col_start = jnp.clip(c - NA_WIN_COLS // 2, 0, GRID_W - NA_WIN_COLS)
    col_mask = (c[None, :] >= col_start[:, None]) & (c[None, :] < col_start[:, None] + NA_WIN_COLS)
    dc = jnp.clip(c[None, :] - c[:, None] + NA_WIN_COLS - 1, 0, 2 * NA_WIN_COLS - 2)
    pick = (dc[None] == jnp.arange(2 * NA_WIN_COLS - 1)[:, None, None]).astype(F32)
    table = jnp.einsum("lhrd,dqc->lhrqc", rpb.astype(F32), pick, precision=lax.Precision.HIGHEST)
    table = table + jnp.where(col_mask, 0.0, -1e30)
    return jnp.concatenate([table[:, :, :-1], table[:, :, 1:]], axis=-1)


def _na_mixer(hn, w_in, layer, bias):
    b = hn.shape[0]
    gw = NA_GROUP * NA_HEAD_DIM
    ng = NA_W // gw
    wcol = _w_in_cols(layer, gw, 3 * HY_W // gw)
    return pl.pallas_call(
        _na_kernel,
        grid=(ng, b),
        in_specs=[pl.BlockSpec((None, SEQ, D_MODEL), lambda j, i: (i, 0, 0)),
                  wcol(0, ng), wcol(1, ng), wcol(2, ng),
                  pl.BlockSpec((None, NA_GROUP, 2 * NA_WIN_ROWS - 2, GRID_W, 2 * GRID_W),
                               lambda j, i: (layer, j, 0, 0, 0))],
        out_specs=pl.BlockSpec((None, SEQ, gw), lambda j, i: (i, 0, j)),
        out_shape=jax.ShapeDtypeStruct((b, SEQ, NA_W), BF16),
        scratch_shapes=[pltpu.VMEM((SEQ, gw), BF16) for _ in range(3)]
        + [pltpu.VMEM((NA_ROW_GROUP, gw, NA_KEYS), F32)],
        compiler_params=_params("arbitrary", "arbitrary"),
        name="na_mixer",
    )(hn, w_in, w_in, w_in, bias)


def _merge_kernel(hn_ref, x_ref, ya0_ref, ya1_ref, yb_ref, yc_ref, wg_ref, gb_ref, wb_ref, wo_ref,
                  g_ref, o_ref):
    def gated_sum(rows):
        hn = hn_ref[rows, :]
        ya = jnp.concatenate([ya0_ref[rows, :], ya1_ref[rows, :]], axis=1)
        merged = None
        for i, y in enumerate((ya, yb_ref[rows, :], yc_ref[rows, :])):
            pre = jnp.dot(hn, wg_ref[:, i * D_MODEL:(i + 1) * D_MODEL], preferred_element_type=F32)
            gate = jax.nn.sigmoid(pre + gb_ref[i:i + 1, :])
            term = gate * jnp.dot(y, wb_ref[i], preferred_element_type=F32)
            merged = term if merged is None else merged + term
        return merged.astype(BF16)

    chunks = [slice(r, r + ROW_TILE) for r in range(0, MERGE_ROWS, ROW_TILE)]
    merged = [gated_sum(rows) for rows in chunks]
    for rows, m in zip(chunks, merged):
        out = jnp.dot(m, wo_ref[...], preferred_element_type=F32)
        o_ref[rows, :] = x_ref[rows, :] + _rms(out, g_ref[...])


def _merge(hn2d, x2d, ya0, ya1, yb, yc, w_gate, gate_bias, w_branch, w_out, gains, layer):
    n = x2d.shape[0]
    tm = MERGE_ROWS
    once = pl.Buffered(1)
    rows = lambda w: pl.BlockSpec((tm, w), lambda i: (i, 0))
    return pl.pallas_call(
        _merge_kernel,
        grid=(n // tm,),
        in_specs=[rows(D_MODEL), rows(D_MODEL), rows(COL_TILE), rows(COL_TILE), rows(NA_W), rows(SC_W),
                  pl.BlockSpec((D_MODEL, N_BRANCH * D_MODEL), lambda i: (0, 0), pipeline_mode=once),
                  _layer_spec(gate_bias, layer),
                  pl.BlockSpec((N_BRANCH, HY_W, D_MODEL), lambda i: (0, 0, 0), pipeline_mode=once),
                  pl.BlockSpec((D_MODEL, D_MODEL), lambda i: (0, 0), pipeline_mode=once),
                  _gain_spec(6 * layer + 1)],
        out_specs=rows(D_MODEL),
        out_shape=jax.ShapeDtypeStruct((n, D_MODEL), F32),
        compiler_params=_params("arbitrary"),
        name="merge",
    )(hn2d, x2d, ya0, ya1, yb, yc, w_gate, gate_bias, w_branch, w_out, gains)


def _kv_kernel(m_ref, g_ref, w_ref, o_ref):
    mn = _rms(m_ref[...], g_ref[...]).astype(BF16)
    o_ref[...] = jnp.dot(mn, w_ref[...], preferred_element_type=F32).astype(BF16)


def _mem_kv(mem, mem_norm, layer, wkv):
    b = mem.shape[0]
    return pl.pallas_call(
        _kv_kernel,
        grid=(b,),
        in_specs=[pl.BlockSpec((None, N_MEM, D_MODEL), lambda i: (i, 0, 0)),
                  _layer_spec(mem_norm, layer),
                  pl.BlockSpec((D_MODEL, 2 * D_MODEL), lambda i: (0, 0))],
        out_specs=pl.BlockSpec((None, N_MEM, 2 * D_MODEL), lambda i: (i, 0, 0)),
        out_shape=jax.ShapeDtypeStruct((b, N_MEM, 2 * D_MODEL), BF16),
        compiler_params=_params("arbitrary"),
        name="mem_kv",
    )(mem, mem_norm, wkv)


def _xattn_kernel(x_ref, kv_ref, wq_ref, wo_ref, gq_ref, go_ref, gn_ref, o_ref, hn_ref):
    chunks = [slice(r, r + XA_CHUNK) for r in range(0, XA_ROWS, XA_CHUNK)]
    head_cols = [slice(i * XA_HEAD_DIM, (i + 1) * XA_HEAD_DIM) for i in range(XA_HEADS)]

    def query(rows):
        h = _rms(x_ref[rows, :], gq_ref[...]).astype(BF16)
        q = jnp.dot(h, wq_ref[...], preferred_element_type=F32) * (XA_HEAD_DIM ** -0.5)
        return q.astype(BF16)

    def scores(q):
        return [lax.dot_general(q[:, sl], kv_ref[:, sl], (((1,), (1,)), ((), ())),
                                preferred_element_type=F32) for sl in head_cols]

    def values(s_heads):
        heads = []
        for i, s in enumerate(s_heads):
            vm = kv_ref[:, D_MODEL + i * XA_HEAD_DIM:D_MODEL + (i + 1) * XA_HEAD_DIM]
            p = jnp.exp(s - jnp.max(s, axis=-1, keepdims=True))
            den = jnp.sum(p, axis=-1, keepdims=True)
            heads.append((jnp.dot(p.astype(BF16), vm, preferred_element_type=F32) / den).astype(BF16))
        return jnp.concatenate(heads, axis=-1)

    s_all = [scores(q) for q in [query(rows) for rows in chunks]]
    attended = [values(s) for s in s_all]
    for rows, a in zip(chunks, attended):
        o = jnp.dot(a, wo_ref[...], preferred_element_type=F32)
        xn = x_ref[rows, :] + _rms(o, go_ref[...])
        o_ref[rows, :] = xn
        hn_ref[rows, :] = _rms(xn, gn_ref[...]).astype(BF16)


def _xattn(x, kv, wq, wo, gains, layer):
    b = x.shape[0]
    tm = XA_ROWS
    once = pl.Buffered(1)
    rows = pl.BlockSpec((None, tm, D_MODEL), lambda i, j: (i, j, 0))
    wfull = pl.BlockSpec((D_MODEL, D_MODEL), lambda i, j: (0, 0), pipeline_mode=once)
    return pl.pallas_call(
        _xattn_kernel,
        grid=(b, SEQ // tm),
        in_specs=[rows, pl.BlockSpec((None, N_MEM, 2 * D_MODEL), lambda i, j: (i, 0, 0)),
                  wfull, wfull, _gain_spec(6 * layer + 2), _gain_spec(6 * layer + 3),
                  _gain_spec(6 * layer + 4)],
        out_specs=(rows, rows),
        out_shape=(jax.ShapeDtypeStruct((b, SEQ, D_MODEL), F32),
                   jax.ShapeDtypeStruct((b, SEQ, D_MODEL), BF16)),
        compiler_params=_params("arbitrary", "arbitrary"),
        name="xattn",
    )(x, kv, wq, wo, gains, gains, gains)


def _gelu_tanh(x):
    c = math.sqrt(2.0 / math.pi)
    half = 0.5 * x
    return half + half * jnp.tanh(x * (c + (c * 0.044715) * (x * x)))


def _ffn_kernel(hn_ref, wu_ref, cw_ref, wd_ref, o_ref):
    def up(r0, c0, c1):
        lo = max(r0 - CONV_HALO, 0)
        hi = min(r0 + CONV_CHUNK + CONV_HALO, SEQ)
        hn = hn_ref[lo:hi, :]
        ug = jnp.dot(hn, wu_ref[:, c0:c1], preferred_element_type=F32)
        uv = jnp.dot(hn, wu_ref[:, D_FF + c0:D_FF + c1], preferred_element_type=F32)
        return r0, r0 - lo, c0, c1, ug, uv

    def down(acc, r0, skip, c0, c1, ug, uv):
        act = (_gelu_tanh(_dwconv3(ug, cw_ref.at[:, c0:c1]))
               * _dwconv3(uv, cw_ref.at[:, D_FF + c0:D_FF + c1]))
        act = act[skip:skip + CONV_CHUNK].astype(BF16)
        part = jnp.dot(act, wd_ref[c0:c1, :], preferred_element_type=F32)
        acc = part if c0 == 0 else acc + part
        if c1 == D_FF:
            o_ref[r0:r0 + CONV_CHUNK, :] = acc.astype(o_ref.dtype)
        return acc

    pending, acc = None, None
    for r0 in range(0, SEQ, CONV_CHUNK):
        for c0, c1 in zip(FFN_CUTS[:-1], FFN_CUTS[1:]):
            current = up(r0, c0, c1)
            if pending is not None:
                acc = down(acc, *pending)
            pending = current
    down(acc, *pending)


def _ffn(hn, w_up, w_conv, w_down, layer):
    b = hn.shape[0]
    once = pl.Buffered(1)
    return pl.pallas_call(
        _ffn_kernel,
        grid=(b,),
        in_specs=[pl.BlockSpec((None, SEQ, D_MODEL), lambda i: (i, 0, 0)),
                  pl.BlockSpec((D_MODEL, 2 * D_FF), lambda i: (0, 0), pipeline_mode=once),
                  _layer_spec(w_conv, layer),
                  pl.BlockSpec((D_FF, D_MODEL), lambda i: (0, 0), pipeline_mode=once)],
        out_specs=pl.BlockSpec((None, SEQ, D_MODEL), lambda i: (i, 0, 0)),
        out_shape=jax.ShapeDtypeStruct((b, SEQ, D_MODEL), BF16),
        compiler_params=_params("arbitrary"),
        name="ffn",
    )(hn, w_up, w_conv, w_down)


def _residual_kernel(x_ref, f_ref, g_ref, gn_ref, o_ref, hn_ref):
    xn = x_ref[...] + _rms(f_ref[...].astype(F32), g_ref[...])
    o_ref[...] = xn
    hn_ref[...] = _rms(xn, gn_ref[...]).astype(BF16)


def _residual_last_kernel(x_ref, f_ref, g_ref, o_ref):
    o_ref[...] = x_ref[...] + _rms(f_ref[...].astype(F32), g_ref[...])


def _residual(x2d, f2d, gains, layer, last):
    n = x2d.shape[0]
    tm = 1024
    rows = pl.BlockSpec((tm, D_MODEL), lambda i: (i, 0))
    x_shape = jax.ShapeDtypeStruct((n, D_MODEL), F32)
    if last:
        return pl.pallas_call(
            _residual_last_kernel,
            grid=(n // tm,),
            in_specs=[rows, rows, _gain_spec(6 * layer + 5)],
            out_specs=rows,
            out_shape=x_shape,
            compiler_params=_params("arbitrary"),
            name="residual_last",
        )(x2d, f2d, gains), None
    return pl.pallas_call(
        _residual_kernel,
        grid=(n // tm,),
        in_specs=[rows, rows, _gain_spec(6 * layer + 5), _gain_spec(6 * (layer + 1))],
        out_specs=(rows, rows),
        out_shape=(x_shape, jax.ShapeDtypeStruct((n, D_MODEL), BF16)),
        compiler_params=_params("arbitrary"),
        name="residual",
    )(x2d, f2d, gains, gains)


def _angle_tables(num, den):
    ang = (num % den).astype(F32) * (2.0 * math.pi / den)
    return jnp.cos(ang), jnp.sin(ang)


def _dft_tables():
    rows = SEQ // FFT_RADIX
    kappa = jnp.arange(rows, dtype=jnp.int32)[None, :, None]
    m = jnp.arange(rows, dtype=jnp.int32)[None, None, :]
    r = jnp.arange(FFT_RADIX, dtype=jnp.int32)[:, None, None]
    c0, s0 = _angle_tables((2 * kappa + 1) * m, 2 * FFT_SUB)
    cr, sr = _angle_tables((2 * kappa + 1) * r, 2 * FFT_N)
    cf = (c0 * cr - s0 * sr).astype(BF16)
    sf = (s0 * cr + c0 * sr).astype(BF16)
    return cf, sf, cf.transpose(0, 2, 1), sf.transpose(0, 2, 1)


def kernel(x, mem, norm_gains, mem_norm, w_in, gate_bias, hy_short_w, hy_w1, hy_b1, hy_w2, hy_b2,
           hy_w3, hy_freq, hy_bias, na_rpb, sc_conv_w, w_branch, w_out, xa_wq, xa_wkv, xa_wo,
           ffn_up, ffn_conv, ffn_down):
    b, l, d = x.shape
    depth = w_in.shape[0]
    assert (l, d) == (SEQ, D_MODEL) and mem.shape[1:] == (N_MEM, D_MODEL)
    n = b * l
    conv_tables = _dft_tables()
    gains = norm_gains.astype(F32).reshape(depth * 6, 1, d)
    mem_gain = mem_norm.astype(F32).reshape(depth, 1, d)
    w1p = jnp.pad(hy_w1.astype(F32), ((0, 0), (0, HY_HIDDEN - HY_EMB), (0, 0)))
    b1 = hy_b1.astype(F32).reshape(depth, 1, HY_HIDDEN)
    b2 = hy_b2.astype(F32).reshape(depth, 1, HY_HIDDEN)
    na_bias = _na_bias(na_rpb)
    w_branch2d = w_branch.reshape(depth, N_BRANCH * HY_W, d)
    x2d = x.reshape(n, d)
    hn = _prenorm(x2d, gains, 0)
    for i in range(depth):
        w_hyena, w_gate, w_br, w_o, wq, wkv, wo, w_up, w_down = _cast_layer(
            i, (w_in, 0, 3 * HY_W), (w_in, 3 * HY_W + 3 * NA_W + 3 * SC_W, N_BRANCH * D_MODEL),
            (w_branch2d, 0, d), (w_out, 0, d), (xa_wq, 0, d), (xa_wkv, 0, 2 * d), (xa_wo, 0, d),
            (ffn_up, 0, 2 * D_FF), (ffn_down, 0, d))
        kr, ki = _hyena_filters(conv_tables, w1p, b1, hy_w2, b2, hy_w3, hy_freq, i)
        hn3 = hn.reshape(b, l, d)
        ya = [_hyena_mixer(hn3, w_hyena, hy_short_w, conv_tables, kr, ki, hy_bias, i, tile)
              for tile in range(HY_W // COL_TILE)]
        yb = _na_mixer(hn3, w_in, i, na_bias)
        yc = _shortconv_mixer(hn3, w_in, i, sc_conv_w)
        x2d = _merge(hn, x2d, ya[0], ya[1], yb.reshape(n, NA_W), yc.reshape(n, SC_W), w_gate, gate_bias,
                     w_br.reshape(N_BRANCH, HY_W, d), w_o, gains, i)
        kv = _mem_kv(mem, mem_gain, i, wkv)
        x3, hn2 = _xattn(x2d.reshape(b, l, d), kv, wq, wo, gains, i)
        f = _ffn(hn2, w_up, ffn_conv, w_down, i)
        x2d, hn = _residual(x3.reshape(n, d), f.reshape(n, d), gains, i, i + 1 == depth)
    return x2d.reshape(b, l, d)
```

```python
import functools
import math

import jax
import jax.numpy as jnp
from jax import lax
from jax.experimental import pallas as pl
from jax.experimental.pallas import tpu as pltpu

D_MODEL = 1024
SEQ = 2048
N_MEM = 256
GRID_W = 64
HY_W = 512
NA_HEADS = 8
NA_HEAD_DIM = 64
NA_W = NA_HEADS * NA_HEAD_DIM
NA_WIN_ROWS = 8
NA_WIN_COLS = 16
SC_W = 512
XA_HEADS = 4
XA_HEAD_DIM = D_MODEL // XA_HEADS
D_FF = 2816
HY_ORDER = 2
HY_EMB = 33
HY_HIDDEN = 64
HY_FAST_DECAY = 0.3
HY_SLOW_DECAY = 1.5
HY_TARGET = 1e-2
N_BRANCH = 3
EPS = 1e-6

FFT_N = 2 * SEQ
FFT_RADIX = 4
FFT_SUB = FFT_N // FFT_RADIX
NA_ROWS = SEQ // GRID_W
NA_KEYS = NA_WIN_ROWS * GRID_W
NA_GROUP = 4
NA_ROW_GROUP = 8
LANES = 128
COL_TILE = 256
ROW_TILE = 512
MERGE_ROWS = 2 * ROW_TILE
XA_ROWS = 2 * ROW_TILE
XA_CHUNK = 256
CAST_STEPS = 8
CONV_PAD = 8
CONV_CHUNK = 512
FFN_CUTS = (0, 6 * COL_TILE, D_FF)
CONV_HALO = 16
VMEM_LIMIT = 60 * 1024 * 1024

BF16 = jnp.bfloat16
F32 = jnp.float32


def _params(*sem):
    return pltpu.CompilerParams(dimension_semantics=sem, vmem_limit_bytes=VMEM_LIMIT)


def _gain_spec(index):
    return pl.BlockSpec((None, 1, D_MODEL), lambda *_: (index, 0, 0))


def _layer_spec(arr, layer, width=None, col=0):
    _, r, c = arr.shape
    return pl.BlockSpec((None, r, c if width is None else width), lambda *_: (layer, 0, col))


def _rms(xf, g):
    ms = jnp.mean(xf * xf, axis=-1, keepdims=True)
    return xf * lax.rsqrt(ms + EPS) * g


def _dwconv3(u, w_ref):
    n = u.shape[0]
    zeros = jnp.zeros((CONV_PAD, u.shape[1]), F32)
    padded = jnp.concatenate([zeros, u, zeros], axis=0)
    m = n + 2 * CONV_PAD
    prev = pltpu.roll(padded, 1, 0)[CONV_PAD:CONV_PAD + n]
    nxt = pltpu.roll(padded, m - 1, 0)[CONV_PAD:CONV_PAD + n]
    return prev * w_ref[0:1, :] + u * w_ref[1:2, :] + nxt * w_ref[2:3, :]


def _prenorm_kernel(x_ref, g_ref, o_ref):
    o_ref[...] = _rms(x_ref[...], g_ref[...]).astype(BF16)


def _prenorm(x2d, gains, gi):
    n = x2d.shape[0]
    tm = 1024
    return pl.pallas_call(
        _prenorm_kernel,
        grid=(n // tm,),
        in_specs=[pl.BlockSpec((tm, D_MODEL), lambda i: (i, 0)),
                  _gain_spec(gi)],
        out_specs=pl.BlockSpec((tm, D_MODEL), lambda i: (i, 0)),
        out_shape=jax.ShapeDtypeStruct((n, D_MODEL), BF16),
        compiler_params=_params("arbitrary"),
        name="prenorm",
    )(x2d, gains)


def _cast_kernel(parts, *refs):
    n_in = sum(parts)
    w_refs, o_refs = refs[:n_in], refs[n_in:]
    k = 0
    for o_ref, n in zip(o_refs, parts):
        width = o_ref.shape[1] // n
        for p in range(n):
            o_ref[:, p * width:(p + 1) * width] = w_refs[k][...].astype(BF16)
            k += 1


def _cast_layer(layer, *weights):
    in_specs, out_specs, out_shapes, operands, parts = [], [], [], [], []
    for w, col0, ncols in weights:
        rb = w.shape[1] // CAST_STEPS
        cb = math.gcd(col0, ncols) if col0 else ncols
        assert w.shape[1] % CAST_STEPS == 0 and rb % 16 == 0 and cb % LANES == 0
        parts.append(ncols // cb)
        for p in range(ncols // cb):
            in_specs.append(pl.BlockSpec((None, rb, cb),
                                         lambda i, cblk=col0 // cb + p: (layer, i, cblk)))
            operands.append(w)
        out_specs.append(pl.BlockSpec((rb, ncols), lambda i: (i, 0)))
        out_shapes.append(jax.ShapeDtypeStruct((w.shape[1], ncols), BF16))
    return pl.pallas_call(
        functools.partial(_cast_kernel, tuple(parts)),
        grid=(CAST_STEPS,),
        in_specs=in_specs,
        out_specs=tuple(out_specs),
        out_shape=tuple(out_shapes),
        compiler_params=_params("arbitrary"),
        name="cast_bf16",
    )(*operands)


def _filter_mlp_kernel(z_ref, w1_ref, b1_ref, w2_ref, b2_ref, w3_ref, f_ref, t_ref, dl_ref,
                       hs_ref, hd_ref):
    hp = lax.Precision.HIGHEST
    h = jnp.sin(f_ref[0:1, :] * (jnp.dot(z_ref[...], w1_ref[...], precision=hp) + b1_ref[...]))
    h = jnp.sin(f_ref[1:2, :] * (jnp.dot(h, w2_ref[...], precision=hp) + b2_ref[...]))
    decay = jnp.exp(-t_ref[...] * dl_ref[...])
    row = lax.broadcasted_iota(jnp.int32, (SEQ, HY_W), 0)
    h_hi = h.astype(BF16)
    h_lo = (h - h_hi.astype(F32)).astype(BF16)

    def out_layer(cols):
        w = w3_ref[:, cols]
        w_hi = w.astype(BF16)
        w_lo = (w - w_hi.astype(F32)).astype(BF16)
        return (jnp.dot(h_hi, w_hi, preferred_element_type=F32)
                + (jnp.dot(h_hi, w_lo, preferred_element_type=F32)
                   + jnp.dot(h_lo, w_hi, preferred_element_type=F32)))

    for o in range(HY_ORDER):
        c_f = o * HY_W
        c_b = HY_ORDER * HY_W + o * HY_W
        hf = out_layer(slice(c_f, c_f + HY_W)) * decay
        hb = out_layer(slice(c_b, c_b + HY_W)) * decay
        hb = jnp.where(row == 0, 0.0, hb)
        hs_ref[:, c_f:c_f + HY_W] = (hf + hb).astype(BF16)
        hd_ref[:, c_f:c_f + HY_W] = (hb - hf).astype(BF16)


def _filter_dft_kernel(cf_ref, sf_ref, hs_ref, hd_ref, kr_ref, ki_ref):
    rows = SEQ // FFT_RADIX
    scale = 2.0 / FFT_N
    sub = lambda ref, r: ref[r * rows:(r + 1) * rows, :]
    cos_dot = lambda ref, r: jnp.dot(cf_ref[r], sub(ref, r), preferred_element_type=F32)
    sin_dot = lambda ref, r: jnp.dot(sf_ref[r], sub(ref, r), preferred_element_type=F32)
    c = [cos_dot(hs_ref, r) for r in range(FFT_RADIX)]
    a, b, e = c[0] + c[2], c[0] - c[2], c[1] + c[3]
    d_im = sin_dot(hs_ref, 3) - sin_dot(hs_ref, 1)
    for f, val in enumerate((a + e, b + d_im, b - d_im, a - e)):
        kr_ref[f] = val * scale
    s = [sin_dot(hd_ref, r) for r in range(FFT_RADIX)]
    a, b, e = s[0] + s[2], s[0] - s[2], s[1] + s[3]
    d_re = cos_dot(hd_ref, 1) - cos_dot(hd_ref, 3)
    for f, val in enumerate((a + e, b + d_re, d_re - b, e - a)):
        ki_ref[f] = val * scale


def _hyena_filters(tables, w1p, b1, w2, b2, w3, freq, layer):
    t = jnp.linspace(0.0, 1.0, SEQ, dtype=F32)[:, None]
    bands = (HY_EMB - 1) // 2
    w = 2.0 * math.pi * jnp.arange(SEQ, dtype=F32)[:, None] / SEQ
    f = jnp.linspace(1e-4, bands - 1, bands, dtype=F32)[None, :]
    z = jnp.concatenate([t, jnp.cos(f * w), -jnp.sin(f * w)], axis=-1)
    z = jnp.pad(z, ((0, 0), (0, HY_HIDDEN - HY_EMB)))
    rows = SEQ // FFT_RADIX
    decimate = lambda a: a.reshape(rows, FFT_RADIX, -1).transpose(1, 0, 2).reshape(SEQ, -1)
    z, t = decimate(z), decimate(t)
    deltas = jnp.abs(jnp.linspace(math.log(HY_TARGET) / HY_SLOW_DECAY,
                                  math.log(HY_TARGET) / HY_FAST_DECAY, HY_W, dtype=F32))[None, :]
    width = HY_ORDER * HY_W
    whole = lambda a: pl.BlockSpec(a.shape, lambda i: (0,) * a.ndim)
    taps = pl.BlockSpec((SEQ, width), lambda i: (0, 0))
    hs, hd = pl.pallas_call(
        _filter_mlp_kernel,
        grid=(1,),
        in_specs=[whole(z)] + [_layer_spec(a, layer) for a in (w1p, b1, w2, b2, w3, freq)]
        + [whole(t), whole(deltas)],
        out_specs=(taps, taps),
        out_shape=(jax.ShapeDtypeStruct((SEQ, width), BF16),
                   jax.ShapeDtypeStruct((SEQ, width), BF16)),
        compiler_params=_params("arbitrary"),
        name="hyena_filter_mlp",
    )(z, w1p, b1, w2, b2, w3, freq, t, deltas)
    cf, sf = tables[:2]
    small = pl.BlockSpec((FFT_RADIX, rows, rows), lambda j: (0, 0, 0))
    col = pl.BlockSpec((SEQ, COL_TILE), lambda j: (0, j))
    fam = pl.BlockSpec((FFT_RADIX, rows, COL_TILE), lambda j: (0, 0, j))
    fam_shape = jax.ShapeDtypeStruct((FFT_RADIX, rows, width), F32)
    return pl.pallas_call(
        _filter_dft_kernel,
        grid=(width // COL_TILE,),
        in_specs=[small, small, col, col],
        out_specs=(fam, fam),
        out_shape=(fam_shape, fam_shape),
        compiler_params=_params("arbitrary"),
        name="hyena_filter_dft",
    )(cf, sf, hs, hd)


def _hyena_kernel(hn_ref, wv_ref, w1_ref, w2_ref, sv_ref, s1_ref, s2_ref, cf_ref, sf_ref, ct_ref,
                  st_ref, kr0_ref, ki0_ref, kr1_ref, ki1_ref, bias_ref, o_ref,
                  slab_v_ref, slab_1_ref, slab_2_ref, slab_o_ref):
    radix = FFT_RADIX
    rows = SEQ // radix
    n_slab = COL_TILE // LANES
    zero_row = jnp.zeros((CONV_PAD, COL_TILE), F32)

    def project(w_ref, slab_ref):
        half = SEQ // 2
        for top in (0, half):
            u = jnp.dot(hn_ref[top:top + half, :], w_ref[...], preferred_element_type=F32)
            for j in range(n_slab):
                slab_ref[j, top:top + half, :] = u[:, j * LANES:(j + 1) * LANES]
        return [jnp.concatenate([slab_ref[j, pl.ds(r, rows, stride=radix), :] for j in range(n_slab)],
                                axis=1) for r in range(radix)]

    def short_conv(u, w_ref):
        n = u[0].shape[0]
        prev_wrap = pltpu.roll(jnp.concatenate([u[-1], zero_row], axis=0), 1, 0)[:n]
        next_wrap = pltpu.roll(jnp.concatenate([zero_row, u[0]], axis=0), n + CONV_PAD - 1, 0)[CONV_PAD:]
        prev = [prev_wrap] + u[:-1]
        nxt = u[1:] + [next_wrap]
        return [prev[r] * w_ref[0:1, :] + u[r] * w_ref[1:2, :] + nxt[r] * w_ref[2:3, :]
                for r in range(radix)]

    def cmul(ar, ai, br, bi):
        return ar * br - ai * bi, ar * bi + ai * br

    def transform(x):
        xb = [v.astype(BF16) for v in x]
        tr = [jnp.dot(cf_ref[r], xb[r], preferred_element_type=F32) for r in range(radix)]
        ti = [-jnp.dot(sf_ref[r], xb[r], preferred_element_type=F32) for r in range(radix)]
        return tr, ti

    def filter_and_invert(tr, ti, kr_ref, ki_ref):
        ar, ai = tr[0] + tr[2], ti[0] + ti[2]
        br, bi = tr[0] - tr[2], ti[0] - ti[2]
        cr, ci = tr[1] + tr[3], ti[1] + ti[3]
        dr, di = tr[1] - tr[3], ti[1] - ti[3]
        fam = [(ar + cr, ai + ci), (br + di, bi - dr), (br - di, -bi - dr), (ar - cr, ci - ai)]
        y = [cmul(fr, fi, kr_ref[f], ki_ref[f]) for f, (fr, fi) in enumerate(fam)]
        er, ei = y[0][0] + y[3][0], y[0][1] - y[3][1]
        fr, fi = y[0][0] - y[3][0], y[0][1] + y[3][1]
        gr, gi = y[1][0] + y[2][0], y[1][1] - y[2][1]
        hr, hi = y[1][0] - y[2][0], y[1][1] + y[2][1]
        p = [(er + gr, ei + gi), (fr - hi, fi + hr), (er - gr, ei - gi), (fr + hi, fi - hr)]
        out = []
        for r in range(radix):
            pr, pi_ = p[r]
            out.append(jnp.dot(ct_ref[r], pr.astype(BF16), preferred_element_type=F32)
                       - jnp.dot(st_ref[r], pi_.astype(BF16), preferred_element_type=F32))
        return out

    v = short_conv(project(wv_ref, slab_v_ref), sv_ref)
    spectrum = transform(v)
    x1 = short_conv(project(w1_ref, slab_1_ref), s1_ref)
    y = filter_and_invert(*spectrum, kr0_ref, ki0_ref)
    z = [x1[r] * (y[r] + v[r] * bias_ref[0:1, :]) for r in range(radix)]
    spectrum = transform(z)
    x2 = short_conv(project(w2_ref, slab_2_ref), s2_ref)
    y = filter_and_invert(*spectrum, kr1_ref, ki1_ref)
    for r in range(radix):
        out = x2[r] * (y[r] + z[r] * bias_ref[1:2, :])
        for j in range(n_slab):
            slab_o_ref[j, pl.ds(r, rows, stride=radix), :] = out[:, j * LANES:(j + 1) * LANES]
    o_ref[...] = jnp.concatenate([slab_o_ref[j] for j in range(n_slab)], axis=1).astype(BF16)


def _hyena_mixer(hn, w_hy, short_w, tables, kr, ki, bias, layer, tile):
    b = hn.shape[0]
    nt = HY_W // COL_TILE
    rows = SEQ // FFT_RADIX
    once = pl.Buffered(1)
    cf, sf, ct, st = tables

    def fixed(shape, index):
        return pl.BlockSpec(shape, lambda i: index, pipeline_mode=once)

    def wcol(k):
        return fixed((D_MODEL, COL_TILE), (0, k * nt + tile))

    def scol(k):
        return _layer_spec(short_w, layer, COL_TILE, k * nt + tile)

    def kfam(o):
        return fixed((FFT_RADIX, rows, COL_TILE), (0, 0, o * nt + tile))

    small = fixed((FFT_RADIX, rows, rows), (0, 0, 0))
    slab = pltpu.VMEM((COL_TILE // LANES, SEQ, LANES), F32)
    out = pl.pallas_call(
        _hyena_kernel,
        grid=(b,),
        in_specs=[pl.BlockSpec((None, SEQ, D_MODEL), lambda i: (i, 0, 0)),
                  wcol(0), wcol(1), wcol(2), scol(0), scol(1), scol(2),
                  small, small, small, small,
                  kfam(0), kfam(0), kfam(1), kfam(1),
                  _layer_spec(bias, layer, COL_TILE, tile)],
        out_specs=pl.BlockSpec((None, SEQ, COL_TILE), lambda i: (i, 0, 0)),
        out_shape=jax.ShapeDtypeStruct((b, SEQ, COL_TILE), BF16),
        scratch_shapes=[slab, slab, slab, slab],
        compiler_params=_params("arbitrary"),
        name="hyena_mixer",
    )(hn, w_hy, w_hy, w_hy, short_w, short_w, short_w, cf, sf, ct, st, kr, ki, kr, ki, bias)
    return out.reshape(b * SEQ, COL_TILE)


def _shortconv_kernel(hn_ref, wb_ref, wc_ref, wx_ref, cw_ref, o_ref):
    wb = wb_ref[...].astype(BF16)
    wc = wc_ref[...].astype(BF16)
    wx = wx_ref[...].astype(BF16)

    def project(r0):
        lo = max(r0 - CONV_HALO, 0)
        hi = min(r0 + CONV_CHUNK + CONV_HALO, SEQ)
        hn = hn_ref[lo:hi, :]
        return (r0, r0 - lo, jnp.dot(hn, wb, preferred_element_type=F32),
                jnp.dot(hn, wc, preferred_element_type=F32), jnp.dot(hn, wx, preferred_element_type=F32))

    def finish(r0, skip, bg, cg, xi):
        out = bg * _dwconv3(cg * xi, cw_ref)
        o_ref[r0:r0 + CONV_CHUNK, :] = out[skip:skip + CONV_CHUNK].astype(BF16)

    pending = None
    for r0 in range(0, SEQ, CONV_CHUNK):
        current = project(r0)
        if pending is not None:
            finish(*pending)
        pending = current
    finish(*pending)


def _w_in_cols(layer, width, first):
    return lambda k, nt: pl.BlockSpec((None, D_MODEL, width),
                                      lambda j, i: (layer, 0, first + k * nt + j))


def _shortconv_mixer(hn, w_in, layer, conv_w):
    b = hn.shape[0]
    nt = SC_W // COL_TILE
    wcol = _w_in_cols(layer, COL_TILE, (3 * HY_W + 3 * NA_W) // COL_TILE)
    return pl.pallas_call(
        _shortconv_kernel,
        grid=(nt, b),
        in_specs=[pl.BlockSpec((None, SEQ, D_MODEL), lambda j, i: (i, 0, 0)),
                  wcol(0, nt), wcol(1, nt), wcol(2, nt),
                  pl.BlockSpec((None, 3, COL_TILE), lambda j, i: (layer, 0, j))],
        out_specs=pl.BlockSpec((None, SEQ, COL_TILE), lambda j, i: (i, 0, j)),
        out_shape=jax.ShapeDtypeStruct((b, SEQ, SC_W), BF16),
        compiler_params=_params("arbitrary", "arbitrary"),
        name="shortconv_mixer",
    )(hn, w_in, w_in, w_in, conv_w)


def _na_kernel(hn_ref, wq_ref, wk_ref, wv_ref, bias_ref, o_ref, q_ref, k_ref, v_ref, s_ref):
    hn = hn_ref[...]
    q = jnp.dot(hn, wq_ref[...].astype(BF16), preferred_element_type=F32)
    q_ref[...] = (q * (NA_HEAD_DIM ** -0.5)).astype(BF16)
    k_ref[...] = jnp.dot(hn, wk_ref[...].astype(BF16), preferred_element_type=F32).astype(BF16)
    v_ref[...] = jnp.dot(hn, wv_ref[...].astype(BF16), preferred_element_type=F32).astype(BF16)
    gw = NA_GROUP * NA_HEAD_DIM
    same_head = (lax.broadcasted_iota(jnp.int32, (gw, gw), 0) // NA_HEAD_DIM
                 == lax.broadcasted_iota(jnp.int32, (gw, gw), 1) // NA_HEAD_DIM)

    def scores(r):
        w0 = jnp.clip(r - NA_WIN_ROWS // 2, 0, NA_ROWS - NA_WIN_ROWS)
        off = w0 - r + (NA_WIN_ROWS - 1)
        q0 = pl.multiple_of(r * GRID_W, GRID_W)
        k0 = pl.multiple_of(w0 * GRID_W, GRID_W)
        q_row = q_ref[pl.ds(q0, GRID_W), :]
        q_heads = jnp.where(same_head, jnp.concatenate([q_row] * NA_GROUP, axis=0), 0)
        s = lax.dot_general(q_heads, k_ref[pl.ds(k0, NA_KEYS), :], (((1,), (1,)), ((), ())),
                            preferred_element_type=F32)
        bias = jnp.concatenate(
            [jnp.concatenate([bias_ref[h, off + 2 * m] for m in range(NA_WIN_ROWS // 2)], axis=1)
             for h in range(NA_GROUP)], axis=0)
        return s + bias

    def attend(r, s):
        w0 = jnp.clip(r - NA_WIN_ROWS // 2, 0, NA_ROWS - NA_WIN_ROWS)
        q0 = pl.multiple_of(r * GRID_W, GRID_W)
        k0 = pl.multiple_of(w0 * GRID_W, GRID_W)
        p = jnp.exp(s - jnp.max(s, axis=-1, keepdims=True))
        inv = 1.0 / jnp.sum(p, axis=-1, keepdims=True)
        pv = jnp.dot(p.astype(BF16), v_ref[pl.ds(k0, NA_KEYS), :], preferred_element_type=F32)
        pv = jnp.where(same_head, pv * inv, 0.0)
        out = pv[0:GRID_W]
        for h in range(1, NA_GROUP):
            out = out + pv[h * GRID_W:(h + 1) * GRID_W]
        o_ref[pl.ds(q0, GRID_W), :] = out.astype(BF16)

    groups = NA_ROWS // NA_ROW_GROUP
    for t in range(NA_ROW_GROUP):
        s_ref[t] = scores(jnp.int32(t))

    def rows_body(i, carry):
        for t in range(NA_ROW_GROUP):
            s_next = scores(i * NA_ROW_GROUP + t)
            attend((i - 1) * NA_ROW_GROUP + t, s_ref[t])
            s_ref[t] = s_next
        return carry

    lax.fori_loop(1, groups, rows_body, 0)
    for t in range(NA_ROW_GROUP):
        attend(jnp.int32((groups - 1) * NA_ROW_GROUP + t), s_ref[t])


def _na_bias(rpb):
    c = jnp.arange(GRID_W)
    col_start = jnp.clip(c - NA_WIN_COLS // 2, 0, GRID_W - NA_WIN_COLS)
    col_mask = (c[None, :] >= col_start[:, None]) & (c[None, :] < col_start[:, None] + NA_WIN_COLS)
    edge = GRID_W - NA_WIN_COLS
    ext = jnp.pad(rpb.astype(F32), ((0, 0), (0, 0), (0, 0), (edge, edge)), mode="edge")
    table = jnp.stack([ext[..., GRID_W - 1 - q:2 * GRID_W - 1 - q] for q in range(GRID_W)], axis=-2)
    table = table + jnp.where(col_mask, 0.0, -1e30)
    return jnp.concatenate([table[:, :, :-1], table[:, :, 1:]], axis=-1)


def _na_mixer(hn, w_in, layer, bias):
    b = hn.shape[0]
    gw = NA_GROUP * NA_HEAD_DIM
    ng = NA_W // gw
    wcol = _w_in_cols(layer, gw, 3 * HY_W // gw)
    return pl.pallas_call(
        _na_kernel,
        grid=(ng, b),
        in_specs=[pl.BlockSpec((None, SEQ, D_MODEL), lambda j, i: (i, 0, 0)),
                  wcol(0, ng), wcol(1, ng), wcol(2, ng),
                  pl.BlockSpec((None, NA_GROUP, 2 * NA_WIN_ROWS - 2, GRID_W, 2 * GRID_W),
                               lambda j, i: (layer, j, 0, 0, 0))],
        out_specs=pl.BlockSpec((None, SEQ, gw), lambda j, i: (i, 0, j)),
        out_shape=jax.ShapeDtypeStruct((b, SEQ, NA_W), BF16),
        scratch_shapes=[pltpu.VMEM((SEQ, gw), BF16) for _ in range(3)]
        + [pltpu.VMEM((NA_ROW_GROUP, gw, NA_KEYS), F32)],
        compiler_params=_params("arbitrary", "arbitrary"),
        name="na_mixer",
    )(hn, w_in, w_in, w_in, bias)


def _merge_kernel(hn_ref, x_ref, ya0_ref, ya1_ref, yb_ref, yc_ref, wg_ref, gb_ref, wb_ref, wo_ref,
                  g_ref, o_ref):
    def gated_sum(rows):
        hn = hn_ref[rows, :]
        ya = jnp.concatenate([ya0_ref[rows, :], ya1_ref[rows, :]], axis=1)
        merged = None
        for i, y in enumerate((ya, yb_ref[rows, :], yc_ref[rows, :])):
            pre = jnp.dot(hn, wg_ref[:, i * D_MODEL:(i + 1) * D_MODEL], preferred_element_type=F32)
            gate = jax.nn.sigmoid(pre + gb_ref[i:i + 1, :])
            term = gate * jnp.dot(y, wb_ref[i], preferred_element_type=F32)
            merged = term if merged is None else merged + term
        return merged.astype(BF16)

    chunks = [slice(r, r + ROW_TILE) for r in range(0, MERGE_ROWS, ROW_TILE)]
    merged = [gated_sum(rows) for rows in chunks]
    for rows, m in zip(chunks, merged):
        out = jnp.dot(m, wo_ref[...], preferred_element_type=F32)
        o_ref[rows, :] = x_ref[rows, :] + _rms(out, g_ref[...])


def _merge(hn2d, x2d, ya0, ya1, yb, yc, w_gate, gate_bias, w_branch, w_out, gains, layer):
    n = x2d.shape[0]
    tm = MERGE_ROWS
    once = pl.Buffered(1)
    rows = lambda w: pl.BlockSpec((tm, w), lambda i: (i, 0))
    return pl.pallas_call(
        _merge_kernel,
        grid=(n // tm,),
        in_specs=[rows(D_MODEL), rows(D_MODEL), rows(COL_TILE), rows(COL_TILE), rows(NA_W), rows(SC_W),
                  pl.BlockSpec((D_MODEL, N_BRANCH * D_MODEL), lambda i: (0, 0), pipeline_mode=once),
                  _layer_spec(gate_bias, layer),
                  pl.BlockSpec((N_BRANCH, HY_W, D_MODEL), lambda i: (0, 0, 0), pipeline_mode=once),
                  pl.BlockSpec((D_MODEL, D_MODEL), lambda i: (0, 0), pipeline_mode=once),
                  _gain_spec(6 * layer + 1)],
        out_specs=rows(D_MODEL),
        out_shape=jax.ShapeDtypeStruct((n, D_MODEL), F32),
        compiler_params=_params("arbitrary"),
        name="merge",
    )(hn2d, x2d, ya0, ya1, yb, yc, w_gate, gate_bias, w_branch, w_out, gains)


def _kv_kernel(m_ref, g_ref, w_ref, o_ref):
    mn = _rms(m_ref[...], g_ref[...]).astype(BF16)
    o_ref[...] = jnp.dot(mn, w_ref[...], preferred_element_type=F32).astype(BF16)


def _mem_kv(mem, mem_norm, layer, wkv):
    b = mem.shape[0]
    return pl.pallas_call(
        _kv_kernel,
        grid=(b,),
        in_specs=[pl.BlockSpec((None, N_MEM, D_MODEL), lambda i: (i, 0, 0)),
                  _layer_spec(mem_norm, layer),
                  pl.BlockSpec((D_MODEL, 2 * D_MODEL), lambda i: (0, 0))],
        out_specs=pl.BlockSpec((None, N_MEM, 2 * D_MODEL), lambda i: (i, 0, 0)),
        out_shape=jax.ShapeDtypeStruct((b, N_MEM, 2 * D_MODEL), BF16),
        compiler_params=_params("arbitrary"),
        name="mem_kv",
    )(mem, mem_norm, wkv)


def _xattn_kernel(x_ref, kv_ref, wq_ref, wo_ref, gq_ref, go_ref, gn_ref, o_ref, hn_ref):
    chunks = [slice(r, r + XA_CHUNK) for r in range(0, XA_ROWS, XA_CHUNK)]
    head_cols = [slice(i * XA_HEAD_DIM, (i + 1) * XA_HEAD_DIM) for i in range(XA_HEADS)]

    def query(rows):
        h = _rms(x_ref[rows, :], gq_ref[...]).astype(BF16)
        q = jnp.dot(h, wq_ref[...], preferred_element_type=F32) * (XA_HEAD_DIM ** -0.5)
        return q.astype(BF16)

    def scores(q):
        return [lax.dot_general(q[:, sl], kv_ref[:, sl], (((1,), (1,)), ((), ())),
                                preferred_element_type=F32) for sl in head_cols]

    def values(s_heads):
        heads = []
        for i, s in enumerate(s_heads):
            vm = kv_ref[:, D_MODEL + i * XA_HEAD_DIM:D_MODEL + (i + 1) * XA_HEAD_DIM]
            p = jnp.exp(s - jnp.max(s, axis=-1, keepdims=True))
            den = jnp.sum(p, axis=-1, keepdims=True)
            heads.append((jnp.dot(p.astype(BF16), vm, preferred_element_type=F32) / den).astype(BF16))
        return jnp.concatenate(heads, axis=-1)

    s_all = [scores(q) for q in [query(rows) for rows in chunks]]
    attended = [values(s) for s in s_all]
    for rows, a in zip(chunks, attended):
        o = jnp.dot(a, wo_ref[...], preferred_element_type=F32)
        xn = x_ref[rows, :] + _rms(o, go_ref[...])
        o_ref[rows, :] = xn
        hn_ref[rows, :] = _rms(xn, gn_ref[...]).astype(BF16)


def _xattn(x, kv, wq, wo, gains, layer):
    b = x.shape[0]
    tm = XA_ROWS
    once = pl.Buffered(1)
    rows = pl.BlockSpec((None, tm, D_MODEL), lambda i, j: (i, j, 0))
    wfull = pl.BlockSpec((D_MODEL, D_MODEL), lambda i, j: (0, 0), pipeline_mode=once)
    return pl.pallas_call(
        _xattn_kernel,
        grid=(b, SEQ // tm),
        in_specs=[rows, pl.BlockSpec((None, N_MEM, 2 * D_MODEL), lambda i, j: (i, 0, 0)),
                  wfull, wfull, _gain_spec(6 * layer + 2), _gain_spec(6 * layer + 3),
                  _gain_spec(6 * layer + 4)],
        out_specs=(rows, rows),
        out_shape=(jax.ShapeDtypeStruct((b, SEQ, D_MODEL), F32),
                   jax.ShapeDtypeStruct((b, SEQ, D_MODEL), BF16)),
        compiler_params=_params("arbitrary", "arbitrary"),
        name="xattn",
    )(x, kv, wq, wo, gains, gains, gains)


def _gelu_tanh(x):
    c = math.sqrt(2.0 / math.pi)
    half = 0.5 * x
    return half + half * jnp.tanh(x * (c + (c * 0.044715) * (x * x)))


def _ffn_kernel(hn_ref, wu_ref, cw_ref, wd_ref, o_ref):
    def up(r0, c0, c1):
        lo = max(r0 - CONV_HALO, 0)
        hi = min(r0 + CONV_CHUNK + CONV_HALO, SEQ)
        hn = hn_ref[lo:hi, :]
        ug = jnp.dot(hn, wu_ref[:, c0:c1], preferred_element_type=F32)
        uv = jnp.dot(hn, wu_ref[:, D_FF + c0:D_FF + c1], preferred_element_type=F32)
        return r0, r0 - lo, c0, c1, ug, uv

    def down(acc, r0, skip, c0, c1, ug, uv):
        act = (_gelu_tanh(_dwconv3(ug, cw_ref.at[:, c0:c1]))
               * _dwconv3(uv, cw_ref.at[:, D_FF + c0:D_FF + c1]))
        act = act[skip:skip + CONV_CHUNK].astype(BF16)
        part = jnp.dot(act, wd_ref[c0:c1, :], preferred_element_type=F32)
        acc = part if c0 == 0 else acc + part
        if c1 == D_FF:
            o_ref[r0:r0 + CONV_CHUNK, :] = acc.astype(o_ref.dtype)
        return acc

    pending, acc = None, None
    for r0 in range(0, SEQ, CONV_CHUNK):
        for c0, c1 in zip(FFN_CUTS[:-1], FFN_CUTS[1:]):
            current = up(r0, c0, c1)
            if pending is not None:
                acc = down(acc, *pending)
            pending = current
    down(acc, *pending)


def _ffn(hn, w_up, w_conv, w_down, layer):
    b = hn.shape[0]
    once = pl.Buffered(1)
    return pl.pallas_call(
        _ffn_kernel,
        grid=(b,),
        in_specs=[pl.BlockSpec((None, SEQ, D_MODEL), lambda i: (i, 0, 0)),
                  pl.BlockSpec((D_MODEL, 2 * D_FF), lambda i: (0, 0), pipeline_mode=once),
                  _layer_spec(w_conv, layer),
                  pl.BlockSpec((D_FF, D_MODEL), lambda i: (0, 0), pipeline_mode=once)],
        out_specs=pl.BlockSpec((None, SEQ, D_MODEL), lambda i: (i, 0, 0)),
        out_shape=jax.ShapeDtypeStruct((b, SEQ, D_MODEL), BF16),
        compiler_params=_params("arbitrary"),
        name="ffn",
    )(hn, w_up, w_conv, w_down)


def _residual_kernel(x_ref, f_ref, g_ref, gn_ref, o_ref, hn_ref):
    xn = x_ref[...] + _rms(f_ref[...].astype(F32), g_ref[...])
    o_ref[...] = xn
    hn_ref[...] = _rms(xn, gn_ref[...]).astype(BF16)


def _residual_last_kernel(x_ref, f_ref, g_ref, o_ref):
    o_ref[...] = x_ref[...] + _rms(f_ref[...].astype(F32), g_ref[...])


def _residual(x2d, f2d, gains, layer, last):
    n = x2d.shape[0]
    tm = 1024
    rows = pl.BlockSpec((tm, D_MODEL), lambda i: (i, 0))
    x_shape = jax.ShapeDtypeStruct((n, D_MODEL), F32)
    if last:
        return pl.pallas_call(
            _residual_last_kernel,
            grid=(n // tm,),
            in_specs=[rows, rows, _gain_spec(6 * layer + 5)],
            out_specs=rows,
            out_shape=x_shape,
            compiler_params=_params("arbitrary"),
            name="residual_last",
        )(x2d, f2d, gains), None
    return pl.pallas_call(
        _residual_kernel,
        grid=(n // tm,),
        in_specs=[rows, rows, _gain_spec(6 * layer + 5), _gain_spec(6 * (layer + 1))],
        out_specs=(rows, rows),
        out_shape=(x_shape, jax.ShapeDtypeStruct((n, D_MODEL), BF16)),
        compiler_params=_params("arbitrary"),
        name="residual",
    )(x2d, f2d, gains, gains)


def _angle_tables(num, den):
    ang = (num % den).astype(F32) * (2.0 * math.pi / den)
    return jnp.cos(ang), jnp.sin(ang)


def _dft_tables():
    rows = SEQ // FFT_RADIX
    kappa = jnp.arange(rows, dtype=jnp.int32)[None, :, None]
    m = jnp.arange(rows, dtype=jnp.int32)[None, None, :]
    r = jnp.arange(FFT_RADIX, dtype=jnp.int32)[:, None, None]
    c0, s0 = _angle_tables((2 * kappa + 1) * m, 2 * FFT_SUB)
    cr, sr = _angle_tables((2 * kappa + 1) * r, 2 * FFT_N)
    cf = (c0 * cr - s0 * sr).astype(BF16)
    sf = (s0 * cr + c0 * sr).astype(BF16)
    return cf, sf, cf.transpose(0, 2, 1), sf.transpose(0, 2, 1)


def kernel(x, mem, norm_gains, mem_norm, w_in, gate_bias, hy_short_w, hy_w1, hy_b1, hy_w2, hy_b2,
           hy_w3, hy_freq, hy_bias, na_rpb, sc_conv_w, w_branch, w_out, xa_wq, xa_wkv, xa_wo,
           ffn_up, ffn_conv, ffn_down):
    b, l, d = x.shape
    depth = w_in.shape[0]
    assert (l, d) == (SEQ, D_MODEL) and mem.shape[1:] == (N_MEM, D_MODEL)
    n = b * l
    conv_tables = _dft_tables()
    gains = norm_gains.astype(F32).reshape(depth * 6, 1, d)
    mem_gain = mem_norm.astype(F32).reshape(depth, 1, d)
    w1p = jnp.pad(hy_w1.astype(F32), ((0, 0), (0, HY_HIDDEN - HY_EMB), (0, 0)))
    b1 = hy_b1.astype(F32).reshape(depth, 1, HY_HIDDEN)
    b2 = hy_b2.astype(F32).reshape(depth, 1, HY_HIDDEN)
    na_bias = _na_bias(na_rpb)
    w_branch2d = w_branch.reshape(depth, N_BRANCH * HY_W, d)
    x2d = x.reshape(n, d)
    hn = _prenorm(x2d, gains, 0)
    for i in range(depth):
        w_hyena, w_gate, w_br, w_o, wq, wkv, wo, w_up, w_down = _cast_layer(
            i, (w_in, 0, 3 * HY_W), (w_in, 3 * HY_W + 3 * NA_W + 3 * SC_W, N_BRANCH * D_MODEL),
            (w_branch2d, 0, d), (w_out, 0, d), (xa_wq, 0, d), (xa_wkv, 0, 2 * d), (xa_wo, 0, d),
            (ffn_up, 0, 2 * D_FF), (ffn_down, 0, d))
        kr, ki = _hyena_filters(conv_tables, w1p, b1, hy_w2, b2, hy_w3, hy_freq, i)
        hn3 = hn.reshape(b, l, d)
        ya = [_hyena_mixer(hn3, w_hyena, hy_short_w, conv_tables, kr, ki, hy_bias, i, tile)
              for tile in range(HY_W // COL_TILE)]
        yb = _na_mixer(hn3, w_in, i, na_bias)
        yc = _shortconv_mixer(hn3, w_in, i, sc_conv_w)
        x2d = _merge(hn, x2d, ya[0], ya[1], yb.reshape(n, NA_W), yc.reshape(n, SC_W), w_gate, gate_bias,
                     w_br.reshape(N_BRANCH, HY_W, d), w_o, gains, i)
        kv = _mem_kv(mem, mem_gain, i, wkv)
        x3, hn2 = _xattn(x2d.reshape(b, l, d), kv, wq, wo, gains, i)
        f = _ffn(hn2, w_up, ffn_conv, w_down, i)
        x2d, hn = _residual(x3.reshape(n, d), f.reshape(n, d), gains, i, i + 1 == depth)
    return x2d.reshape(b, l, d)
```

```python
import functools
import math

import jax
import jax.numpy as jnp
from jax import lax
from jax.experimental import pallas as pl
from jax.experimental.pallas import tpu as pltpu

D_MODEL = 1024
SEQ = 2048
N_MEM = 256
GRID_W = 64
HY_W = 512
NA_HEADS = 8
NA_HEAD_DIM = 64
NA_W = NA_HEADS * NA_HEAD_DIM
NA_WIN_ROWS = 8
NA_WIN_COLS = 16
SC_W = 512
XA_HEADS = 4
XA_HEAD_DIM = D_MODEL // XA_HEADS
D_FF = 2816
HY_ORDER = 2
HY_EMB = 33
HY_HIDDEN = 64
HY_FAST_DECAY = 0.3
HY_SLOW_DECAY = 1.5
HY_TARGET = 1e-2
N_BRANCH = 3
EPS = 1e-6

FFT_N = 2 * SEQ
FFT_RADIX = 4
FFT_SUB = FFT_N // FFT_RADIX
NA_ROWS = SEQ // GRID_W
NA_KEYS = NA_WIN_ROWS * GRID_W
NA_GROUP = 4
NA_ROW_GROUP = 8
LANES = 128
COL_TILE = 256
ROW_TILE = 512
MERGE_ROWS = 2 * ROW_TILE
XA_ROWS = 2 * ROW_TILE
XA_CHUNK = 256
CAST_STEPS = 8
CONV_PAD = 8
CONV_CHUNK = 512
FFN_CUTS = (0, 4 * COL_TILE, 8 * COL_TILE, D_FF)
CONV_HALO = 16
VMEM_LIMIT = 60 * 1024 * 1024

BF16 = jnp.bfloat16
F32 = jnp.float32


def _params(*sem):
    return pltpu.CompilerParams(dimension_semantics=sem, vmem_limit_bytes=VMEM_LIMIT)


def _gain_spec(index):
    return pl.BlockSpec((None, 1, D_MODEL), lambda *_: (index, 0, 0))


def _layer_spec(arr, layer, width=None, col=0):
    _, r, c = arr.shape
    return pl.BlockSpec((None, r, c if width is None else width), lambda *_: (layer, 0, col))


def _rms(xf, g):
    ms = jnp.mean(xf * xf, axis=-1, keepdims=True)
    return xf * lax.rsqrt(ms + EPS) * g


def _dwconv3(u, w_ref):
    n = u.shape[0]
    zeros = jnp.zeros((CONV_PAD, u.shape[1]), F32)
    padded = jnp.concatenate([zeros, u, zeros], axis=0)
    m = n + 2 * CONV_PAD
    prev = pltpu.roll(padded, 1, 0)[CONV_PAD:CONV_PAD + n]
    nxt = pltpu.roll(padded, m - 1, 0)[CONV_PAD:CONV_PAD + n]
    return prev * w_ref[0:1, :] + u * w_ref[1:2, :] + nxt * w_ref[2:3, :]


def _prenorm_kernel(x_ref, g_ref, o_ref):
    o_ref[...] = _rms(x_ref[...], g_ref[...]).astype(BF16)


def _prenorm(x2d, gains, gi):
    n = x2d.shape[0]
    tm = 1024
    return pl.pallas_call(
        _prenorm_kernel,
        grid=(n // tm,),
        in_specs=[pl.BlockSpec((tm, D_MODEL), lambda i: (i, 0)),
                  _gain_spec(gi)],
        out_specs=pl.BlockSpec((tm, D_MODEL), lambda i: (i, 0)),
        out_shape=jax.ShapeDtypeStruct((n, D_MODEL), BF16),
        compiler_params=_params("arbitrary"),
        name="prenorm",
    )(x2d, gains)


def _cast_kernel(parts, *refs):
    n_in = sum(parts)
    w_refs, o_refs = refs[:n_in], refs[n_in:]
    k = 0
    for o_ref, n in zip(o_refs, parts):
        width = o_ref.shape[1] // n
        for p in range(n):
            o_ref[:, p * width:(p + 1) * width] = w_refs[k][...].astype(BF16)
            k += 1


def _cast_layer(layer, *weights):
    in_specs, out_specs, out_shapes, operands, parts = [], [], [], [], []
    for w, col0, ncols in weights:
        rb = w.shape[1] // CAST_STEPS
        cb = math.gcd(col0, ncols) if col0 else ncols
        assert w.shape[1] % CAST_STEPS == 0 and rb % 16 == 0 and cb % LANES == 0
        parts.append(ncols // cb)
        for p in range(ncols // cb):
            in_specs.append(pl.BlockSpec((None, rb, cb),
                                         lambda i, cblk=col0 // cb + p: (layer, i, cblk)))
            operands.append(w)
        out_specs.append(pl.BlockSpec((rb, ncols), lambda i: (i, 0)))
        out_shapes.append(jax.ShapeDtypeStruct((w.shape[1], ncols), BF16))
    return pl.pallas_call(
        functools.partial(_cast_kernel, tuple(parts)),
        grid=(CAST_STEPS,),
        in_specs=in_specs,
        out_specs=tuple(out_specs),
        out_shape=tuple(out_shapes),
        compiler_params=_params("arbitrary"),
        name="cast_bf16",
    )(*operands)


def _filter_mlp_kernel(z_ref, w1_ref, b1_ref, w2_ref, b2_ref, w3_ref, f_ref, t_ref, dl_ref,
                       hs_ref, hd_ref):
    hp = lax.Precision.HIGHEST
    h = jnp.sin(f_ref[0:1, :] * (jnp.dot(z_ref[...], w1_ref[...], precision=hp) + b1_ref[...]))
    h = jnp.sin(f_ref[1:2, :] * (jnp.dot(h, w2_ref[...], precision=hp) + b2_ref[...]))
    decay = jnp.exp(-t_ref[...] * dl_ref[...])
    row = lax.broadcasted_iota(jnp.int32, (SEQ, HY_W), 0)
    h_hi = h.astype(BF16)
    h_lo = (h - h_hi.astype(F32)).astype(BF16)

    def out_layer(cols):
        w = w3_ref[:, cols]
        w_hi = w.astype(BF16)
        w_lo = (w - w_hi.astype(F32)).astype(BF16)
        return (jnp.dot(h_hi, w_hi, preferred_element_type=F32)
                + (jnp.dot(h_hi, w_lo, preferred_element_type=F32)
                   + jnp.dot(h_lo, w_hi, preferred_element_type=F32)))

    for o in range(HY_ORDER):
        c_f = o * HY_W
        c_b = HY_ORDER * HY_W + o * HY_W
        hf = out_layer(slice(c_f, c_f + HY_W)) * decay
        hb = out_layer(slice(c_b, c_b + HY_W)) * decay
        hb = jnp.where(row == 0, 0.0, hb)
        hs_ref[:, c_f:c_f + HY_W] = (hf + hb).astype(BF16)
        hd_ref[:, c_f:c_f + HY_W] = (hb - hf).astype(BF16)


def _filter_dft_kernel(cf_ref, sf_ref, hs_ref, hd_ref, kr_ref, ki_ref):
    rows = SEQ // FFT_RADIX
    scale = 2.0 / FFT_N
    sub = lambda ref, r: ref[r * rows:(r + 1) * rows, :]
    cos_dot = lambda ref, r: jnp.dot(cf_ref[r], sub(ref, r), preferred_element_type=F32)
    sin_dot = lambda ref, r: jnp.dot(sf_ref[r], sub(ref, r), preferred_element_type=F32)
    c = [cos_dot(hs_ref, r) for r in range(FFT_RADIX)]
    a, b, e = c[0] + c[2], c[0] - c[2], c[1] + c[3]
    d_im = sin_dot(hs_ref, 3) - sin_dot(hs_ref, 1)
    for f, val in enumerate((a + e, b + d_im, b - d_im, a - e)):
        kr_ref[f] = val * scale
    s = [sin_dot(hd_ref, r) for r in range(FFT_RADIX)]
    a, b, e = s[0] + s[2], s[0] - s[2], s[1] + s[3]
    d_re = cos_dot(hd_ref, 1) - cos_dot(hd_ref, 3)
    for f, val in enumerate((a + e, b + d_re, d_re - b, e - a)):
        ki_ref[f] = val * scale


def _hyena_filters(tables, w1p, b1, w2, b2, w3, freq, layer):
    t = jnp.linspace(0.0, 1.0, SEQ, dtype=F32)[:, None]
    bands = (HY_EMB - 1) // 2
    w = 2.0 * math.pi * jnp.arange(SEQ, dtype=F32)[:, None] / SEQ
    f = jnp.linspace(1e-4, bands - 1, bands, dtype=F32)[None, :]
    z = jnp.concatenate([t, jnp.cos(f * w), -jnp.sin(f * w)], axis=-1)
    z = jnp.pad(z, ((0, 0), (0, HY_HIDDEN - HY_EMB)))
    rows = SEQ // FFT_RADIX
    decimate = lambda a: a.reshape(rows, FFT_RADIX, -1).transpose(1, 0, 2).reshape(SEQ, -1)
    z, t = decimate(z), decimate(t)
    deltas = jnp.abs(jnp.linspace(math.log(HY_TARGET) / HY_SLOW_DECAY,
                                  math.log(HY_TARGET) / HY_FAST_DECAY, HY_W, dtype=F32))[None, :]
    width = HY_ORDER * HY_W
    whole = lambda a: pl.BlockSpec(a.shape, lambda i: (0,) * a.ndim)
    taps = pl.BlockSpec((SEQ, width), lambda i: (0, 0))
    hs, hd = pl.pallas_call(
        _filter_mlp_kernel,
        grid=(1,),
        in_specs=[whole(z)] + [_layer_spec(a, layer) for a in (w1p, b1, w2, b2, w3, freq)]
        + [whole(t), whole(deltas)],
        out_specs=(taps, taps),
        out_shape=(jax.ShapeDtypeStruct((SEQ, width), BF16),
                   jax.ShapeDtypeStruct((SEQ, width), BF16)),
        compiler_params=_params("arbitrary"),
        name="hyena_filter_mlp",
    )(z, w1p, b1, w2, b2, w3, freq, t, deltas)
    cf, sf = tables[:2]
    small = pl.BlockSpec((FFT_RADIX, rows, rows), lambda j: (0, 0, 0))
    col = pl.BlockSpec((SEQ, COL_TILE), lambda j: (0, j))
    fam = pl.BlockSpec((FFT_RADIX, rows, COL_TILE), lambda j: (0, 0, j))
    fam_shape = jax.ShapeDtypeStruct((FFT_RADIX, rows, width), F32)
    return pl.pallas_call(
        _filter_dft_kernel,
        grid=(width // COL_TILE,),
        in_specs=[small, small, col, col],
        out_specs=(fam, fam),
        out_shape=(fam_shape, fam_shape),
        compiler_params=_params("arbitrary"),
        name="hyena_filter_dft",
    )(cf, sf, hs, hd)


def _hyena_kernel(hn_ref, wv_ref, w1_ref, w2_ref, sv_ref, s1_ref, s2_ref, cf_ref, sf_ref, ct_ref,
                  st_ref, kr0_ref, ki0_ref, kr1_ref, ki1_ref, bias_ref, o_ref,
                  slab_v_ref, slab_1_ref, slab_2_ref, slab_o_ref):
    radix = FFT_RADIX
    rows = SEQ // radix
    n_slab = COL_TILE // LANES
    zero_row = jnp.zeros((CONV_PAD, COL_TILE), F32)

    def project(w_ref, slab_ref):
        half = SEQ // 2
        for top in (0, half):
            u = jnp.dot(hn_ref[top:top + half, :], w_ref[...], preferred_element_type=F32)
            for j in range(n_slab):
                slab_ref[j, top:top + half, :] = u[:, j * LANES:(j + 1) * LANES]
        return [jnp.concatenate([slab_ref[j, pl.ds(r, rows, stride=radix), :] for j in range(n_slab)],
                                axis=1) for r in range(radix)]

    def short_conv(u, w_ref):
        n = u[0].shape[0]
        prev_wrap = pltpu.roll(jnp.concatenate([u[-1], zero_row], axis=0), 1, 0)[:n]
        next_wrap = pltpu.roll(jnp.concatenate([zero_row, u[0]], axis=0), n + CONV_PAD - 1, 0)[CONV_PAD:]
        prev = [prev_wrap] + u[:-1]
        nxt = u[1:] + [next_wrap]
        return [prev[r] * w_ref[0:1, :] + u[r] * w_ref[1:2, :] + nxt[r] * w_ref[2:3, :]
                for r in range(radix)]

    def cmul(ar, ai, br, bi):
        return ar * br - ai * bi, ar * bi + ai * br

    def transform(x):
        xb = [v.astype(BF16) for v in x]
        tr = [jnp.dot(cf_ref[r], xb[r], preferred_element_type=F32) for r in range(radix)]
        ti = [-jnp.dot(sf_ref[r], xb[r], preferred_element_type=F32) for r in range(radix)]
        return tr, ti

    def filter_and_invert(tr, ti, kr_ref, ki_ref):
        ar, ai = tr[0] + tr[2], ti[0] + ti[2]
        br, bi = tr[0] - tr[2], ti[0] - ti[2]
        cr, ci = tr[1] + tr[3], ti[1] + ti[3]
        dr, di = tr[1] - tr[3], ti[1] - ti[3]
        fam = [(ar + cr, ai + ci), (br + di, bi - dr), (br - di, -bi - dr), (ar - cr, ci - ai)]
        y = [cmul(fr, fi, kr_ref[f], ki_ref[f]) for f, (fr, fi) in enumerate(fam)]
        er, ei = y[0][0] + y[3][0], y[0][1] - y[3][1]
        fr, fi = y[0][0] - y[3][0], y[0][1] + y[3][1]
        gr, gi = y[1][0] + y[2][0], y[1][1] - y[2][1]
        hr, hi = y[1][0] - y[2][0], y[1][1] + y[2][1]
        p = [(er + gr, ei + gi), (fr - hi, fi + hr), (er - gr, ei - gi), (fr + hi, fi - hr)]
        out = []
        for r in range(radix):
            pr, pi_ = p[r]
            out.append(jnp.dot(ct_ref[r], pr.astype(BF16), preferred_element_type=F32)
                       - jnp.dot(st_ref[r], pi_.astype(BF16), preferred_element_type=F32))
        return out

    v = short_conv(project(wv_ref, slab_v_ref), sv_ref)
    spectrum = transform(v)
    x1 = short_conv(project(w1_ref, slab_1_ref), s1_ref)
    y = filter_and_invert(*spectrum, kr0_ref, ki0_ref)
    z = [x1[r] * (y[r] + v[r] * bias_ref[0:1, :]) for r in range(radix)]
    spectrum = transform(z)
    x2 = short_conv(project(w2_ref, slab_2_ref), s2_ref)
    y = filter_and_invert(*spectrum, kr1_ref, ki1_ref)
    for r in range(radix):
        out = x2[r] * (y[r] + z[r] * bias_ref[1:2, :])
        for j in range(n_slab):
            slab_o_ref[j, pl.ds(r, rows, stride=radix), :] = out[:, j * LANES:(j + 1) * LANES]
    o_ref[...] = jnp.concatenate([slab_o_ref[j] for j in range(n_slab)], axis=1).astype(BF16)


def _hyena_mixer(hn, w_hy, short_w, tables, kr, ki, bias, layer, tile):
    b = hn.shape[0]
    nt = HY_W // COL_TILE
    rows = SEQ // FFT_RADIX
    once = pl.Buffered(1)
    cf, sf, ct, st = tables

    def fixed(shape, index):
        return pl.BlockSpec(shape, lambda i: index, pipeline_mode=once)

    def wcol(k):
        return fixed((D_MODEL, COL_TILE), (0, k * nt + tile))

    def scol(k):
        return _layer_spec(short_w, layer, COL_TILE, k * nt + tile)

    def kfam(o):
        return fixed((FFT_RADIX, rows, COL_TILE), (0, 0, o * nt + tile))

    small = fixed((FFT_RADIX, rows, rows), (0, 0, 0))
    slab = pltpu.VMEM((COL_TILE // LANES, SEQ, LANES), F32)
    out = pl.pallas_call(
        _hyena_kernel,
        grid=(b,),
        in_specs=[pl.BlockSpec((None, SEQ, D_MODEL), lambda i: (i, 0, 0)),
                  wcol(0), wcol(1), wcol(2), scol(0), scol(1), scol(2),
                  small, small, small, small,
                  kfam(0), kfam(0), kfam(1), kfam(1),
                  _layer_spec(bias, layer, COL_TILE, tile)],
        out_specs=pl.BlockSpec((None, SEQ, COL_TILE), lambda i: (i, 0, 0)),
        out_shape=jax.ShapeDtypeStruct((b, SEQ, COL_TILE), BF16),
        scratch_shapes=[slab, slab, slab, slab],
        compiler_params=_params("arbitrary"),
        name="hyena_mixer",
    )(hn, w_hy, w_hy, w_hy, short_w, short_w, short_w, cf, sf, ct, st, kr, ki, kr, ki, bias)
    return out.reshape(b * SEQ, COL_TILE)


def _shortconv_kernel(hn_ref, wb_ref, wc_ref, wx_ref, cw_ref, o_ref):
    wb = wb_ref[...].astype(BF16)
    wc = wc_ref[...].astype(BF16)
    wx = wx_ref[...].astype(BF16)

    def project(r0):
        lo = max(r0 - CONV_HALO, 0)
        hi = min(r0 + CONV_CHUNK + CONV_HALO, SEQ)
        hn = hn_ref[lo:hi, :]
        return (r0, r0 - lo, jnp.dot(hn, wb, preferred_element_type=F32),
                jnp.dot(hn, wc, preferred_element_type=F32), jnp.dot(hn, wx, preferred_element_type=F32))

    def finish(r0, skip, bg, cg, xi):
        out = bg * _dwconv3(cg * xi, cw_ref)
        o_ref[r0:r0 + CONV_CHUNK, :] = out[skip:skip + CONV_CHUNK].astype(BF16)

    pending = None
    for r0 in range(0, SEQ, CONV_CHUNK):
        current = project(r0)
        if pending is not None:
            finish(*pending)
        pending = current
    finish(*pending)


def _w_in_cols(layer, width, first):
    return lambda k, nt: pl.BlockSpec((None, D_MODEL, width),
                                      lambda j, i: (layer, 0, first + k * nt + j))


def _shortconv_mixer(hn, w_in, layer, conv_w):
    b = hn.shape[0]
    nt = SC_W // COL_TILE
    wcol = _w_in_cols(layer, COL_TILE, (3 * HY_W + 3 * NA_W) // COL_TILE)
    return pl.pallas_call(
        _shortconv_kernel,
        grid=(nt, b),
        in_specs=[pl.BlockSpec((None, SEQ, D_MODEL), lambda j, i: (i, 0, 0)),
                  wcol(0, nt), wcol(1, nt), wcol(2, nt),
                  pl.BlockSpec((None, 3, COL_TILE), lambda j, i: (layer, 0, j))],
        out_specs=pl.BlockSpec((None, SEQ, COL_TILE), lambda j, i: (i, 0, j)),
        out_shape=jax.ShapeDtypeStruct((b, SEQ, SC_W), BF16),
        compiler_params=_params("arbitrary", "arbitrary"),
        name="shortconv_mixer",
    )(hn, w_in, w_in, w_in, conv_w)


def _na_kernel(hn_ref, wq_ref, wk_ref, wv_ref, bias_ref, o_ref, q_ref, k_ref, v_ref, s_ref):
    hn = hn_ref[...]
    q = jnp.dot(hn, wq_ref[...].astype(BF16), preferred_element_type=F32)
    q_ref[...] = (q * (NA_HEAD_DIM ** -0.5)).astype(BF16)
    k_ref[...] = jnp.dot(hn, wk_ref[...].astype(BF16), preferred_element_type=F32).astype(BF16)
    v_ref[...] = jnp.dot(hn, wv_ref[...].astype(BF16), preferred_element_type=F32).astype(BF16)
    gw = NA_GROUP * NA_HEAD_DIM
    same_head = (lax.broadcasted_iota(jnp.int32, (gw, gw), 0) // NA_HEAD_DIM
                 == lax.broadcasted_iota(jnp.int32, (gw, gw), 1) // NA_HEAD_DIM)

    def scores(r):
        w0 = jnp.clip(r - NA_WIN_ROWS // 2, 0, NA_ROWS - NA_WIN_ROWS)
        off = w0 - r + (NA_WIN_ROWS - 1)
        q0 = pl.multiple_of(r * GRID_W, GRID_W)
        k0 = pl.multiple_of(w0 * GRID_W, GRID_W)
        q_row = q_ref[pl.ds(q0, GRID_W), :]
        q_heads = jnp.where(same_head, jnp.concatenate([q_row] * NA_GROUP, axis=0), 0)
        s = lax.dot_general(q_heads, k_ref[pl.ds(k0, NA_KEYS), :], (((1,), (1,)), ((), ())),
                            preferred_element_type=F32)
        bias = jnp.concatenate(
            [jnp.concatenate([bias_ref[h, off + 2 * m] for m in range(NA_WIN_ROWS // 2)], axis=1)
             for h in range(NA_GROUP)], axis=0)
        return s + bias

    def attend(r, s):
        w0 = jnp.clip(r - NA_WIN_ROWS // 2, 0, NA_ROWS - NA_WIN_ROWS)
        q0 = pl.multiple_of(r * GRID_W, GRID_W)
        k0 = pl.multiple_of(w0 * GRID_W, GRID_W)
        p = jnp.exp(s - jnp.max(s, axis=-1, keepdims=True))
        inv = 1.0 / jnp.sum(p, axis=-1, keepdims=True)
        pv = jnp.dot(p.astype(BF16), v_ref[pl.ds(k0, NA_KEYS), :], preferred_element_type=F32)
        pv = jnp.where(same_head, pv * inv, 0.0)
        out = pv[0:GRID_W]
        for h in range(1, NA_GROUP):
            out = out + pv[h * GRID_W:(h + 1) * GRID_W]
        o_ref[pl.ds(q0, GRID_W), :] = out.astype(BF16)

    groups = NA_ROWS // NA_ROW_GROUP
    for t in range(NA_ROW_GROUP):
        s_ref[t] = scores(jnp.int32(t))

    def rows_body(i, carry):
        for t in range(NA_ROW_GROUP):
            s_next = scores(i * NA_ROW_GROUP + t)
            attend((i - 1) * NA_ROW_GROUP + t, s_ref[t])
            s_ref[t] = s_next
        return carry

    lax.fori_loop(1, groups, rows_body, 0)
    for t in range(NA_ROW_GROUP):
        attend(jnp.int32((groups - 1) * NA_ROW_GROUP + t), s_ref[t])


def _na_bias(rpb):
    c = jnp.arange(GRID_W)
    col_start = jnp.clip(c - NA_WIN_COLS // 2, 0, GRID_W - NA_WIN_COLS)
    col_mask = (c[None, :] >= col_start[:, None]) & (c[None, :] < col_start[:, None] + NA_WIN_COLS)
    dc = jnp.clip(c[None, :] - c[:, None] + NA_WIN_COLS - 1, 0, 2 * NA_WIN_COLS - 2)
    pick = (dc[None] == jnp.arange(2 * NA_WIN_COLS - 1)[:, None, None]).astype(F32)
    table = jnp.einsum("lhrd,dqc->lhrqc", rpb.astype(F32), pick, precision=lax.Precision.HIGHEST)
    table = table + jnp.where(col_mask, 0.0, -1e30)
    return jnp.concatenate([table[:, :, :-1], table[:, :, 1:]], axis=-1)


def _na_mixer(hn, w_in, layer, bias):
    b = hn.shape[0]
    gw = NA_GROUP * NA_HEAD_DIM
    ng = NA_W // gw
    wcol = _w_in_cols(layer, gw, 3 * HY_W // gw)
    return pl.pallas_call(
        _na_kernel,
        grid=(ng, b),
        in_specs=[pl.BlockSpec((None, SEQ, D_MODEL), lambda j, i: (i, 0, 0)),
                  wcol(0, ng), wcol(1, ng), wcol(2, ng),
                  pl.BlockSpec((None, NA_GROUP, 2 * NA_WIN_ROWS - 2, GRID_W, 2 * GRID_W),
                               lambda j, i: (layer, j, 0, 0, 0))],
        out_specs=pl.BlockSpec((None, SEQ, gw), lambda j, i: (i, 0, j)),
        out_shape=jax.ShapeDtypeStruct((b, SEQ, NA_W), BF16),
        scratch_shapes=[pltpu.VMEM((SEQ, gw), BF16) for _ in range(3)]
        + [pltpu.VMEM((NA_ROW_GROUP, gw, NA_KEYS), F32)],
        compiler_params=_params("arbitrary", "arbitrary"),
        name="na_mixer",
    )(hn, w_in, w_in, w_in, bias)


def _merge_kernel(hn_ref, x_ref, ya0_ref, ya1_ref, yb_ref, yc_ref, wg_ref, gb_ref, wb_ref, wo_ref,
                  g_ref, o_ref):
    def gated_sum(rows):
        hn = hn_ref[rows, :]
        ya = jnp.concatenate([ya0_ref[rows, :], ya1_ref[rows, :]], axis=1)
        merged = None
        for i, y in enumerate((ya, yb_ref[rows, :], yc_ref[rows, :])):
            pre = jnp.dot(hn, wg_ref[:, i * D_MODEL:(i + 1) * D_MODEL], preferred_element_type=F32)
            gate = jax.nn.sigmoid(pre + gb_ref[i:i + 1, :])
            term = gate * jnp.dot(y, wb_ref[i], preferred_element_type=F32)
            merged = term if merged is None else merged + term
        return merged.astype(BF16)

    chunks = [slice(r, r + ROW_TILE) for r in range(0, MERGE_ROWS, ROW_TILE)]
    merged = [gated_sum(rows) for rows in chunks]
    for rows, m in zip(chunks, merged):
        out = jnp.dot(m, wo_ref[...], preferred_element_type=F32)
        o_ref[rows, :] = x_ref[rows, :] + _rms(out, g_ref[...])


def _merge(hn2d, x2d, ya0, ya1, yb, yc, w_gate, gate_bias, w_branch, w_out, gains, layer):
    n = x2d.shape[0]
    tm = MERGE_ROWS
    once = pl.Buffered(1)
    rows = lambda w: pl.BlockSpec((tm, w), lambda i: (i, 0))
    return pl.pallas_call(
        _merge_kernel,
        grid=(n // tm,),
        in_specs=[rows(D_MODEL), rows(D_MODEL), rows(COL_TILE), rows(COL_TILE), rows(NA_W), rows(SC_W),
                  pl.BlockSpec((D_MODEL, N_BRANCH * D_MODEL), lambda i: (0, 0), pipeline_mode=once),
                  _layer_spec(gate_bias, layer),
                  pl.BlockSpec((N_BRANCH, HY_W, D_MODEL), lambda i: (0, 0, 0), pipeline_mode=once),
                  pl.BlockSpec((D_MODEL, D_MODEL), lambda i: (0, 0), pipeline_mode=once),
                  _gain_spec(6 * layer + 1)],
        out_specs=rows(D_MODEL),
        out_shape=jax.ShapeDtypeStruct((n, D_MODEL), F32),
        compiler_params=_params("arbitrary"),
        name="merge",
    )(hn2d, x2d, ya0, ya1, yb, yc, w_gate, gate_bias, w_branch, w_out, gains)


def _kv_kernel(m_ref, g_ref, w_ref, o_ref):
    mn = _rms(m_ref[...], g_ref[...]).astype(BF16)
    o_ref[...] = jnp.dot(mn, w_ref[...], preferred_element_type=F32).astype(BF16)


def _mem_kv(mem, mem_norm, layer, wkv):
    b = mem.shape[0]
    return pl.pallas_call(
        _kv_kernel,
        grid=(b,),
        in_specs=[pl.BlockSpec((None, N_MEM, D_MODEL), lambda i: (i, 0, 0)),
                  _layer_spec(mem_norm, layer),
                  pl.BlockSpec((D_MODEL, 2 * D_MODEL), lambda i: (0, 0))],
        out_specs=pl.BlockSpec((None, N_MEM, 2 * D_MODEL), lambda i: (i, 0, 0)),
        out_shape=jax.ShapeDtypeStruct((b, N_MEM, 2 * D_MODEL), BF16),
        compiler_params=_params("arbitrary"),
        name="mem_kv",
    )(mem, mem_norm, wkv)


def _xattn_kernel(x_ref, kv_ref, wq_ref, wo_ref, gq_ref, go_ref, gn_ref, o_ref, hn_ref):
    chunks = [slice(r, r + XA_CHUNK) for r in range(0, XA_ROWS, XA_CHUNK)]
    head_cols = [slice(i * XA_HEAD_DIM, (i + 1) * XA_HEAD_DIM) for i in range(XA_HEADS)]

    def query(rows):
        h = _rms(x_ref[rows, :], gq_ref[...]).astype(BF16)
        q = jnp.dot(h, wq_ref[...], preferred_element_type=F32) * (XA_HEAD_DIM ** -0.5)
        return q.astype(BF16)

    def scores(q):
        return [lax.dot_general(q[:, sl], kv_ref[:, sl], (((1,), (1,)), ((), ())),
                                preferred_element_type=F32) for sl in head_cols]

    def values(s_heads):
        heads = []
        for i, s in enumerate(s_heads):
            vm = kv_ref[:, D_MODEL + i * XA_HEAD_DIM:D_MODEL + (i + 1) * XA_HEAD_DIM]
            p = jnp.exp(s - jnp.max(s, axis=-1, keepdims=True))
            den = jnp.sum(p, axis=-1, keepdims=True)
            heads.append((jnp.dot(p.astype(BF16), vm, preferred_element_type=F32) / den).astype(BF16))
        return jnp.concatenate(heads, axis=-1)

    s_all = [scores(q) for q in [query(rows) for rows in chunks]]
    attended = [values(s) for s in s_all]
    for rows, a in zip(chunks, attended):
        o = jnp.dot(a, wo_ref[...], preferred_element_type=F32)
        xn = x_ref[rows, :] + _rms(o, go_ref[...])
        o_ref[rows, :] = xn
        hn_ref[rows, :] = _rms(xn, gn_ref[...]).astype(BF16)


def _xattn(x, kv, wq, wo, gains, layer):
    b = x.shape[0]
    tm = XA_ROWS
    once = pl.Buffered(1)
    rows = pl.BlockSpec((None, tm, D_MODEL), lambda i, j: (i, j, 0))
    wfull = pl.BlockSpec((D_MODEL, D_MODEL), lambda i, j: (0, 0), pipeline_mode=once)
    return pl.pallas_call(
        _xattn_kernel,
        grid=(b, SEQ // tm),
        in_specs=[rows, pl.BlockSpec((None, N_MEM, 2 * D_MODEL), lambda i, j: (i, 0, 0)),
                  wfull, wfull, _gain_spec(6 * layer + 2), _gain_spec(6 * layer + 3),
                  _gain_spec(6 * layer + 4)],
        out_specs=(rows, rows),
        out_shape=(jax.ShapeDtypeStruct((b, SEQ, D_MODEL), F32),
                   jax.ShapeDtypeStruct((b, SEQ, D_MODEL), BF16)),
        compiler_params=_params("arbitrary", "arbitrary"),
        name="xattn",
    )(x, kv, wq, wo, gains, gains, gains)


def _gelu_tanh(x):
    c = math.sqrt(2.0 / math.pi)
    half = 0.5 * x
    return half + half * jnp.tanh(x * (c + (c * 0.044715) * (x * x)))


def _ffn_kernel(hn_ref, wu_ref, cw_ref, wd_ref, o_ref):
    def up(r0, c0, c1):
        lo = max(r0 - CONV_HALO, 0)
        hi = min(r0 + CONV_CHUNK + CONV_HALO, SEQ)
        hn = hn_ref[lo:hi, :]
        ug = jnp.dot(hn, wu_ref[:, c0:c1], preferred_element_type=F32)
        uv = jnp.dot(hn, wu_ref[:, D_FF + c0:D_FF + c1], preferred_element_type=F32)
        return r0, r0 - lo, c0, c1, ug, uv

    def down(acc, r0, skip, c0, c1, ug, uv):
        act = (_gelu_tanh(_dwconv3(ug, cw_ref.at[:, c0:c1]))
               * _dwconv3(uv, cw_ref.at[:, D_FF + c0:D_FF + c1]))
        act = act[skip:skip + CONV_CHUNK].astype(BF16)
        part = jnp.dot(act, wd_ref[c0:c1, :], preferred_element_type=F32)
        acc = part if c0 == 0 else acc + part
        if c1 == D_FF:
            o_ref[r0:r0 + CONV_CHUNK, :] = acc.astype(o_ref.dtype)
        return acc

    pending, acc = None, None
    for r0 in range(0, SEQ, CONV_CHUNK):
        for c0, c1 in zip(FFN_CUTS[:-1], FFN_CUTS[1:]):
            current = up(r0, c0, c1)
            if pending is not None:
                acc = down(acc, *pending)
            pending = current
    down(acc, *pending)


def _ffn(hn, w_up, w_conv, w_down, layer):
    b = hn.shape[0]
    once = pl.Buffered(1)
    return pl.pallas_call(
        _ffn_kernel,
        grid=(b,),
        in_specs=[pl.BlockSpec((None, SEQ, D_MODEL), lambda i: (i, 0, 0)),
                  pl.BlockSpec((D_MODEL, 2 * D_FF), lambda i: (0, 0), pipeline_mode=once),
                  _layer_spec(w_conv, layer),
                  pl.BlockSpec((D_FF, D_MODEL), lambda i: (0, 0), pipeline_mode=once)],
        out_specs=pl.BlockSpec((None, SEQ, D_MODEL), lambda i: (i, 0, 0)),
        out_shape=jax.ShapeDtypeStruct((b, SEQ, D_MODEL), BF16),
        compiler_params=_params("arbitrary"),
        name="ffn",
    )(hn, w_up, w_conv, w_down)


def _residual_kernel(x_ref, f_ref, g_ref, gn_ref, o_ref, hn_ref):
    xn = x_ref[...] + _rms(f_ref[...].astype(F32), g_ref[...])
    o_ref[...] = xn
    hn_ref[...] = _rms(xn, gn_ref[...]).astype(BF16)


def _residual_last_kernel(x_ref, f_ref, g_ref, o_ref):
    o_ref[...] = x_ref[...] + _rms(f_ref[...].astype(F32), g_ref[...])


def _residual(x2d, f2d, gains, layer, last):
    n = x2d.shape[0]
    tm = 1024
    rows = pl.BlockSpec((tm, D_MODEL), lambda i: (i, 0))
    x_shape = jax.ShapeDtypeStruct((n, D_MODEL), F32)
    if last:
        return pl.pallas_call(
            _residual_last_kernel,
            grid=(n // tm,),
            in_specs=[rows, rows, _gain_spec(6 * layer + 5)],
            out_specs=rows,
            out_shape=x_shape,
            compiler_params=_params("arbitrary"),
            name="residual_last",
        )(x2d, f2d, gains), None
    return pl.pallas_call(
        _residual_kernel,
        grid=(n // tm,),
        in_specs=[rows, rows, _gain_spec(6 * layer + 5), _gain_spec(6 * (layer + 1))],
        out_specs=(rows, rows),
        out_shape=(x_shape, jax.ShapeDtypeStruct((n, D_MODEL), BF16)),
        compiler_params=_params("arbitrary"),
        name="residual",
    )(x2d, f2d, gains, gains)


def _angle_tables(num, den):
    ang = (num % den).astype(F32) * (2.0 * math.pi / den)
    return jnp.cos(ang), jnp.sin(ang)


def _dft_tables():
    rows = SEQ // FFT_RADIX
    kappa = jnp.arange(rows, dtype=jnp.int32)[None, :, None]
    m = jnp.arange(rows, dtype=jnp.int32)[None, None, :]
    r = jnp.arange(FFT_RADIX, dtype=jnp.int32)[:, None, None]
    c0, s0 = _angle_tables((2 * kappa + 1) * m, 2 * FFT_SUB)
    cr, sr = _angle_tables((2 * kappa + 1) * r, 2 * FFT_N)
    cf = (c0 * cr - s0 * sr).astype(BF16)
    sf = (s0 * cr + c0 * sr).astype(BF16)
    return cf, sf, cf.transpose(0, 2, 1), sf.transpose(0, 2, 1)


def kernel(x, mem, norm_gains, mem_norm, w_in, gate_bias, hy_short_w, hy_w1, hy_b1, hy_w2, hy_b2,
           hy_w3, hy_freq, hy_bias, na_rpb, sc_conv_w, w_branch, w_out, xa_wq, xa_wkv, xa_wo,
           ffn_up, ffn_conv, ffn_down):
    b, l, d = x.shape
    depth = w_in.shape[0]
    assert (l, d) == (SEQ, D_MODEL) and mem.shape[1:] == (N_MEM, D_MODEL)
    n = b * l
    conv_tables = _dft_tables()
    gains = norm_gains.astype(F32).reshape(depth * 6, 1, d)
    mem_gain = mem_norm.astype(F32).reshape(depth, 1, d)
    w1p = jnp.pad(hy_w1.astype(F32), ((0, 0), (0, HY_HIDDEN - HY_EMB), (0, 0)))
    b1 = hy_b1.astype(F32).reshape(depth, 1, HY_HIDDEN)
    b2 = hy_b2.astype(F32).reshape(depth, 1, HY_HIDDEN)
    na_bias = _na_bias(na_rpb)
    w_branch2d = w_branch.reshape(depth, N_BRANCH * HY_W, d)
    x2d = x.reshape(n, d)
    hn = _prenorm(x2d, gains, 0)
    for i in range(depth):
        w_hyena, w_gate, w_br, w_o, wq, wkv, wo, w_up, w_down = _cast_layer(
            i, (w_in, 0, 3 * HY_W), (w_in, 3 * HY_W + 3 * NA_W + 3 * SC_W, N_BRANCH * D_MODEL),
            (w_branch2d, 0, d), (w_out, 0, d), (xa_wq, 0, d), (xa_wkv, 0, 2 * d), (xa_wo, 0, d),
            (ffn_up, 0, 2 * D_FF), (ffn_down, 0, d))
        kr, ki = _hyena_filters(conv_tables, w1p, b1, hy_w2, b2, hy_w3, hy_freq, i)
        hn3 = hn.reshape(b, l, d)
        ya = [_hyena_mixer(hn3, w_hyena, hy_short_w, conv_tables, kr, ki, hy_bias, i, tile)
              for tile in range(HY_W // COL_TILE)]
        yb = _na_mixer(hn3, w_in, i, na_bias)
        yc = _shortconv_mixer(hn3, w_in, i, sc_conv_w)
        x2d = _merge(hn, x2d, ya[0], ya[1], yb.reshape(n, NA_W), yc.reshape(n, SC_W), w_gate, gate_bias,
                     w_br.reshape(N_BRANCH, HY_W, d), w_o, gains, i)
        kv = _mem_kv(mem, mem_gain, i, wkv)
        x3, hn2 = _xattn(x2d.reshape(b, l, d), kv, wq, wo, gains, i)
        f = _ffn(hn2, w_up, ffn_conv, w_down, i)
        x2d, hn = _residual(x3.reshape(n, d), f.reshape(n, d), gains, i, i + 1 == depth)
    return x2d.reshape(b, l, d)
```

```python
import functools
import math

import jax
import jax.numpy as jnp
from jax import lax
from jax.experimental import pallas as pl
from jax.experimental.pallas import tpu as pltpu

D_MODEL = 1024
SEQ = 2048
N_MEM = 256
GRID_W = 64
HY_W = 512
NA_HEADS = 8
NA_HEAD_DIM = 64
NA_W = NA_HEADS * NA_HEAD_DIM
NA_WIN_ROWS = 8
NA_WIN_COLS = 16
SC_W = 512
XA_HEADS = 4
XA_HEAD_DIM = D_MODEL // XA_HEADS
D_FF = 2816
HY_ORDER = 2
HY_EMB = 33
HY_HIDDEN = 64
HY_FAST_DECAY = 0.3
HY_SLOW_DECAY = 1.5
HY_TARGET = 1e-2
N_BRANCH = 3
EPS = 1e-6

FFT_N = 2 * SEQ
FFT_RADIX = 4
FFT_SUB = FFT_N // FFT_RADIX
NA_ROWS = SEQ // GRID_W
NA_KEYS = NA_WIN_ROWS * GRID_W
NA_GROUP = 4
NA_ROW_GROUP = 8
LANES = 128
COL_TILE = 256
ROW_TILE = 512
MERGE_ROWS = 2 * ROW_TILE
XA_ROWS = 2 * ROW_TILE
XA_CHUNK = 256
CAST_STEPS = 8
CONV_PAD = 8
CONV_CHUNK = 512
FFN_CUTS = (0, D_FF)
CONV_HALO = 16
VMEM_LIMIT = 60 * 1024 * 1024

BF16 = jnp.bfloat16
F32 = jnp.float32


def _params(*sem):
    return pltpu.CompilerParams(dimension_semantics=sem, vmem_limit_bytes=VMEM_LIMIT)


def _gain_spec(index):
    return pl.BlockSpec((None, 1, D_MODEL), lambda *_: (index, 0, 0))


def _layer_spec(arr, layer, width=None, col=0):
    _, r, c = arr.shape
    return pl.BlockSpec((None, r, c if width is None else width), lambda *_: (layer, 0, col))


def _rms(xf, g):
    ms = jnp.mean(xf * xf, axis=-1, keepdims=True)
    return xf * lax.rsqrt(ms + EPS) * g


def _dwconv3(u, w_ref):
    n = u.shape[0]
    zeros = jnp.zeros((CONV_PAD, u.shape[1]), F32)
    padded = jnp.concatenate([zeros, u, zeros], axis=0)
    m = n + 2 * CONV_PAD
    prev = pltpu.roll(padded, 1, 0)[CONV_PAD:CONV_PAD + n]
    nxt = pltpu.roll(padded, m - 1, 0)[CONV_PAD:CONV_PAD + n]
    return prev * w_ref[0:1, :] + u * w_ref[1:2, :] + nxt * w_ref[2:3, :]


def _prenorm_kernel(x_ref, g_ref, o_ref):
    o_ref[...] = _rms(x_ref[...], g_ref[...]).astype(BF16)


def _prenorm(x2d, gains, gi):
    n = x2d.shape[0]
    tm = 1024
    return pl.pallas_call(
        _prenorm_kernel,
        grid=(n // tm,),
        in_specs=[pl.BlockSpec((tm, D_MODEL), lambda i: (i, 0)),
                  _gain_spec(gi)],
        out_specs=pl.BlockSpec((tm, D_MODEL), lambda i: (i, 0)),
        out_shape=jax.ShapeDtypeStruct((n, D_MODEL), BF16),
        compiler_params=_params("arbitrary"),
        name="prenorm",
    )(x2d, gains)


def _cast_kernel(parts, *refs):
    n_in = sum(parts)
    w_refs, o_refs = refs[:n_in], refs[n_in:]
    k = 0
    for o_ref, n in zip(o_refs, parts):
        width = o_ref.shape[1] // n
        for p in range(n):
            o_ref[:, p * width:(p + 1) * width] = w_refs[k][...].astype(BF16)
            k += 1


def _cast_layer(layer, *weights):
    in_specs, out_specs, out_shapes, operands, parts = [], [], [], [], []
    for w, col0, ncols in weights:
        rb = w.shape[1] // CAST_STEPS
        cb = math.gcd(col0, ncols) if col0 else ncols
        assert w.shape[1] % CAST_STEPS == 0 and rb % 16 == 0 and cb % LANES == 0
        parts.append(ncols // cb)
        for p in range(ncols // cb):
            in_specs.append(pl.BlockSpec((None, rb, cb),
                                         lambda i, cblk=col0 // cb + p: (layer, i, cblk)))
            operands.append(w)
        out_specs.append(pl.BlockSpec((rb, ncols), lambda i: (i, 0)))
        out_shapes.append(jax.ShapeDtypeStruct((w.shape[1], ncols), BF16))
    return pl.pallas_call(
        functools.partial(_cast_kernel, tuple(parts)),
        grid=(CAST_STEPS,),
        in_specs=in_specs,
        out_specs=tuple(out_specs),
        out_shape=tuple(out_shapes),
        compiler_params=_params("arbitrary"),
        name="cast_bf16",
    )(*operands)


def _filter_mlp_kernel(z_ref, w1_ref, b1_ref, w2_ref, b2_ref, w3_ref, f_ref, t_ref, dl_ref,
                       hs_ref, hd_ref):
    hp = lax.Precision.HIGHEST
    h = jnp.sin(f_ref[0:1, :] * (jnp.dot(z_ref[...], w1_ref[...], precision=hp) + b1_ref[...]))
    h = jnp.sin(f_ref[1:2, :] * (jnp.dot(h, w2_ref[...], precision=hp) + b2_ref[...]))
    decay = jnp.exp(-t_ref[...] * dl_ref[...])
    row = lax.broadcasted_iota(jnp.int32, (SEQ, HY_W), 0)
    h_hi = h.astype(BF16)
    h_lo = (h - h_hi.astype(F32)).astype(BF16)

    def out_layer(cols):
        w = w3_ref[:, cols]
        w_hi = w.astype(BF16)
        w_lo = (w - w_hi.astype(F32)).astype(BF16)
        return (jnp.dot(h_hi, w_hi, preferred_element_type=F32)
                + (jnp.dot(h_hi, w_lo, preferred_element_type=F32)
                   + jnp.dot(h_lo, w_hi, preferred_element_type=F32)))

    for o in range(HY_ORDER):
        c_f = o * HY_W
        c_b = HY_ORDER * HY_W + o * HY_W
        hf = out_layer(slice(c_f, c_f + HY_W)) * decay
        hb = out_layer(slice(c_b, c_b + HY_W)) * decay
        hb = jnp.where(row == 0, 0.0, hb)
        hs_ref[:, c_f:c_f + HY_W] = (hf + hb).astype(BF16)
        hd_ref[:, c_f:c_f + HY_W] = (hb - hf).astype(BF16)


def _filter_dft_kernel(cf_ref, sf_ref, hs_ref, hd_ref, kr_ref, ki_ref):
    rows = SEQ // FFT_RADIX
    scale = 2.0 / FFT_N
    sub = lambda ref, r: ref[r * rows:(r + 1) * rows, :]
    cos_dot = lambda ref, r: jnp.dot(cf_ref[r], sub(ref, r), preferred_element_type=F32)
    sin_dot = lambda ref, r: jnp.dot(sf_ref[r], sub(ref, r), preferred_element_type=F32)
    c = [cos_dot(hs_ref, r) for r in range(FFT_RADIX)]
    a, b, e = c[0] + c[2], c[0] - c[2], c[1] + c[3]
    d_im = sin_dot(hs_ref, 3) - sin_dot(hs_ref, 1)
    for f, val in enumerate((a + e, b + d_im, b - d_im, a - e)):
        kr_ref[f] = val * scale
    s = [sin_dot(hd_ref, r) for r in range(FFT_RADIX)]
    a, b, e = s[0] + s[2], s[0] - s[2], s[1] + s[3]
    d_re = cos_dot(hd_ref, 1) - cos_dot(hd_ref, 3)
    for f, val in enumerate((a + e, b + d_re, d_re - b, e - a)):
        ki_ref[f] = val * scale


def _hyena_filters(tables, w1p, b1, w2, b2, w3, freq, layer):
    t = jnp.linspace(0.0, 1.0, SEQ, dtype=F32)[:, None]
    bands = (HY_EMB - 1) // 2
    w = 2.0 * math.pi * jnp.arange(SEQ, dtype=F32)[:, None] / SEQ
    f = jnp.linspace(1e-4, bands - 1, bands, dtype=F32)[None, :]
    z = jnp.concatenate([t, jnp.cos(f * w), -jnp.sin(f * w)], axis=-1)
    z = jnp.pad(z, ((0, 0), (0, HY_HIDDEN - HY_EMB)))
    rows = SEQ // FFT_RADIX
    decimate = lambda a: a.reshape(rows, FFT_RADIX, -1).transpose(1, 0, 2).reshape(SEQ, -1)
    z, t = decimate(z), decimate(t)
    deltas = jnp.abs(jnp.linspace(math.log(HY_TARGET) / HY_SLOW_DECAY,
                                  math.log(HY_TARGET) / HY_FAST_DECAY, HY_W, dtype=F32))[None, :]
    width = HY_ORDER * HY_W
    whole = lambda a: pl.BlockSpec(a.shape, lambda i: (0,) * a.ndim)
    taps = pl.BlockSpec((SEQ, width), lambda i: (0, 0))
    hs, hd = pl.pallas_call(
        _filter_mlp_kernel,
        grid=(1,),
        in_specs=[whole(z)] + [_layer_spec(a, layer) for a in (w1p, b1, w2, b2, w3, freq)]
        + [whole(t), whole(deltas)],
        out_specs=(taps, taps),
        out_shape=(jax.ShapeDtypeStruct((SEQ, width), BF16),
                   jax.ShapeDtypeStruct((SEQ, width), BF16)),
        compiler_params=_params("arbitrary"),
        name="hyena_filter_mlp",
    )(z, w1p, b1, w2, b2, w3, freq, t, deltas)
    cf, sf = tables[:2]
    small = pl.BlockSpec((FFT_RADIX, rows, rows), lambda j: (0, 0, 0))
    col = pl.BlockSpec((SEQ, COL_TILE), lambda j: (0, j))
    fam = pl.BlockSpec((FFT_RADIX, rows, COL_TILE), lambda j: (0, 0, j))
    fam_shape = jax.ShapeDtypeStruct((FFT_RADIX, rows, width), F32)
    return pl.pallas_call(
        _filter_dft_kernel,
        grid=(width // COL_TILE,),
        in_specs=[small, small, col, col],
        out_specs=(fam, fam),
        out_shape=(fam_shape, fam_shape),
        compiler_params=_params("arbitrary"),
        name="hyena_filter_dft",
    )(cf, sf, hs, hd)


def _hyena_kernel(hn_ref, wv_ref, w1_ref, w2_ref, sv_ref, s1_ref, s2_ref, cf_ref, sf_ref, ct_ref,
                  st_ref, kr0_ref, ki0_ref, kr1_ref, ki1_ref, bias_ref, o_ref,
                  slab_v_ref, slab_1_ref, slab_2_ref, slab_o_ref):
    radix = FFT_RADIX
    rows = SEQ // radix
    n_slab = COL_TILE // LANES
    zero_row = jnp.zeros((CONV_PAD, COL_TILE), F32)

    def project(w_ref, slab_ref):
        half = SEQ // 2
        for top in (0, half):
            u = jnp.dot(hn_ref[top:top + half, :], w_ref[...], preferred_element_type=F32)
            for j in range(n_slab):
                slab_ref[j, top:top + half, :] = u[:, j * LANES:(j + 1) * LANES]
        return [jnp.concatenate([slab_ref[j, pl.ds(r, rows, stride=radix), :] for j in range(n_slab)],
                                axis=1) for r in range(radix)]

    def short_conv(u, w_ref):
        n = u[0].shape[0]
        prev_wrap = pltpu.roll(jnp.concatenate([u[-1], zero_row], axis=0), 1, 0)[:n]
        next_wrap = pltpu.roll(jnp.concatenate([zero_row, u[0]], axis=0), n + CONV_PAD - 1, 0)[CONV_PAD:]
        prev = [prev_wrap] + u[:-1]
        nxt = u[1:] + [next_wrap]
        return [prev[r] * w_ref[0:1, :] + u[r] * w_ref[1:2, :] + nxt[r] * w_ref[2:3, :]
                for r in range(radix)]

    def cmul(ar, ai, br, bi):
        return ar * br - ai * bi, ar * bi + ai * br

    def transform(x):
        xb = [v.astype(BF16) for v in x]
        tr = [jnp.dot(cf_ref[r], xb[r], preferred_element_type=F32) for r in range(radix)]
        ti = [-jnp.dot(sf_ref[r], xb[r], preferred_element_type=F32) for r in range(radix)]
        return tr, ti

    def filter_and_invert(tr, ti, kr_ref, ki_ref):
        ar, ai = tr[0] + tr[2], ti[0] + ti[2]
        br, bi = tr[0] - tr[2], ti[0] - ti[2]
        cr, ci = tr[1] + tr[3], ti[1] + ti[3]
        dr, di = tr[1] - tr[3], ti[1] - ti[3]
        fam = [(ar + cr, ai + ci), (br + di, bi - dr), (br - di, -bi - dr), (ar - cr, ci - ai)]
        y = [cmul(fr, fi, kr_ref[f], ki_ref[f]) for f, (fr, fi) in enumerate(fam)]
        er, ei = y[0][0] + y[3][0], y[0][1] - y[3][1]
        fr, fi = y[0][0] - y[3][0], y[0][1] + y[3][1]
        gr, gi = y[1][0] + y[2][0], y[1][1] - y[2][1]
        hr, hi = y[1][0] - y[2][0], y[1][1] + y[2][1]
        p = [(er + gr, ei + gi), (fr - hi, fi + hr), (er - gr, ei - gi), (fr + hi, fi - hr)]
        out = []
        for r in range(radix):
            pr, pi_ = p[r]
            out.append(jnp.dot(ct_ref[r], pr.astype(BF16), preferred_element_type=F32)
                       - jnp.dot(st_ref[r], pi_.astype(BF16), preferred_element_type=F32))
        return out

    v = short_conv(project(wv_ref, slab_v_ref), sv_ref)
    spectrum = transform(v)
    x1 = short_conv(project(w1_ref, slab_1_ref), s1_ref)
    y = filter_and_invert(*spectrum, kr0_ref, ki0_ref)
    z = [x1[r] * (y[r] + v[r] * bias_ref[0:1, :]) for r in range(radix)]
    spectrum = transform(z)
    x2 = short_conv(project(w2_ref, slab_2_ref), s2_ref)
    y = filter_and_invert(*spectrum, kr1_ref, ki1_ref)
    for r in range(radix):
        out = x2[r] * (y[r] + z[r] * bias_ref[1:2, :])
        for j in range(n_slab):
            slab_o_ref[j, pl.ds(r, rows, stride=radix), :] = out[:, j * LANES:(j + 1) * LANES]
    o_ref[...] = jnp.concatenate([slab_o_ref[j] for j in range(n_slab)], axis=1).astype(BF16)


def _hyena_mixer(hn, w_hy, short_w, tables, kr, ki, bias, layer, tile):
    b = hn.shape[0]
    nt = HY_W // COL_TILE
    rows = SEQ // FFT_RADIX
    once = pl.Buffered(1)
    cf, sf, ct, st = tables

    def fixed(shape, index):
        return pl.BlockSpec(shape, lambda i: index, pipeline_mode=once)

    def wcol(k):
        return fixed((D_MODEL, COL_TILE), (0, k * nt + tile))

    def scol(k):
        return _layer_spec(short_w, layer, COL_TILE, k * nt + tile)

    def kfam(o):
        return fixed((FFT_RADIX, rows, COL_TILE), (0, 0, o * nt + tile))

    small = fixed((FFT_RADIX, rows, rows), (0, 0, 0))
    slab = pltpu.VMEM((COL_TILE // LANES, SEQ, LANES), F32)
    out = pl.pallas_call(
        _hyena_kernel,
        grid=(b,),
        in_specs=[pl.BlockSpec((None, SEQ, D_MODEL), lambda i: (i, 0, 0)),
                  wcol(0), wcol(1), wcol(2), scol(0), scol(1), scol(2),
                  small, small, small, small,
                  kfam(0), kfam(0), kfam(1), kfam(1),
                  _layer_spec(bias, layer, COL_TILE, tile)],
        out_specs=pl.BlockSpec((None, SEQ, COL_TILE), lambda i: (i, 0, 0)),
        out_shape=jax.ShapeDtypeStruct((b, SEQ, COL_TILE), BF16),
        scratch_shapes=[slab, slab, slab, slab],
        compiler_params=_params("arbitrary"),
        name="hyena_mixer",
    )(hn, w_hy, w_hy, w_hy, short_w, short_w, short_w, cf, sf, ct, st, kr, ki, kr, ki, bias)
    return out.reshape(b * SEQ, COL_TILE)


def _shortconv_kernel(hn_ref, wb_ref, wc_ref, wx_ref, cw_ref, o_ref):
    wb = wb_ref[...].astype(BF16)
    wc = wc_ref[...].astype(BF16)
    wx = wx_ref[...].astype(BF16)

    def project(r0):
        lo = max(r0 - CONV_HALO, 0)
        hi = min(r0 + CONV_CHUNK + CONV_HALO, SEQ)
        hn = hn_ref[lo:hi, :]
        return (r0, r0 - lo, jnp.dot(hn, wb, preferred_element_type=F32),
                jnp.dot(hn, wc, preferred_element_type=F32), jnp.dot(hn, wx, preferred_element_type=F32))

    def finish(r0, skip, bg, cg, xi):
        out = bg * _dwconv3(cg * xi, cw_ref)
        o_ref[r0:r0 + CONV_CHUNK, :] = out[skip:skip + CONV_CHUNK].astype(BF16)

    pending = None
    for r0 in range(0, SEQ, CONV_CHUNK):
        current = project(r0)
        if pending is not None:
            finish(*pending)
        pending = current
    finish(*pending)


def _w_in_cols(layer, width, first):
    return lambda k, nt: pl.BlockSpec((None, D_MODEL, width),
                                      lambda j, i: (layer, 0, first + k * nt + j))


def _shortconv_mixer(hn, w_in, layer, conv_w):
    b = hn.shape[0]
    nt = SC_W // COL_TILE
    wcol = _w_in_cols(layer, COL_TILE, (3 * HY_W + 3 * NA_W) // COL_TILE)
    return pl.pallas_call(
        _shortconv_kernel,
        grid=(nt, b),
        in_specs=[pl.BlockSpec((None, SEQ, D_MODEL), lambda j, i: (i, 0, 0)),
                  wcol(0, nt), wcol(1, nt), wcol(2, nt),
                  pl.BlockSpec((None, 3, COL_TILE), lambda j, i: (layer, 0, j))],
        out_specs=pl.BlockSpec((None, SEQ, COL_TILE), lambda j, i: (i, 0, j)),
        out_shape=jax.ShapeDtypeStruct((b, SEQ, SC_W), BF16),
        compiler_params=_params("arbitrary", "arbitrary"),
        name="shortconv_mixer",
    )(hn, w_in, w_in, w_in, conv_w)


def _na_kernel(hn_ref, wq_ref, wk_ref, wv_ref, bias_ref, o_ref, q_ref, k_ref, v_ref, s_ref):
    hn = hn_ref[...]
    q = jnp.dot(hn, wq_ref[...].astype(BF16), preferred_element_type=F32)
    q_ref[...] = (q * (NA_HEAD_DIM ** -0.5)).astype(BF16)
    k_ref[...] = jnp.dot(hn, wk_ref[...].astype(BF16), preferred_element_type=F32).astype(BF16)
    v_ref[...] = jnp.dot(hn, wv_ref[...].astype(BF16), preferred_element_type=F32).astype(BF16)
    gw = NA_GROUP * NA_HEAD_DIM
    same_head = (lax.broadcasted_iota(jnp.int32, (gw, gw), 0) // NA_HEAD_DIM
                 == lax.broadcasted_iota(jnp.int32, (gw, gw), 1) // NA_HEAD_DIM)

    def scores(r):
        w0 = jnp.clip(r - NA_WIN_ROWS // 2, 0, NA_ROWS - NA_WIN_ROWS)
        off = w0 - r + (NA_WIN_ROWS - 1)
        q0 = pl.multiple_of(r * GRID_W, GRID_W)
        k0 = pl.multiple_of(w0 * GRID_W, GRID_W)
        q_row = q_ref[pl.ds(q0, GRID_W), :]
        q_heads = jnp.where(same_head, jnp.concatenate([q_row] * NA_GROUP, axis=0), 0)
        s = lax.dot_general(q_heads, k_ref[pl.ds(k0, NA_KEYS), :], (((1,), (1,)), ((), ())),
                            preferred_element_type=F32)
        bias = jnp.concatenate(
            [jnp.concatenate([bias_ref[h, off + 2 * m] for m in range(NA_WIN_ROWS // 2)], axis=1)
             for h in range(NA_GROUP)], axis=0)
        return s + bias

    def attend(r, s):
        w0 = jnp.clip(r - NA_WIN_ROWS // 2, 0, NA_ROWS - NA_WIN_ROWS)
        q0 = pl.multiple_of(r * GRID_W, GRID_W)
        k0 = pl.multiple_of(w0 * GRID_W, GRID_W)
        p = jnp.exp(s - jnp.max(s, axis=-1, keepdims=True))
        inv = 1.0 / jnp.sum(p, axis=-1, keepdims=True)
        pv = jnp.dot(p.astype(BF16), v_ref[pl.ds(k0, NA_KEYS), :], preferred_element_type=F32)
        pv = jnp.where(same_head, pv * inv, 0.0)
        out = pv[0:GRID_W]
        for h in range(1, NA_GROUP):
            out = out + pv[h * GRID_W:(h + 1) * GRID_W]
        o_ref[pl.ds(q0, GRID_W), :] = out.astype(BF16)

    groups = NA_ROWS // NA_ROW_GROUP
    for t in range(NA_ROW_GROUP):
        s_ref[t] = scores(jnp.int32(t))

    def rows_body(i, carry):
        for t in range(NA_ROW_GROUP):
            s_next = scores(i * NA_ROW_GROUP + t)
            attend((i - 1) * NA_ROW_GROUP + t, s_ref[t])
            s_ref[t] = s_next
        return carry

    lax.fori_loop(1, groups, rows_body, 0)
    for t in range(NA_ROW_GROUP):
        attend(jnp.int32((groups - 1) * NA_ROW_GROUP + t), s_ref[t])


def _na_bias(rpb):
    c = jnp.arange(GRID_W)
    col_start = jnp.clip(c - NA_WIN_COLS // 2, 0, GRID_W - NA_WIN_COLS)
    col_mask = (c[None, :] >= col_start[:, None]) & (c[None, :] < col_start[:, None] + NA_WIN_COLS)
    dc = jnp.clip(c[None, :] - c[:, None] + NA_WIN_COLS - 1, 0, 2 * NA_WIN_COLS - 2)
    pick = (dc[None] == jnp.arange(2 * NA_WIN_COLS - 1)[:, None, None]).astype(F32)
    table = jnp.einsum("lhrd,dqc->lhrqc", rpb.astype(F32), pick, precision=lax.Precision.HIGHEST)
    table = table + jnp.where(col_mask, 0.0, -1e30)
    return jnp.concatenate([table[:, :, :-1], table[:, :, 1:]], axis=-1)


def _na_mixer(hn, w_in, layer, bias):
    b = hn.shape[0]
    gw = NA_GROUP * NA_HEAD_DIM
    ng = NA_W // gw
    wcol = _w_in_cols(layer, gw, 3 * HY_W // gw)
    return pl.pallas_call(
        _na_kernel,
        grid=(ng, b),
        in_specs=[pl.BlockSpec((None, SEQ, D_MODEL), lambda j, i: (i, 0, 0)),
                  wcol(0, ng), wcol(1, ng), wcol(2, ng),
                  pl.BlockSpec((None, NA_GROUP, 2 * NA_WIN_ROWS - 2, GRID_W, 2 * GRID_W),
                               lambda j, i: (layer, j, 0, 0, 0))],
        out_specs=pl.BlockSpec((None, SEQ, gw), lambda j, i: (i, 0, j)),
        out_shape=jax.ShapeDtypeStruct((b, SEQ, NA_W), BF16),
        scratch_shapes=[pltpu.VMEM((SEQ, gw), BF16) for _ in range(3)]
        + [pltpu.VMEM((NA_ROW_GROUP, gw, NA_KEYS), F32)],
        compiler_params=_params("arbitrary", "arbitrary"),
        name="na_mixer",
    )(hn, w_in, w_in, w_in, bias)


def _merge_kernel(hn_ref, x_ref, ya0_ref, ya1_ref, yb_ref, yc_ref, wg_ref, gb_ref, wb_ref, wo_ref,
                  g_ref, o_ref):
    def gated_sum(rows):
        hn = hn_ref[rows, :]
        ya = jnp.concatenate([ya0_ref[rows, :], ya1_ref[rows, :]], axis=1)
        merged = None
        for i, y in enumerate((ya, yb_ref[rows, :], yc_ref[rows, :])):
            pre = jnp.dot(hn, wg_ref[:, i * D_MODEL:(i + 1) * D_MODEL], preferred_element_type=F32)
            gate = jax.nn.sigmoid(pre + gb_ref[i:i + 1, :])
            term = gate * jnp.dot(y, wb_ref[i], preferred_element_type=F32)
            merged = term if merged is None else merged + term
        return merged.astype(BF16)

    chunks = [slice(r, r + ROW_TILE) for r in range(0, MERGE_ROWS, ROW_TILE)]
    merged = [gated_sum(rows) for rows in chunks]
    for rows, m in zip(chunks, merged):
        out = jnp.dot(m, wo_ref[...], preferred_element_type=F32)
        o_ref[rows, :] = x_ref[rows, :] + _rms(out, g_ref[...])


def _merge(hn2d, x2d, ya0, ya1, yb, yc, w_gate, gate_bias, w_branch, w_out, gains, layer):
    n = x2d.shape[0]
    tm = MERGE_ROWS
    once = pl.Buffered(1)
    rows = lambda w: pl.BlockSpec((tm, w), lambda i: (i, 0))
    return pl.pallas_call(
        _merge_kernel,
        grid=(n // tm,),
        in_specs=[rows(D_MODEL), rows(D_MODEL), rows(COL_TILE), rows(COL_TILE), rows(NA_W), rows(SC_W),
                  pl.BlockSpec((D_MODEL, N_BRANCH * D_MODEL), lambda i: (0, 0), pipeline_mode=once),
                  _layer_spec(gate_bias, layer),
                  pl.BlockSpec((N_BRANCH, HY_W, D_MODEL), lambda i: (0, 0, 0), pipeline_mode=once),
                  pl.BlockSpec((D_MODEL, D_MODEL), lambda i: (0, 0), pipeline_mode=once),
                  _gain_spec(6 * layer + 1)],
        out_specs=rows(D_MODEL),
        out_shape=jax.ShapeDtypeStruct((n, D_MODEL), F32),
        compiler_params=_params("arbitrary"),
        name="merge",
    )(hn2d, x2d, ya0, ya1, yb, yc, w_gate, gate_bias, w_branch, w_out, gains)


def _kv_kernel(m_ref, g_ref, w_ref, o_ref):
    mn = _rms(m_ref[...], g_ref[...]).astype(BF16)
    o_ref[...] = jnp.dot(mn, w_ref[...], preferred_element_type=F32).astype(BF16)


def _mem_kv(mem, mem_norm, layer, wkv):
    b = mem.shape[0]
    return pl.pallas_call(
        _kv_kernel,
        grid=(b,),
        in_specs=[pl.BlockSpec((None, N_MEM, D_MODEL), lambda i: (i, 0, 0)),
                  _layer_spec(mem_norm, layer),
                  pl.BlockSpec((D_MODEL, 2 * D_MODEL), lambda i: (0, 0))],
        out_specs=pl.BlockSpec((None, N_MEM, 2 * D_MODEL), lambda i: (i, 0, 0)),
        out_shape=jax.ShapeDtypeStruct((b, N_MEM, 2 * D_MODEL), BF16),
        compiler_params=_params("arbitrary"),
        name="mem_kv",
    )(mem, mem_norm, wkv)


def _xattn_kernel(x_ref, kv_ref, wq_ref, wo_ref, gq_ref, go_ref, gn_ref, o_ref, hn_ref):
    chunks = [slice(r, r + XA_CHUNK) for r in range(0, XA_ROWS, XA_CHUNK)]
    head_cols = [slice(i * XA_HEAD_DIM, (i + 1) * XA_HEAD_DIM) for i in range(XA_HEADS)]

    def query(rows):
        h = _rms(x_ref[rows, :], gq_ref[...]).astype(BF16)
        q = jnp.dot(h, wq_ref[...], preferred_element_type=F32) * (XA_HEAD_DIM ** -0.5)
        return q.astype(BF16)

    def scores(q):
        return [lax.dot_general(q[:, sl], kv_ref[:, sl], (((1,), (1,)), ((), ())),
                                preferred_element_type=F32) for sl in head_cols]

    def values(s_heads):
        heads = []
        for i, s in enumerate(s_heads):
            vm = kv_ref[:, D_MODEL + i * XA_HEAD_DIM:D_MODEL + (i + 1) * XA_HEAD_DIM]
            p = jnp.exp(s - jnp.max(s, axis=-1, keepdims=True))
            den = jnp.sum(p, axis=-1, keepdims=True)
            heads.append((jnp.dot(p.astype(BF16), vm, preferred_element_type=F32) / den).astype(BF16))
        return jnp.concatenate(heads, axis=-1)

    s_all = [scores(q) for q in [query(rows) for rows in chunks]]
    attended = [values(s) for s in s_all]
    for rows, a in zip(chunks, attended):
        o = jnp.dot(a, wo_ref[...], preferred_element_type=F32)
        xn = x_ref[rows, :] + _rms(o, go_ref[...])
        o_ref[rows, :] = xn
        hn_ref[rows, :] = _rms(xn, gn_ref[...]).astype(BF16)


def _xattn(x, kv, wq, wo, gains, layer):
    b = x.shape[0]
    tm = XA_ROWS
    once = pl.Buffered(1)
    rows = pl.BlockSpec((None, tm, D_MODEL), lambda i, j: (i, j, 0))
    wfull = pl.BlockSpec((D_MODEL, D_MODEL), lambda i, j: (0, 0), pipeline_mode=once)
    return pl.pallas_call(
        _xattn_kernel,
        grid=(b, SEQ // tm),
        in_specs=[rows, pl.BlockSpec((None, N_MEM, 2 * D_MODEL), lambda i, j: (i, 0, 0)),
                  wfull, wfull, _gain_spec(6 * layer + 2), _gain_spec(6 * layer + 3),
                  _gain_spec(6 * layer + 4)],
        out_specs=(rows, rows),
        out_shape=(jax.ShapeDtypeStruct((b, SEQ, D_MODEL), F32),
                   jax.ShapeDtypeStruct((b, SEQ, D_MODEL), BF16)),
        compiler_params=_params("arbitrary", "arbitrary"),
        name="xattn",
    )(x, kv, wq, wo, gains, gains, gains)


def _gelu_tanh(x):
    c = math.sqrt(2.0 / math.pi)
    half = 0.5 * x
    return half + half * jnp.tanh(x * (c + (c * 0.044715) * (x * x)))


def _ffn_kernel(hn_ref, wu_ref, cw_ref, wd_ref, o_ref):
    def up(r0, c0, c1):
        lo = max(r0 - CONV_HALO, 0)
        hi = min(r0 + CONV_CHUNK + CONV_HALO, SEQ)
        hn = hn_ref[lo:hi, :]
        ug = jnp.dot(hn, wu_ref[:, c0:c1], preferred_element_type=F32)
        uv = jnp.dot(hn, wu_ref[:, D_FF + c0:D_FF + c1], preferred_element_type=F32)
        return r0, r0 - lo, c0, c1, ug, uv

    def down(acc, r0, skip, c0, c1, ug, uv):
        act = (_gelu_tanh(_dwconv3(ug, cw_ref.at[:, c0:c1]))
               * _dwconv3(uv, cw_ref.at[:, D_FF + c0:D_FF + c1]))
        act = act[skip:skip + CONV_CHUNK].astype(BF16)
        part = jnp.dot(act, wd_ref[c0:c1, :], preferred_element_type=F32)
        acc = part if c0 == 0 else acc + part
        if c1 == D_FF:
            o_ref[r0:r0 + CONV_CHUNK, :] = acc.astype(o_ref.dtype)
        return acc

    pending, acc = None, None
    for r0 in range(0, SEQ, CONV_CHUNK):
        for c0, c1 in zip(FFN_CUTS[:-1], FFN_CUTS[1:]):
            current = up(r0, c0, c1)
            if pending is not None:
                acc = down(acc, *pending)
            pending = current
    down(acc, *pending)


def _ffn(hn, w_up, w_conv, w_down, layer):
    b = hn.shape[0]
    once = pl.Buffered(1)
    return pl.pallas_call(
        _ffn_kernel,
        grid=(b,),
        in_specs=[pl.BlockSpec((None, SEQ, D_MODEL), lambda i: (i, 0, 0)),
                  pl.BlockSpec((D_MODEL, 2 * D_FF), lambda i: (0, 0), pipeline_mode=once),
                  _layer_spec(w_conv, layer),
                  pl.BlockSpec((D_FF, D_MODEL), lambda i: (0, 0), pipeline_mode=once)],
        out_specs=pl.BlockSpec((None, SEQ, D_MODEL), lambda i: (i, 0, 0)),
        out_shape=jax.ShapeDtypeStruct((b, SEQ, D_MODEL), BF16),
        compiler_params=_params("arbitrary"),
        name="ffn",
    )(hn, w_up, w_conv, w_down)


def _residual_kernel(x_ref, f_ref, g_ref, gn_ref, o_ref, hn_ref):
    xn = x_ref[...] + _rms(f_ref[...].astype(F32), g_ref[...])
    o_ref[...] = xn
    hn_ref[...] = _rms(xn, gn_ref[...]).astype(BF16)


def _residual_last_kernel(x_ref, f_ref, g_ref, o_ref):
    o_ref[...] = x_ref[...] + _rms(f_ref[...].astype(F32), g_ref[...])


def _residual(x2d, f2d, gains, layer, last):
    n = x2d.shape[0]
    tm = 1024
    rows = pl.BlockSpec((tm, D_MODEL), lambda i: (i, 0))
    x_shape = jax.ShapeDtypeStruct((n, D_MODEL), F32)
    if last:
        return pl.pallas_call(
            _residual_last_kernel,
            grid=(n // tm,),
            in_specs=[rows, rows, _gain_spec(6 * layer + 5)],
            out_specs=rows,
            out_shape=x_shape,
            compiler_params=_params("arbitrary"),
            name="residual_last",
        )(x2d, f2d, gains), None
    return pl.pallas_call(
        _residual_kernel,
        grid=(n // tm,),
        in_specs=[rows, rows, _gain_spec(6 * layer + 5), _gain_spec(6 * (layer + 1))],
        out_specs=(rows, rows),
        out_shape=(x_shape, jax.ShapeDtypeStruct((n, D_MODEL), BF16)),
        compiler_params=_params("arbitrary"),
        name="residual",
    )(x2d, f2d, gains, gains)


def _angle_tables(num, den):
    ang = (num % den).astype(F32) * (2.0 * math.pi / den)
    return jnp.cos(ang), jnp.sin(ang)


def _dft_tables():
    rows = SEQ // FFT_RADIX
    kappa = jnp.arange(rows, dtype=jnp.int32)[None, :, None]
    m = jnp.arange(rows, dtype=jnp.int32)[None, None, :]
    r = jnp.arange(FFT_RADIX, dtype=jnp.int32)[:, None, None]
    c0, s0 = _angle_tables((2 * kappa + 1) * m, 2 * FFT_SUB)
    cr, sr = _angle_tables((2 * kappa + 1) * r, 2 * FFT_N)
    cf = (c0 * cr - s0 * sr).astype(BF16)
    sf = (s0 * cr + c0 * sr).astype(BF16)
    return cf, sf, cf.transpose(0, 2, 1), sf.transpose(0, 2, 1)


def kernel(x, mem, norm_gains, mem_norm, w_in, gate_bias, hy_short_w, hy_w1, hy_b1, hy_w2, hy_b2,
           hy_w3, hy_freq, hy_bias, na_rpb, sc_conv_w, w_branch, w_out, xa_wq, xa_wkv, xa_wo,
           ffn_up, ffn_conv, ffn_down):
    b, l, d = x.shape
    depth = w_in.shape[0]
    assert (l, d) == (SEQ, D_MODEL) and mem.shape[1:] == (N_MEM, D_MODEL)
    n = b * l
    conv_tables = _dft_tables()
    gains = norm_gains.astype(F32).reshape(depth * 6, 1, d)
    mem_gain = mem_norm.astype(F32).reshape(depth, 1, d)
    w1p = jnp.pad(hy_w1.astype(F32), ((0, 0), (0, HY_HIDDEN - HY_EMB), (0, 0)))
    b1 = hy_b1.astype(F32).reshape(depth, 1, HY_HIDDEN)
    b2 = hy_b2.astype(F32).reshape(depth, 1, HY_HIDDEN)
    na_bias = _na_bias(na_rpb)
    w_branch2d = w_branch.reshape(depth, N_BRANCH * HY_W, d)
    x2d = x.reshape(n, d)
    hn = _prenorm(x2d, gains, 0)
    for i in range(depth):
        w_hyena, w_gate, w_br, w_o, wq, wkv, wo, w_up, w_down = _cast_layer(
            i, (w_in, 0, 3 * HY_W), (w_in, 3 * HY_W + 3 * NA_W + 3 * SC_W, N_BRANCH * D_MODEL),
            (w_branch2d, 0, d), (w_out, 0, d), (xa_wq, 0, d), (xa_wkv, 0, 2 * d), (xa_wo, 0, d),
            (ffn_up, 0, 2 * D_FF), (ffn_down, 0, d))
        kr, ki = _hyena_filters(conv_tables, w1p, b1, hy_w2, b2, hy_w3, hy_freq, i)
        hn3 = hn.reshape(b, l, d)
        ya = [_hyena_mixer(hn3, w_hyena, hy_short_w, conv_tables, kr, ki, hy_bias, i, tile)
              for tile in range(HY_W // COL_TILE)]
        yb = _na_mixer(hn3, w_in, i, na_bias)
        yc = _shortconv_mixer(hn3, w_in, i, sc_conv_w)
        x2d = _merge(hn, x2d, ya[0], ya[1], yb.reshape(n, NA_W), yc.reshape(n, SC_W), w_gate, gate_bias,
                     w_br.reshape(N_BRANCH, HY_W, d), w_o, gains, i)
        kv = _mem_kv(mem, mem_gain, i, wkv)
        x3, hn2 = _xattn(x2d.reshape(b, l, d), kv, wq, wo, gains, i)
        f = _ffn(hn2, w_up, ffn_conv, w_down, i)
        x2d, hn = _residual(x3.reshape(n, d), f.reshape(n, d), gains, i, i + 1 == depth)
    return x2d.reshape(b, l, d)
```

```python
import functools
import math

import jax
import jax.numpy as jnp
from jax import lax
from jax.experimental import pallas as pl
from jax.experimental.pallas import tpu as pltpu

D_MODEL = 1024
SEQ = 2048
N_MEM = 256
GRID_W = 64
HY_W = 512
NA_HEADS = 8
NA_HEAD_DIM = 64
NA_W = NA_HEADS * NA_HEAD_DIM
NA_WIN_ROWS = 8
NA_WIN_COLS = 16
SC_W = 512
XA_HEADS = 4
XA_HEAD_DIM = D_MODEL // XA_HEADS
D_FF = 2816
HY_ORDER = 2
HY_EMB = 33
HY_HIDDEN = 64
HY_FAST_DECAY = 0.3
HY_SLOW_DECAY = 1.5
HY_TARGET = 1e-2
N_BRANCH = 3
EPS = 1e-6

FFT_N = 2 * SEQ
FFT_RADIX = 4
FFT_SUB = FFT_N // FFT_RADIX
NA_ROWS = SEQ // GRID_W
NA_KEYS = NA_WIN_ROWS * GRID_W
NA_GROUP = 4
NA_ROW_GROUP = 8
LANES = 128
COL_TILE = 256
ROW_TILE = 512
MERGE_ROWS = 2 * ROW_TILE
XA_ROWS = 2 * ROW_TILE
XA_CHUNK = 256
CAST_STEPS = 8
CONV_PAD = 8
CONV_CHUNK = 512
FFN_CUTS = (0, D_FF)
CONV_HALO = 16
VMEM_LIMIT = 60 * 1024 * 1024

BF16 = jnp.bfloat16
F32 = jnp.float32


def _params(*sem):
    return pltpu.CompilerParams(dimension_semantics=sem, vmem_limit_bytes=VMEM_LIMIT)


def _gain_spec(index):
    return pl.BlockSpec((None, 1, D_MODEL), lambda *_: (index, 0, 0))


def _layer_spec(arr, layer, width=None, col=0):
    _, r, c = arr.shape
    return pl.BlockSpec((None, r, c if width is None else width), lambda *_: (layer, 0, col))


def _rms(xf, g):
    ms = jnp.mean(xf * xf, axis=-1, keepdims=True)
    return xf * lax.rsqrt(ms + EPS) * g


def _dwconv3(u, w_ref):
    n = u.shape[0]
    zeros = jnp.zeros((CONV_PAD, u.shape[1]), F32)
    padded = jnp.concatenate([zeros, u, zeros], axis=0)
    m = n + 2 * CONV_PAD
    prev = pltpu.roll(padded, 1, 0)[CONV_PAD:CONV_PAD + n]
    nxt = pltpu.roll(padded, m - 1, 0)[CONV_PAD:CONV_PAD + n]
    return prev * w_ref[0:1, :] + u * w_ref[1:2, :] + nxt * w_ref[2:3, :]


def _prenorm_kernel(x_ref, g_ref, o_ref):
    o_ref[...] = _rms(x_ref[...], g_ref[...]).astype(BF16)


def _prenorm(x2d, gains, gi):
    n = x2d.shape[0]
    tm = 1024
    return pl.pallas_call(
        _prenorm_kernel,
        grid=(n // tm,),
        in_specs=[pl.BlockSpec((tm, D_MODEL), lambda i: (i, 0)),
                  _gain_spec(gi)],
        out_specs=pl.BlockSpec((tm, D_MODEL), lambda i: (i, 0)),
        out_shape=jax.ShapeDtypeStruct((n, D_MODEL), BF16),
        compiler_params=_params("arbitrary"),
        name="prenorm",
    )(x2d, gains)


def _cast_kernel(parts, *refs):
    n_in = sum(parts)
    w_refs, o_refs = refs[:n_in], refs[n_in:]
    k = 0
    for o_ref, n in zip(o_refs, parts):
        width = o_ref.shape[1] // n
        for p in range(n):
            o_ref[:, p * width:(p + 1) * width] = w_refs[k][...].astype(BF16)
            k += 1


def _cast_layer(layer, *weights):
    in_specs, out_specs, out_shapes, operands, parts = [], [], [], [], []
    for w, col0, ncols in weights:
        rb = w.shape[1] // CAST_STEPS
        cb = math.gcd(col0, ncols) if col0 else ncols
        assert w.shape[1] % CAST_STEPS == 0 and rb % 16 == 0 and cb % LANES == 0
        parts.append(ncols // cb)
        for p in range(ncols // cb):
            in_specs.append(pl.BlockSpec((None, rb, cb),
                                         lambda i, cblk=col0 // cb + p: (layer, i, cblk)))
            operands.append(w)
        out_specs.append(pl.BlockSpec((rb, ncols), lambda i: (i, 0)))
        out_shapes.append(jax.ShapeDtypeStruct((w.shape[1], ncols), BF16))
    return pl.pallas_call(
        functools.partial(_cast_kernel, tuple(parts)),
        grid=(CAST_STEPS,),
        in_specs=in_specs,
        out_specs=tuple(out_specs),
        out_shape=tuple(out_shapes),
        compiler_params=_params("arbitrary"),
        name="cast_bf16",
    )(*operands)


def _filter_mlp_kernel(z_ref, w1_ref, b1_ref, w2_ref, b2_ref, w3_ref, f_ref, t_ref, dl_ref,
                       hs_ref, hd_ref):
    hp = lax.Precision.HIGHEST
    h = jnp.sin(f_ref[0:1, :] * (jnp.dot(z_ref[...], w1_ref[...], precision=hp) + b1_ref[...]))
    h = jnp.sin(f_ref[1:2, :] * (jnp.dot(h, w2_ref[...], precision=hp) + b2_ref[...]))
    decay = jnp.exp(-t_ref[...] * dl_ref[...])
    row = lax.broadcasted_iota(jnp.int32, (SEQ, HY_W), 0)
    h_hi = h.astype(BF16)
    h_lo = (h - h_hi.astype(F32)).astype(BF16)

    def out_layer(cols):
        w = w3_ref[:, cols]
        w_hi = w.astype(BF16)
        w_lo = (w - w_hi.astype(F32)).astype(BF16)
        return (jnp.dot(h_hi, w_hi, preferred_element_type=F32)
                + (jnp.dot(h_hi, w_lo, preferred_element_type=F32)
                   + jnp.dot(h_lo, w_hi, preferred_element_type=F32)))

    for o in range(HY_ORDER):
        c_f = o * HY_W
        c_b = HY_ORDER * HY_W + o * HY_W
        hf = out_layer(slice(c_f, c_f + HY_W)) * decay
        hb = out_layer(slice(c_b, c_b + HY_W)) * decay
        hb = jnp.where(row == 0, 0.0, hb)
        hs_ref[:, c_f:c_f + HY_W] = (hf + hb).astype(BF16)
        hd_ref[:, c_f:c_f + HY_W] = (hb - hf).astype(BF16)


def _filter_dft_kernel(cf_ref, sf_ref, hs_ref, hd_ref, kr_ref, ki_ref):
    rows = SEQ // FFT_RADIX
    scale = 2.0 / FFT_N
    sub = lambda ref, r: ref[r * rows:(r + 1) * rows, :]
    cos_dot = lambda ref, r: jnp.dot(cf_ref[r], sub(ref, r), preferred_element_type=F32)
    sin_dot = lambda ref, r: jnp.dot(sf_ref[r], sub(ref, r), preferred_element_type=F32)
    c = [cos_dot(hs_ref, r) for r in range(FFT_RADIX)]
    a, b, e = c[0] + c[2], c[0] - c[2], c[1] + c[3]
    d_im = sin_dot(hs_ref, 3) - sin_dot(hs_ref, 1)
    for f, val in enumerate((a + e, b + d_im, b - d_im, a - e)):
        kr_ref[f] = val * scale
    s = [sin_dot(hd_ref, r) for r in range(FFT_RADIX)]
    a, b, e = s[0] + s[2], s[0] - s[2], s[1] + s[3]
    d_re = cos_dot(hd_ref, 1) - cos_dot(hd_ref, 3)
    for f, val in enumerate((a + e, b + d_re, d_re - b, e - a)):
        ki_ref[f] = val * scale


def _hyena_filters(tables, w1p, b1, w2, b2, w3, freq, layer):
    t = jnp.linspace(0.0, 1.0, SEQ, dtype=F32)[:, None]
    bands = (HY_EMB - 1) // 2
    w = 2.0 * math.pi * jnp.arange(SEQ, dtype=F32)[:, None] / SEQ
    f = jnp.linspace(1e-4, bands - 1, bands, dtype=F32)[None, :]
    z = jnp.concatenate([t, jnp.cos(f * w), -jnp.sin(f * w)], axis=-1)
    z = jnp.pad(z, ((0, 0), (0, HY_HIDDEN - HY_EMB)))
    rows = SEQ // FFT_RADIX
    decimate = lambda a: a.reshape(rows, FFT_RADIX, -1).transpose(1, 0, 2).reshape(SEQ, -1)
    z, t = decimate(z), decimate(t)
    deltas = jnp.abs(jnp.linspace(math.log(HY_TARGET) / HY_SLOW_DECAY,
                                  math.log(HY_TARGET) / HY_FAST_DECAY, HY_W, dtype=F32))[None, :]
    width = HY_ORDER * HY_W
    whole = lambda a: pl.BlockSpec(a.shape, lambda i: (0,) * a.ndim)
    taps = pl.BlockSpec((SEQ, width), lambda i: (0, 0))
    hs, hd = pl.pallas_call(
        _filter_mlp_kernel,
        grid=(1,),
        in_specs=[whole(z)] + [_layer_spec(a, layer) for a in (w1p, b1, w2, b2, w3, freq)]
        + [whole(t), whole(deltas)],
        out_specs=(taps, taps),
        out_shape=(jax.ShapeDtypeStruct((SEQ, width), BF16),
                   jax.ShapeDtypeStruct((SEQ, width), BF16)),
        compiler_params=_params("arbitrary"),
        name="hyena_filter_mlp",
    )(z, w1p, b1, w2, b2, w3, freq, t, deltas)
    cf, sf = tables[:2]
    small = pl.BlockSpec((FFT_RADIX, rows, rows), lambda j: (0, 0, 0))
    col = pl.BlockSpec((SEQ, COL_TILE), lambda j: (0, j))
    fam = pl.BlockSpec((FFT_RADIX, rows, COL_TILE), lambda j: (0, 0, j))
    fam_shape = jax.ShapeDtypeStruct((FFT_RADIX, rows, width), F32)
    return pl.pallas_call(
        _filter_dft_kernel,
        grid=(width // COL_TILE,),
        in_specs=[small, small, col, col],
        out_specs=(fam, fam),
        out_shape=(fam_shape, fam_shape),
        compiler_params=_params("arbitrary"),
        name="hyena_filter_dft",
    )(cf, sf, hs, hd)


def _hyena_kernel(hn_ref, wv_ref, w1_ref, w2_ref, sv_ref, s1_ref, s2_ref, cf_ref, sf_ref, ct_ref,
                  st_ref, kr0_ref, ki0_ref, kr1_ref, ki1_ref, bias_ref, o_ref,
                  slab_v_ref, slab_1_ref, slab_2_ref, slab_o_ref):
    radix = FFT_RADIX
    rows = SEQ // radix
    n_slab = COL_TILE // LANES
    zero_row = jnp.zeros((CONV_PAD, COL_TILE), F32)

    def project(w_ref, slab_ref):
        half = SEQ // 2
        for top in (0, half):
            u = jnp.dot(hn_ref[top:top + half, :], w_ref[...], preferred_element_type=F32)
            for j in range(n_slab):
                slab_ref[j, top:top + half, :] = u[:, j * LANES:(j + 1) * LANES]
        return [jnp.concatenate([slab_ref[j, pl.ds(r, rows, stride=radix), :] for j in range(n_slab)],
                                axis=1) for r in range(radix)]

    def short_conv(u, w_ref):
        n = u[0].shape[0]
        prev_wrap = pltpu.roll(jnp.concatenate([u[-1], zero_row], axis=0), 1, 0)[:n]
        next_wrap = pltpu.roll(jnp.concatenate([zero_row, u[0]], axis=0), n + CONV_PAD - 1, 0)[CONV_PAD:]
        prev = [prev_wrap] + u[:-1]
        nxt = u[1:] + [next_wrap]
        return [prev[r] * w_ref[0:1, :] + u[r] * w_ref[1:2, :] + nxt[r] * w_ref[2:3, :]
                for r in range(radix)]

    def cmul(ar, ai, br, bi):
        return ar * br - ai * bi, ar * bi + ai * br

    def transform(x):
        xb = [v.astype(BF16) for v in x]
        tr = [jnp.dot(cf_ref[r], xb[r], preferred_element_type=F32) for r in range(radix)]
        ti = [-jnp.dot(sf_ref[r], xb[r], preferred_element_type=F32) for r in range(radix)]
        return tr, ti

    def filter_and_invert(tr, ti, kr_ref, ki_ref):
        ar, ai = tr[0] + tr[2], ti[0] + ti[2]
        br, bi = tr[0] - tr[2], ti[0] - ti[2]
        cr, ci = tr[1] + tr[3], ti[1] + ti[3]
        dr, di = tr[1] - tr[3], ti[1] - ti[3]
        fam = [(ar + cr, ai + ci), (br + di, bi - dr), (br - di, -bi - dr), (ar - cr, ci - ai)]
        y = [cmul(fr, fi, kr_ref[f], ki_ref[f]) for f, (fr, fi) in enumerate(fam)]
        er, ei = y[0][0] + y[3][0], y[0][1] - y[3][1]
        fr, fi = y[0][0] - y[3][0], y[0][1] + y[3][1]
        gr, gi = y[1][0] + y[2][0], y[1][1] - y[2][1]
        hr, hi = y[1][0] - y[2][0], y[1][1] + y[2][1]
        p = [(er + gr, ei + gi), (fr - hi, fi + hr), (er - gr, ei - gi), (fr + hi, fi - hr)]
        out = []
        for r in range(radix):
            pr, pi_ = p[r]
            out.append(jnp.dot(ct_ref[r], pr.astype(BF16), preferred_element_type=F32)
                       - jnp.dot(st_ref[r], pi_.astype(BF16), preferred_element_type=F32))
        return out

    v = short_conv(project(wv_ref, slab_v_ref), sv_ref)
    spectrum = transform(v)
    x1 = short_conv(project(w1_ref, slab_1_ref), s1_ref)
    y = filter_and_invert(*spectrum, kr0_ref, ki0_ref)
    z = [x1[r] * (y[r] + v[r] * bias_ref[0:1, :]) for r in range(radix)]
    spectrum = transform(z)
    x2 = short_conv(project(w2_ref, slab_2_ref), s2_ref)
    y = filter_and_invert(*spectrum, kr1_ref, ki1_ref)
    for r in range(radix):
        out = x2[r] * (y[r] + z[r] * bias_ref[1:2, :])
        for j in range(n_slab):
            slab_o_ref[j, pl.ds(r, rows, stride=radix), :] = out[:, j * LANES:(j + 1) * LANES]
    o_ref[...] = jnp.concatenate([slab_o_ref[j] for j in range(n_slab)], axis=1).astype(BF16)


def _hyena_mixer(hn, w_hy, short_w, tables, kr, ki, bias, layer):
    b = hn.shape[0]
    nt = HY_W // COL_TILE
    rows = SEQ // FFT_RADIX
    once = pl.Buffered(1)
    cf, sf, ct, st = tables

    def wcol(k):
        return pl.BlockSpec((D_MODEL, COL_TILE), lambda j, i: (0, k * nt + j), pipeline_mode=once)

    def scol(k):
        return pl.BlockSpec((None, 3, COL_TILE), lambda j, i: (layer, 0, k * nt + j))

    def kfam(o):
        return pl.BlockSpec((FFT_RADIX, rows, COL_TILE), lambda j, i: (0, 0, o * nt + j),
                            pipeline_mode=once)

    small = pl.BlockSpec((FFT_RADIX, rows, rows), lambda j, i: (0, 0, 0), pipeline_mode=once)
    slab = pltpu.VMEM((COL_TILE // LANES, SEQ, LANES), F32)
    out = pl.pallas_call(
        _hyena_kernel,
        grid=(nt, b),
        in_specs=[pl.BlockSpec((None, SEQ, D_MODEL), lambda j, i: (i, 0, 0)),
                  wcol(0), wcol(1), wcol(2), scol(0), scol(1), scol(2),
                  small, small, small, small,
                  kfam(0), kfam(0), kfam(1), kfam(1),
                  pl.BlockSpec((None, HY_ORDER, COL_TILE), lambda j, i: (layer, 0, j))],
        out_specs=pl.BlockSpec((None, SEQ, COL_TILE), lambda j, i: (i, 0, j)),
        out_shape=jax.ShapeDtypeStruct((b, SEQ, HY_W), BF16),
        scratch_shapes=[slab, slab, slab, slab],
        compiler_params=_params("arbitrary", "arbitrary"),
        name="hyena_mixer",
    )(hn, w_hy, w_hy, w_hy, short_w, short_w, short_w, cf, sf, ct, st, kr, ki, kr, ki, bias)
    return out.reshape(b * SEQ, HY_W)


def _shortconv_kernel(hn_ref, wb_ref, wc_ref, wx_ref, cw_ref, o_ref):
    wb = wb_ref[...].astype(BF16)
    wc = wc_ref[...].astype(BF16)
    wx = wx_ref[...].astype(BF16)

    def project(r0):
        lo = max(r0 - CONV_HALO, 0)
        hi = min(r0 + CONV_CHUNK + CONV_HALO, SEQ)
        hn = hn_ref[lo:hi, :]
        return (r0, r0 - lo, jnp.dot(hn, wb, preferred_element_type=F32),
                jnp.dot(hn, wc, preferred_element_type=F32), jnp.dot(hn, wx, preferred_element_type=F32))

    def finish(r0, skip, bg, cg, xi):
        out = bg * _dwconv3(cg * xi, cw_ref)
        o_ref[r0:r0 + CONV_CHUNK, :] = out[skip:skip + CONV_CHUNK].astype(BF16)

    pending = None
    for r0 in range(0, SEQ, CONV_CHUNK):
        current = project(r0)
        if pending is not None:
            finish(*pending)
        pending = current
    finish(*pending)


def _w_in_cols(layer, width, first):
    return lambda k, nt: pl.BlockSpec((None, D_MODEL, width),
                                      lambda j, i: (layer, 0, first + k * nt + j))


def _shortconv_mixer(hn, w_in, layer, conv_w):
    b = hn.shape[0]
    nt = SC_W // COL_TILE
    wcol = _w_in_cols(layer, COL_TILE, (3 * HY_W + 3 * NA_W) // COL_TILE)
    return pl.pallas_call(
        _shortconv_kernel,
        grid=(nt, b),
        in_specs=[pl.BlockSpec((None, SEQ, D_MODEL), lambda j, i: (i, 0, 0)),
                  wcol(0, nt), wcol(1, nt), wcol(2, nt),
                  pl.BlockSpec((None, 3, COL_TILE), lambda j, i: (layer, 0, j))],
        out_specs=pl.BlockSpec((None, SEQ, COL_TILE), lambda j, i: (i, 0, j)),
        out_shape=jax.ShapeDtypeStruct((b, SEQ, SC_W), BF16),
        compiler_params=_params("arbitrary", "arbitrary"),
        name="shortconv_mixer",
    )(hn, w_in, w_in, w_in, conv_w)


def _na_kernel(hn_ref, wq_ref, wk_ref, wv_ref, bias_ref, o_ref, q_ref, k_ref, v_ref, s_ref):
    hn = hn_ref[...]
    q = jnp.dot(hn, wq_ref[...].astype(BF16), preferred_element_type=F32)
    q_ref[...] = (q * (NA_HEAD_DIM ** -0.5)).astype(BF16)
    k_ref[...] = jnp.dot(hn, wk_ref[...].astype(BF16), preferred_element_type=F32).astype(BF16)
    v_ref[...] = jnp.dot(hn, wv_ref[...].astype(BF16), preferred_element_type=F32).astype(BF16)
    gw = NA_GROUP * NA_HEAD_DIM
    same_head = (lax.broadcasted_iota(jnp.int32, (gw, gw), 0) // NA_HEAD_DIM
                 == lax.broadcasted_iota(jnp.int32, (gw, gw), 1) // NA_HEAD_DIM)

    def scores(r):
        w0 = jnp.clip(r - NA_WIN_ROWS // 2, 0, NA_ROWS - NA_WIN_ROWS)
        off = w0 - r + (NA_WIN_ROWS - 1)
        q0 = pl.multiple_of(r * GRID_W, GRID_W)
        k0 = pl.multiple_of(w0 * GRID_W, GRID_W)
        q_row = q_ref[pl.ds(q0, GRID_W), :]
        q_heads = jnp.where(same_head, jnp.concatenate([q_row] * NA_GROUP, axis=0), 0)
        s = lax.dot_general(q_heads, k_ref[pl.ds(k0, NA_KEYS), :], (((1,), (1,)), ((), ())),
                            preferred_element_type=F32)
        bias = jnp.concatenate(
            [jnp.concatenate([bias_ref[h, off + 2 * m] for m in range(NA_WIN_ROWS // 2)], axis=1)
             for h in range(NA_GROUP)], axis=0)
        return s + bias

    def attend(r, s):
        w0 = jnp.clip(r - NA_WIN_ROWS // 2, 0, NA_ROWS - NA_WIN_ROWS)
        q0 = pl.multiple_of(r * GRID_W, GRID_W)
        k0 = pl.multiple_of(w0 * GRID_W, GRID_W)
        p = jnp.exp(s - jnp.max(s, axis=-1, keepdims=True))
        inv = 1.0 / jnp.sum(p, axis=-1, keepdims=True)
        pv = jnp.dot(p.astype(BF16), v_ref[pl.ds(k0, NA_KEYS), :], preferred_element_type=F32)
        pv = jnp.where(same_head, pv * inv, 0.0)
        out = pv[0:GRID_W]
        for h in range(1, NA_GROUP):
            out = out + pv[h * GRID_W:(h + 1) * GRID_W]
        o_ref[pl.ds(q0, GRID_W), :] = out.astype(BF16)

    groups = NA_ROWS // NA_ROW_GROUP
    for t in range(NA_ROW_GROUP):
        s_ref[t] = scores(jnp.int32(t))

    def rows_body(i, carry):
        for t in range(NA_ROW_GROUP):
            s_next = scores(i * NA_ROW_GROUP + t)
            attend((i - 1) * NA_ROW_GROUP + t, s_ref[t])
            s_ref[t] = s_next
        return carry

    lax.fori_loop(1, groups, rows_body, 0)
    for t in range(NA_ROW_GROUP):
        attend(jnp.int32((groups - 1) * NA_ROW_GROUP + t), s_ref[t])


def _na_bias(rpb):
    c = jnp.arange(GRID_W)
    col_start = jnp.clip(c - NA_WIN_COLS // 2, 0, GRID_W - NA_WIN_COLS)
    col_mask = (c[None, :] >= col_start[:, None]) & (c[None, :] < col_start[:, None] + NA_WIN_COLS)
    dc = jnp.clip(c[None, :] - c[:, None] + NA_WIN_COLS - 1, 0, 2 * NA_WIN_COLS - 2)
    pick = (dc[None] == jnp.arange(2 * NA_WIN_COLS - 1)[:, None, None]).astype(F32)
    table = jnp.einsum("lhrd,dqc->lhrqc", rpb.astype(F32), pick, precision=lax.Precision.HIGHEST)
    table = table + jnp.where(col_mask, 0.0, -1e30)
    return jnp.concatenate([table[:, :, :-1], table[:, :, 1:]], axis=-1)


def _na_mixer(hn, w_in, layer, bias):
    b = hn.shape[0]
    gw = NA_GROUP * NA_HEAD_DIM
    ng = NA_W // gw
    wcol = _w_in_cols(layer, gw, 3 * HY_W // gw)
    return pl.pallas_call(
        _na_kernel,
        grid=(ng, b),
        in_specs=[pl.BlockSpec((None, SEQ, D_MODEL), lambda j, i: (i, 0, 0)),
                  wcol(0, ng), wcol(1, ng), wcol(2, ng),
                  pl.BlockSpec((None, NA_GROUP, 2 * NA_WIN_ROWS - 2, GRID_W, 2 * GRID_W),
                               lambda j, i: (layer, j, 0, 0, 0))],
        out_specs=pl.BlockSpec((None, SEQ, gw), lambda j, i: (i, 0, j)),
        out_shape=jax.ShapeDtypeStruct((b, SEQ, NA_W), BF16),
        scratch_shapes=[pltpu.VMEM((SEQ, gw), BF16) for _ in range(3)]
        + [pltpu.VMEM((NA_ROW_GROUP, gw, NA_KEYS), F32)],
        compiler_params=_params("arbitrary", "arbitrary"),
        name="na_mixer",
    )(hn, w_in, w_in, w_in, bias)


def _merge_kernel(hn_ref, x_ref, ya_ref, yb_ref, yc_ref, wg_ref, gb_ref, wb_ref, wo_ref,
                  g_ref, o_ref):
    def gated_sum(rows):
        hn = hn_ref[rows, :]
        merged = None
        for i, y in enumerate((ya_ref[rows, :], yb_ref[rows, :], yc_ref[rows, :])):
            pre = jnp.dot(hn, wg_ref[:, i * D_MODEL:(i + 1) * D_MODEL], preferred_element_type=F32)
            gate = jax.nn.sigmoid(pre + gb_ref[i:i + 1, :])
            term = gate * jnp.dot(y, wb_ref[i], preferred_element_type=F32)
            merged = term if merged is None else merged + term
        return merged.astype(BF16)

    chunks = [slice(r, r + ROW_TILE) for r in range(0, MERGE_ROWS, ROW_TILE)]
    merged = [gated_sum(rows) for rows in chunks]
    for rows, m in zip(chunks, merged):
        out = jnp.dot(m, wo_ref[...], preferred_element_type=F32)
        o_ref[rows, :] = x_ref[rows, :] + _rms(out, g_ref[...])


def _merge(hn2d, x2d, ya, yb, yc, w_gate, gate_bias, w_branch, w_out, gains, layer):
    n = x2d.shape[0]
    tm = MERGE_ROWS
    once = pl.Buffered(1)
    rows = lambda w: pl.BlockSpec((tm, w), lambda i: (i, 0))
    return pl.pallas_call(
        _merge_kernel,
        grid=(n // tm,),
        in_specs=[rows(D_MODEL), rows(D_MODEL), rows(HY_W), rows(NA_W), rows(SC_W),
                  pl.BlockSpec((D_MODEL, N_BRANCH * D_MODEL), lambda i: (0, 0), pipeline_mode=once),
                  _layer_spec(gate_bias, layer),
                  pl.BlockSpec((N_BRANCH, HY_W, D_MODEL), lambda i: (0, 0, 0), pipeline_mode=once),
                  pl.BlockSpec((D_MODEL, D_MODEL), lambda i: (0, 0), pipeline_mode=once),
                  _gain_spec(6 * layer + 1)],
        out_specs=rows(D_MODEL),
        out_shape=jax.ShapeDtypeStruct((n, D_MODEL), F32),
        compiler_params=_params("arbitrary"),
        name="merge",
    )(hn2d, x2d, ya, yb, yc, w_gate, gate_bias, w_branch, w_out, gains)


def _kv_kernel(m_ref, g_ref, w_ref, o_ref):
    mn = _rms(m_ref[...], g_ref[...]).astype(BF16)
    o_ref[...] = jnp.dot(mn, w_ref[...], preferred_element_type=F32).astype(BF16)


def _mem_kv(mem, mem_norm, layer, wkv):
    b = mem.shape[0]
    return pl.pallas_call(
        _kv_kernel,
        grid=(b,),
        in_specs=[pl.BlockSpec((None, N_MEM, D_MODEL), lambda i: (i, 0, 0)),
                  _layer_spec(mem_norm, layer),
                  pl.BlockSpec((D_MODEL, 2 * D_MODEL), lambda i: (0, 0))],
        out_specs=pl.BlockSpec((None, N_MEM, 2 * D_MODEL), lambda i: (i, 0, 0)),
        out_shape=jax.ShapeDtypeStruct((b, N_MEM, 2 * D_MODEL), BF16),
        compiler_params=_params("arbitrary"),
        name="mem_kv",
    )(mem, mem_norm, wkv)


def _xattn_kernel(x_ref, kv_ref, wq_ref, wo_ref, gq_ref, go_ref, gn_ref, o_ref, hn_ref):
    chunks = [slice(r, r + XA_CHUNK) for r in range(0, XA_ROWS, XA_CHUNK)]
    head_cols = [slice(i * XA_HEAD_DIM, (i + 1) * XA_HEAD_DIM) for i in range(XA_HEADS)]

    def query(rows):
        h = _rms(x_ref[rows, :], gq_ref[...]).astype(BF16)
        q = jnp.dot(h, wq_ref[...], preferred_element_type=F32) * (XA_HEAD_DIM ** -0.5)
        return q.astype(BF16)

    def scores(q):
        return [lax.dot_general(q[:, sl], kv_ref[:, sl], (((1,), (1,)), ((), ())),
                                preferred_element_type=F32) for sl in head_cols]

    def values(s_heads):
        heads = []
        for i, s in enumerate(s_heads):
            vm = kv_ref[:, D_MODEL + i * XA_HEAD_DIM:D_MODEL + (i + 1) * XA_HEAD_DIM]
            p = jnp.exp(s - jnp.max(s, axis=-1, keepdims=True))
            den = jnp.sum(p, axis=-1, keepdims=True)
            heads.append((jnp.dot(p.astype(BF16), vm, preferred_element_type=F32) / den).astype(BF16))
        return jnp.concatenate(heads, axis=-1)

    s_all = [scores(q) for q in [query(rows) for rows in chunks]]
    attended = [values(s) for s in s_all]
    for rows, a in zip(chunks, attended):
        o = jnp.dot(a, wo_ref[...], preferred_element_type=F32)
        xn = x_ref[rows, :] + _rms(o, go_ref[...])
        o_ref[rows, :] = xn
        hn_ref[rows, :] = _rms(xn, gn_ref[...]).astype(BF16)


def _xattn(x, kv, wq, wo, gains, layer):
    b = x.shape[0]
    tm = XA_ROWS
    once = pl.Buffered(1)
    rows = pl.BlockSpec((None, tm, D_MODEL), lambda i, j: (i, j, 0))
    wfull = pl.BlockSpec((D_MODEL, D_MODEL), lambda i, j: (0, 0), pipeline_mode=once)
    return pl.pallas_call(
        _xattn_kernel,
        grid=(b, SEQ // tm),
        in_specs=[rows, pl.BlockSpec((None, N_MEM, 2 * D_MODEL), lambda i, j: (i, 0, 0)),
                  wfull, wfull, _gain_spec(6 * layer + 2), _gain_spec(6 * layer + 3),
                  _gain_spec(6 * layer + 4)],
        out_specs=(rows, rows),
        out_shape=(jax.ShapeDtypeStruct((b, SEQ, D_MODEL), F32),
                   jax.ShapeDtypeStruct((b, SEQ, D_MODEL), BF16)),
        compiler_params=_params("arbitrary", "arbitrary"),
        name="xattn",
    )(x, kv, wq, wo, gains, gains, gains)


def _gelu_tanh(x):
    c = math.sqrt(2.0 / math.pi)
    half = 0.5 * x
    return half + half * jnp.tanh(x * (c + (c * 0.044715) * (x * x)))


def _ffn_kernel(hn_ref, wu_ref, cw_ref, wd_ref, o_ref):
    def up(r0, c0, c1):
        lo = max(r0 - CONV_HALO, 0)
        hi = min(r0 + CONV_CHUNK + CONV_HALO, SEQ)
        hn = hn_ref[lo:hi, :]
        ug = jnp.dot(hn, wu_ref[:, c0:c1], preferred_element_type=F32)
        uv = jnp.dot(hn, wu_ref[:, D_FF + c0:D_FF + c1], preferred_element_type=F32)
        return r0, r0 - lo, c0, c1, ug, uv

    def down(acc, r0, skip, c0, c1, ug, uv):
        act = (_gelu_tanh(_dwconv3(ug, cw_ref.at[:, c0:c1]))
               * _dwconv3(uv, cw_ref.at[:, D_FF + c0:D_FF + c1]))
        act = act[skip:skip + CONV_CHUNK].astype(BF16)
        part = jnp.dot(act, wd_ref[c0:c1, :], preferred_element_type=F32)
        acc = part if c0 == 0 else acc + part
        if c1 == D_FF:
            o_ref[r0:r0 + CONV_CHUNK, :] = acc.astype(o_ref.dtype)
        return acc

    pending, acc = None, None
    for r0 in range(0, SEQ, CONV_CHUNK):
        for c0, c1 in zip(FFN_CUTS[:-1], FFN_CUTS[1:]):
            current = up(r0, c0, c1)
            if pending is not None:
                acc = down(acc, *pending)
            pending = current
    down(acc, *pending)


def _ffn(hn, w_up, w_conv, w_down, layer):
    b = hn.shape[0]
    once = pl.Buffered(1)
    return pl.pallas_call(
        _ffn_kernel,
        grid=(b,),
        in_specs=[pl.BlockSpec((None, SEQ, D_MODEL), lambda i: (i, 0, 0)),
                  pl.BlockSpec((D_MODEL, 2 * D_FF), lambda i: (0, 0), pipeline_mode=once),
                  _layer_spec(w_conv, layer),
                  pl.BlockSpec((D_FF, D_MODEL), lambda i: (0, 0), pipeline_mode=once)],
        out_specs=pl.BlockSpec((None, SEQ, D_MODEL), lambda i: (i, 0, 0)),
        out_shape=jax.ShapeDtypeStruct((b, SEQ, D_MODEL), BF16),
        compiler_params=_params("arbitrary"),
        name="ffn",
    )(hn, w_up, w_conv, w_down)


def _residual_kernel(x_ref, f_ref, g_ref, gn_ref, o_ref, hn_ref):
    xn = x_ref[...] + _rms(f_ref[...].astype(F32), g_ref[...])
    o_ref[...] = xn
    hn_ref[...] = _rms(xn, gn_ref[...]).astype(BF16)


def _residual_last_kernel(x_ref, f_ref, g_ref, o_ref):
    o_ref[...] = x_ref[...] + _rms(f_ref[...].astype(F32), g_ref[...])


def _residual(x2d, f2d, gains, layer, last):
    n = x2d.shape[0]
    tm = 1024
    rows = pl.BlockSpec((tm, D_MODEL), lambda i: (i, 0))
    x_shape = jax.ShapeDtypeStruct((n, D_MODEL), F32)
    if last:
        return pl.pallas_call(
            _residual_last_kernel,
            grid=(n // tm,),
            in_specs=[rows, rows, _gain_spec(6 * layer + 5)],
            out_specs=rows,
            out_shape=x_shape,
            compiler_params=_params("arbitrary"),
            name="residual_last",
        )(x2d, f2d, gains), None
    return pl.pallas_call(
        _residual_kernel,
        grid=(n // tm,),
        in_specs=[rows, rows, _gain_spec(6 * layer + 5), _gain_spec(6 * (layer + 1))],
        out_specs=(rows, rows),
        out_shape=(x_shape, jax.ShapeDtypeStruct((n, D_MODEL), BF16)),
        compiler_params=_params("arbitrary"),
        name="residual",
    )(x2d, f2d, gains, gains)


def _angle_tables(num, den):
    ang = (num % den).astype(F32) * (2.0 * math.pi / den)
    return jnp.cos(ang), jnp.sin(ang)


def _dft_tables():
    rows = SEQ // FFT_RADIX
    kappa = jnp.arange(rows, dtype=jnp.int32)[None, :, None]
    m = jnp.arange(rows, dtype=jnp.int32)[None, None, :]
    r = jnp.arange(FFT_RADIX, dtype=jnp.int32)[:, None, None]
    c0, s0 = _angle_tables((2 * kappa + 1) * m, 2 * FFT_SUB)
    cr, sr = _angle_tables((2 * kappa + 1) * r, 2 * FFT_N)
    cf = (c0 * cr - s0 * sr).astype(BF16)
    sf = (s0 * cr + c0 * sr).astype(BF16)
    return cf, sf, cf.transpose(0, 2, 1), sf.transpose(0, 2, 1)


def kernel(x, mem, norm_gains, mem_norm, w_in, gate_bias, hy_short_w, hy_w1, hy_b1, hy_w2, hy_b2,
           hy_w3, hy_freq, hy_bias, na_rpb, sc_conv_w, w_branch, w_out, xa_wq, xa_wkv, xa_wo,
           ffn_up, ffn_conv, ffn_down):
    b, l, d = x.shape
    depth = w_in.shape[0]
    assert (l, d) == (SEQ, D_MODEL) and mem.shape[1:] == (N_MEM, D_MODEL)
    n = b * l
    conv_tables = _dft_tables()
    gains = norm_gains.astype(F32).reshape(depth * 6, 1, d)
    mem_gain = mem_norm.astype(F32).reshape(depth, 1, d)
    w1p = jnp.pad(hy_w1.astype(F32), ((0, 0), (0, HY_HIDDEN - HY_EMB), (0, 0)))
    b1 = hy_b1.astype(F32).reshape(depth, 1, HY_HIDDEN)
    b2 = hy_b2.astype(F32).reshape(depth, 1, HY_HIDDEN)
    na_bias = _na_bias(na_rpb)
    w_branch2d = w_branch.reshape(depth, N_BRANCH * HY_W, d)
    x2d = x.reshape(n, d)
    hn = _prenorm(x2d, gains, 0)
    for i in range(depth):
        w_hyena, w_gate, w_br, w_o, wq, wkv, wo, w_up, w_down = _cast_layer(
            i, (w_in, 0, 3 * HY_W), (w_in, 3 * HY_W + 3 * NA_W + 3 * SC_W, N_BRANCH * D_MODEL),
            (w_branch2d, 0, d), (w_out, 0, d), (xa_wq, 0, d), (xa_wkv, 0, 2 * d), (xa_wo, 0, d),
            (ffn_up, 0, 2 * D_FF), (ffn_down, 0, d))
        kr, ki = _hyena_filters(conv_tables, w1p, b1, hy_w2, b2, hy_w3, hy_freq, i)
        hn3 = hn.reshape(b, l, d)
        ya = _hyena_mixer(hn3, w_hyena, hy_short_w, conv_tables, kr, ki, hy_bias, i)
        yb = _na_mixer(hn3, w_in, i, na_bias)
        yc = _shortconv_mixer(hn3, w_in, i, sc_conv_w)
        x2d = _merge(hn, x2d, ya, yb.reshape(n, NA_W), yc.reshape(n, SC_W), w_gate, gate_bias,
                     w_br.reshape(N_BRANCH, HY_W, d), w_o, gains, i)
        kv = _mem_kv(mem, mem_gain, i, wkv)
        x3, hn2 = _xattn(x2d.reshape(b, l, d), kv, wq, wo, gains, i)
        f = _ffn(hn2, w_up, ffn_conv, w_down, i)
        x2d, hn = _residual(x3.reshape(n, d), f.reshape(n, d), gains, i, i + 1 == depth)
    return x2d.reshape(b, l, d)
```
